```python
import math
import jax, jax.numpy as jnp
from jax import lax
import numpy as np

D_MODEL = 1024
BATCH = 8
SEQ = 4096
DEPTH = 4

BRANCH_W = D_MODEL // 2
N_BRANCH = 4
EPS = 1e-6
LRU_HEADS = 8
LRU_HD = BRANCH_W // LRU_HEADS
CONV_W = 4
LRU_C = 8.0
S5_GROUP = 16
S5_GROUPS = BRANCH_W // S5_GROUP
S5_STATE = 64
RWKV_HD = 64
RWKV_HEADS = BRANCH_W // RWKV_HD
W_LORA = 64
A_LORA = 64
G_LORA = 128
RWKV_LN_EPS = 64e-5
MLA_HEADS = 8
QK_NOPE = 64
QK_ROPE = 32
QK_HD = QK_NOPE + QK_ROPE
V_HD = 64
Q_LORA = 256
KV_LORA = 128
ROPE_THETA = 10000.0
BLOCK_Q = 128
D_FF = 4 * D_MODEL

IN_WIDTHS = (BRANCH_W, BRANCH_W,
             BRANCH_W,
             BRANCH_W, BRANCH_W, BRANCH_W,
             W_LORA, A_LORA, G_LORA,
             Q_LORA, KV_LORA, QK_ROPE,
             N_BRANCH * D_MODEL)
D_IN = int(sum(IN_WIDTHS))
IN_SPLIT_POINTS = tuple(int(v) for v in np.cumsum(IN_WIDTHS)[:-1])

kernel_name = "hybrid_rglru_s5_rwkv7_mla_trunk"


def _rmsnorm(x, g, eps=EPS):
    x32 = x.astype(jnp.float32)
    y = x32 * lax.rsqrt(jnp.mean(x32 * x32, axis=-1, keepdims=True) + eps)
    return (y * g.astype(jnp.float32)).astype(x.dtype)


def _shift_mix(p, mu):
    prev = jnp.pad(p, ((0, 0), (1, 0), (0, 0)))[:, :-1]
    return p + (prev - p) * mu


def _linear_recurrence(a, u):
    def step(h, au):
        h = au[0] * h + au[1]
        return h, h
    _, hs = lax.scan(step, jnp.zeros_like(u[:, 0]), (jnp.swapaxes(a, 0, 1), jnp.swapaxes(u, 0, 1)))
    return jnp.swapaxes(hs, 0, 1)


def _rglru_branch(xb, gb, conv_w, conv_b, gate_w, gate_b, lam):
    f32 = jnp.float32
    bsz, seq, width = xb.shape
    xc = lax.conv_general_dilated(
        xb, conv_w[:, None, :], window_strides=(1,), padding=((CONV_W - 1, 0),),
        dimension_numbers=("NWC", "WIO", "NWC"), feature_group_count=width) + conv_b
    xh = xc.reshape(bsz, seq, LRU_HEADS, LRU_HD)
    gates = jnp.einsum("bthi,ghij->gbthj", xh, gate_w).reshape(2, bsz, seq, width) + gate_b[:, None, None, :]
    gates = gates.astype(f32)
    r = jax.nn.sigmoid(gates[0])
    i = jax.nn.sigmoid(gates[1])
    log_a = -LRU_C * r * jax.nn.softplus(-lam.astype(f32))
    a = jnp.exp(log_a)
    u = jnp.sqrt(-jnp.expm1(2.0 * log_a)) * (i * xc.astype(f32))
    h = _linear_recurrence(a, u)
    return (h * jax.nn.gelu(gb.astype(f32))).astype(xb.dtype)


def _s5_branch(u, a_re, a_im, b_re, b_im, c_re, c_im, d_skip, log_dt, w_glu):
    f32 = jnp.float32
    bsz, seq, width = u.shape
    lam = lax.complex(a_re.astype(f32), a_im.astype(f32))
    dt = jnp.exp(log_dt.astype(f32))[:, None]
    lam_bar = jnp.exp(lam * dt)
    b_bar = ((lam_bar - 1.0) / lam)[:, :, None] * lax.complex(b_re.astype(f32), b_im.astype(f32))
    u32 = u.astype(f32)
    ug = u32.reshape(bsz, seq, S5_GROUPS, S5_GROUP).astype(jnp.complex64)
    bu = jnp.einsum("gpm,btgm->tbgp", b_bar, ug)
    a_el = jnp.broadcast_to(lam_bar[None, None], (seq, 1, S5_GROUPS, S5_STATE))

    def combine(e1, e2):
        return e2[0] * e1[0], e2[0] * e1[1] + e2[1]

    _, states = lax.associative_scan(combine, (a_el, bu), axis=0)
    c = lax.complex(c_re.astype(f32), c_im.astype(f32))
    y = jnp.einsum("gmp,tbgp->btgm", c, states).real.reshape(bsz, seq, width) + d_skip.astype(f32) * u32
    y = jax.nn.gelu(y).astype(u.dtype)
    z = y @ w_glu
    return z[..., :width] * jax.nn.sigmoid(z[..., width:])


def _rwkv7_branch(r_in, k_in, v_in, wd_in, ad_in, gd_in, mu_rkv, mu_w, mu_a, mu_g, w0, w2, a0, a2, g2,
                  k_k, k_a, r_k, lnx_g, lnx_b):
    f32 = jnp.float32
    bsz, seq, width = r_in.shape
    r = _shift_mix(r_in, mu_rkv[0])
    k = _shift_mix(k_in, mu_rkv[1])
    v = _shift_mix(v_in, mu_rkv[2])
    wl = _shift_mix(wd_in, mu_w)
    al = _shift_mix(ad_in, mu_a)
    gl = _shift_mix(gd_in, mu_g)
    w = -jax.nn.softplus(-(w0 + jnp.tanh(wl) @ w2).astype(f32)) - 0.5
    decay = jnp.exp(-jnp.exp(w))
    a = jax.nn.sigmoid((a0 + al @ a2).astype(f32))
    g = jax.nn.sigmoid(gl) @ g2

    def heads(t):
        return t.astype(f32).reshape(bsz, seq, RWKV_HEADS, RWKV_HD)

    kk = heads(k * k_k)
    kk = kk * lax.rsqrt(jnp.sum(kk * kk, axis=-1, keepdims=True) + 1e-12)
    k_mod = k.astype(f32) * (1.0 + (a - 1.0) * k_a.astype(f32))
    r_h, k_h, v_h, w_h = heads(r), heads(k_mod), heads(v), heads(decay)
    a_vec = -kk
    b_vec = kk * heads(a)

    def step(S, inp):
        r_t, w_t, k_t, v_t, a_t, b_t = inp
        sa = jnp.einsum("bhvk,bhk->bhv", S, a_t)
        S = S * w_t[:, :, None, :] + sa[..., None] * b_t[:, :, None, :] + v_t[..., None] * k_t[:, :, None, :]
        return S, jnp.einsum("bhvk,bhk->bhv", S, r_t)

    s0 = jnp.zeros((bsz, RWKV_HEADS, RWKV_HD, RWKV_HD), f32)
    xs = tuple(jnp.swapaxes(t, 0, 1) for t in (r_h, w_h, k_h, v_h, a_vec, b_vec))
    _, y = lax.scan(step, s0, xs)
    y = jnp.swapaxes(y, 0, 1)
    mean = jnp.mean(y, axis=-1, keepdims=True)
    var = jnp.mean(jnp.square(y - mean), axis=-1, keepdims=True)
    y = ((y - mean) * lax.rsqrt(var + RWKV_LN_EPS)).reshape(bsz, seq, width)
    y = y * lnx_g.astype(f32) + lnx_b.astype(f32)
    bonus = jnp.sum(r_h * k_h * r_k.astype(f32), axis=-1, keepdims=True) * v_h
    y = y + bonus.reshape(bsz, seq, width)
    return (y * g.astype(f32)).astype(r_in.dtype)


def _rope(x, cos, sin):
    x1, x2 = jnp.split(x, 2, axis=-1)
    return jnp.concatenate([x1 * cos - x2 * sin, x1 * sin + x2 * cos], axis=-1)


def _causal_block_attention(q, k, v):
    bsz, seq, n_heads, _ = q.shape
    scale = QK_HD ** -0.5
    outs = []
    for blk in range(seq // BLOCK_Q):
        q0, q1 = blk * BLOCK_Q, (blk + 1) * BLOCK_Q
        s = jnp.einsum("bqhd,bkhd->bhqk", q[:, q0:q1], k[:, :q1]).astype(jnp.float32) * scale
        mask = jnp.arange(q1)[None, :] <= (q0 + jnp.arange(BLOCK_Q))[:, None]
        p = jax.nn.softmax(jnp.where(mask, s, -jnp.inf), axis=-1).astype(v.dtype)
        outs.append(jnp.einsum("bhqk,bkhd->bqhd", p, v[:, :q1]))
    return jnp.concatenate(outs, axis=1)


def _mla_branch(c_q, c_kv, k_rope_in, q_norm, w_uq, kv_norm, w_ukv, qk_norm_q, qk_norm_k):
    f32 = jnp.float32
    bsz, seq, _ = c_q.shape
    q = (_rmsnorm(c_q, q_norm) @ w_uq).reshape(bsz, seq, MLA_HEADS, QK_HD)
    kv = (_rmsnorm(c_kv, kv_norm) @ w_ukv).reshape(bsz, seq, MLA_HEADS, QK_NOPE + V_HD)
    k_nope, v = kv[..., :QK_NOPE], kv[..., QK_NOPE:]
    k_r = jnp.broadcast_to(k_rope_in[:, :, None, :], (bsz, seq, MLA_HEADS, QK_ROPE))
    k = jnp.concatenate([k_nope, k_r], axis=-1)
    q = _rmsnorm(q, qk_norm_q)
    k = _rmsnorm(k, qk_norm_k)
    pos = jnp.arange(seq, dtype=f32)
    inv_freq = ROPE_THETA ** (-jnp.arange(0, QK_ROPE, 2, dtype=f32) / QK_ROPE)
    ang = pos[:, None] * inv_freq[None, :]
    cos = jnp.cos(ang)[None, :, None, :]
    sin = jnp.sin(ang)[None, :, None, :]
    q = jnp.concatenate([q[..., :QK_NOPE], _rope(q[..., QK_NOPE:].astype(f32), cos, sin).astype(q.dtype)], axis=-1)
    k = jnp.concatenate([k[..., :QK_NOPE], _rope(k[..., QK_NOPE:].astype(f32), cos, sin).astype(k.dtype)], axis=-1)
    o = _causal_block_attention(q, k, v)
    return o.reshape(bsz, seq, MLA_HEADS * V_HD)


def setup_inputs(seed: int = 0) -> dict:
    key = jax.random.key(seed)
    keys = jax.random.split(key, 42)
    f32 = jnp.float32
    L, C, G, P, M = DEPTH, BRANCH_W, S5_GROUPS, S5_STATE, S5_GROUP

    def nrm(i, shape, scale):
        return scale * jax.random.normal(keys[i], shape, f32)

    def unif(i, shape, lo, hi):
        return jax.random.uniform(keys[i], shape, f32, lo, hi)

    a0_lru = unif(7, (L, C), 0.9, 0.999) ** (1.0 / LRU_C)
    return {
        "x": nrm(0, (BATCH, SEQ, D_MODEL), 1.0),
        "norm_mix": 1.0 + nrm(1, (L, D_MODEL), 0.05),
        "w_in": nrm(2, (L, D_MODEL, D_IN), D_MODEL ** -0.5),
        "lru_conv_w": nrm(3, (L, CONV_W, C), CONV_W ** -0.5),
        "lru_conv_b": nrm(4, (L, C), 0.02),
        "lru_gate_w": nrm(5, (L, 2, LRU_HEADS, LRU_HD, LRU_HD), LRU_HD ** -0.5),
        "lru_gate_b": nrm(6, (L, 2, C), 0.02),
        "lru_lambda": jnp.log(a0_lru) - jnp.log1p(-a0_lru),
        "s5_a_re": -0.5 + nrm(8, (L, G, P), 0.01),
        "s5_a_im": math.pi * jnp.arange(P, dtype=f32) + nrm(9, (L, G, P), 0.01),
        "s5_b_re": nrm(10, (L, G, P, M), (2.0 * M) ** -0.5),
        "s5_b_im": nrm(11, (L, G, P, M), (2.0 * M) ** -0.5),
        "s5_c_re": nrm(12, (L, G, M, P), 0.5 ** 0.5),
        "s5_c_im": nrm(13, (L, G, M, P), 0.5 ** 0.5),
        "s5_d": nrm(14, (L, C), 1.0),
        "s5_log_dt": unif(15, (L, G), math.log(0.001), math.log(0.1)),
        "s5_w_glu": nrm(16, (L, C, 2 * C), C ** -0.5),
        "rwkv_mu_rkv": unif(17, (L, 3, C), 0.0, 1.0),
        "rwkv_mu_w": unif(18, (L, W_LORA), 0.0, 1.0),
        "rwkv_mu_a": unif(19, (L, A_LORA), 0.0, 1.0),
        "rwkv_mu_g": unif(20, (L, G_LORA), 0.0, 1.0),
        "rwkv_w0": unif(21, (L, C), -5.0, 1.0),
        "rwkv_w2": nrm(22, (L, W_LORA, C), W_LORA ** -0.5),
        "rwkv_a0": nrm(23, (L, C), 0.1),
        "rwkv_a2": nrm(24, (L, A_LORA, C), A_LORA ** -0.5),
        "rwkv_g2": nrm(25, (L, G_LORA, C), G_LORA ** -0.5),
        "rwkv_k_k": 0.85 + nrm(26, (L, C), 0.05),
        "rwkv_k_a": unif(27, (L, C), 0.0, 1.0),
        "rwkv_r_k": nrm(28, (L, RWKV_HEADS, RWKV_HD), 0.1),
        "rwkv_lnx_g": 1.0 + nrm(29, (L, C), 0.05),
        "rwkv_lnx_b": nrm(30, (L, C), 0.02),
        "mla_q_norm": 1.0 + nrm(31, (L, Q_LORA), 0.05),
        "mla_w_uq": nrm(32, (L, Q_LORA, MLA_HEADS * QK_HD), Q_LORA ** -0.5),
        "mla_kv_norm": 1.0 + nrm(33, (L, KV_LORA), 0.05),
        "mla_w_ukv": nrm(34, (L, KV_LORA, MLA_HEADS * (QK_NOPE + V_HD)), KV_LORA ** -0.5),
        "mla_qk_norm_q": 1.0 + nrm(35, (L, QK_HD), 0.05),
        "mla_qk_norm_k": 1.0 + nrm(36, (L, QK_HD), 0.05),
        "w_branch": nrm(37, (L, N_BRANCH, C, D_MODEL), C ** -0.5),
        "w_out": nrm(38, (L, D_MODEL, D_MODEL), D_MODEL ** -0.5),
        "norm_mlp": 1.0 + nrm(39, (L, D_MODEL), 0.05),
        "w_ff1": nrm(40, (L, D_MODEL, D_FF), D_MODEL ** -0.5),
        "w_ff2": nrm(41, (L, D_FF, D_MODEL), D_FF ** -0.5),
    }


def reference(x, norm_mix, w_in, lru_conv_w, lru_conv_b, lru_gate_w, lru_gate_b, lru_lambda,
              s5_a_re, s5_a_im, s5_b_re, s5_b_im, s5_c_re, s5_c_im, s5_d, s5_log_dt, s5_w_glu,
              rwkv_mu_rkv, rwkv_mu_w, rwkv_mu_a, rwkv_mu_g, rwkv_w0, rwkv_w2, rwkv_a0, rwkv_a2, rwkv_g2,
              rwkv_k_k, rwkv_k_a, rwkv_r_k, rwkv_lnx_g, rwkv_lnx_b,
              mla_q_norm, mla_w_uq, mla_kv_norm, mla_w_ukv, mla_qk_norm_q, mla_qk_norm_k,
              w_branch, w_out, norm_mlp, w_ff1, w_ff2):
    bsz, seq, _ = x.shape
    for l in range(DEPTH):
        h = _rmsnorm(x, norm_mix[l])
        (lru_x, lru_g, s5_u, rw_r, rw_k, rw_v, rw_w, rw_a, rw_g,
         mla_cq, mla_ckv, mla_kr, gate_logits) = jnp.split(h @ w_in[l], IN_SPLIT_POINTS, axis=-1)
        y_lru = _rglru_branch(lru_x, lru_g, lru_conv_w[l], lru_conv_b[l], lru_gate_w[l], lru_gate_b[l], lru_lambda[l])
        y_s5 = _s5_branch(s5_u, s5_a_re[l], s5_a_im[l], s5_b_re[l], s5_b_im[l], s5_c_re[l], s5_c_im[l],
                          s5_d[l], s5_log_dt[l], s5_w_glu[l])
        y_rwkv = _rwkv7_branch(rw_r, rw_k, rw_v, rw_w, rw_a, rw_g, rwkv_mu_rkv[l], rwkv_mu_w[l], rwkv_mu_a[l],
                               rwkv_mu_g[l], rwkv_w0[l], rwkv_w2[l], rwkv_a0[l], rwkv_a2[l], rwkv_g2[l],
                               rwkv_k_k[l], rwkv_k_a[l], rwkv_r_k[l], rwkv_lnx_g[l], rwkv_lnx_b[l])
        y_mla = _mla_branch(mla_cq, mla_ckv, mla_kr, mla_q_norm[l], mla_w_uq[l], mla_kv_norm[l], mla_w_ukv[l],
                            mla_qk_norm_q[l], mla_qk_norm_k[l])
        ys = jnp.stack([y_lru, y_s5, y_rwkv, y_mla], axis=2)
        gates = jax.nn.sigmoid(gate_logits.reshape(bsz, seq, N_BRANCH, D_MODEL))
        merged = jnp.sum(jnp.einsum("btnc,ncd->btnd", ys, w_branch[l]) * gates, axis=2)
        x = x + merged @ w_out[l]
        h2 = _rmsnorm(x, norm_mlp[l])
        x = x + jnp.square(jax.nn.relu(h2 @ w_ff1[l])) @ w_ff2[l]
    return x
```

```python
import functools
import math

import jax
import jax.numpy as jnp
import numpy as np
from jax import lax
from jax.experimental import pallas as pl
from jax.experimental.pallas import tpu as pltpu

F32 = jnp.float32
BF16 = jnp.bfloat16

EPS = 1e-6
LRU_HEADS = 8
CONV_W = 4
LRU_C = 8.0
S5_GROUP = 16
S5_STATE = 64
RWKV_HD = 64
W_LORA, A_LORA, G_LORA = 64, 64, 128
RWKV_LN_EPS = 64e-5
MLA_HEADS = 8
QK_NOPE, QK_ROPE, V_HD = 64, 32, 64
QK_HD = QK_NOPE + QK_ROPE
Q_LORA, KV_LORA = 256, 128
ROPE_THETA = 10000.0

LANES = 128
SUBLANES = 8
VMEM_LIMIT = 56 * 1024 * 1024

D_MODEL = 1024
C = D_MODEL // 2
COL_GATE = 0
COL_LRU_X = 4 * D_MODEL
COL_LRU_G = COL_LRU_X + C
COL_S5_U = COL_LRU_G + C
COL_RW_R = COL_S5_U + C
COL_RW_K = COL_RW_R + C
COL_RW_V = COL_RW_K + C
COL_LORA = COL_RW_V + C
COL_CQ = COL_LORA + 256
COL_CKV = COL_CQ + Q_LORA
COL_KR = COL_CKV + KV_LORA
D_INP = COL_KR + LANES


def _cparams(sem):
    return pltpu.CompilerParams(dimension_semantics=sem, vmem_limit_bytes=VMEM_LIMIT)


def _gelu_tanh(x):
    return 0.5 * x * (1.0 + jnp.tanh(math.sqrt(2.0 / math.pi) * (x + 0.044715 * (x * x * x))))


def _sigmoid(x):
    return 1.0 / (1.0 + jnp.exp(-x))


def _softplus(x):
    return jnp.maximum(x, 0.0) + jnp.log(1.0 + jnp.exp(-jnp.abs(x)))


def _shift_rows(x, halo, k):
    xe = jnp.concatenate([halo, x], axis=0)
    n = x.shape[0]
    return xe[SUBLANES - k:SUBLANES - k + n, :]


def _inproj_kernel(x_ref, g_ref, w_ref, o_ref, h_scr):
    @pl.when(pl.program_id(1) == 0)
    def _():
        x = x_ref[...]
        ms = jnp.mean(x * x, axis=-1, keepdims=True)
        h_scr[...] = (x * lax.rsqrt(ms + EPS) * g_ref[...]).astype(BF16)

    o_ref[...] = jnp.dot(h_scr[...], w_ref[...], preferred_element_type=F32)


def _inproj(x2, g, w):
    n, d = x2.shape
    dn = w.shape[1]
    tm = min(512, n)
    tn = dn // 2
    return pl.pallas_call(
        _inproj_kernel,
        grid=(n // tm, dn // tn),
        in_specs=[
            pl.BlockSpec((tm, d), lambda i, j: (i, 0)),
            pl.BlockSpec((1, d), lambda i, j: (0, 0)),
            pl.BlockSpec((d, tn), lambda i, j: (0, j)),
        ],
        out_specs=pl.BlockSpec((tm, tn), lambda i, j: (i, j)),
        out_shape=jax.ShapeDtypeStruct((n, dn), F32),
        scratch_shapes=[pltpu.VMEM((tm, d), BF16)],
        compiler_params=_cparams(("parallel", "arbitrary")),
        name="inproj",
    )(x2, g, w)


def _lru_kernel(x_ref, xh_ref, g_ref, cw_ref, cb_ref, gw_ref, gb_ref, lam_ref, o_ref, carry):
    t = pl.program_id(1)
    tm, c = x_ref.shape

    @pl.when(t == 0)
    def _():
        carry[...] = jnp.zeros_like(carry)

    x = x_ref[...]
    halo = jnp.where(t == 0, 0.0, xh_ref[...])
    cw = cw_ref[...]
    xc = cb_ref[...] + cw[CONV_W - 1:CONV_W, :] * x
    for k in range(1, CONV_W):
        xc = xc + cw[CONV_W - 1 - k:CONV_W - k, :] * _shift_rows(x, halo, k)

    gates = jnp.dot(xc.astype(BF16), gw_ref[...], preferred_element_type=F32) + gb_ref[...]
    r = _sigmoid(gates[:, :c])
    i = _sigmoid(gates[:, c:])
    log_a = (-LRU_C) * r * _softplus(-lam_ref[...])
    a = jnp.exp(log_a)
    u = jnp.sqrt(1.0 - jnp.exp(2.0 * log_a)) * (i * xc)

    row = lax.broadcasted_iota(jnp.int32, (tm, c), 0)
    s = 1
    while s < tm:
        a_sh = jnp.where(row >= s, pltpu.roll(a, s, axis=0), 1.0)
        u_sh = jnp.where(row >= s, pltpu.roll(u, s, axis=0), 0.0)
        u = u + a * u_sh
        a = a * a_sh
        s *= 2
    h = u + a * carry[SUBLANES - 1:SUBLANES, :]
    carry[...] = h[tm - SUBLANES:, :]
    o_ref[...] = h * _gelu_tanh(g_ref[...])


def _lru_branch(z, conv_w, conv_b, gate_w_bd, gate_b, lam):
    b, t, _ = z.shape
    tm = min(256, t)
    cb = COL_LRU_X // C
    hb = tm // SUBLANES
    return pl.pallas_call(
        _lru_kernel,
        grid=(b, t // tm),
        in_specs=[
            pl.BlockSpec((None, tm, C), lambda i, j: (i, j, cb)),
            pl.BlockSpec((None, SUBLANES, C), lambda i, j: (i, jnp.maximum(j * hb - 1, 0), cb)),
            pl.BlockSpec((None, tm, C), lambda i, j: (i, j, cb + 1)),
            pl.BlockSpec((CONV_W, C), lambda i, j: (0, 0)),
            pl.BlockSpec((1, C), lambda i, j: (0, 0)),
            pl.BlockSpec((C, 2 * C), lambda i, j: (0, 0)),
            pl.BlockSpec((1, 2 * C), lambda i, j: (0, 0)),
            pl.BlockSpec((1, C), lambda i, j: (0, 0)),
        ],
        out_specs=pl.BlockSpec((None, tm, C), lambda i, j: (i, j, 0)),
        out_shape=jax.ShapeDtypeStruct((b, t, C), F32),
        scratch_shapes=[pltpu.VMEM((SUBLANES, C), F32)],
        compiler_params=_cparams(("parallel", "arbitrary")),
        name="rglru",
    )(z, z, z, conv_w, conv_b, gate_w_bd, gate_b, lam)


S5_L = 16
S5_OCT = LANES // S5_GROUP


def _s5_kernel(u_ref, win_ref, wintra_ref, wout_ref, lstep_ref, lrow_ref, y_ref, uf, carry):
    t = pl.program_id(2)
    tm = u_ref.shape[0]
    nc = tm // S5_L
    ns = carry.shape[1]
    half = ns // 2

    @pl.when(t == 0)
    def _():
        carry[...] = jnp.zeros_like(carry)

    for i in range(S5_L):
        uf[:, i * LANES:(i + 1) * LANES] = u_ref[pl.ds(i, nc, stride=S5_L), :].astype(BF16)
    ufv = uf[...]
    s = jnp.dot(ufv, win_ref[...], preferred_element_type=F32)

    def cmul(v, ar, ai):
        sw = jnp.concatenate([v[:, half:], v[:, :half]], axis=1)
        return v * ar + sw * ai

    row = lax.broadcasted_iota(jnp.int32, (nc, ns), 0)
    k = 0
    st = 1
    while st < nc:
        sh = jnp.where(row >= st, pltpu.roll(s, st, axis=0), 0.0)
        s = s + cmul(sh, lstep_ref[2 * k:2 * k + 1, :], lstep_ref[2 * k + 1:2 * k + 2, :])
        k += 1
        st *= 2
    cin = carry[0:1, :]
    s = s + cmul(jnp.broadcast_to(cin, (nc, ns)), lrow_ref[0], lrow_ref[1])
    carry[...] = jnp.broadcast_to(s[nc - 1:nc, :], carry.shape)
    sprev = jnp.where(row >= 1, pltpu.roll(s, 1, axis=0), cin)

    yf = jnp.dot(ufv, wintra_ref[...], preferred_element_type=F32)
    yf = yf + jnp.dot(sprev.astype(BF16), wout_ref[...], preferred_element_type=F32)
    for j in range(S5_L):
        y_ref[pl.ds(j, nc, stride=S5_L), :] = yf[:, j * LANES:(j + 1) * LANES]


def _s5_tail_kernel(y_ref, u_ref, d_ref, w_ref, o_ref):
    c = u_ref.shape[1]
    y = _gelu_tanh(y_ref[...] + d_ref[...] * u_ref[...])
    z = jnp.dot(y.astype(BF16), w_ref[...], preferred_element_type=F32)
    o_ref[...] = z[:, :c] * _sigmoid(z[:, c:])


def _s5_tile(t):
    return min(2048, t)


def _s5_branch(z, tabs, d_skip, w_glu):
    win, wintra, wout, lstep, lrow = tabs
    b, t, _ = z.shape
    tm = _s5_tile(t)
    nc = tm // S5_L
    noct = C // LANES
    ns = win.shape[-1]
    fl = S5_L * LANES
    cb = COL_S5_U // LANES
    y = pl.pallas_call(
        _s5_kernel,
        grid=(noct, b, t // tm),
        in_specs=[
            pl.BlockSpec((None, tm, LANES), lambda q, i, j: (i, j, cb + q)),
            pl.BlockSpec((None, fl, ns), lambda q, i, j: (q, 0, 0)),
            pl.BlockSpec((None, fl, fl), lambda q, i, j: (q, 0, 0)),
            pl.BlockSpec((None, ns, fl), lambda q, i, j: (q, 0, 0)),
            pl.BlockSpec((None, lstep.shape[1], ns), lambda q, i, j: (q, 0, 0)),
            pl.BlockSpec((None, 2, nc, ns), lambda q, i, j: (q, 0, 0, 0)),
        ],
        out_specs=pl.BlockSpec((None, tm, LANES), lambda q, i, j: (i, j, q)),
        out_shape=jax.ShapeDtypeStruct((b, t, C), F32),
        scratch_shapes=[pltpu.VMEM((nc, fl), BF16), pltpu.VMEM((SUBLANES, ns), F32)],
        compiler_params=_cparams(("arbitrary", "arbitrary", "arbitrary")),
        name="s5_ssm",
    )(z, win, wintra, wout, lstep, lrow)

    tt = min(512, t)
    return pl.pallas_call(
        _s5_tail_kernel,
        grid=(b, t // tt),
        in_specs=[
            pl.BlockSpec((None, tt, C), lambda i, j: (i, j, 0)),
            pl.BlockSpec((None, tt, C), lambda i, j: (i, j, COL_S5_U // C)),
            pl.BlockSpec((1, C), lambda i, j: (0, 0)),
            pl.BlockSpec((C, 2 * C), lambda i, j: (0, 0)),
        ],
        out_specs=pl.BlockSpec((None, tt, C), lambda i, j: (i, j, 0)),
        out_shape=jax.ShapeDtypeStruct((b, t, C), F32),
        compiler_params=_cparams(("parallel", "parallel")),
        name="s5_glu",
    )(y, z, d_skip, w_glu)


def _prep_s5(a_re, a_im, b_re, b_im, c_re, c_im, log_dt, t):
    nl, g, p = a_re.shape
    m = S5_GROUP
    noct = g // S5_OCT
    nc = _s5_tile(t) // S5_L
    lam = lax.complex(a_re.astype(F32), a_im.astype(F32))
    dt = jnp.exp(log_dt.astype(F32))[..., None]
    ldt = lam * dt
    lam_bar = jnp.exp(ldt)
    b_bar = ((lam_bar - 1.0) / lam)[..., None] * lax.complex(b_re.astype(F32), b_im.astype(F32))
    cc = lax.complex(c_re.astype(F32), c_im.astype(F32))

    def powers(e):
        return jnp.exp(ldt[..., None] * e.astype(F32).reshape((1, 1, 1, -1))).reshape(ldt.shape + e.shape)

    steps = jnp.arange(S5_L)
    eye = jnp.eye(S5_OCT, dtype=F32)

    kd = jnp.einsum("lgop,lgpd,lgpi->lgdoi", cc, powers(steps), b_bar).real
    dmat = steps[None, :] - steps[:, None]
    ktoe = jnp.where((dmat >= 0)[None, None, :, :, None, None],
                     kd[:, :, jnp.clip(dmat, 0, S5_L - 1)], 0.0)
    ktoe = ktoe.reshape(nl, noct, S5_OCT, S5_L, S5_L, m, m)
    wintra = jnp.einsum("lqgijoa,gh->lqigajho", ktoe, eye).reshape(nl, noct, S5_L * LANES, S5_L * LANES)

    wi = powers(S5_L - 1 - steps)[..., None] * b_bar[:, :, :, None, :]
    wi = jnp.stack([wi.real, wi.imag], axis=2)
    wi = wi.reshape(nl, noct, S5_OCT, 2, p, S5_L, m)
    win = jnp.einsum("lqgrpia,gh->lqigarhp", wi, eye).reshape(nl, noct, S5_L * LANES, 2 * S5_OCT * p)

    wo = cc[:, :, :, :, None] * powers(steps + 1)[:, :, None, :, :]
    wo = jnp.stack([wo.real, -wo.imag], axis=2)
    wo = wo.reshape(nl, noct, S5_OCT, 2, m, p, S5_L)
    wout = jnp.einsum("lqgropj,gh->lqrgpjho", wo, eye).reshape(nl, noct, 2 * S5_OCT * p, S5_L * LANES)

    def state_row(pw):
        n = pw.shape[-1]
        pw = jnp.transpose(pw, (0, 3, 1, 2)).reshape(nl, n, noct, S5_OCT * p)
        pw = jnp.moveaxis(pw, 2, 1)
        ar = jnp.concatenate([pw.real, pw.real], axis=-1)
        ai = jnp.concatenate([-pw.imag, pw.imag], axis=-1)
        return jnp.stack([ar, ai], axis=3)

    nsteps = max(int(math.log2(nc)), 1)
    lstep = state_row(powers(S5_L * (2 ** jnp.arange(nsteps))))
    lstep = lstep.reshape(nl, noct, 2 * nsteps, -1)
    lrow = state_row(powers(S5_L * (jnp.arange(nc) + 1)))
    lrow = jnp.moveaxis(lrow, 3, 2)
    return win.astype(BF16), wintra.astype(BF16), wout.astype(BF16), lstep, lrow


RW_L = 64
HIGHEST = lax.Precision.HIGHEST


def _mm(a, b, dims=(((1,), (0,)), ((), ()))):
    return lax.dot_general(a.astype(BF16), b.astype(BF16), dims, preferred_element_type=F32)


def _mm_nt(a, b):
    return _mm(a, b, (((1,), (1,)), ((), ())))


def _mm_tn(a, b):
    return _mm(a, b, (((0,), (0,)), ((), ())))


def _mm_exact(a, b):
    return lax.dot_general(a, b, (((1,), (0,)), ((), ())), precision=HIGHEST, preferred_element_type=F32)


def _head_sum(x, m0):
    s0 = jnp.sum(jnp.where(m0, x, 0.0), axis=-1, keepdims=True)
    s1 = jnp.sum(jnp.where(m0, 0.0, x), axis=-1, keepdims=True)
    return jnp.where(m0, s0, s1)


def _rwkv_kernel(r_ref, k_ref, v_ref, lo_ref, rh_ref, kh_ref, vh_ref, loh_ref, mu_ref, mul_ref, w0_ref, a0_ref,
                 w2_ref, a2_ref, g2_ref, kk_ref, ka_ref, rk_ref, lng_ref, lnb_ref, o_ref, hstate):
    t = pl.program_id(1)
    tm, c = r_ref.shape
    npair = c // LANES
    ll = RW_L

    @pl.when(t == 0)
    def _():
        hstate[...] = jnp.zeros_like(hstate)

    def mix(ref, href, mu):
        p = ref[...]
        prev = _shift_rows(p, jnp.where(t == 0, 0.0, href[...]), 1)
        return p + (prev - p) * mu

    mu = mu_ref[...]
    r = mix(r_ref, rh_ref, mu[0:1, :])
    k = mix(k_ref, kh_ref, mu[1:2, :])
    v = mix(v_ref, vh_ref, mu[2:3, :])
    lo = mix(lo_ref, loh_ref, mul_ref[...])
    wlog = -_softplus(-(w0_ref[...] + _mm(jnp.tanh(lo), w2_ref[...]))) - 0.5
    lw = -jnp.exp(wlog)
    a = _sigmoid(a0_ref[...] + _mm(lo, a2_ref[...]))
    g = _mm(_sigmoid(lo), g2_ref[...])
    kk = k * kk_ref[...]
    k = k * (1.0 + (a - 1.0) * ka_ref[...])

    lane = lax.broadcasted_iota(jnp.int32, (ll, LANES), 1)
    m0 = lane < RWKV_HD
    ri = lax.broadcasted_iota(jnp.int32, (2 * ll, 2 * ll), 0)
    ci = lax.broadcasted_iota(jnp.int32, (2 * ll, 2 * ll), 1)
    same = (ri >= ll) == (ci >= ll)
    strict = same & (ri > ci)
    incl = same & (ri >= ci)
    eye = ri == ci
    tri = (lax.broadcasted_iota(jnp.int32, (ll, ll), 0) >= lax.broadcasted_iota(jnp.int32, (ll, ll), 1)).astype(F32)

    def st(x):
        return jnp.concatenate([jnp.where(m0, x, 0.0), jnp.where(m0, 0.0, x)], axis=0)

    for ch in range(tm // ll):
        rows = slice(ch * ll, (ch + 1) * ll)
        cum_all = _mm_exact(tri, lw[rows, :])
        for p in range(npair):
            cols = slice(p * LANES, (p + 1) * LANES)
            rp, kp, vp, ap, lwp, cum = r[rows, cols], k[rows, cols], v[rows, cols], a[rows, cols], lw[rows, cols], cum_all[:, cols]
            kkp = kk[rows, cols]
            kkp = kkp * lax.rsqrt(_head_sum(kkp * kkp, m0) + 1e-12)
            cum_l = cum[ll - 1:ll, :]
            e_neg = jnp.exp(-cum)
            e_last = jnp.exp(cum_l - cum)
            kb = kkp * ap
            a_st = st(-kkp * jnp.exp(cum - lwp))
            r_st = st(rp * jnp.exp(cum))
            v_st = st(vp)
            gram = _mm_nt(jnp.concatenate([a_st, r_st], axis=0),
                          jnp.concatenate([st(kb * e_neg), st(kp * e_neg)], axis=0))
            a_ab = jnp.where(strict, gram[:2 * ll, :2 * ll], 0.0)
            a_ak = jnp.where(strict, gram[:2 * ll, 2 * ll:], 0.0)
            m_rb = jnp.where(incl, gram[2 * ll:, :2 * ll], 0.0)
            m_rk = jnp.where(incl, gram[2 * ll:, 2 * ll:], 0.0)
            tinv = jnp.where(eye, 1.0, a_ab)
            pw = a_ab
            for _ in range(int(math.log2(ll)) - 1):
                pw = _mm(pw, pw)
                tinv = tinv + _mm(tinv, pw)
            xa = _mm(tinv, jnp.concatenate([a_st, _mm(a_ak, v_st)], axis=1))
            mb = _mm(m_rb, xa)
            r_eff = r_st + mb[:, :LANES]
            y_loc = mb[:, LANES:] + _mm(m_rk, v_st)
            bt = _mm_tn(st(kb * e_last), xa)
            g_eff = jnp.where(eye, jnp.exp(cum_l), 0.0) + bt[:, :LANES]
            f_loc = bt[:, LANES:] + _mm_tn(st(kp * e_last), v_st)
            h0 = hstate[p]
            hs = _mm(jnp.concatenate([g_eff, r_eff], axis=0), h0)
            hstate[p] = hs[:2 * ll, :] + f_loc
            y_st = hs[2 * ll:, :] + y_loc
            y = y_st[:ll, :] + y_st[ll:, :]
            mean = _head_sum(y, m0) * (1.0 / RWKV_HD)
            d = y - mean
            var = _head_sum(d * d, m0) * (1.0 / RWKV_HD)
            yn = d * lax.rsqrt(var + RWKV_LN_EPS) * lng_ref[:, cols] + lnb_ref[:, cols]
            bonus = _head_sum(rp * kp * rk_ref[:, cols], m0) * vp
            o_ref[rows, cols] = (yn + bonus) * g[rows, cols]


def _rwkv_branch(z, prm):
    (mu, mul, w0, a0, w2, a2, g2, kk, ka, rk, lng, lnb) = prm
    b, t, _ = z.shape
    tm = min(2 * RW_L, t)
    hb = tm // SUBLANES
    cr = COL_RW_R // C
    cl = COL_LORA // 256
    npair = C // LANES

    def tile(cb, w):
        return pl.BlockSpec((None, tm, w), lambda i, j: (i, j, cb))

    def halo(cb, w):
        return pl.BlockSpec((None, SUBLANES, w), lambda i, j: (i, jnp.maximum(j * hb - 1, 0), cb))

    def full(arr):
        return pl.BlockSpec(arr.shape, lambda i, j: (0,) * arr.ndim)

    return pl.pallas_call(
        _rwkv_kernel,
        grid=(b, t // tm),
        in_specs=[tile(cr, C), tile(cr + 1, C), tile(cr + 2, C), tile(cl, 256),
                  halo(cr, C), halo(cr + 1, C), halo(cr + 2, C), halo(cl, 256)] + [full(a) for a in prm],
        out_specs=pl.BlockSpec((None, tm, C), lambda i, j: (i, j, 0)),
        out_shape=jax.ShapeDtypeStruct((b, t, C), F32),
        scratch_shapes=[pltpu.VMEM((npair, LANES, LANES), F32)],
        compiler_params=_cparams(("parallel", "arbitrary")),
        name="rwkv7",
    )(z, z, z, z, z, z, z, z, *prm)


def _prep_rwkv(mu_rkv, mu_w, mu_a, mu_g, w0, w2, a0, a2, g2, k_k, k_a, r_k, lnx_g, lnx_b):
    nl = mu_rkv.shape[0]
    row = lambda x: x.reshape(nl, 1, -1).astype(F32)
    mul = jnp.concatenate([mu_w, mu_a, mu_g], axis=-1)
    zw = lambda n: jnp.zeros((nl, n, C), F32)
    w2p = jnp.concatenate([w2, zw(A_LORA + G_LORA)], axis=1).astype(BF16)
    a2p = jnp.concatenate([zw(W_LORA), a2, zw(G_LORA)], axis=1).astype(BF16)
    g2p = jnp.concatenate([zw(W_LORA + A_LORA), g2], axis=1).astype(BF16)
    return (mu_rkv.astype(F32), row(mul), row(w0), row(a0), w2p, a2p, g2p, row(k_k), row(k_a), row(r_k),
            row(lnx_g), row(lnx_b))


SLOT = LANES
ROT_HALF = QK_ROPE // 2


def _mla_prep_kernel(cq_ref, ckv_ref, kr_ref, rc_ref, rs1_ref, rs2_ref, qn_ref, wq_ref, kvn_ref, wk_ref, wv_ref,
                     gq_ref, gk_ref, q_ref, k_ref, v_ref):
    def rms(x, g, n):
        ms = jnp.sum(x * x, axis=-1, keepdims=True) * (1.0 / n)
        return x * lax.rsqrt(ms + EPS) * g

    def rope(x):
        return (x * rc_ref[...] + pltpu.roll(x, SLOT - ROT_HALF, axis=1) * rs1_ref[...]
                + pltpu.roll(x, ROT_HALF, axis=1) * rs2_ref[...])

    cq = rms(cq_ref[...], qn_ref[...], Q_LORA)
    ckv = rms(ckv_ref[...], kvn_ref[...], KV_LORA)
    qf = _mm(cq, wq_ref[...])
    kf = _mm(ckv, wk_ref[...])
    v_ref[...] = _mm(ckv, wv_ref[...]).astype(BF16)
    kr = kr_ref[...]
    scale = QK_HD ** -0.5
    for h in range(MLA_HEADS):
        cols = slice(h * SLOT, (h + 1) * SLOT)
        q = rope(rms(qf[:, cols], gq_ref[...], QK_HD)) * scale
        k = rope(rms(kf[:, cols] + kr, gk_ref[...], QK_HD))
        q_ref[:, cols] = q.astype(BF16)
        k_ref[:, cols] = k.astype(BF16)


def _flash_kernel(q_ref, k_ref, v_ref, o_ref, m_scr, l_scr, acc):
    i = pl.program_id(1)
    j = pl.program_id(2)
    bq = q_ref.shape[0]
    bk = k_ref.shape[0]

    @pl.when(j == 0)
    def _():
        m_scr[...] = jnp.full_like(m_scr, -jnp.inf)
        l_scr[...] = jnp.zeros_like(l_scr)
        acc[...] = jnp.zeros_like(acc)

    @pl.when(j <= i)
    def _():
        row = i * bq + lax.broadcasted_iota(jnp.int32, (bq, bk), 0)
        col = j * bk + lax.broadcasted_iota(jnp.int32, (bq, bk), 1)
        causal = col <= row
        m0 = lax.broadcasted_iota(jnp.int32, (bq, LANES), 1) < V_HD
        for pr in range(MLA_HEADS // 2):
            vp = v_ref[:, pr * LANES:(pr + 1) * LANES]
            alphas, pvs = [], []
            for hh in range(2):
                h = 2 * pr + hh
                cols = slice(h * SLOT, (h + 1) * SLOT)
                s = _mm_nt(q_ref[:, cols], k_ref[:, cols])
                s = jnp.where(causal, s, -jnp.inf)
                m_prev = m_scr[h]
                m_new = jnp.maximum(m_prev, jnp.max(s, axis=-1, keepdims=True))
                alpha = jnp.exp(m_prev - m_new)
                pexp = jnp.exp(s - m_new)
                l_scr[h] = alpha * l_scr[h] + jnp.sum(pexp, axis=-1, keepdims=True)
                m_scr[h] = m_new
                alphas.append(alpha)
                pvs.append(_mm(pexp, vp))
            pc = slice(pr * LANES, (pr + 1) * LANES)
            acc[:, pc] = acc[:, pc] * jnp.where(m0, alphas[0], alphas[1]) + jnp.where(m0, pvs[0], pvs[1])

    @pl.when(j == i)
    def _():
        m0 = lax.broadcasted_iota(jnp.int32, (bq, LANES), 1) < V_HD
        for pr in range(MLA_HEADS // 2):
            pc = slice(pr * LANES, (pr + 1) * LANES)
            o_ref[:, pc] = acc[:, pc] / jnp.where(m0, l_scr[2 * pr], l_scr[2 * pr + 1])


def _mla_branch(z, prm, rope):
    qn, wq, kvn, wk, wv, gq, gk = prm
    rc, rs1, rs2 = rope
    b, t, _ = z.shape
    tm = min(512, t)
    hs = MLA_HEADS * SLOT

    def full(arr):
        return pl.BlockSpec(arr.shape, lambda i, j: (0,) * arr.ndim)

    def tab():
        return pl.BlockSpec((tm, SLOT), lambda i, j: (j, 0))

    q, k, v = pl.pallas_call(
        _mla_prep_kernel,
        grid=(b, t // tm),
        in_specs=[pl.BlockSpec((None, tm, Q_LORA), lambda i, j: (i, j, COL_CQ // Q_LORA)),
                  pl.BlockSpec((None, tm, KV_LORA), lambda i, j: (i, j, COL_CKV // KV_LORA)),
                  pl.BlockSpec((None, tm, LANES), lambda i, j: (i, j, COL_KR // LANES)),
                  tab(), tab(), tab()] + [full(a) for a in prm],
        out_specs=[pl.BlockSpec((None, tm, hs), lambda i, j: (i, j, 0)),
                   pl.BlockSpec((None, tm, hs), lambda i, j: (i, j, 0)),
                   pl.BlockSpec((None, tm, C), lambda i, j: (i, j, 0))],
        out_shape=[jax.ShapeDtypeStruct((b, t, hs), BF16), jax.ShapeDtypeStruct((b, t, hs), BF16),
                   jax.ShapeDtypeStruct((b, t, C), BF16)],
        compiler_params=_cparams(("parallel", "parallel")),
        name="mla_prep",
    )(z, z, z, rc, rs1, rs2, *prm)

    bq = min(512, t)
    nq = t // bq
    return pl.pallas_call(
        _flash_kernel,
        grid=(b, nq, nq),
        in_specs=[pl.BlockSpec((None, bq, hs), lambda bi, i, j: (bi, i, 0)),
                  pl.BlockSpec((None, bq, hs), lambda bi, i, j: (bi, jnp.minimum(j, i), 0)),
                  pl.BlockSpec((None, bq, C), lambda bi, i, j: (bi, jnp.minimum(j, i), 0))],
        out_specs=pl.BlockSpec((None, bq, C), lambda bi, i, j: (bi, i, 0)),
        out_shape=jax.ShapeDtypeStruct((b, t, C), F32),
        scratch_shapes=[pltpu.VMEM((MLA_HEADS, bq, 1), F32), pltpu.VMEM((MLA_HEADS, bq, 1), F32),
                        pltpu.VMEM((bq, C), F32)],
        compiler_params=_cparams(("parallel", "parallel", "arbitrary")),
        name="mla_attn",
    )(q, k, v)


def _rope_tables(t):
    pos = jnp.arange(t, dtype=F32)
    inv_freq = ROPE_THETA ** (-jnp.arange(0, QK_ROPE, 2, dtype=F32) / QK_ROPE)
    ang = pos[:, None] * inv_freq[None, :]
    cos, sin = jnp.cos(ang), jnp.sin(ang)
    z = lambda n: jnp.zeros((t, n), F32)
    pad = SLOT - QK_HD
    rc = jnp.concatenate([jnp.ones((t, QK_NOPE), F32), cos, cos, z(pad)], axis=1)
    rs1 = jnp.concatenate([z(QK_NOPE), -sin, z(ROT_HALF + pad)], axis=1)
    rs2 = jnp.concatenate([z(QK_NOPE + ROT_HALF), sin, z(pad)], axis=1)
    return rc, rs1, rs2


def _prep_mla(q_norm, w_uq, kv_norm, w_ukv, qk_norm_q, qk_norm_k):
    nl = q_norm.shape[0]
    nh = MLA_HEADS
    pad = SLOT - QK_HD
    wq = w_uq.reshape(nl, Q_LORA, nh, QK_HD)
    wq = jnp.pad(wq, ((0, 0), (0, 0), (0, 0), (0, pad))).reshape(nl, Q_LORA, nh * SLOT).astype(BF16)
    wkv = w_ukv.reshape(nl, KV_LORA, nh, QK_NOPE + V_HD)
    wk = jnp.pad(wkv[..., :QK_NOPE], ((0, 0), (0, 0), (0, 0), (0, SLOT - QK_NOPE)))
    wk = wk.reshape(nl, KV_LORA, nh * SLOT).astype(BF16)
    wv = wkv[..., QK_NOPE:].reshape(nl, KV_LORA, nh * V_HD).astype(BF16)
    slot = lambda g: jnp.pad(g.astype(F32), ((0, 0), (0, pad))).reshape(nl, 1, SLOT)
    return (q_norm.reshape(nl, 1, -1).astype(F32), wq, kv_norm.reshape(nl, 1, -1).astype(F32), wk, wv,
            slot(qk_norm_q), slot(qk_norm_k))


def _merge_kernel(y0_ref, y1_ref, y2_ref, y3_ref, gl_ref, x_ref, wb_ref, wo_ref, o_ref):
    d = x_ref.shape[1]
    merged = None
    for n, y_ref in enumerate((y0_ref, y1_ref, y2_ref, y3_ref)):
        term = _mm(y_ref[...], wb_ref[n]) * _sigmoid(gl_ref[:, n * d:(n + 1) * d])
        merged = term if merged is None else merged + term
    o_ref[...] = x_ref[...] + _mm(merged, wo_ref[...])


def _merge(ys, z2, x2, wb, wo):
    n, d = x2.shape
    tm = min(256, n)
    ytile = pl.BlockSpec((tm, C), lambda i: (i, 0))
    return pl.pallas_call(
        _merge_kernel,
        grid=(n // tm,),
        in_specs=[ytile, ytile, ytile, ytile,
                  pl.BlockSpec((tm, 4 * d), lambda i: (i, COL_GATE)),
                  pl.BlockSpec((tm, d), lambda i: (i, 0)),
                  pl.BlockSpec(wb.shape, lambda i: (0, 0, 0)),
                  pl.BlockSpec(wo.shape, lambda i: (0, 0))],
        out_specs=pl.BlockSpec((tm, d), lambda i: (i, 0)),
        out_shape=jax.ShapeDtypeStruct((n, d), F32),
        compiler_params=_cparams(("parallel",)),
        name="merge",
    )(*ys, z2, x2, wb, wo)


def _mlp_kernel(x_ref, g_ref, w1_ref, w2_ref, o_ref, h_scr, acc):
    j = pl.program_id(1)

    @pl.when(j == 0)
    def _():
        x = x_ref[...]
        ms = jnp.mean(x * x, axis=-1, keepdims=True)
        h_scr[...] = (x * lax.rsqrt(ms + EPS) * g_ref[...]).astype(BF16)
        acc[...] = x

    a = jnp.maximum(jnp.dot(h_scr[...], w1_ref[...], preferred_element_type=F32), 0.0)
    acc[...] += _mm(a * a, w2_ref[...])

    @pl.when(j == pl.num_programs(1) - 1)
    def _():
        o_ref[...] = acc[...]


def _mlp(x2, g, w1, w2):
    n, d = x2.shape
    dff = w1.shape[1]
    tm = min(1024, n)
    tf = 512
    return pl.pallas_call(
        _mlp_kernel,
        grid=(n // tm, dff // tf),
        in_specs=[pl.BlockSpec((tm, d), lambda i, j: (i, 0)),
                  pl.BlockSpec((1, d), lambda i, j: (0, 0)),
                  pl.BlockSpec((d, tf), lambda i, j: (0, j)),
                  pl.BlockSpec((tf, d), lambda i, j: (j, 0))],
        out_specs=pl.BlockSpec((tm, d), lambda i, j: (i, 0)),
        out_shape=jax.ShapeDtypeStruct((n, d), F32),
        scratch_shapes=[pltpu.VMEM((tm, d), BF16), pltpu.VMEM((tm, d), F32)],
        compiler_params=_cparams(("parallel", "arbitrary")),
        name="mlp",
    )(x2, g, w1, w2)


def _prep_w_in(w_in):
    nl, d, _ = w_in.shape
    gate0 = 6 * C + W_LORA + A_LORA + G_LORA + Q_LORA + KV_LORA + QK_ROPE
    kr0 = gate0 - QK_ROPE
    parts = [
        w_in[:, :, gate0:],
        w_in[:, :, :kr0],
        jnp.zeros((nl, d, QK_NOPE), w_in.dtype),
        w_in[:, :, kr0:gate0],
        jnp.zeros((nl, d, LANES - QK_NOPE - QK_ROPE), w_in.dtype),
    ]
    return jnp.concatenate(parts, axis=-1).astype(BF16)


def _prep_lru_gate(gate_w):
    nl, _, nh, hd, _ = gate_w.shape
    eye = jnp.eye(nh, dtype=gate_w.dtype)
    dense = jnp.einsum("lghij,hk->lghikj", gate_w, eye).reshape(nl, 2, nh * hd, nh * hd)
    return jnp.concatenate([dense[:, 0], dense[:, 1]], axis=-1).astype(BF16)


def kernel(x, norm_mix, w_in, lru_conv_w, lru_conv_b, lru_gate_w, lru_gate_b, lru_lambda, s5_a_re, s5_a_im, s5_b_re, s5_b_im, s5_c_re, s5_c_im, s5_d, s5_log_dt, s5_w_glu, rwkv_mu_rkv, rwkv_mu_w, rwkv_mu_a, rwkv_mu_g, rwkv_w0, rwkv_w2, rwkv_a0, rwkv_a2, rwkv_g2, rwkv_k_k, rwkv_k_a, rwkv_r_k, rwkv_lnx_g, rwkv_lnx_b, mla_q_norm, mla_w_uq, mla_kv_norm, mla_w_ukv, mla_qk_norm_q, mla_qk_norm_k, w_branch, w_out, norm_mlp, w_ff1, w_ff2):
    b, t, d = x.shape
    depth = w_in.shape[0]
    n = b * t
    row = lambda a: a.reshape(depth, 1, -1).astype(F32)

    w_in_p = _prep_w_in(w_in)
    lru_gw = _prep_lru_gate(lru_gate_w)
    s5_tabs = _prep_s5(s5_a_re, s5_a_im, s5_b_re, s5_b_im, s5_c_re, s5_c_im, s5_log_dt, t)
    rw_prm = _prep_rwkv(rwkv_mu_rkv, rwkv_mu_w, rwkv_mu_a, rwkv_mu_g, rwkv_w0, rwkv_w2, rwkv_a0, rwkv_a2, rwkv_g2,
                        rwkv_k_k, rwkv_k_a, rwkv_r_k, rwkv_lnx_g, rwkv_lnx_b)
    mla_prm = _prep_mla(mla_q_norm, mla_w_uq, mla_kv_norm, mla_w_ukv, mla_qk_norm_q, mla_qk_norm_k)
    rope = _rope_tables(t)
    g_mix, g_mlp = row(norm_mix), row(norm_mlp)
    conv_b, gate_b, lam, s5_dr = row(lru_conv_b), row(lru_gate_b), row(lru_lambda), row(s5_d)
    w_glu, wb, wo = s5_w_glu.astype(BF16), w_branch.astype(BF16), w_out.astype(BF16)
    w1, w2 = w_ff1.astype(BF16), w_ff2.astype(BF16)

    x2 = x.reshape(n, d)
    for l in range(depth):
        z2 = _inproj(x2, g_mix[l], w_in_p[l])
        z = z2.reshape(b, t, -1)
        y_lru = _lru_branch(z, lru_conv_w[l], conv_b[l], lru_gw[l], gate_b[l], lam[l])
        y_s5 = _s5_branch(z, [tb[l] for tb in s5_tabs], s5_dr[l], w_glu[l])
        y_rw = _rwkv_branch(z, [a[l] for a in rw_prm])
        y_mla = _mla_branch(z, [a[l] for a in mla_prm], rope)
        ys = [y.reshape(n, C) for y in (y_lru, y_s5, y_rw, y_mla)]
        x2 = _merge(ys, z2, x2, wb[l], wo[l])
        x2 = _mlp(x2, g_mlp[l], w1[l], w2[l])
    return x2.reshape(b, t, d)
```

```python
import functools
import math

import jax
import jax.numpy as jnp
import numpy as np
from jax import lax
from jax.experimental import pallas as pl
from jax.experimental.pallas import tpu as pltpu

F32 = jnp.float32
BF16 = jnp.bfloat16

EPS = 1e-6
LRU_HEADS = 8
CONV_W = 4
LRU_C = 8.0
S5_GROUP = 16
S5_STATE = 64
RWKV_HD = 64
W_LORA, A_LORA, G_LORA = 64, 64, 128
RWKV_LN_EPS = 64e-5
MLA_HEADS = 8
QK_NOPE, QK_ROPE, V_HD = 64, 32, 64
QK_HD = QK_NOPE + QK_ROPE
Q_LORA, KV_LORA = 256, 128
ROPE_THETA = 10000.0

LANES = 128
SUBLANES = 8
VMEM_LIMIT = 56 * 1024 * 1024

D_MODEL = 1024
C = D_MODEL // 2
COL_GATE = 0
COL_LRU_X = 4 * D_MODEL
COL_LRU_G = COL_LRU_X + C
COL_S5_U = COL_LRU_G + C
COL_RW_R = COL_S5_U + C
COL_RW_K = COL_RW_R + C
COL_RW_V = COL_RW_K + C
COL_LORA = COL_RW_V + C
COL_CQ = COL_LORA + 256
COL_CKV = COL_CQ + Q_LORA
COL_KR = COL_CKV + KV_LORA
D_INP = COL_KR + LANES


def _cparams(sem):
    return pltpu.CompilerParams(dimension_semantics=sem, vmem_limit_bytes=VMEM_LIMIT)


def _gelu_tanh(x):
    return 0.5 * x * (1.0 + jnp.tanh(math.sqrt(2.0 / math.pi) * (x + 0.044715 * (x * x * x))))


def _sigmoid(x):
    return 1.0 / (1.0 + jnp.exp(-x))


def _softplus(x):
    return jnp.maximum(x, 0.0) + jnp.log(1.0 + jnp.exp(-jnp.abs(x)))


def _shift_rows(x, halo, k):
    xe = jnp.concatenate([halo, x], axis=0)
    n = x.shape[0]
    return xe[SUBLANES - k:SUBLANES - k + n, :]


def _inproj_kernel(x_ref, g_ref, w_ref, o_ref, h_scr):
    @pl.when(pl.program_id(1) == 0)
    def _():
        x = x_ref[...]
        ms = jnp.mean(x * x, axis=-1, keepdims=True)
        h_scr[...] = (x * lax.rsqrt(ms + EPS) * g_ref[...]).astype(BF16)

    o_ref[...] = jnp.dot(h_scr[...], w_ref[...], preferred_element_type=F32)


def _inproj(x2, g, w):
    n, d = x2.shape
    dn = w.shape[1]
    tm = min(512, n)
    tn = dn // 2
    return pl.pallas_call(
        _inproj_kernel,
        grid=(n // tm, dn // tn),
        in_specs=[
            pl.BlockSpec((tm, d), lambda i, j: (i, 0)),
            pl.BlockSpec((1, d), lambda i, j: (0, 0)),
            pl.BlockSpec((d, tn), lambda i, j: (0, j)),
        ],
        out_specs=pl.BlockSpec((tm, tn), lambda i, j: (i, j)),
        out_shape=jax.ShapeDtypeStruct((n, dn), F32),
        scratch_shapes=[pltpu.VMEM((tm, d), BF16)],
        compiler_params=_cparams(("parallel", "arbitrary")),
        name="inproj",
    )(x2, g, w)


def _lru_kernel(x_ref, xh_ref, g_ref, cw_ref, cb_ref, gw_ref, gb_ref, lam_ref, o_ref, carry):
    t = pl.program_id(1)
    tm, c = x_ref.shape

    @pl.when(t == 0)
    def _():
        carry[...] = jnp.zeros_like(carry)

    x = x_ref[...]
    halo = jnp.where(t == 0, 0.0, xh_ref[...])
    cw = cw_ref[...]
    xc = cb_ref[...] + cw[CONV_W - 1:CONV_W, :] * x
    for k in range(1, CONV_W):
        xc = xc + cw[CONV_W - 1 - k:CONV_W - k, :] * _shift_rows(x, halo, k)

    gates = jnp.dot(xc.astype(BF16), gw_ref[...], preferred_element_type=F32) + gb_ref[...]
    r = _sigmoid(gates[:, :c])
    i = _sigmoid(gates[:, c:])
    log_a = (-LRU_C) * r * _softplus(-lam_ref[...])
    a = jnp.exp(log_a)
    u = jnp.sqrt(1.0 - jnp.exp(2.0 * log_a)) * (i * xc)

    row = lax.broadcasted_iota(jnp.int32, (tm, c), 0)
    s = 1
    while s < tm:
        a_sh = jnp.where(row >= s, pltpu.roll(a, s, axis=0), 1.0)
        u_sh = jnp.where(row >= s, pltpu.roll(u, s, axis=0), 0.0)
        u = u + a * u_sh
        a = a * a_sh
        s *= 2
    h = u + a * carry[SUBLANES - 1:SUBLANES, :]
    carry[...] = h[tm - SUBLANES:, :]
    o_ref[...] = h * _gelu_tanh(g_ref[...])


def _lru_branch(z, conv_w, conv_b, gate_w_bd, gate_b, lam):
    b, t, _ = z.shape
    tm = min(256, t)
    cb = COL_LRU_X // C
    hb = tm // SUBLANES
    return pl.pallas_call(
        _lru_kernel,
        grid=(b, t // tm),
        in_specs=[
            pl.BlockSpec((None, tm, C), lambda i, j: (i, j, cb)),
            pl.BlockSpec((None, SUBLANES, C), lambda i, j: (i, jnp.maximum(j * hb - 1, 0), cb)),
            pl.BlockSpec((None, tm, C), lambda i, j: (i, j, cb + 1)),
            pl.BlockSpec((CONV_W, C), lambda i, j: (0, 0)),
            pl.BlockSpec((1, C), lambda i, j: (0, 0)),
            pl.BlockSpec((C, 2 * C), lambda i, j: (0, 0)),
            pl.BlockSpec((1, 2 * C), lambda i, j: (0, 0)),
            pl.BlockSpec((1, C), lambda i, j: (0, 0)),
        ],
        out_specs=pl.BlockSpec((None, tm, C), lambda i, j: (i, j, 0)),
        out_shape=jax.ShapeDtypeStruct((b, t, C), F32),
        scratch_shapes=[pltpu.VMEM((SUBLANES, C), F32)],
        compiler_params=_cparams(("parallel", "arbitrary")),
        name="rglru",
    )(z, z, z, conv_w, conv_b, gate_w_bd, gate_b, lam)


S5_L = 8
S5_OCT = LANES // S5_GROUP


def _s5_kernel(u_ref, win_ref, wintra_ref, wout_ref, lstep_ref, y_ref, uf, carry):
    t = pl.program_id(2)
    tm = u_ref.shape[0]
    nc = tm // S5_L
    ns = carry.shape[1]
    half = ns // 2

    @pl.when(t == 0)
    def _():
        carry[...] = jnp.zeros_like(carry)

    for i in range(S5_L):
        uf[:, i * LANES:(i + 1) * LANES] = u_ref[pl.ds(i, nc, stride=S5_L), :].astype(BF16)
    ufv = uf[...]
    s = jnp.dot(ufv, win_ref[...], preferred_element_type=F32)

    def cmul(v, k):
        sw = jnp.concatenate([v[:, half:], v[:, :half]], axis=1)
        return v * lstep_ref[2 * k:2 * k + 1, :] + sw * lstep_ref[2 * k + 1:2 * k + 2, :]

    row = lax.broadcasted_iota(jnp.int32, (nc, ns), 0)
    cin = carry[0:1, :]
    s = s + jnp.where(row == 0, cmul(cin, 0), 0.0)
    k = 0
    st = 1
    while st < nc:
        sh = jnp.where(row >= st, pltpu.roll(s, st, axis=0), 0.0)
        s = s + cmul(sh, k)
        k += 1
        st *= 2
    carry[...] = jnp.broadcast_to(s[nc - 1:nc, :], carry.shape)
    sprev = jnp.where(row >= 1, pltpu.roll(s, 1, axis=0), cin)

    yf = jnp.dot(ufv, wintra_ref[...], preferred_element_type=F32)
    yf = yf + jnp.dot(sprev.astype(BF16), wout_ref[...], preferred_element_type=F32)
    for j in range(S5_L):
        y_ref[pl.ds(j, nc, stride=S5_L), :] = yf[:, j * LANES:(j + 1) * LANES]


def _s5_tail_kernel(y_ref, u_ref, d_ref, w_ref, o_ref):
    c = u_ref.shape[1]
    y = _gelu_tanh(y_ref[...] + d_ref[...] * u_ref[...])
    z = jnp.dot(y.astype(BF16), w_ref[...], preferred_element_type=F32)
    o_ref[...] = z[:, :c] * _sigmoid(z[:, c:])


def _s5_tile(t):
    return min(2048, t)


def _s5_branch(z, tabs, d_skip, w_glu):
    win, wintra, wout, lstep = tabs
    b, t, _ = z.shape
    tm = _s5_tile(t)
    nc = tm // S5_L
    noct = C // LANES
    ns = win.shape[-1]
    fl = S5_L * LANES
    cb = COL_S5_U // LANES
    y = pl.pallas_call(
        _s5_kernel,
        grid=(noct, b, t // tm),
        in_specs=[
            pl.BlockSpec((None, tm, LANES), lambda q, i, j: (i, j, cb + q)),
            pl.BlockSpec((None, fl, ns), lambda q, i, j: (q, 0, 0)),
            pl.BlockSpec((None, fl, fl), lambda q, i, j: (q, 0, 0)),
            pl.BlockSpec((None, ns, fl), lambda q, i, j: (q, 0, 0)),
            pl.BlockSpec((None, lstep.shape[1], ns), lambda q, i, j: (q, 0, 0)),
        ],
        out_specs=pl.BlockSpec((None, tm, LANES), lambda q, i, j: (i, j, q)),
        out_shape=jax.ShapeDtypeStruct((b, t, C), F32),
        scratch_shapes=[pltpu.VMEM((nc, fl), BF16), pltpu.VMEM((SUBLANES, ns), F32)],
        compiler_params=_cparams(("arbitrary", "arbitrary", "arbitrary")),
        name="s5_ssm",
    )(z, win, wintra, wout, lstep)

    tt = min(512, t)
    return pl.pallas_call(
        _s5_tail_kernel,
        grid=(b, t // tt),
        in_specs=[
            pl.BlockSpec((None, tt, C), lambda i, j: (i, j, 0)),
            pl.BlockSpec((None, tt, C), lambda i, j: (i, j, COL_S5_U // C)),
            pl.BlockSpec((1, C), lambda i, j: (0, 0)),
            pl.BlockSpec((C, 2 * C), lambda i, j: (0, 0)),
        ],
        out_specs=pl.BlockSpec((None, tt, C), lambda i, j: (i, j, 0)),
        out_shape=jax.ShapeDtypeStruct((b, t, C), F32),
        compiler_params=_cparams(("parallel", "parallel")),
        name="s5_glu",
    )(y, z, d_skip, w_glu)


def _prep_s5(a_re, a_im, b_re, b_im, c_re, c_im, log_dt, t):
    nl, g, p = a_re.shape
    m = S5_GROUP
    noct = g // S5_OCT
    nc = _s5_tile(t) // S5_L
    lam = lax.complex(a_re.astype(F32), a_im.astype(F32))
    dt = jnp.exp(log_dt.astype(F32))[..., None]
    ldt = lam * dt
    lam_bar = jnp.exp(ldt)
    b_bar = ((lam_bar - 1.0) / lam)[..., None] * lax.complex(b_re.astype(F32), b_im.astype(F32))
    cc = lax.complex(c_re.astype(F32), c_im.astype(F32))

    def powers(e):
        return jnp.exp(ldt[..., None] * e.astype(F32).reshape((1, 1, 1, -1)))

    steps = jnp.arange(S5_L)
    same = jnp.eye(S5_OCT, dtype=bool)

    def octet_bd(x, rows, cols):
        nr, ncol = len(rows[1]), len(cols[1])
        r_pre, r_post = rows
        c_pre, c_post = cols
        x = jnp.moveaxis(x, 2, 2 + len(r_pre))
        pos = 2 + len(r_pre) + 1 + len(r_post) + len(c_pre)
        x = jnp.expand_dims(x, pos)
        mshape = [1] * x.ndim
        mshape[2 + len(r_pre)] = S5_OCT
        mshape[pos] = S5_OCT
        x = jnp.where(same.reshape(mshape), x, 0.0)
        nrow = int(np.prod([x.shape[i] for i in range(2, 2 + len(r_pre) + 1 + len(r_post))]))
        return x.astype(BF16).reshape(nl, noct, nrow, -1)

    kd = jnp.einsum("lgop,lgpd,lgpi->lgdoi", cc, powers(steps), b_bar).real
    dmat = steps[None, :] - steps[:, None]
    ktoe = jnp.where((dmat >= 0)[None, None, :, :, None, None],
                     kd[:, :, jnp.clip(dmat, 0, S5_L - 1)], 0.0)
    ktoe = jnp.transpose(ktoe, (0, 1, 2, 5, 3, 4)).reshape(nl, noct, S5_OCT, S5_L, m, S5_L, m)
    wintra = octet_bd(ktoe, ((0,), (1,)), ((0,), (1,)))

    wi = powers(S5_L - 1 - steps)[..., None] * b_bar[:, :, :, None, :]
    wi = jnp.stack([wi.real, wi.imag], axis=2)
    wi = jnp.transpose(wi, (0, 1, 4, 5, 2, 3)).reshape(nl, noct, S5_OCT, S5_L, m, 2, p)
    win = octet_bd(wi, ((0,), (1,)), ((0,), (1,)))

    wo = cc[:, :, :, :, None] * powers(steps + 1)[:, :, None, :, :]
    wo = jnp.stack([wo.real, -wo.imag], axis=2)
    wo = jnp.transpose(wo, (0, 1, 2, 4, 5, 3)).reshape(nl, noct, S5_OCT, 2, p, S5_L, m)
    wout = octet_bd(wo, ((0,), (1,)), ((0,), (1,)))

    nsteps = max(int(math.log2(nc)), 1)
    pw = powers(S5_L * (2 ** jnp.arange(nsteps)))
    pw = jnp.transpose(pw, (0, 3, 1, 2)).reshape(nl, nsteps, noct, S5_OCT * p)
    pw = jnp.moveaxis(pw, 2, 1)
    ar = jnp.concatenate([pw.real, pw.real], axis=-1)
    ai = jnp.concatenate([-pw.imag, pw.imag], axis=-1)
    lstep = jnp.stack([ar, ai], axis=3).reshape(nl, noct, 2 * nsteps, -1)
    return win, wintra, wout, lstep


RW_L = 64
HIGHEST = lax.Precision.HIGHEST


def _mm(a, b, dims=(((1,), (0,)), ((), ()))):
    return lax.dot_general(a.astype(BF16), b.astype(BF16), dims, preferred_element_type=F32)


def _mm_nt(a, b):
    return _mm(a, b, (((1,), (1,)), ((), ())))


def _mm_tn(a, b):
    return _mm(a, b, (((0,), (0,)), ((), ())))


def _mm_exact(a, b):
    return lax.dot_general(a, b, (((1,), (0,)), ((), ())), precision=HIGHEST, preferred_element_type=F32)


def _head_sum(x, m0):
    s0 = jnp.sum(jnp.where(m0, x, 0.0), axis=-1, keepdims=True)
    s1 = jnp.sum(jnp.where(m0, 0.0, x), axis=-1, keepdims=True)
    return jnp.where(m0, s0, s1)


def _rwkv_kernel(r_ref, k_ref, v_ref, lo_ref, rh_ref, kh_ref, vh_ref, loh_ref, mu_ref, mul_ref, w0_ref, a0_ref,
                 w2_ref, a2_ref, g2_ref, kk_ref, ka_ref, rk_ref, lng_ref, lnb_ref, o_ref, hstate):
    t = pl.program_id(1)
    tm, c = r_ref.shape
    npair = c // LANES
    ll = RW_L

    @pl.when(t == 0)
    def _():
        hstate[...] = jnp.zeros_like(hstate)

    def mix(ref, href, mu):
        p = ref[...]
        prev = _shift_rows(p, jnp.where(t == 0, 0.0, href[...]), 1)
        return p + (prev - p) * mu

    mu = mu_ref[...]
    r = mix(r_ref, rh_ref, mu[0:1, :])
    k = mix(k_ref, kh_ref, mu[1:2, :])
    v = mix(v_ref, vh_ref, mu[2:3, :])
    lo = mix(lo_ref, loh_ref, mul_ref[...])
    wlog = -_softplus(-(w0_ref[...] + _mm(jnp.tanh(lo), w2_ref[...]))) - 0.5
    lw = -jnp.exp(wlog)
    a = _sigmoid(a0_ref[...] + _mm(lo, a2_ref[...]))
    g = _mm(_sigmoid(lo), g2_ref[...])
    kk = k * kk_ref[...]
    k = k * (1.0 + (a - 1.0) * ka_ref[...])

    lane = lax.broadcasted_iota(jnp.int32, (ll, LANES), 1)
    m0 = lane < RWKV_HD
    ri = lax.broadcasted_iota(jnp.int32, (2 * ll, 2 * ll), 0)
    ci = lax.broadcasted_iota(jnp.int32, (2 * ll, 2 * ll), 1)
    same = (ri >= ll) == (ci >= ll)
    strict = same & (ri > ci)
    incl = same & (ri >= ci)
    eye = ri == ci
    tri = (lax.broadcasted_iota(jnp.int32, (ll, ll), 0) >= lax.broadcasted_iota(jnp.int32, (ll, ll), 1)).astype(F32)

    def st(x):
        return jnp.concatenate([jnp.where(m0, x, 0.0), jnp.where(m0, 0.0, x)], axis=0)

    for ch in range(tm // ll):
        rows = slice(ch * ll, (ch + 1) * ll)
        cum_all = _mm_exact(tri, lw[rows, :])
        for p in range(npair):
            cols = slice(p * LANES, (p + 1) * LANES)
            rp, kp, vp, ap, lwp, cum = r[rows, cols], k[rows, cols], v[rows, cols], a[rows, cols], lw[rows, cols], cum_all[:, cols]
            kkp = kk[rows, cols]
            kkp = kkp * lax.rsqrt(_head_sum(kkp * kkp, m0) + 1e-12)
            cum_l = cum[ll - 1:ll, :]
            e_neg = jnp.exp(-cum)
            e_last = jnp.exp(cum_l - cum)
            kb = kkp * ap
            a_st = st(-kkp * jnp.exp(cum - lwp))
            r_st = st(rp * jnp.exp(cum))
            v_st = st(vp)
            gram = _mm_nt(jnp.concatenate([a_st, r_st], axis=0),
                          jnp.concatenate([st(kb * e_neg), st(kp * e_neg)], axis=0))
            a_ab = jnp.where(strict, gram[:2 * ll, :2 * ll], 0.0)
            a_ak = jnp.where(strict, gram[:2 * ll, 2 * ll:], 0.0)
            m_rb = jnp.where(incl, gram[2 * ll:, :2 * ll], 0.0)
            m_rk = jnp.where(incl, gram[2 * ll:, 2 * ll:], 0.0)
            tinv = jnp.where(eye, 1.0, a_ab)
            pw = a_ab
            for _ in range(int(math.log2(ll)) - 1):
                pw = _mm(pw, pw)
                tinv = tinv + _mm(tinv, pw)
            xa = _mm(tinv, jnp.concatenate([a_st, _mm(a_ak, v_st)], axis=1))
            mb = _mm(m_rb, xa)
            r_eff = r_st + mb[:, :LANES]
            y_loc = mb[:, LANES:] + _mm(m_rk, v_st)
            bt = _mm_tn(st(kb * e_last), xa)
            g_eff = jnp.where(eye, jnp.exp(cum_l), 0.0) + bt[:, :LANES]
            f_loc = bt[:, LANES:] + _mm_tn(st(kp * e_last), v_st)
            h0 = hstate[p]
            hs = _mm(jnp.concatenate([g_eff, r_eff], axis=0), h0)
            hstate[p] = hs[:2 * ll, :] + f_loc
            y_st = hs[2 * ll:, :] + y_loc
            y = y_st[:ll, :] + y_st[ll:, :]
            mean = _head_sum(y, m0) * (1.0 / RWKV_HD)
            d = y - mean
            var = _head_sum(d * d, m0) * (1.0 / RWKV_HD)
            yn = d * lax.rsqrt(var + RWKV_LN_EPS) * lng_ref[:, cols] + lnb_ref[:, cols]
            bonus = _head_sum(rp * kp * rk_ref[:, cols], m0) * vp
            o_ref[rows, cols] = (yn + bonus) * g[rows, cols]


def _rwkv_branch(z, prm):
    (mu, mul, w0, a0, w2, a2, g2, kk, ka, rk, lng, lnb) = prm
    b, t, _ = z.shape
    tm = min(2 * RW_L, t)
    hb = tm // SUBLANES
    cr = COL_RW_R // C
    cl = COL_LORA // 256
    npair = C // LANES

    def tile(cb, w):
        return pl.BlockSpec((None, tm, w), lambda i, j: (i, j, cb))

    def halo(cb, w):
        return pl.BlockSpec((None, SUBLANES, w), lambda i, j: (i, jnp.maximum(j * hb - 1, 0), cb))

    def full(arr):
        return pl.BlockSpec(arr.shape, lambda i, j: (0,) * arr.ndim)

    return pl.pallas_call(
        _rwkv_kernel,
        grid=(b, t // tm),
        in_specs=[tile(cr, C), tile(cr + 1, C), tile(cr + 2, C), tile(cl, 256),
                  halo(cr, C), halo(cr + 1, C), halo(cr + 2, C), halo(cl, 256)] + [full(a) for a in prm],
        out_specs=pl.BlockSpec((None, tm, C), lambda i, j: (i, j, 0)),
        out_shape=jax.ShapeDtypeStruct((b, t, C), F32),
        scratch_shapes=[pltpu.VMEM((npair, LANES, LANES), F32)],
        compiler_params=_cparams(("parallel", "arbitrary")),
        name="rwkv7",
    )(z, z, z, z, z, z, z, z, *prm)


def _prep_rwkv(mu_rkv, mu_w, mu_a, mu_g, w0, w2, a0, a2, g2, k_k, k_a, r_k, lnx_g, lnx_b):
    nl = mu_rkv.shape[0]
    row = lambda x: x.reshape(nl, 1, -1).astype(F32)
    mul = jnp.concatenate([mu_w, mu_a, mu_g], axis=-1)
    zw = lambda n: jnp.zeros((nl, n, C), F32)
    w2p = jnp.concatenate([w2, zw(A_LORA + G_LORA)], axis=1).astype(BF16)
    a2p = jnp.concatenate([zw(W_LORA), a2, zw(G_LORA)], axis=1).astype(BF16)
    g2p = jnp.concatenate([zw(W_LORA + A_LORA), g2], axis=1).astype(BF16)
    return (mu_rkv.astype(F32), row(mul), row(w0), row(a0), w2p, a2p, g2p, row(k_k), row(k_a), row(r_k),
            row(lnx_g), row(lnx_b))


SLOT = LANES
ROT_HALF = QK_ROPE // 2


def _mla_prep_kernel(cq_ref, ckv_ref, kr_ref, rc_ref, rs1_ref, rs2_ref, qn_ref, wq_ref, kvn_ref, wk_ref, wv_ref,
                     gq_ref, gk_ref, q_ref, k_ref, v_ref, v_scr):
    def rms(x, g, n):
        ms = jnp.sum(x * x, axis=-1, keepdims=True) * (1.0 / n)
        return x * lax.rsqrt(ms + EPS) * g

    def rope(x):
        return (x * rc_ref[...] + pltpu.roll(x, SLOT - ROT_HALF, axis=1) * rs1_ref[...]
                + pltpu.roll(x, ROT_HALF, axis=1) * rs2_ref[...])

    cq = rms(cq_ref[...], qn_ref[...], Q_LORA)
    ckv = rms(ckv_ref[...], kvn_ref[...], KV_LORA)
    qf = _mm(cq, wq_ref[...])
    kf = _mm(ckv, wk_ref[...])
    v_scr[...] = _mm(ckv, wv_ref[...])
    v_ref[...] = jnp.transpose(v_scr[...]).astype(BF16)
    kr = kr_ref[...]
    scale = QK_HD ** -0.5
    for h in range(MLA_HEADS):
        cols = slice(h * SLOT, (h + 1) * SLOT)
        q = rope(rms(qf[:, cols], gq_ref[...], QK_HD)) * scale
        k = rope(rms(kf[:, cols] + kr, gk_ref[...], QK_HD))
        q_ref[:, cols] = q.astype(BF16)
        k_ref[:, cols] = k.astype(BF16)


def _flash_kernel(q_ref, k_ref, vt_ref, o_ref, m_scr, l_scr, acc):
    i = pl.program_id(1)
    j = pl.program_id(2)
    bq = q_ref.shape[0]
    bk = k_ref.shape[0]

    @pl.when(j == 0)
    def _():
        m_scr[...] = jnp.full_like(m_scr, -jnp.inf)
        l_scr[...] = jnp.zeros_like(l_scr)
        acc[...] = jnp.zeros_like(acc)

    def step(masked):
        if masked:
            causal = (lax.broadcasted_iota(jnp.int32, (bk, bq), 0) <= lax.broadcasted_iota(jnp.int32, (bk, bq), 1))
        for h in range(MLA_HEADS):
            cols = slice(h * SLOT, (h + 1) * SLOT)
            rows = slice(h * V_HD, (h + 1) * V_HD)
            st = _mm_nt(k_ref[:, cols], q_ref[:, cols])
            if masked:
                st = jnp.where(causal, st, -jnp.inf)
            m_prev = m_scr[h:h + 1, :]
            m_new = jnp.maximum(m_prev, jnp.max(st, axis=0, keepdims=True))
            alpha = jnp.exp(m_prev - m_new)
            pt = jnp.exp(st - m_new)
            l_scr[h:h + 1, :] = alpha * l_scr[h:h + 1, :] + jnp.sum(pt, axis=0, keepdims=True)
            m_scr[h:h + 1, :] = m_new
            acc[rows, :] = acc[rows, :] * alpha + _mm(vt_ref[rows, :], pt)

    @pl.when(j < i)
    def _():
        step(False)

    @pl.when(j == i)
    def _():
        step(True)
        for h in range(MLA_HEADS):
            rows = slice(h * V_HD, (h + 1) * V_HD)
            acc[rows, :] = acc[rows, :] / l_scr[h:h + 1, :]
        o_ref[...] = jnp.transpose(acc[...])


def _mla_branch(z, prm, rope):
    qn, wq, kvn, wk, wv, gq, gk = prm
    rc, rs1, rs2 = rope
    b, t, _ = z.shape
    tm = min(512, t)
    hs = MLA_HEADS * SLOT

    def full(arr):
        return pl.BlockSpec(arr.shape, lambda i, j: (0,) * arr.ndim)

    def tab():
        return pl.BlockSpec((tm, SLOT), lambda i, j: (j, 0))

    q, k, vt = pl.pallas_call(
        _mla_prep_kernel,
        grid=(b, t // tm),
        in_specs=[pl.BlockSpec((None, tm, Q_LORA), lambda i, j: (i, j, COL_CQ // Q_LORA)),
                  pl.BlockSpec((None, tm, KV_LORA), lambda i, j: (i, j, COL_CKV // KV_LORA)),
                  pl.BlockSpec((None, tm, LANES), lambda i, j: (i, j, COL_KR // LANES)),
                  tab(), tab(), tab()] + [full(a) for a in prm],
        out_specs=[pl.BlockSpec((None, tm, hs), lambda i, j: (i, j, 0)),
                   pl.BlockSpec((None, tm, hs), lambda i, j: (i, j, 0)),
                   pl.BlockSpec((None, C, tm), lambda i, j: (i, 0, j))],
        out_shape=[jax.ShapeDtypeStruct((b, t, hs), BF16), jax.ShapeDtypeStruct((b, t, hs), BF16),
                   jax.ShapeDtypeStruct((b, C, t), BF16)],
        scratch_shapes=[pltpu.VMEM((tm, C), F32)],
        compiler_params=_cparams(("parallel", "parallel")),
        name="mla_prep",
    )(z, z, z, rc, rs1, rs2, *prm)

    bq = min(512, t)
    nq = t // bq
    return pl.pallas_call(
        _flash_kernel,
        grid=(b, nq, nq),
        in_specs=[pl.BlockSpec((None, bq, hs), lambda bi, i, j: (bi, i, 0)),
                  pl.BlockSpec((None, bq, hs), lambda bi, i, j: (bi, jnp.minimum(j, i), 0)),
                  pl.BlockSpec((None, C, bq), lambda bi, i, j: (bi, 0, jnp.minimum(j, i)))],
        out_specs=pl.BlockSpec((None, bq, C), lambda bi, i, j: (bi, i, 0)),
        out_shape=jax.ShapeDtypeStruct((b, t, C), F32),
        scratch_shapes=[pltpu.VMEM((MLA_HEADS, bq), F32), pltpu.VMEM((MLA_HEADS, bq), F32),
                        pltpu.VMEM((C, bq), F32)],
        compiler_params=_cparams(("parallel", "parallel", "arbitrary")),
        name="mla_attn",
    )(q, k, vt)


def _rope_tables(t):
    pos = jnp.arange(t, dtype=F32)
    inv_freq = ROPE_THETA ** (-jnp.arange(0, QK_ROPE, 2, dtype=F32) / QK_ROPE)
    ang = pos[:, None] * inv_freq[None, :]
    cos, sin = jnp.cos(ang), jnp.sin(ang)
    z = lambda n: jnp.zeros((t, n), F32)
    pad = SLOT - QK_HD
    rc = jnp.concatenate([jnp.ones((t, QK_NOPE), F32), cos, cos, z(pad)], axis=1)
    rs1 = jnp.concatenate([z(QK_NOPE), -sin, z(ROT_HALF + pad)], axis=1)
    rs2 = jnp.concatenate([z(QK_NOPE + ROT_HALF), sin, z(pad)], axis=1)
    return rc, rs1, rs2


def _prep_mla(q_norm, w_uq, kv_norm, w_ukv, qk_norm_q, qk_norm_k):
    nl = q_norm.shape[0]
    nh = MLA_HEADS
    pad = SLOT - QK_HD
    wq = w_uq.reshape(nl, Q_LORA, nh, QK_HD)
    wq = jnp.pad(wq, ((0, 0), (0, 0), (0, 0), (0, pad))).reshape(nl, Q_LORA, nh * SLOT).astype(BF16)
    wkv = w_ukv.reshape(nl, KV_LORA, nh, QK_NOPE + V_HD)
    wk = jnp.pad(wkv[..., :QK_NOPE], ((0, 0), (0, 0), (0, 0), (0, SLOT - QK_NOPE)))
    wk = wk.reshape(nl, KV_LORA, nh * SLOT).astype(BF16)
    wv = wkv[..., QK_NOPE:].reshape(nl, KV_LORA, nh * V_HD).astype(BF16)
    slot = lambda g: jnp.pad(g.astype(F32), ((0, 0), (0, pad))).reshape(nl, 1, SLOT)
    return (q_norm.reshape(nl, 1, -1).astype(F32), wq, kv_norm.reshape(nl, 1, -1).astype(F32), wk, wv,
            slot(qk_norm_q), slot(qk_norm_k))


def _merge_kernel(y0_ref, y1_ref, y2_ref, y3_ref, gl_ref, x_ref, wb_ref, wo_ref, o_ref):
    d = x_ref.shape[1]
    merged = None
    for n, y_ref in enumerate((y0_ref, y1_ref, y2_ref, y3_ref)):
        term = _mm(y_ref[...], wb_ref[n]) * _sigmoid(gl_ref[:, n * d:(n + 1) * d])
        merged = term if merged is None else merged + term
    o_ref[...] = x_ref[...] + _mm(merged, wo_ref[...])


def _merge(ys, z2, x2, wb, wo):
    n, d = x2.shape
    tm = min(256, n)
    ytile = pl.BlockSpec((tm, C), lambda i: (i, 0))
    return pl.pallas_call(
        _merge_kernel,
        grid=(n // tm,),
        in_specs=[ytile, ytile, ytile, ytile,
                  pl.BlockSpec((tm, 4 * d), lambda i: (i, COL_GATE)),
                  pl.BlockSpec((tm, d), lambda i: (i, 0)),
                  pl.BlockSpec(wb.shape, lambda i: (0, 0, 0)),
                  pl.BlockSpec(wo.shape, lambda i: (0, 0))],
        out_specs=pl.BlockSpec((tm, d), lambda i: (i, 0)),
        out_shape=jax.ShapeDtypeStruct((n, d), F32),
        compiler_params=_cparams(("parallel",)),
        name="merge",
    )(*ys, z2, x2, wb, wo)


def _mlp_kernel(x_ref, g_ref, w1_ref, w2_ref, o_ref, h_scr, acc):
    j = pl.program_id(1)

    @pl.when(j == 0)
    def _():
        x = x_ref[...]
        ms = jnp.mean(x * x, axis=-1, keepdims=True)
        h_scr[...] = (x * lax.rsqrt(ms + EPS) * g_ref[...]).astype(BF16)
        acc[...] = x

    a = jnp.maximum(jnp.dot(h_scr[...], w1_ref[...], preferred_element_type=F32), 0.0)
    acc[...] += _mm(a * a, w2_ref[...])

    @pl.when(j == pl.num_programs(1) - 1)
    def _():
        o_ref[...] = acc[...]


def _mlp(x2, g, w1, w2):
    n, d = x2.shape
    dff = w1.shape[1]
    tm = min(1024, n)
    tf = 512
    return pl.pallas_call(
        _mlp_kernel,
        grid=(n // tm, dff // tf),
        in_specs=[pl.BlockSpec((tm, d), lambda i, j: (i, 0)),
                  pl.BlockSpec((1, d), lambda i, j: (0, 0)),
                  pl.BlockSpec((d, tf), lambda i, j: (0, j)),
                  pl.BlockSpec((tf, d), lambda i, j: (j, 0))],
        out_specs=pl.BlockSpec((tm, d), lambda i, j: (i, 0)),
        out_shape=jax.ShapeDtypeStruct((n, d), F32),
        scratch_shapes=[pltpu.VMEM((tm, d), BF16), pltpu.VMEM((tm, d), F32)],
        compiler_params=_cparams(("parallel", "arbitrary")),
        name="mlp",
    )(x2, g, w1, w2)


def _prep_w_in(w_in):
    nl, d, _ = w_in.shape
    gate0 = 6 * C + W_LORA + A_LORA + G_LORA + Q_LORA + KV_LORA + QK_ROPE
    kr0 = gate0 - QK_ROPE
    parts = [
        w_in[:, :, gate0:],
        w_in[:, :, :kr0],
        jnp.zeros((nl, d, QK_NOPE), w_in.dtype),
        w_in[:, :, kr0:gate0],
        jnp.zeros((nl, d, LANES - QK_NOPE - QK_ROPE), w_in.dtype),
    ]
    return jnp.concatenate(parts, axis=-1).astype(BF16)


def _prep_lru_gate(gate_w):
    nl, _, nh, hd, _ = gate_w.shape
    eye = jnp.eye(nh, dtype=gate_w.dtype)
    dense = jnp.einsum("lghij,hk->lghikj", gate_w, eye).reshape(nl, 2, nh * hd, nh * hd)
    return jnp.concatenate([dense[:, 0], dense[:, 1]], axis=-1).astype(BF16)


def kernel(x, norm_mix, w_in, lru_conv_w, lru_conv_b, lru_gate_w, lru_gate_b, lru_lambda, s5_a_re, s5_a_im, s5_b_re, s5_b_im, s5_c_re, s5_c_im, s5_d, s5_log_dt, s5_w_glu, rwkv_mu_rkv, rwkv_mu_w, rwkv_mu_a, rwkv_mu_g, rwkv_w0, rwkv_w2, rwkv_a0, rwkv_a2, rwkv_g2, rwkv_k_k, rwkv_k_a, rwkv_r_k, rwkv_lnx_g, rwkv_lnx_b, mla_q_norm, mla_w_uq, mla_kv_norm, mla_w_ukv, mla_qk_norm_q, mla_qk_norm_k, w_branch, w_out, norm_mlp, w_ff1, w_ff2):
    b, t, d = x.shape
    depth = w_in.shape[0]
    n = b * t
    row = lambda a: a.reshape(depth, 1, -1).astype(F32)

    w_in_p = _prep_w_in(w_in)
    lru_gw = _prep_lru_gate(lru_gate_w)
    s5_tabs = _prep_s5(s5_a_re, s5_a_im, s5_b_re, s5_b_im, s5_c_re, s5_c_im, s5_log_dt, t)
    rw_prm = _prep_rwkv(rwkv_mu_rkv, rwkv_mu_w, rwkv_mu_a, rwkv_mu_g, rwkv_w0, rwkv_w2, rwkv_a0, rwkv_a2, rwkv_g2,
                        rwkv_k_k, rwkv_k_a, rwkv_r_k, rwkv_lnx_g, rwkv_lnx_b)
    mla_prm = _prep_mla(mla_q_norm, mla_w_uq, mla_kv_norm, mla_w_ukv, mla_qk_norm_q, mla_qk_norm_k)
    rope = _rope_tables(t)
    g_mix, g_mlp = row(norm_mix), row(norm_mlp)
    conv_b, gate_b, lam, s5_dr = row(lru_conv_b), row(lru_gate_b), row(lru_lambda), row(s5_d)
    w_glu, wb, wo = s5_w_glu.astype(BF16), w_branch.astype(BF16), w_out.astype(BF16)
    w1, w2 = w_ff1.astype(BF16), w_ff2.astype(BF16)

    x2 = x.reshape(n, d)
    for l in range(depth):
        z2 = _inproj(x2, g_mix[l], w_in_p[l])
        z = z2.reshape(b, t, -1)
        y_lru = _lru_branch(z, lru_conv_w[l], conv_b[l], lru_gw[l], gate_b[l], lam[l])
        y_s5 = _s5_branch(z, [tb[l] for tb in s5_tabs], s5_dr[l], w_glu[l])
        y_rw = _rwkv_branch(z, [a[l] for a in rw_prm])
        y_mla = _mla_branch(z, [a[l] for a in mla_prm], rope)
        ys = [y.reshape(n, C) for y in (y_lru, y_s5, y_rw, y_mla)]
        x2 = _merge(ys, z2, x2, wb[l], wo[l])
        x2 = _mlp(x2, g_mlp[l], w1[l], w2[l])
    return x2.reshape(b, t, d)
```

```python
import functools
import math

import jax
import jax.numpy as jnp
import numpy as np
from jax import lax
from jax.experimental import pallas as pl
from jax.experimental.pallas import tpu as pltpu

F32 = jnp.float32
BF16 = jnp.bfloat16

EPS = 1e-6
LRU_HEADS = 8
CONV_W = 4
LRU_C = 8.0
S5_GROUP = 16
S5_STATE = 64
RWKV_HD = 64
W_LORA, A_LORA, G_LORA = 64, 64, 128
RWKV_LN_EPS = 64e-5
MLA_HEADS = 8
QK_NOPE, QK_ROPE, V_HD = 64, 32, 64
QK_HD = QK_NOPE + QK_ROPE
Q_LORA, KV_LORA = 256, 128
ROPE_THETA = 10000.0

LANES = 128
SUBLANES = 8
VMEM_LIMIT = 56 * 1024 * 1024

D_MODEL = 1024
C = D_MODEL // 2
COL_GATE = 0
COL_LRU_X = 4 * D_MODEL
COL_LRU_G = COL_LRU_X + C
COL_S5_U = COL_LRU_G + C
COL_RW_R = COL_S5_U + C
COL_RW_K = COL_RW_R + C
COL_RW_V = COL_RW_K + C
COL_LORA = COL_RW_V + C
COL_CQ = COL_LORA + 256
COL_CKV = COL_CQ + Q_LORA
COL_KR = COL_CKV + KV_LORA
D_INP = COL_KR + LANES


def _cparams(sem):
    return pltpu.CompilerParams(dimension_semantics=sem, vmem_limit_bytes=VMEM_LIMIT)


def _gelu_tanh(x):
    return 0.5 * x * (1.0 + jnp.tanh(math.sqrt(2.0 / math.pi) * (x + 0.044715 * (x * x * x))))


def _sigmoid(x):
    return 1.0 / (1.0 + jnp.exp(-x))


def _softplus(x):
    return jnp.maximum(x, 0.0) + jnp.log(1.0 + jnp.exp(-jnp.abs(x)))


def _shift_rows(x, halo, k):
    xe = jnp.concatenate([halo, x], axis=0)
    n = x.shape[0]
    return xe[SUBLANES - k:SUBLANES - k + n, :]


def _inproj_kernel(x_ref, g_ref, w_ref, o_ref, h_scr):
    @pl.when(pl.program_id(1) == 0)
    def _():
        x = x_ref[...]
        ms = jnp.mean(x * x, axis=-1, keepdims=True)
        h_scr[...] = (x * lax.rsqrt(ms + EPS) * g_ref[...]).astype(BF16)

    o_ref[...] = jnp.dot(h_scr[...], w_ref[...], preferred_element_type=F32)


def _inproj(x2, g, w):
    n, d = x2.shape
    dn = w.shape[1]
    tm = min(512, n)
    tn = dn // 2
    return pl.pallas_call(
        _inproj_kernel,
        grid=(n // tm, dn // tn),
        in_specs=[
            pl.BlockSpec((tm, d), lambda i, j: (i, 0)),
            pl.BlockSpec((1, d), lambda i, j: (0, 0)),
            pl.BlockSpec((d, tn), lambda i, j: (0, j)),
        ],
        out_specs=pl.BlockSpec((tm, tn), lambda i, j: (i, j)),
        out_shape=jax.ShapeDtypeStruct((n, dn), F32),
        scratch_shapes=[pltpu.VMEM((tm, d), BF16)],
        compiler_params=_cparams(("parallel", "arbitrary")),
        name="inproj",
    )(x2, g, w)


def _lru_kernel(x_ref, xh_ref, g_ref, cw_ref, cb_ref, gw_ref, gb_ref, lam_ref, o_ref, carry):
    t = pl.program_id(1)
    tm, c = x_ref.shape

    @pl.when(t == 0)
    def _():
        carry[...] = jnp.zeros_like(carry)

    x = x_ref[...]
    halo = jnp.where(t == 0, 0.0, xh_ref[...])
    cw = cw_ref[...]
    xc = cb_ref[...] + cw[CONV_W - 1:CONV_W, :] * x
    for k in range(1, CONV_W):
        xc = xc + cw[CONV_W - 1 - k:CONV_W - k, :] * _shift_rows(x, halo, k)

    gates = jnp.dot(xc.astype(BF16), gw_ref[...], preferred_element_type=F32) + gb_ref[...]
    r = _sigmoid(gates[:, :c])
    i = _sigmoid(gates[:, c:])
    log_a = (-LRU_C) * r * _softplus(-lam_ref[...])
    a = jnp.exp(log_a)
    u = jnp.sqrt(1.0 - jnp.exp(2.0 * log_a)) * (i * xc)

    row = lax.broadcasted_iota(jnp.int32, (tm, c), 0)
    s = 1
    while s < tm:
        a_sh = jnp.where(row >= s, pltpu.roll(a, s, axis=0), 1.0)
        u_sh = jnp.where(row >= s, pltpu.roll(u, s, axis=0), 0.0)
        u = u + a * u_sh
        a = a * a_sh
        s *= 2
    h = u + a * carry[SUBLANES - 1:SUBLANES, :]
    carry[...] = h[tm - SUBLANES:, :]
    o_ref[...] = h * _gelu_tanh(g_ref[...])


def _lru_branch(z, conv_w, conv_b, gate_w_bd, gate_b, lam):
    b, t, _ = z.shape
    tm = min(256, t)
    cb = COL_LRU_X // C
    hb = tm // SUBLANES
    return pl.pallas_call(
        _lru_kernel,
        grid=(b, t // tm),
        in_specs=[
            pl.BlockSpec((None, tm, C), lambda i, j: (i, j, cb)),
            pl.BlockSpec((None, SUBLANES, C), lambda i, j: (i, jnp.maximum(j * hb - 1, 0), cb)),
            pl.BlockSpec((None, tm, C), lambda i, j: (i, j, cb + 1)),
            pl.BlockSpec((CONV_W, C), lambda i, j: (0, 0)),
            pl.BlockSpec((1, C), lambda i, j: (0, 0)),
            pl.BlockSpec((C, 2 * C), lambda i, j: (0, 0)),
            pl.BlockSpec((1, 2 * C), lambda i, j: (0, 0)),
            pl.BlockSpec((1, C), lambda i, j: (0, 0)),
        ],
        out_specs=pl.BlockSpec((None, tm, C), lambda i, j: (i, j, 0)),
        out_shape=jax.ShapeDtypeStruct((b, t, C), F32),
        scratch_shapes=[pltpu.VMEM((SUBLANES, C), F32)],
        compiler_params=_cparams(("parallel", "arbitrary")),
        name="rglru",
    )(z, z, z, conv_w, conv_b, gate_w_bd, gate_b, lam)


S5_L = 8
S5_OCT = LANES // S5_GROUP


def _s5_kernel(u_ref, win_ref, wintra_ref, wout_ref, lstep_ref, y_ref, uf, carry):
    t = pl.program_id(2)
    tm = u_ref.shape[0]
    nc = tm // S5_L
    ns = carry.shape[1]
    half = ns // 2

    @pl.when(t == 0)
    def _():
        carry[...] = jnp.zeros_like(carry)

    for i in range(S5_L):
        uf[:, i * LANES:(i + 1) * LANES] = u_ref[pl.ds(i, nc, stride=S5_L), :].astype(BF16)
    ufv = uf[...]
    s = jnp.dot(ufv, win_ref[...], preferred_element_type=F32)

    def cmul(v, k):
        sw = jnp.concatenate([v[:, half:], v[:, :half]], axis=1)
        return v * lstep_ref[2 * k:2 * k + 1, :] + sw * lstep_ref[2 * k + 1:2 * k + 2, :]

    row = lax.broadcasted_iota(jnp.int32, (nc, ns), 0)
    cin = carry[0:1, :]
    s = s + jnp.where(row == 0, cmul(cin, 0), 0.0)
    k = 0
    st = 1
    while st < nc:
        sh = jnp.where(row >= st, pltpu.roll(s, st, axis=0), 0.0)
        s = s + cmul(sh, k)
        k += 1
        st *= 2
    carry[...] = jnp.broadcast_to(s[nc - 1:nc, :], carry.shape)
    sprev = jnp.where(row >= 1, pltpu.roll(s, 1, axis=0), cin)

    yf = jnp.dot(ufv, wintra_ref[...], preferred_element_type=F32)
    yf = yf + jnp.dot(sprev.astype(BF16), wout_ref[...], preferred_element_type=F32)
    for j in range(S5_L):
        y_ref[pl.ds(j, nc, stride=S5_L), :] = yf[:, j * LANES:(j + 1) * LANES]


def _s5_tail_kernel(y_ref, u_ref, d_ref, w_ref, o_ref):
    c = u_ref.shape[1]
    y = _gelu_tanh(y_ref[...] + d_ref[...] * u_ref[...])
    z = jnp.dot(y.astype(BF16), w_ref[...], preferred_element_type=F32)
    o_ref[...] = z[:, :c] * _sigmoid(z[:, c:])


def _s5_tile(t):
    return min(2048, t)


def _s5_branch(z, tabs, d_skip, w_glu):
    win, wintra, wout, lstep = tabs
    b, t, _ = z.shape
    tm = _s5_tile(t)
    nc = tm // S5_L
    noct = C // LANES
    ns = win.shape[-1]
    fl = S5_L * LANES
    cb = COL_S5_U // LANES
    y = pl.pallas_call(
        _s5_kernel,
        grid=(noct, b, t // tm),
        in_specs=[
            pl.BlockSpec((None, tm, LANES), lambda q, i, j: (i, j, cb + q)),
            pl.BlockSpec((None, fl, ns), lambda q, i, j: (q, 0, 0)),
            pl.BlockSpec((None, fl, fl), lambda q, i, j: (q, 0, 0)),
            pl.BlockSpec((None, ns, fl), lambda q, i, j: (q, 0, 0)),
            pl.BlockSpec((None, lstep.shape[1], ns), lambda q, i, j: (q, 0, 0)),
        ],
        out_specs=pl.BlockSpec((None, tm, LANES), lambda q, i, j: (i, j, q)),
        out_shape=jax.ShapeDtypeStruct((b, t, C), F32),
        scratch_shapes=[pltpu.VMEM((nc, fl), BF16), pltpu.VMEM((SUBLANES, ns), F32)],
        compiler_params=_cparams(("arbitrary", "arbitrary", "arbitrary")),
        name="s5_ssm",
    )(z, win, wintra, wout, lstep)

    tt = min(512, t)
    return pl.pallas_call(
        _s5_tail_kernel,
        grid=(b, t // tt),
        in_specs=[
            pl.BlockSpec((None, tt, C), lambda i, j: (i, j, 0)),
            pl.BlockSpec((None, tt, C), lambda i, j: (i, j, COL_S5_U // C)),
            pl.BlockSpec((1, C), lambda i, j: (0, 0)),
            pl.BlockSpec((C, 2 * C), lambda i, j: (0, 0)),
        ],
        out_specs=pl.BlockSpec((None, tt, C), lambda i, j: (i, j, 0)),
        out_shape=jax.ShapeDtypeStruct((b, t, C), F32),
        compiler_params=_cparams(("parallel", "parallel")),
        name="s5_glu",
    )(y, z, d_skip, w_glu)


def _prep_s5(a_re, a_im, b_re, b_im, c_re, c_im, log_dt, t):
    nl, g, p = a_re.shape
    m = S5_GROUP
    noct = g // S5_OCT
    nc = _s5_tile(t) // S5_L
    lam = lax.complex(a_re.astype(F32), a_im.astype(F32))
    dt = jnp.exp(log_dt.astype(F32))[..., None]
    ldt = lam * dt
    lam_bar = jnp.exp(ldt)
    b_bar = ((lam_bar - 1.0) / lam)[..., None] * lax.complex(b_re.astype(F32), b_im.astype(F32))
    cc = lax.complex(c_re.astype(F32), c_im.astype(F32))

    def powers(e):
        return jnp.exp(ldt[..., None] * e.astype(F32).reshape((1, 1, 1, -1)))

    steps = jnp.arange(S5_L)
    same = jnp.eye(S5_OCT, dtype=bool)

    def octet_bd(x, rows, cols):
        nr, ncol = len(rows[1]), len(cols[1])
        r_pre, r_post = rows
        c_pre, c_post = cols
        x = jnp.moveaxis(x, 2, 2 + len(r_pre))
        pos = 2 + len(r_pre) + 1 + len(r_post) + len(c_pre)
        x = jnp.expand_dims(x, pos)
        mshape = [1] * x.ndim
        mshape[2 + len(r_pre)] = S5_OCT
        mshape[pos] = S5_OCT
        x = jnp.where(same.reshape(mshape), x, 0.0)
        nrow = int(np.prod([x.shape[i] for i in range(2, 2 + len(r_pre) + 1 + len(r_post))]))
        return x.astype(BF16).reshape(nl, noct, nrow, -1)

    kd = jnp.einsum("lgop,lgpd,lgpi->lgdoi", cc, powers(steps), b_bar).real
    dmat = steps[None, :] - steps[:, None]
    ktoe = jnp.where((dmat >= 0)[None, None, :, :, None, None],
                     kd[:, :, jnp.clip(dmat, 0, S5_L - 1)], 0.0)
    ktoe = jnp.transpose(ktoe, (0, 1, 2, 5, 3, 4)).reshape(nl, noct, S5_OCT, S5_L, m, S5_L, m)
    wintra = octet_bd(ktoe, ((0,), (1,)), ((0,), (1,)))

    wi = powers(S5_L - 1 - steps)[..., None] * b_bar[:, :, :, None, :]
    wi = jnp.stack([wi.real, wi.imag], axis=2)
    wi = jnp.transpose(wi, (0, 1, 4, 5, 2, 3)).reshape(nl, noct, S5_OCT, S5_L, m, 2, p)
    win = octet_bd(wi, ((0,), (1,)), ((0,), (1,)))

    wo = cc[:, :, :, :, None] * powers(steps + 1)[:, :, None, :, :]
    wo = jnp.stack([wo.real, -wo.imag], axis=2)
    wo = jnp.transpose(wo, (0, 1, 2, 4, 5, 3)).reshape(nl, noct, S5_OCT, 2, p, S5_L, m)
    wout = octet_bd(wo, ((0,), (1,)), ((0,), (1,)))

    nsteps = max(int(math.log2(nc)), 1)
    pw = powers(S5_L * (2 ** jnp.arange(nsteps)))
    pw = jnp.transpose(pw, (0, 3, 1, 2)).reshape(nl, nsteps, noct, S5_OCT * p)
    pw = jnp.moveaxis(pw, 2, 1)
    ar = jnp.concatenate([pw.real, pw.real], axis=-1)
    ai = jnp.concatenate([-pw.imag, pw.imag], axis=-1)
    lstep = jnp.stack([ar, ai], axis=3).reshape(nl, noct, 2 * nsteps, -1)
    return win, wintra, wout, lstep


RW_L = 64
HIGHEST = lax.Precision.HIGHEST


def _mm(a, b, dims=(((1,), (0,)), ((), ()))):
    return lax.dot_general(a.astype(BF16), b.astype(BF16), dims, preferred_element_type=F32)


def _mm_nt(a, b):
    return _mm(a, b, (((1,), (1,)), ((), ())))


def _mm_tn(a, b):
    return _mm(a, b, (((0,), (0,)), ((), ())))


def _mm_exact(a, b):
    return lax.dot_general(a, b, (((1,), (0,)), ((), ())), precision=HIGHEST, preferred_element_type=F32)


def _head_sum(x, m0):
    s0 = jnp.sum(jnp.where(m0, x, 0.0), axis=-1, keepdims=True)
    s1 = jnp.sum(jnp.where(m0, 0.0, x), axis=-1, keepdims=True)
    return jnp.where(m0, s0, s1)


def _rwkv_kernel(r_ref, k_ref, v_ref, lo_ref, rh_ref, kh_ref, vh_ref, loh_ref, mu_ref, mul_ref, w0_ref, a0_ref,
                 w2_ref, a2_ref, g2_ref, kk_ref, ka_ref, rk_ref, lng_ref, lnb_ref, o_ref, hstate):
    t = pl.program_id(1)
    tm, c = r_ref.shape
    npair = c // LANES
    ll = RW_L

    @pl.when(t == 0)
    def _():
        hstate[...] = jnp.zeros_like(hstate)

    def mix(ref, href, mu):
        p = ref[...]
        prev = _shift_rows(p, jnp.where(t == 0, 0.0, href[...]), 1)
        return p + (prev - p) * mu

    mu = mu_ref[...]
    r = mix(r_ref, rh_ref, mu[0:1, :])
    k = mix(k_ref, kh_ref, mu[1:2, :])
    v = mix(v_ref, vh_ref, mu[2:3, :])
    lo = mix(lo_ref, loh_ref, mul_ref[...])
    wlog = -_softplus(-(w0_ref[...] + _mm(jnp.tanh(lo), w2_ref[...]))) - 0.5
    lw = -jnp.exp(wlog)
    a = _sigmoid(a0_ref[...] + _mm(lo, a2_ref[...]))
    g = _mm(_sigmoid(lo), g2_ref[...])
    kk = k * kk_ref[...]
    k = k * (1.0 + (a - 1.0) * ka_ref[...])

    lane = lax.broadcasted_iota(jnp.int32, (ll, LANES), 1)
    m0 = lane < RWKV_HD
    ri = lax.broadcasted_iota(jnp.int32, (2 * ll, 2 * ll), 0)
    ci = lax.broadcasted_iota(jnp.int32, (2 * ll, 2 * ll), 1)
    same = (ri >= ll) == (ci >= ll)
    strict = same & (ri > ci)
    incl = same & (ri >= ci)
    eye = ri == ci
    tri = (lax.broadcasted_iota(jnp.int32, (ll, ll), 0) >= lax.broadcasted_iota(jnp.int32, (ll, ll), 1)).astype(F32)

    def st(x):
        return jnp.concatenate([jnp.where(m0, x, 0.0), jnp.where(m0, 0.0, x)], axis=0)

    nch = tm // ll
    pcs = [(ch, p) for ch in range(nch) for p in range(npair)]
    cum_all = [_mm_exact(tri, lw[ch * ll:(ch + 1) * ll, :]) for ch in range(nch)]
    pre = {}
    for (ch, p) in pcs:
        rows, cols = slice(ch * ll, (ch + 1) * ll), slice(p * LANES, (p + 1) * LANES)
        rp, kp, vp, ap, lwp, cum = r[rows, cols], k[rows, cols], v[rows, cols], a[rows, cols], lw[rows, cols], cum_all[ch][:, cols]
        kkp = kk[rows, cols]
        kkp = kkp * lax.rsqrt(_head_sum(kkp * kkp, m0) + 1e-12)
        cum_l = cum[ll - 1:ll, :]
        e_neg = jnp.exp(-cum)
        e_last = jnp.exp(cum_l - cum)
        kb = kkp * ap
        pre[ch, p] = dict(
            rows=rows, cols=cols, rp=rp, kp=kp, vp=vp,
            a_st=st(-kkp * jnp.exp(cum - lwp)).astype(BF16), r_st=st(rp * jnp.exp(cum)),
            v_st=st(vp).astype(BF16), b_st=st(kb * e_neg).astype(BF16), k_st=st(kp * e_neg).astype(BF16),
            bh_st=st(kb * e_last).astype(BF16), kh_st=st(kp * e_last).astype(BF16), g_l=jnp.exp(cum_l))

    gram = {pc: _mm_nt(jnp.concatenate([pre[pc]["a_st"], pre[pc]["r_st"].astype(BF16)], axis=0),
                       jnp.concatenate([pre[pc]["b_st"], pre[pc]["k_st"]], axis=0)) for pc in pcs}
    a_ab = {pc: jnp.where(strict, gram[pc][:2 * ll, :2 * ll], 0.0) for pc in pcs}
    a_ak = {pc: jnp.where(strict, gram[pc][:2 * ll, 2 * ll:], 0.0).astype(BF16) for pc in pcs}
    m_rb = {pc: jnp.where(incl, gram[pc][2 * ll:, :2 * ll], 0.0).astype(BF16) for pc in pcs}
    m_rk = {pc: jnp.where(incl, gram[pc][2 * ll:, 2 * ll:], 0.0).astype(BF16) for pc in pcs}
    akv = {pc: _mm(a_ak[pc], pre[pc]["v_st"]) for pc in pcs}
    mrkv = {pc: _mm(m_rk[pc], pre[pc]["v_st"]) for pc in pcs}
    kv = {pc: _mm_tn(pre[pc]["kh_st"], pre[pc]["v_st"]) for pc in pcs}
    tinv = {pc: jnp.where(eye, 1.0, a_ab[pc]) for pc in pcs}
    pw = {pc: a_ab[pc].astype(BF16) for pc in pcs}
    for _ in range(int(math.log2(ll)) - 1):
        pw = {pc: _mm(pw[pc], pw[pc]).astype(BF16) for pc in pcs}
        tinv = {pc: tinv[pc] + _mm(tinv[pc], pw[pc]) for pc in pcs}
    xa = {pc: _mm(tinv[pc], jnp.concatenate([pre[pc]["a_st"], akv[pc].astype(BF16)], axis=1)).astype(BF16)
          for pc in pcs}
    mb = {pc: _mm(m_rb[pc], xa[pc]) for pc in pcs}
    bt = {pc: _mm_tn(pre[pc]["bh_st"], xa[pc]) for pc in pcs}
    for (ch, p) in pcs:
        pc = (ch, p)
        q = pre[pc]
        r_eff = q["r_st"] + mb[pc][:, :LANES]
        y_loc = mb[pc][:, LANES:] + mrkv[pc]
        g_eff = jnp.where(eye, q["g_l"], 0.0) + bt[pc][:, :LANES]
        f_loc = bt[pc][:, LANES:] + kv[pc]
        hs = _mm(jnp.concatenate([g_eff, r_eff], axis=0), hstate[p])
        hstate[p] = hs[:2 * ll, :] + f_loc
        y_st = hs[2 * ll:, :] + y_loc
        y = y_st[:ll, :] + y_st[ll:, :]
        rows, cols = q["rows"], q["cols"]
        mean = _head_sum(y, m0) * (1.0 / RWKV_HD)
        d = y - mean
        var = _head_sum(d * d, m0) * (1.0 / RWKV_HD)
        yn = d * lax.rsqrt(var + RWKV_LN_EPS) * lng_ref[:, cols] + lnb_ref[:, cols]
        bonus = _head_sum(q["rp"] * q["kp"] * rk_ref[:, cols], m0) * q["vp"]
        o_ref[rows, cols] = (yn + bonus) * g[rows, cols]


def _rwkv_branch(z, prm):
    (mu, mul, w0, a0, w2, a2, g2, kk, ka, rk, lng, lnb) = prm
    b, t, _ = z.shape
    tm = min(4 * RW_L, t)
    hb = tm // SUBLANES
    cr = COL_RW_R // C
    cl = COL_LORA // 256
    npair = C // LANES

    def tile(cb, w):
        return pl.BlockSpec((None, tm, w), lambda i, j: (i, j, cb))

    def halo(cb, w):
        return pl.BlockSpec((None, SUBLANES, w), lambda i, j: (i, jnp.maximum(j * hb - 1, 0), cb))

    def full(arr):
        return pl.BlockSpec(arr.shape, lambda i, j: (0,) * arr.ndim)

    return pl.pallas_call(
        _rwkv_kernel,
        grid=(b, t // tm),
        in_specs=[tile(cr, C), tile(cr + 1, C), tile(cr + 2, C), tile(cl, 256),
                  halo(cr, C), halo(cr + 1, C), halo(cr + 2, C), halo(cl, 256)] + [full(a) for a in prm],
        out_specs=pl.BlockSpec((None, tm, C), lambda i, j: (i, j, 0)),
        out_shape=jax.ShapeDtypeStruct((b, t, C), F32),
        scratch_shapes=[pltpu.VMEM((npair, LANES, LANES), F32)],
        compiler_params=_cparams(("parallel", "arbitrary")),
        name="rwkv7",
    )(z, z, z, z, z, z, z, z, *prm)


def _prep_rwkv(mu_rkv, mu_w, mu_a, mu_g, w0, w2, a0, a2, g2, k_k, k_a, r_k, lnx_g, lnx_b):
    nl = mu_rkv.shape[0]
    row = lambda x: x.reshape(nl, 1, -1).astype(F32)
    mul = jnp.concatenate([mu_w, mu_a, mu_g], axis=-1)
    zw = lambda n: jnp.zeros((nl, n, C), F32)
    w2p = jnp.concatenate([w2, zw(A_LORA + G_LORA)], axis=1).astype(BF16)
    a2p = jnp.concatenate([zw(W_LORA), a2, zw(G_LORA)], axis=1).astype(BF16)
    g2p = jnp.concatenate([zw(W_LORA + A_LORA), g2], axis=1).astype(BF16)
    return (mu_rkv.astype(F32), row(mul), row(w0), row(a0), w2p, a2p, g2p, row(k_k), row(k_a), row(r_k),
            row(lnx_g), row(lnx_b))


SLOT = LANES
ROT_HALF = QK_ROPE // 2


def _mla_prep_kernel(cq_ref, ckv_ref, kr_ref, rc_ref, rs1_ref, rs2_ref, qn_ref, wq_ref, kvn_ref, wk_ref, wv_ref,
                     gq_ref, gk_ref, q_ref, k_ref, v_ref, v_scr):
    def rms(x, g, n):
        ms = jnp.sum(x * x, axis=-1, keepdims=True) * (1.0 / n)
        return x * lax.rsqrt(ms + EPS) * g

    def rope(x):
        return (x * rc_ref[...] + pltpu.roll(x, SLOT - ROT_HALF, axis=1) * rs1_ref[...]
                + pltpu.roll(x, ROT_HALF, axis=1) * rs2_ref[...])

    cq = rms(cq_ref[...], qn_ref[...], Q_LORA)
    ckv = rms(ckv_ref[...], kvn_ref[...], KV_LORA)
    qf = _mm(cq, wq_ref[...])
    kf = _mm(ckv, wk_ref[...])
    v_scr[...] = _mm(ckv, wv_ref[...])
    v_ref[...] = jnp.transpose(v_scr[...]).astype(BF16)
    kr = kr_ref[...]
    scale = QK_HD ** -0.5 * math.log2(math.e)
    for h in range(MLA_HEADS):
        cols = slice(h * SLOT, (h + 1) * SLOT)
        q = rope(rms(qf[:, cols], gq_ref[...], QK_HD)) * scale
        k = rope(rms(kf[:, cols] + kr, gk_ref[...], QK_HD))
        q_ref[:, cols] = q.astype(BF16)
        k_ref[:, cols] = k.astype(BF16)


def _flash_kernel(q_ref, k_ref, vt_ref, o_ref, m_scr, l_scr, acc):
    i = pl.program_id(1)
    j = pl.program_id(2)
    bq = q_ref.shape[0]
    bk = k_ref.shape[0]

    @pl.when(j == 0)
    def _():
        m_scr[...] = jnp.full_like(m_scr, -jnp.inf)
        l_scr[...] = jnp.zeros_like(l_scr)
        acc[...] = jnp.zeros_like(acc)

    def step(masked):
        if masked:
            causal = (lax.broadcasted_iota(jnp.int32, (bk, bq), 0) <= lax.broadcasted_iota(jnp.int32, (bk, bq), 1))
        def scores(h):
            cols = slice(h * SLOT, (h + 1) * SLOT)
            return _mm_nt(k_ref[:, cols], q_ref[:, cols])

        st_next = scores(0)
        for h in range(MLA_HEADS):
            rows = slice(h * V_HD, (h + 1) * V_HD)
            st = st_next
            if h + 1 < MLA_HEADS:
                st_next = scores(h + 1)
            if masked:
                st = jnp.where(causal, st, -jnp.inf)
            m_prev = m_scr[h:h + 1, :]
            m_new = jnp.maximum(m_prev, jnp.max(st, axis=0, keepdims=True))
            alpha = jnp.exp2(m_prev - m_new)
            pt = jnp.exp2(st - m_new)
            l_scr[h:h + 1, :] = alpha * l_scr[h:h + 1, :] + jnp.sum(pt, axis=0, keepdims=True)
            m_scr[h:h + 1, :] = m_new
            acc[rows, :] = acc[rows, :] * alpha + _mm(vt_ref[rows, :], pt)

    @pl.when(j < i)
    def _():
        step(False)

    @pl.when(j == i)
    def _():
        step(True)
        for h in range(MLA_HEADS):
            rows = slice(h * V_HD, (h + 1) * V_HD)
            acc[rows, :] = acc[rows, :] / l_scr[h:h + 1, :]
        o_ref[...] = jnp.transpose(acc[...])


def _mla_branch(z, prm, rope):
    qn, wq, kvn, wk, wv, gq, gk = prm
    rc, rs1, rs2 = rope
    b, t, _ = z.shape
    tm = min(512, t)
    hs = MLA_HEADS * SLOT

    def full(arr):
        return pl.BlockSpec(arr.shape, lambda i, j: (0,) * arr.ndim)

    def tab():
        return pl.BlockSpec((tm, SLOT), lambda i, j: (j, 0))

    q, k, vt = pl.pallas_call(
        _mla_prep_kernel,
        grid=(b, t // tm),
        in_specs=[pl.BlockSpec((None, tm, Q_LORA), lambda i, j: (i, j, COL_CQ // Q_LORA)),
                  pl.BlockSpec((None, tm, KV_LORA), lambda i, j: (i, j, COL_CKV // KV_LORA)),
                  pl.BlockSpec((None, tm, LANES), lambda i, j: (i, j, COL_KR // LANES)),
                  tab(), tab(), tab()] + [full(a) for a in prm],
        out_specs=[pl.BlockSpec((None, tm, hs), lambda i, j: (i, j, 0)),
                   pl.BlockSpec((None, tm, hs), lambda i, j: (i, j, 0)),
                   pl.BlockSpec((None, C, tm), lambda i, j: (i, 0, j))],
        out_shape=[jax.ShapeDtypeStruct((b, t, hs), BF16), jax.ShapeDtypeStruct((b, t, hs), BF16),
                   jax.ShapeDtypeStruct((b, C, t), BF16)],
        scratch_shapes=[pltpu.VMEM((tm, C), F32)],
        compiler_params=_cparams(("parallel", "parallel")),
        name="mla_prep",
    )(z, z, z, rc, rs1, rs2, *prm)

    bq = min(512, t)
    nq = t // bq
    return pl.pallas_call(
        _flash_kernel,
        grid=(b, nq, nq),
        in_specs=[pl.BlockSpec((None, bq, hs), lambda bi, i, j: (bi, i, 0)),
                  pl.BlockSpec((None, bq, hs), lambda bi, i, j: (bi, jnp.minimum(j, i), 0)),
                  pl.BlockSpec((None, C, bq), lambda bi, i, j: (bi, 0, jnp.minimum(j, i)))],
        out_specs=pl.BlockSpec((None, bq, C), lambda bi, i, j: (bi, i, 0)),
        out_shape=jax.ShapeDtypeStruct((b, t, C), F32),
        scratch_shapes=[pltpu.VMEM((MLA_HEADS, bq), F32), pltpu.VMEM((MLA_HEADS, bq), F32),
                        pltpu.VMEM((C, bq), F32)],
        compiler_params=_cparams(("parallel", "parallel", "arbitrary")),
        name="mla_attn",
    )(q, k, vt)


def _rope_tables(t):
    pos = jnp.arange(t, dtype=F32)
    inv_freq = ROPE_THETA ** (-jnp.arange(0, QK_ROPE, 2, dtype=F32) / QK_ROPE)
    ang = pos[:, None] * inv_freq[None, :]
    cos, sin = jnp.cos(ang), jnp.sin(ang)
    z = lambda n: jnp.zeros((t, n), F32)
    pad = SLOT - QK_HD
    rc = jnp.concatenate([jnp.ones((t, QK_NOPE), F32), cos, cos, z(pad)], axis=1)
    rs1 = jnp.concatenate([z(QK_NOPE), -sin, z(ROT_HALF + pad)], axis=1)
    rs2 = jnp.concatenate([z(QK_NOPE + ROT_HALF), sin, z(pad)], axis=1)
    return rc, rs1, rs2


def _prep_mla(q_norm, w_uq, kv_norm, w_ukv, qk_norm_q, qk_norm_k):
    nl = q_norm.shape[0]
    nh = MLA_HEADS
    pad = SLOT - QK_HD
    wq = w_uq.reshape(nl, Q_LORA, nh, QK_HD)
    wq = jnp.pad(wq, ((0, 0), (0, 0), (0, 0), (0, pad))).reshape(nl, Q_LORA, nh * SLOT).astype(BF16)
    wkv = w_ukv.reshape(nl, KV_LORA, nh, QK_NOPE + V_HD)
    wk = jnp.pad(wkv[..., :QK_NOPE], ((0, 0), (0, 0), (0, 0), (0, SLOT - QK_NOPE)))
    wk = wk.reshape(nl, KV_LORA, nh * SLOT).astype(BF16)
    wv = wkv[..., QK_NOPE:].reshape(nl, KV_LORA, nh * V_HD).astype(BF16)
    slot = lambda g: jnp.pad(g.astype(F32), ((0, 0), (0, pad))).reshape(nl, 1, SLOT)
    return (q_norm.reshape(nl, 1, -1).astype(F32), wq, kv_norm.reshape(nl, 1, -1).astype(F32), wk, wv,
            slot(qk_norm_q), slot(qk_norm_k))


def _merge_kernel(y0_ref, y1_ref, y2_ref, y3_ref, gl_ref, x_ref, wb_ref, wo_ref, o_ref):
    d = x_ref.shape[1]
    merged = None
    for n, y_ref in enumerate((y0_ref, y1_ref, y2_ref, y3_ref)):
        term = _mm(y_ref[...], wb_ref[n]) * _sigmoid(gl_ref[:, n * d:(n + 1) * d])
        merged = term if merged is None else merged + term
    o_ref[...] = x_ref[...] + _mm(merged, wo_ref[...])


def _merge(ys, z2, x2, wb, wo):
    n, d = x2.shape
    tm = min(256, n)
    ytile = pl.BlockSpec((tm, C), lambda i: (i, 0))
    return pl.pallas_call(
        _merge_kernel,
        grid=(n // tm,),
        in_specs=[ytile, ytile, ytile, ytile,
                  pl.BlockSpec((tm, 4 * d), lambda i: (i, COL_GATE)),
                  pl.BlockSpec((tm, d), lambda i: (i, 0)),
                  pl.BlockSpec(wb.shape, lambda i: (0, 0, 0)),
                  pl.BlockSpec(wo.shape, lambda i: (0, 0))],
        out_specs=pl.BlockSpec((tm, d), lambda i: (i, 0)),
        out_shape=jax.ShapeDtypeStruct((n, d), F32),
        compiler_params=_cparams(("parallel",)),
        name="merge",
    )(*ys, z2, x2, wb, wo)


def _mlp_kernel(x_ref, g_ref, w1_ref, w2_ref, o_ref, h_scr, acc):
    j = pl.program_id(1)

    @pl.when(j == 0)
    def _():
        x = x_ref[...]
        ms = jnp.mean(x * x, axis=-1, keepdims=True)
        h_scr[...] = (x * lax.rsqrt(ms + EPS) * g_ref[...]).astype(BF16)
        acc[...] = x

    a = jnp.maximum(jnp.dot(h_scr[...], w1_ref[...], preferred_element_type=F32), 0.0)
    acc[...] += _mm(a * a, w2_ref[...])

    @pl.when(j == pl.num_programs(1) - 1)
    def _():
        o_ref[...] = acc[...]


def _mlp(x2, g, w1, w2):
    n, d = x2.shape
    dff = w1.shape[1]
    tm = min(1024, n)
    tf = 512
    return pl.pallas_call(
        _mlp_kernel,
        grid=(n // tm, dff // tf),
        in_specs=[pl.BlockSpec((tm, d), lambda i, j: (i, 0)),
                  pl.BlockSpec((1, d), lambda i, j: (0, 0)),
                  pl.BlockSpec((d, tf), lambda i, j: (0, j)),
                  pl.BlockSpec((tf, d), lambda i, j: (j, 0))],
        out_specs=pl.BlockSpec((tm, d), lambda i, j: (i, 0)),
        out_shape=jax.ShapeDtypeStruct((n, d), F32),
        scratch_shapes=[pltpu.VMEM((tm, d), BF16), pltpu.VMEM((tm, d), F32)],
        compiler_params=_cparams(("parallel", "arbitrary")),
        name="mlp",
    )(x2, g, w1, w2)


def _prep_w_in(w_in):
    nl, d, _ = w_in.shape
    gate0 = 6 * C + W_LORA + A_LORA + G_LORA + Q_LORA + KV_LORA + QK_ROPE
    kr0 = gate0 - QK_ROPE
    parts = [
        w_in[:, :, gate0:],
        w_in[:, :, :kr0],
        jnp.zeros((nl, d, QK_NOPE), w_in.dtype),
        w_in[:, :, kr0:gate0],
        jnp.zeros((nl, d, LANES - QK_NOPE - QK_ROPE), w_in.dtype),
    ]
    return jnp.concatenate(parts, axis=-1).astype(BF16)


def _prep_lru_gate(gate_w):
    nl, _, nh, hd, _ = gate_w.shape
    eye = jnp.eye(nh, dtype=gate_w.dtype)
    dense = jnp.einsum("lghij,hk->lghikj", gate_w, eye).reshape(nl, 2, nh * hd, nh * hd)
    return jnp.concatenate([dense[:, 0], dense[:, 1]], axis=-1).astype(BF16)


def kernel(x, norm_mix, w_in, lru_conv_w, lru_conv_b, lru_gate_w, lru_gate_b, lru_lambda, s5_a_re, s5_a_im, s5_b_re, s5_b_im, s5_c_re, s5_c_im, s5_d, s5_log_dt, s5_w_glu, rwkv_mu_rkv, rwkv_mu_w, rwkv_mu_a, rwkv_mu_g, rwkv_w0, rwkv_w2, rwkv_a0, rwkv_a2, rwkv_g2, rwkv_k_k, rwkv_k_a, rwkv_r_k, rwkv_lnx_g, rwkv_lnx_b, mla_q_norm, mla_w_uq, mla_kv_norm, mla_w_ukv, mla_qk_norm_q, mla_qk_norm_k, w_branch, w_out, norm_mlp, w_ff1, w_ff2):
    b, t, d = x.shape
    depth = w_in.shape[0]
    n = b * t
    row = lambda a: a.reshape(depth, 1, -1).astype(F32)

    w_in_p = _prep_w_in(w_in)
    lru_gw = _prep_lru_gate(lru_gate_w)
    s5_tabs = _prep_s5(s5_a_re, s5_a_im, s5_b_re, s5_b_im, s5_c_re, s5_c_im, s5_log_dt, t)
    rw_prm = _prep_rwkv(rwkv_mu_rkv, rwkv_mu_w, rwkv_mu_a, rwkv_mu_g, rwkv_w0, rwkv_w2, rwkv_a0, rwkv_a2, rwkv_g2,
                        rwkv_k_k, rwkv_k_a, rwkv_r_k, rwkv_lnx_g, rwkv_lnx_b)
    mla_prm = _prep_mla(mla_q_norm, mla_w_uq, mla_kv_norm, mla_w_ukv, mla_qk_norm_q, mla_qk_norm_k)
    rope = _rope_tables(t)
    g_mix, g_mlp = row(norm_mix), row(norm_mlp)
    conv_b, gate_b, lam, s5_dr = row(lru_conv_b), row(lru_gate_b), row(lru_lambda), row(s5_d)
    w_glu, wb, wo = s5_w_glu.astype(BF16), w_branch.astype(BF16), w_out.astype(BF16)
    w1, w2 = w_ff1.astype(BF16), w_ff2.astype(BF16)

    x2 = x.reshape(n, d)
    for l in range(depth):
        z2 = _inproj(x2, g_mix[l], w_in_p[l])
        z = z2.reshape(b, t, -1)
        y_lru = _lru_branch(z, lru_conv_w[l], conv_b[l], lru_gw[l], gate_b[l], lam[l])
        y_s5 = _s5_branch(z, [tb[l] for tb in s5_tabs], s5_dr[l], w_glu[l])
        y_rw = _rwkv_branch(z, [a[l] for a in rw_prm])
        y_mla = _mla_branch(z, [a[l] for a in mla_prm], rope)
        ys = [y.reshape(n, C) for y in (y_lru, y_s5, y_rw, y_mla)]
        x2 = _merge(ys, z2, x2, wb[l], wo[l])
        x2 = _mlp(x2, g_mlp[l], w1[l], w2[l])
    return x2.reshape(b, t, d)
```

```python
import functools
import math

import jax
import jax.numpy as jnp
import numpy as np
from jax import lax
from jax.experimental import pallas as pl
from jax.experimental.pallas import tpu as pltpu

F32 = jnp.float32
BF16 = jnp.bfloat16

EPS = 1e-6
LRU_HEADS = 8
CONV_W = 4
LRU_C = 8.0
S5_GROUP = 16
S5_STATE = 64
RWKV_HD = 64
W_LORA, A_LORA, G_LORA = 64, 64, 128
RWKV_LN_EPS = 64e-5
MLA_HEADS = 8
QK_NOPE, QK_ROPE, V_HD = 64, 32, 64
QK_HD = QK_NOPE + QK_ROPE
Q_LORA, KV_LORA = 256, 128
ROPE_THETA = 10000.0

LANES = 128
SUBLANES = 8
VMEM_LIMIT = 56 * 1024 * 1024

D_MODEL = 1024
C = D_MODEL // 2
COL_GATE = 0
COL_LRU_X = 4 * D_MODEL
COL_LRU_G = COL_LRU_X + C
COL_S5_U = COL_LRU_G + C
COL_RW_R = COL_S5_U + C
COL_RW_K = COL_RW_R + C
COL_RW_V = COL_RW_K + C
COL_LORA = COL_RW_V + C
COL_CQ = COL_LORA + 256
COL_CKV = COL_CQ + Q_LORA
COL_KR = COL_CKV + KV_LORA
D_INP = 8 * D_MODEL


def _cparams(sem):
    return pltpu.CompilerParams(dimension_semantics=sem, vmem_limit_bytes=VMEM_LIMIT)


def _gelu_tanh(x):
    return 0.5 * x * (1.0 + jnp.tanh(math.sqrt(2.0 / math.pi) * (x + 0.044715 * (x * x * x))))


def _sigmoid(x):
    return 1.0 / (1.0 + jnp.exp(-x))


def _softplus(x):
    return jnp.maximum(x, 0.0) + jnp.log(1.0 + jnp.exp(-jnp.abs(x)))


HALO = 16


def _shift_rows(x, halo, k):
    xe = jnp.concatenate([halo, x], axis=0)
    n = x.shape[0]
    return xe[HALO - k:HALO - k + n, :]


def _layer_spec(arr, l):
    zeros = (0,) * (arr.ndim - 1)
    return pl.BlockSpec((None,) + arr.shape[1:], lambda *idx: (l,) + zeros)


def _inproj_kernel(x_ref, g_ref, w_ref, o_ref, u_ref, h_scr):
    j = pl.program_id(1)

    @pl.when(j == 0)
    def _():
        x = x_ref[...]
        ms = jnp.mean(x * x, axis=-1, keepdims=True)
        h_scr[...] = (x * lax.rsqrt(ms + EPS) * g_ref[...]).astype(BF16)

    y = jnp.dot(h_scr[...], w_ref[...], preferred_element_type=F32)
    o_ref[...] = y.astype(BF16)
    tn = w_ref.shape[1]
    u0 = COL_S5_U % tn

    @pl.when(j == COL_S5_U // tn)
    def _():
        u_ref[...] = y[:, u0:u0 + C]


INPROJ_TN = 1024


def _inproj(x2, g, w, l):
    n, d = x2.shape
    dn = w.shape[2]
    tm = min(1024, n)
    tn = INPROJ_TN
    assert COL_S5_U % tn + C <= tn and dn % tn == 0
    return pl.pallas_call(
        _inproj_kernel,
        grid=(n // tm, dn // tn),
        in_specs=[
            pl.BlockSpec((tm, d), lambda i, j: (i, 0)),
            pl.BlockSpec((None, 1, d), lambda i, j: (l, 0, 0)),
            pl.BlockSpec((None, d, tn), lambda i, j: (l, 0, j)),
        ],
        out_specs=[pl.BlockSpec((tm, tn), lambda i, j: (i, j)),
                   pl.BlockSpec((tm, C), lambda i, j: (i, 0))],
        out_shape=[jax.ShapeDtypeStruct((n, dn), BF16), jax.ShapeDtypeStruct((n, C), F32)],
        scratch_shapes=[pltpu.VMEM((tm, d), BF16)],
        compiler_params=_cparams(("parallel", "arbitrary")),
        name="inproj",
    )(x2, g, w)


def _lru_kernel(x_ref, xh_ref, g_ref, cw_ref, cb_ref, gw_ref, gb_ref, lam_ref, o_ref, carry):
    t = pl.program_id(1)
    tm, c = x_ref.shape

    @pl.when(t == 0)
    def _():
        carry[...] = jnp.zeros_like(carry)

    x = x_ref[...].astype(F32)
    halo = jnp.where(t == 0, 0.0, xh_ref[...].astype(F32))
    cw = cw_ref[...]
    xc = cb_ref[...] + cw[CONV_W - 1:CONV_W, :] * x
    for k in range(1, CONV_W):
        xc = xc + cw[CONV_W - 1 - k:CONV_W - k, :] * _shift_rows(x, halo, k)

    gates = jnp.dot(xc.astype(BF16), gw_ref[...], preferred_element_type=F32) + gb_ref[...]
    r = _sigmoid(gates[:, :c])
    i = _sigmoid(gates[:, c:])
    log_a = (-LRU_C) * r * _softplus(-lam_ref[...])
    a = jnp.exp(log_a)
    u = jnp.sqrt(1.0 - jnp.exp(2.0 * log_a)) * (i * xc)

    row = lax.broadcasted_iota(jnp.int32, (tm, c), 0)
    s = 1
    while s < tm:
        a_sh = jnp.where(row >= s, pltpu.roll(a, s, axis=0), 1.0)
        u_sh = jnp.where(row >= s, pltpu.roll(u, s, axis=0), 0.0)
        u = u + a * u_sh
        a = a * a_sh
        s *= 2
    h = u + a * carry[SUBLANES - 1:SUBLANES, :]
    carry[...] = h[tm - SUBLANES:, :]
    o_ref[...] = h * _gelu_tanh(g_ref[...].astype(F32))


def _lru_branch(z, prm, l):
    b, t, _ = z.shape
    tm = min(256, t)
    cb = COL_LRU_X // C
    hb = tm // HALO
    return pl.pallas_call(
        _lru_kernel,
        grid=(b, t // tm),
        in_specs=[
            pl.BlockSpec((None, tm, C), lambda i, j: (i, j, cb)),
            pl.BlockSpec((None, HALO, C), lambda i, j: (i, jnp.maximum(j * hb - 1, 0), cb)),
            pl.BlockSpec((None, tm, C), lambda i, j: (i, j, cb + 1)),
        ] + [_layer_spec(a, l) for a in prm],
        out_specs=pl.BlockSpec((None, tm, C), lambda i, j: (i, j, 0)),
        out_shape=jax.ShapeDtypeStruct((b, t, C), F32),
        scratch_shapes=[pltpu.VMEM((SUBLANES, C), F32)],
        compiler_params=_cparams(("parallel", "arbitrary")),
        name="rglru",
    )(z, z, z, *prm)


S5_L = 8
S5_OCT = LANES // S5_GROUP


def _s5_kernel(u_ref, win_ref, bd_ref, wout_ref, lstep_ref, y_ref, uf, wintra, carry):
    t = pl.program_id(2)
    tm = u_ref.shape[0]
    nc = tm // S5_L
    ns = carry.shape[1]
    half = ns // 2

    @pl.when((pl.program_id(1) == 0) & (t == 0))
    def _():
        wintra[...] = jnp.zeros_like(wintra)
        for i in range(S5_L):
            for j in range(i, S5_L):
                wintra[i * LANES:(i + 1) * LANES, j * LANES:(j + 1) * LANES] = bd_ref[j - i]

    @pl.when(t == 0)
    def _():
        carry[...] = jnp.zeros_like(carry)

    for i in range(S5_L):
        uf[:, i * LANES:(i + 1) * LANES] = u_ref[pl.ds(i, nc, stride=S5_L), :].astype(BF16)
    ufv = uf[...]
    s = jnp.dot(ufv, win_ref[...], preferred_element_type=F32)

    def cmul(v, k):
        sw = jnp.concatenate([v[:, half:], v[:, :half]], axis=1)
        return v * lstep_ref[2 * k:2 * k + 1, :] + sw * lstep_ref[2 * k + 1:2 * k + 2, :]

    row = lax.broadcasted_iota(jnp.int32, (nc, ns), 0)
    cin = carry[0:1, :]
    s = s + jnp.where(row == 0, cmul(cin, 0), 0.0)
    k = 0
    st = 1
    while st < nc:
        sh = jnp.where(row >= st, pltpu.roll(s, st, axis=0), 0.0)
        s = s + cmul(sh, k)
        k += 1
        st *= 2
    carry[...] = jnp.broadcast_to(s[nc - 1:nc, :], carry.shape)
    sprev = jnp.where(row >= 1, pltpu.roll(s, 1, axis=0), cin)

    yf = jnp.dot(ufv, wintra[...], preferred_element_type=F32)
    yf = yf + jnp.dot(sprev.astype(BF16), wout_ref[...], preferred_element_type=F32)
    for j in range(S5_L):
        y_ref[pl.ds(j, nc, stride=S5_L), :] = yf[:, j * LANES:(j + 1) * LANES]


def _s5_tail_kernel(y_ref, u_ref, d_ref, w_ref, o_ref):
    c = u_ref.shape[1]
    y = _gelu_tanh(y_ref[...] + d_ref[...] * u_ref[...])
    z = jnp.dot(y.astype(BF16), w_ref[...], preferred_element_type=F32)
    o_ref[...] = z[:, :c] * _sigmoid(z[:, c:])


def _s5_tile(t):
    return min(2048, t)


def _s5_branch(u, tabs, d_skip, w_glu, l):
    win, bd, wout, lstep = tabs
    b, t, _ = u.shape
    tm = _s5_tile(t)
    nc = tm // S5_L
    noct = C // LANES
    ns = win.shape[-1]
    fl = S5_L * LANES

    def table(arr):
        zeros = (0,) * (arr.ndim - 2)
        return pl.BlockSpec((None, None) + arr.shape[2:], lambda q, i, j: (l, q) + zeros)

    y = pl.pallas_call(
        _s5_kernel,
        grid=(noct, b, t // tm),
        in_specs=[pl.BlockSpec((None, tm, LANES), lambda q, i, j: (i, j, q)),
                  table(win), table(bd), table(wout), table(lstep)],
        out_specs=pl.BlockSpec((None, tm, LANES), lambda q, i, j: (i, j, q)),
        out_shape=jax.ShapeDtypeStruct((b, t, C), F32),
        scratch_shapes=[pltpu.VMEM((nc, fl), BF16), pltpu.VMEM((fl, fl), BF16), pltpu.VMEM((SUBLANES, ns), F32)],
        compiler_params=_cparams(("arbitrary", "arbitrary", "arbitrary")),
        name="s5_ssm",
    )(u, win, bd, wout, lstep)

    tt = min(512, t)
    return pl.pallas_call(
        _s5_tail_kernel,
        grid=(b, t // tt),
        in_specs=[
            pl.BlockSpec((None, tt, C), lambda i, j: (i, j, 0)),
            pl.BlockSpec((None, tt, C), lambda i, j: (i, j, 0)),
            _layer_spec(d_skip, l),
            _layer_spec(w_glu, l),
        ],
        out_specs=pl.BlockSpec((None, tt, C), lambda i, j: (i, j, 0)),
        out_shape=jax.ShapeDtypeStruct((b, t, C), F32),
        compiler_params=_cparams(("parallel", "parallel")),
        name="s5_glu",
    )(y, u, d_skip, w_glu)


def _prep_s5(a_re, a_im, b_re, b_im, c_re, c_im, log_dt, t):
    nl, g, p = a_re.shape
    m = S5_GROUP
    noct = g // S5_OCT
    nc = _s5_tile(t) // S5_L
    lam = lax.complex(a_re.astype(F32), a_im.astype(F32))
    dt = jnp.exp(log_dt.astype(F32))[..., None]
    ldt = lam * dt
    lam_bar = jnp.exp(ldt)
    b_bar = ((lam_bar - 1.0) / lam)[..., None] * lax.complex(b_re.astype(F32), b_im.astype(F32))
    cc = lax.complex(c_re.astype(F32), c_im.astype(F32))

    def powers(e):
        return jnp.exp(ldt[..., None] * e.astype(F32).reshape((1, 1, 1, -1)))

    steps = jnp.arange(S5_L)
    same = jnp.eye(S5_OCT, dtype=bool)

    def octet_bd(x):
        x = jnp.moveaxis(x, 2, 3)
        x = jnp.expand_dims(x, 6)
        x = jnp.where(same.reshape(1, 1, 1, S5_OCT, 1, 1, S5_OCT, 1), x, 0.0)
        nrow = int(np.prod(x.shape[2:5]))
        return x.astype(BF16).reshape(nl, noct, nrow, -1)

    kd = jnp.einsum("lgop,lgpd,lgpi->lgdio", cc, powers(steps), b_bar).real
    kd = kd.reshape(nl, noct, S5_OCT, S5_L, m, 1, m)
    bd = octet_bd(kd).reshape(nl, noct, S5_L, LANES, LANES)

    wi = powers(S5_L - 1 - steps)[..., None] * b_bar[:, :, :, None, :]
    wi = jnp.stack([wi.real, wi.imag], axis=2)
    wi = jnp.transpose(wi, (0, 1, 4, 5, 2, 3)).reshape(nl, noct, S5_OCT, S5_L, m, 2, p)
    win = octet_bd(wi)

    wo = cc[:, :, :, :, None] * powers(steps + 1)[:, :, None, :, :]
    wo = jnp.stack([wo.real, -wo.imag], axis=2)
    wo = jnp.transpose(wo, (0, 1, 2, 4, 5, 3)).reshape(nl, noct, S5_OCT, 2, p, S5_L, m)
    wout = octet_bd(wo)

    nsteps = max(int(math.log2(nc)), 1)
    pw = powers(S5_L * (2 ** jnp.arange(nsteps)))
    pw = jnp.transpose(pw, (0, 3, 1, 2)).reshape(nl, nsteps, noct, S5_OCT * p)
    pw = jnp.moveaxis(pw, 2, 1)
    ar = jnp.concatenate([pw.real, pw.real], axis=-1)
    ai = jnp.concatenate([-pw.imag, pw.imag], axis=-1)
    lstep = jnp.stack([ar, ai], axis=3).reshape(nl, noct, 2 * nsteps, -1)
    return win, bd, wout, lstep


RW_L = 64
HIGHEST = lax.Precision.HIGHEST


def _mm(a, b, dims=(((1,), (0,)), ((), ()))):
    return lax.dot_general(a.astype(BF16), b.astype(BF16), dims, preferred_element_type=F32)


def _mm_nt(a, b):
    return _mm(a, b, (((1,), (1,)), ((), ())))


def _mm_tn(a, b):
    return _mm(a, b, (((0,), (0,)), ((), ())))


def _mm_exact(a, b):
    return lax.dot_general(a, b, (((1,), (0,)), ((), ())), precision=HIGHEST, preferred_element_type=F32)


def _head_sum(x, m0):
    s0 = jnp.sum(jnp.where(m0, x, 0.0), axis=-1, keepdims=True)
    s1 = jnp.sum(jnp.where(m0, 0.0, x), axis=-1, keepdims=True)
    return jnp.where(m0, s0, s1)


def _rwkv_kernel(r_ref, k_ref, v_ref, lo_ref, rh_ref, kh_ref, vh_ref, loh_ref, mu_ref, mul_ref, w0_ref, a0_ref,
                 w2_ref, a2_ref, g2_ref, kk_ref, ka_ref, rk_ref, lng_ref, lnb_ref, o_ref, hstate):
    t = pl.program_id(1)
    tm, c = r_ref.shape
    npair = c // LANES
    ll = RW_L

    @pl.when(t == 0)
    def _():
        hstate[...] = jnp.zeros_like(hstate)

    def mix(ref, href, mu):
        p = ref[...].astype(F32)
        prev = _shift_rows(p, jnp.where(t == 0, 0.0, href[...].astype(F32)), 1)
        return p + (prev - p) * mu

    mu = mu_ref[...]
    r = mix(r_ref, rh_ref, mu[0:1, :])
    k = mix(k_ref, kh_ref, mu[1:2, :])
    v = mix(v_ref, vh_ref, mu[2:3, :])
    lo = mix(lo_ref, loh_ref, mul_ref[...])
    wlog = -_softplus(-(w0_ref[...] + _mm(jnp.tanh(lo), w2_ref[...]))) - 0.5
    lw = -jnp.exp(wlog)
    a = _sigmoid(a0_ref[...] + _mm(lo, a2_ref[...]))
    g = _mm(_sigmoid(lo), g2_ref[...])
    kk = k * kk_ref[...]
    k = k * (1.0 + (a - 1.0) * ka_ref[...])

    lane = lax.broadcasted_iota(jnp.int32, (ll, LANES), 1)
    m0 = lane < RWKV_HD
    ri = lax.broadcasted_iota(jnp.int32, (2 * ll, 2 * ll), 0)
    ci = lax.broadcasted_iota(jnp.int32, (2 * ll, 2 * ll), 1)
    same = (ri >= ll) == (ci >= ll)
    strict = same & (ri > ci)
    incl = same & (ri >= ci)
    eye = ri == ci
    tri = (lax.broadcasted_iota(jnp.int32, (ll, ll), 0) >= lax.broadcasted_iota(jnp.int32, (ll, ll), 1)).astype(F32)

    def st(x):
        return jnp.concatenate([jnp.where(m0, x, 0.0), jnp.where(m0, 0.0, x)], axis=0)

    nch = tm // ll
    pcs = [(ch, p) for ch in range(nch) for p in range(npair)]
    cum_all = [_mm_exact(tri, lw[ch * ll:(ch + 1) * ll, :]) for ch in range(nch)]
    pre = {}
    for (ch, p) in pcs:
        rows, cols = slice(ch * ll, (ch + 1) * ll), slice(p * LANES, (p + 1) * LANES)
        rp, kp, vp, ap, lwp, cum = r[rows, cols], k[rows, cols], v[rows, cols], a[rows, cols], lw[rows, cols], cum_all[ch][:, cols]
        kkp = kk[rows, cols]
        kkp = kkp * lax.rsqrt(_head_sum(kkp * kkp, m0) + 1e-12)
        cum_l = cum[ll - 1:ll, :]
        e_neg = jnp.exp(-cum)
        e_last = jnp.exp(cum_l - cum)
        kb = kkp * ap
        pre[ch, p] = dict(
            rows=rows, cols=cols, rp=rp, kp=kp, vp=vp,
            a_st=st(-kkp * jnp.exp(cum - lwp)).astype(BF16), r_st=st(rp * jnp.exp(cum)),
            v_st=st(vp).astype(BF16), b_st=st(kb * e_neg).astype(BF16), k_st=st(kp * e_neg).astype(BF16),
            bh_st=st(kb * e_last).astype(BF16), kh_st=st(kp * e_last).astype(BF16), g_l=jnp.exp(cum_l))

    gram = {pc: _mm_nt(jnp.concatenate([pre[pc]["a_st"], pre[pc]["r_st"].astype(BF16)], axis=0),
                       jnp.concatenate([pre[pc]["b_st"], pre[pc]["k_st"]], axis=0)) for pc in pcs}
    a_ab = {pc: jnp.where(strict, gram[pc][:2 * ll, :2 * ll], 0.0) for pc in pcs}
    a_ak = {pc: jnp.where(strict, gram[pc][:2 * ll, 2 * ll:], 0.0).astype(BF16) for pc in pcs}
    m_rb = {pc: jnp.where(incl, gram[pc][2 * ll:, :2 * ll], 0.0).astype(BF16) for pc in pcs}
    m_rk = {pc: jnp.where(incl, gram[pc][2 * ll:, 2 * ll:], 0.0).astype(BF16) for pc in pcs}
    akv = {pc: _mm(a_ak[pc], pre[pc]["v_st"]) for pc in pcs}
    mrkv = {pc: _mm(m_rk[pc], pre[pc]["v_st"]) for pc in pcs}
    kv = {pc: _mm_tn(pre[pc]["kh_st"], pre[pc]["v_st"]) for pc in pcs}
    tinv = {pc: jnp.where(eye, 1.0, a_ab[pc]) for pc in pcs}
    pw = {pc: a_ab[pc].astype(BF16) for pc in pcs}
    for _ in range(int(math.log2(ll)) - 1):
        pw = {pc: _mm(pw[pc], pw[pc]).astype(BF16) for pc in pcs}
        tinv = {pc: tinv[pc] + _mm(tinv[pc], pw[pc]) for pc in pcs}
    xa = {pc: _mm(tinv[pc], jnp.concatenate([pre[pc]["a_st"], akv[pc].astype(BF16)], axis=1)).astype(BF16)
          for pc in pcs}
    mb = {pc: _mm(m_rb[pc], xa[pc]) for pc in pcs}
    bt = {pc: _mm_tn(pre[pc]["bh_st"], xa[pc]) for pc in pcs}
    for (ch, p) in pcs:
        pc = (ch, p)
        q = pre[pc]
        r_eff = q["r_st"] + mb[pc][:, :LANES]
        y_loc = mb[pc][:, LANES:] + mrkv[pc]
        g_eff = jnp.where(eye, q["g_l"], 0.0) + bt[pc][:, :LANES]
        f_loc = bt[pc][:, LANES:] + kv[pc]
        hs = _mm(jnp.concatenate([g_eff, r_eff], axis=0), hstate[p])
        hstate[p] = hs[:2 * ll, :] + f_loc
        y_st = hs[2 * ll:, :] + y_loc
        y = y_st[:ll, :] + y_st[ll:, :]
        rows, cols = q["rows"], q["cols"]
        mean = _head_sum(y, m0) * (1.0 / RWKV_HD)
        d = y - mean
        var = _head_sum(d * d, m0) * (1.0 / RWKV_HD)
        yn = d * lax.rsqrt(var + RWKV_LN_EPS) * lng_ref[:, cols] + lnb_ref[:, cols]
        bonus = _head_sum(q["rp"] * q["kp"] * rk_ref[:, cols], m0) * q["vp"]
        o_ref[rows, cols] = (yn + bonus) * g[rows, cols]


def _rwkv_branch(z, prm, l):
    b, t, _ = z.shape
    tm = min(4 * RW_L, t)
    hb = tm // HALO
    cr = COL_RW_R // C
    cl = COL_LORA // 256
    npair = C // LANES

    def tile(cb, w):
        return pl.BlockSpec((None, tm, w), lambda i, j: (i, j, cb))

    def halo(cb, w):
        return pl.BlockSpec((None, HALO, w), lambda i, j: (i, jnp.maximum(j * hb - 1, 0), cb))

    return pl.pallas_call(
        _rwkv_kernel,
        grid=(b, t // tm),
        in_specs=[tile(cr, C), tile(cr + 1, C), tile(cr + 2, C), tile(cl, 256),
                  halo(cr, C), halo(cr + 1, C), halo(cr + 2, C), halo(cl, 256)]
                 + [_layer_spec(a, l) for a in prm],
        out_specs=pl.BlockSpec((None, tm, C), lambda i, j: (i, j, 0)),
        out_shape=jax.ShapeDtypeStruct((b, t, C), F32),
        scratch_shapes=[pltpu.VMEM((npair, LANES, LANES), F32)],
        compiler_params=_cparams(("parallel", "arbitrary")),
        name="rwkv7",
    )(z, z, z, z, z, z, z, z, *prm)


def _prep_rwkv(mu_rkv, mu_w, mu_a, mu_g, w0, w2, a0, a2, g2, k_k, k_a, r_k, lnx_g, lnx_b):
    nl = mu_rkv.shape[0]
    row = lambda x: x.reshape(nl, 1, -1).astype(F32)
    mul = jnp.concatenate([mu_w, mu_a, mu_g], axis=-1)
    zw = lambda n: jnp.zeros((nl, n, C), F32)
    w2p = jnp.concatenate([w2, zw(A_LORA + G_LORA)], axis=1).astype(BF16)
    a2p = jnp.concatenate([zw(W_LORA), a2, zw(G_LORA)], axis=1).astype(BF16)
    g2p = jnp.concatenate([zw(W_LORA + A_LORA), g2], axis=1).astype(BF16)
    return (mu_rkv.astype(F32), row(mul), row(w0), row(a0), w2p, a2p, g2p, row(k_k), row(k_a), row(r_k),
            row(lnx_g), row(lnx_b))


SLOT = LANES
ROT_HALF = QK_ROPE // 2


def _mla_prep_kernel(cq_ref, ckv_ref, kr_ref, rc_ref, rs1_ref, rs2_ref, qn_ref, wq_ref, kvn_ref, wk_ref, wv_ref,
                     gq_ref, gk_ref, q_ref, k_ref, v_ref, v_scr):
    def rms(x, g, n):
        ms = jnp.sum(x * x, axis=-1, keepdims=True) * (1.0 / n)
        return x * lax.rsqrt(ms + EPS) * g

    def rope(x):
        return (x * rc_ref[...] + pltpu.roll(x, SLOT - ROT_HALF, axis=1) * rs1_ref[...]
                + pltpu.roll(x, ROT_HALF, axis=1) * rs2_ref[...])

    cq = rms(cq_ref[...].astype(F32), qn_ref[...], Q_LORA)
    ckv = rms(ckv_ref[...].astype(F32), kvn_ref[...], KV_LORA)
    qf = _mm(cq, wq_ref[...])
    kf = _mm(ckv, wk_ref[...])
    v_scr[...] = _mm(ckv, wv_ref[...])
    v_ref[...] = jnp.transpose(v_scr[...]).astype(BF16)
    kr = kr_ref[...].astype(F32)
    scale = QK_HD ** -0.5 * math.log2(math.e)
    for h in range(MLA_HEADS):
        cols = slice(h * SLOT, (h + 1) * SLOT)
        q = rope(rms(qf[:, cols], gq_ref[...], QK_HD)) * scale
        k = rope(rms(kf[:, cols] + kr, gk_ref[...], QK_HD))
        q_ref[:, cols] = q.astype(BF16)
        k_ref[:, cols] = k.astype(BF16)


def _flash_kernel(q_ref, k_ref, vt_ref, o_ref, m_scr, l_scr, acc):
    i = pl.program_id(1)
    j = pl.program_id(2)
    bq = q_ref.shape[0]
    bk = k_ref.shape[0]

    @pl.when(j == 0)
    def _():
        m_scr[...] = jnp.full_like(m_scr, -jnp.inf)
        l_scr[...] = jnp.zeros_like(l_scr)
        acc[...] = jnp.zeros_like(acc)

    def step(masked):
        if masked:
            causal = (lax.broadcasted_iota(jnp.int32, (bk, bq), 0) <= lax.broadcasted_iota(jnp.int32, (bk, bq), 1))
        def scores(h):
            cols = slice(h * SLOT, (h + 1) * SLOT)
            return _mm_nt(k_ref[:, cols], q_ref[:, cols])

        st_next = scores(0)
        for h in range(MLA_HEADS):
            rows = slice(h * V_HD, (h + 1) * V_HD)
            st = st_next
            if h + 1 < MLA_HEADS:
                st_next = scores(h + 1)
            if masked:
                st = jnp.where(causal, st, -jnp.inf)
            m_prev = m_scr[h:h + 1, :]
            m_new = jnp.maximum(m_prev, jnp.max(st, axis=0, keepdims=True))
            alpha = jnp.exp2(m_prev - m_new)
            pt = jnp.exp2(st - m_new)
            l_scr[h:h + 1, :] = alpha * l_scr[h:h + 1, :] + jnp.sum(pt, axis=0, keepdims=True)
            m_scr[h:h + 1, :] = m_new
            acc[rows, :] = acc[rows, :] * alpha + _mm(vt_ref[rows, :], pt)

    @pl.when(j < i)
    def _():
        step(False)

    @pl.when(j == i)
    def _():
        step(True)
        for h in range(MLA_HEADS):
            rows = slice(h * V_HD, (h + 1) * V_HD)
            acc[rows, :] = acc[rows, :] / l_scr[h:h + 1, :]
        o_ref[...] = jnp.transpose(acc[...])


def _mla_branch(z, prm, rope, l):
    rc, rs1, rs2 = rope
    b, t, _ = z.shape
    tm = min(512, t)
    hs = MLA_HEADS * SLOT

    def tab():
        return pl.BlockSpec((tm, SLOT), lambda i, j: (j, 0))

    q, k, vt = pl.pallas_call(
        _mla_prep_kernel,
        grid=(b, t // tm),
        in_specs=[pl.BlockSpec((None, tm, Q_LORA), lambda i, j: (i, j, COL_CQ // Q_LORA)),
                  pl.BlockSpec((None, tm, KV_LORA), lambda i, j: (i, j, COL_CKV // KV_LORA)),
                  pl.BlockSpec((None, tm, LANES), lambda i, j: (i, j, COL_KR // LANES)),
                  tab(), tab(), tab()] + [_layer_spec(a, l) for a in prm],
        out_specs=[pl.BlockSpec((None, tm, hs), lambda i, j: (i, j, 0)),
                   pl.BlockSpec((None, tm, hs), lambda i, j: (i, j, 0)),
                   pl.BlockSpec((None, C, tm), lambda i, j: (i, 0, j))],
        out_shape=[jax.ShapeDtypeStruct((b, t, hs), BF16), jax.ShapeDtypeStruct((b, t, hs), BF16),
                   jax.ShapeDtypeStruct((b, C, t), BF16)],
        scratch_shapes=[pltpu.VMEM((tm, C), F32)],
        compiler_params=_cparams(("parallel", "parallel")),
        name="mla_prep",
    )(z, z, z, rc, rs1, rs2, *prm)

    bq = min(512, t)
    nq = t // bq
    return pl.pallas_call(
        _flash_kernel,
        grid=(b, nq, nq),
        in_specs=[pl.BlockSpec((None, bq, hs), lambda bi, i, j: (bi, i, 0)),
                  pl.BlockSpec((None, bq, hs), lambda bi, i, j: (bi, jnp.minimum(j, i), 0)),
                  pl.BlockSpec((None, C, bq), lambda bi, i, j: (bi, 0, jnp.minimum(j, i)))],
        out_specs=pl.BlockSpec((None, bq, C), lambda bi, i, j: (bi, i, 0)),
        out_shape=jax.ShapeDtypeStruct((b, t, C), F32),
        scratch_shapes=[pltpu.VMEM((MLA_HEADS, bq), F32), pltpu.VMEM((MLA_HEADS, bq), F32),
                        pltpu.VMEM((C, bq), F32)],
        compiler_params=_cparams(("parallel", "parallel", "arbitrary")),
        name="mla_attn",
    )(q, k, vt)


def _rope_tables(t):
    pos = jnp.arange(t, dtype=F32)
    inv_freq = ROPE_THETA ** (-jnp.arange(0, QK_ROPE, 2, dtype=F32) / QK_ROPE)
    ang = pos[:, None] * inv_freq[None, :]
    cos, sin = jnp.cos(ang), jnp.sin(ang)
    z = lambda n: jnp.zeros((t, n), F32)
    pad = SLOT - QK_HD
    rc = jnp.concatenate([jnp.ones((t, QK_NOPE), F32), cos, cos, z(pad)], axis=1)
    rs1 = jnp.concatenate([z(QK_NOPE), -sin, z(ROT_HALF + pad)], axis=1)
    rs2 = jnp.concatenate([z(QK_NOPE + ROT_HALF), sin, z(pad)], axis=1)
    return rc, rs1, rs2


def _prep_mla(q_norm, w_uq, kv_norm, w_ukv, qk_norm_q, qk_norm_k):
    nl = q_norm.shape[0]
    nh = MLA_HEADS
    pad = SLOT - QK_HD
    wq = w_uq.reshape(nl, Q_LORA, nh, QK_HD)
    wq = jnp.pad(wq, ((0, 0), (0, 0), (0, 0), (0, pad))).reshape(nl, Q_LORA, nh * SLOT).astype(BF16)
    wkv = w_ukv.reshape(nl, KV_LORA, nh, QK_NOPE + V_HD)
    wk = jnp.pad(wkv[..., :QK_NOPE], ((0, 0), (0, 0), (0, 0), (0, SLOT - QK_NOPE)))
    wk = wk.reshape(nl, KV_LORA, nh * SLOT).astype(BF16)
    wv = wkv[..., QK_NOPE:].reshape(nl, KV_LORA, nh * V_HD).astype(BF16)
    slot = lambda g: jnp.pad(g.astype(F32), ((0, 0), (0, pad))).reshape(nl, 1, SLOT)
    return (q_norm.reshape(nl, 1, -1).astype(F32), wq, kv_norm.reshape(nl, 1, -1).astype(F32), wk, wv,
            slot(qk_norm_q), slot(qk_norm_k))


def _merge_kernel(y0_ref, y1_ref, y2_ref, y3_ref, gl_ref, x_ref, wb_ref, wo_ref, o_ref):
    d = x_ref.shape[1]
    merged = None
    for n, y_ref in enumerate((y0_ref, y1_ref, y2_ref, y3_ref)):
        term = _mm(y_ref[...], wb_ref[n]) * _sigmoid(gl_ref[:, n * d:(n + 1) * d].astype(F32))
        merged = term if merged is None else merged + term
    o_ref[...] = x_ref[...] + _mm(merged, wo_ref[...])


def _merge(ys, z2, x2, wb, wo, l):
    n, d = x2.shape
    tm = min(256, n)
    ytile = pl.BlockSpec((tm, C), lambda i: (i, 0))
    return pl.pallas_call(
        _merge_kernel,
        grid=(n // tm,),
        in_specs=[ytile, ytile, ytile, ytile,
                  pl.BlockSpec((tm, 4 * d), lambda i: (i, COL_GATE)),
                  pl.BlockSpec((tm, d), lambda i: (i, 0)),
                  _layer_spec(wb, l), _layer_spec(wo, l)],
        out_specs=pl.BlockSpec((tm, d), lambda i: (i, 0)),
        out_shape=jax.ShapeDtypeStruct((n, d), F32),
        compiler_params=_cparams(("parallel",)),
        name="merge",
    )(*ys, z2, x2, wb, wo)


def _mlp_kernel(x_ref, g_ref, w1_ref, w2_ref, o_ref, h_scr, acc):
    j = pl.program_id(1)

    @pl.when(j == 0)
    def _():
        x = x_ref[...]
        ms = jnp.mean(x * x, axis=-1, keepdims=True)
        h_scr[...] = (x * lax.rsqrt(ms + EPS) * g_ref[...]).astype(BF16)
        acc[...] = x

    a = jnp.maximum(jnp.dot(h_scr[...], w1_ref[...], preferred_element_type=F32), 0.0)
    acc[...] += _mm(a * a, w2_ref[...])

    @pl.when(j == pl.num_programs(1) - 1)
    def _():
        o_ref[...] = acc[...]


def _mlp(x2, g, w1, w2, l):
    n, d = x2.shape
    dff = w1.shape[2]
    tm = min(1024, n)
    tf = 512
    return pl.pallas_call(
        _mlp_kernel,
        grid=(n // tm, dff // tf),
        in_specs=[pl.BlockSpec((tm, d), lambda i, j: (i, 0)),
                  pl.BlockSpec((None, 1, d), lambda i, j: (l, 0, 0)),
                  pl.BlockSpec((None, d, tf), lambda i, j: (l, 0, j)),
                  pl.BlockSpec((None, tf, d), lambda i, j: (l, j, 0))],
        out_specs=pl.BlockSpec((tm, d), lambda i, j: (i, 0)),
        out_shape=jax.ShapeDtypeStruct((n, d), F32),
        scratch_shapes=[pltpu.VMEM((tm, d), BF16), pltpu.VMEM((tm, d), F32)],
        compiler_params=_cparams(("parallel", "arbitrary")),
        name="mlp",
    )(x2, g, w1, w2)


def _prep_w_in(w_in):
    nl, d, _ = w_in.shape
    gate0 = 6 * C + W_LORA + A_LORA + G_LORA + Q_LORA + KV_LORA + QK_ROPE
    kr0 = gate0 - QK_ROPE
    parts = [
        w_in[:, :, gate0:],
        w_in[:, :, :kr0],
        jnp.zeros((nl, d, QK_NOPE), w_in.dtype),
        w_in[:, :, kr0:gate0],
        jnp.zeros((nl, d, D_INP - COL_KR - QK_NOPE - QK_ROPE), w_in.dtype),
    ]
    return jnp.concatenate(parts, axis=-1).astype(BF16)


def _prep_lru_gate(gate_w):
    nl, _, nh, hd, _ = gate_w.shape
    eye = jnp.eye(nh, dtype=gate_w.dtype)
    dense = jnp.einsum("lghij,hk->lghikj", gate_w, eye).reshape(nl, 2, nh * hd, nh * hd)
    return jnp.concatenate([dense[:, 0], dense[:, 1]], axis=-1).astype(BF16)


def kernel(x, norm_mix, w_in, lru_conv_w, lru_conv_b, lru_gate_w, lru_gate_b, lru_lambda, s5_a_re, s5_a_im, s5_b_re, s5_b_im, s5_c_re, s5_c_im, s5_d, s5_log_dt, s5_w_glu, rwkv_mu_rkv, rwkv_mu_w, rwkv_mu_a, rwkv_mu_g, rwkv_w0, rwkv_w2, rwkv_a0, rwkv_a2, rwkv_g2, rwkv_k_k, rwkv_k_a, rwkv_r_k, rwkv_lnx_g, rwkv_lnx_b, mla_q_norm, mla_w_uq, mla_kv_norm, mla_w_ukv, mla_qk_norm_q, mla_qk_norm_k, w_branch, w_out, norm_mlp, w_ff1, w_ff2):
    b, t, d = x.shape
    depth = w_in.shape[0]
    n = b * t
    row = lambda a: a.reshape(depth, 1, -1).astype(F32)

    w_in_p = _prep_w_in(w_in)
    lru_gw = _prep_lru_gate(lru_gate_w)
    s5_tabs = _prep_s5(s5_a_re, s5_a_im, s5_b_re, s5_b_im, s5_c_re, s5_c_im, s5_log_dt, t)
    rw_prm = _prep_rwkv(rwkv_mu_rkv, rwkv_mu_w, rwkv_mu_a, rwkv_mu_g, rwkv_w0, rwkv_w2, rwkv_a0, rwkv_a2, rwkv_g2,
                        rwkv_k_k, rwkv_k_a, rwkv_r_k, rwkv_lnx_g, rwkv_lnx_b)
    mla_prm = _prep_mla(mla_q_norm, mla_w_uq, mla_kv_norm, mla_w_ukv, mla_qk_norm_q, mla_qk_norm_k)
    rope = _rope_tables(t)
    g_mix, g_mlp = row(norm_mix), row(norm_mlp)
    conv_b, gate_b, lam, s5_dr = row(lru_conv_b), row(lru_gate_b), row(lru_lambda), row(s5_d)
    w_glu, wb, wo = s5_w_glu.astype(BF16), w_branch.astype(BF16), w_out.astype(BF16)
    w1, w2 = w_ff1.astype(BF16), w_ff2.astype(BF16)

    lru_prm = (lru_conv_w.astype(F32), conv_b, lru_gw, gate_b, lam)
    x2 = x.reshape(n, d)
    for l in range(depth):
        z2, u2 = _inproj(x2, g_mix, w_in_p, l)
        z = z2.reshape(b, t, -1)
        y_lru = _lru_branch(z, lru_prm, l)
        y_s5 = _s5_branch(u2.reshape(b, t, C), s5_tabs, s5_dr, w_glu, l)
        y_rw = _rwkv_branch(z, rw_prm, l)
        y_mla = _mla_branch(z, mla_prm, rope, l)
        ys = [y.reshape(n, C) for y in (y_lru, y_s5, y_rw, y_mla)]
        x2 = _merge(ys, z2, x2, wb, wo, l)
        x2 = _mlp(x2, g_mlp, w1, w2, l)
    return x2.reshape(b, t, d)
```

```python
import functools
import math

import jax
import jax.numpy as jnp
import numpy as np
from jax import lax
from jax.experimental import pallas as pl
from jax.experimental.pallas import tpu as pltpu

F32 = jnp.float32
BF16 = jnp.bfloat16

EPS = 1e-6
LRU_HEADS = 8
CONV_W = 4
LRU_C = 8.0
S5_GROUP = 16
S5_STATE = 64
RWKV_HD = 64
W_LORA, A_LORA, G_LORA = 64, 64, 128
RWKV_LN_EPS = 64e-5
MLA_HEADS = 8
QK_NOPE, QK_ROPE, V_HD = 64, 32, 64
QK_HD = QK_NOPE + QK_ROPE
Q_LORA, KV_LORA = 256, 128
ROPE_THETA = 10000.0

LANES = 128
SUBLANES = 8
VMEM_LIMIT = 56 * 1024 * 1024

D_MODEL = 1024
C = D_MODEL // 2
COL_GATE = 0
COL_LRU_X = 4 * D_MODEL
COL_LRU_G = COL_LRU_X + C
COL_S5_U = COL_LRU_G + C
COL_RW_R = COL_S5_U + C
COL_RW_K = COL_RW_R + C
COL_RW_V = COL_RW_K + C
COL_LORA = COL_RW_V + C
COL_CQ = COL_LORA + 256
COL_CKV = COL_CQ + Q_LORA
COL_KR = COL_CKV + KV_LORA
D_INP = 8 * D_MODEL


def _cparams(sem):
    return pltpu.CompilerParams(dimension_semantics=sem, vmem_limit_bytes=VMEM_LIMIT)


def _gelu_tanh(x):
    return 0.5 * x * (1.0 + jnp.tanh(math.sqrt(2.0 / math.pi) * (x + 0.044715 * (x * x * x))))


def _sigmoid(x):
    return 1.0 / (1.0 + jnp.exp(-x))


def _softplus(x):
    return jnp.maximum(x, 0.0) + jnp.log(1.0 + jnp.exp(-jnp.abs(x)))


HALO = 16


def _shift_rows(x, halo, k):
    xe = jnp.concatenate([halo, x], axis=0)
    n = x.shape[0]
    return xe[HALO - k:HALO - k + n, :]


def _layer_spec(arr, l):
    zeros = (0,) * (arr.ndim - 1)
    return pl.BlockSpec((None,) + arr.shape[1:], lambda *idx: (l,) + zeros)


def _inproj_kernel(x_ref, g_ref, w_ref, o_ref, u_ref, h_scr):
    j = pl.program_id(1)

    @pl.when(j == 0)
    def _():
        x = x_ref[...]
        ms = jnp.mean(x * x, axis=-1, keepdims=True)
        h_scr[...] = (x * lax.rsqrt(ms + EPS) * g_ref[...]).astype(BF16)

    y = jnp.dot(h_scr[...], w_ref[...], preferred_element_type=F32)
    o_ref[...] = y.astype(BF16)
    tn = w_ref.shape[1]
    u0 = COL_S5_U % tn

    @pl.when(j == COL_S5_U // tn)
    def _():
        u_ref[...] = y[:, u0:u0 + C]


INPROJ_TN = 1024


def _inproj(x2, g, w, l):
    n, d = x2.shape
    dn = w.shape[2]
    tm = min(1024, n)
    tn = INPROJ_TN
    assert COL_S5_U % tn + C <= tn and dn % tn == 0
    return pl.pallas_call(
        _inproj_kernel,
        grid=(n // tm, dn // tn),
        in_specs=[
            pl.BlockSpec((tm, d), lambda i, j: (i, 0)),
            pl.BlockSpec((None, 1, d), lambda i, j: (l, 0, 0)),
            pl.BlockSpec((None, d, tn), lambda i, j: (l, 0, j)),
        ],
        out_specs=[pl.BlockSpec((tm, tn), lambda i, j: (i, j)),
                   pl.BlockSpec((tm, C), lambda i, j: (i, 0))],
        out_shape=[jax.ShapeDtypeStruct((n, dn), BF16), jax.ShapeDtypeStruct((n, C), F32)],
        scratch_shapes=[pltpu.VMEM((tm, d), BF16)],
        compiler_params=_cparams(("parallel", "arbitrary")),
        name="inproj",
    )(x2, g, w)


def _lru_kernel(x_ref, xh_ref, g_ref, cw_ref, cb_ref, gw_ref, gb_ref, lam_ref, o_ref, carry):
    t = pl.program_id(1)
    tm, c = x_ref.shape

    @pl.when(t == 0)
    def _():
        carry[...] = jnp.zeros_like(carry)

    x = x_ref[...].astype(F32)
    halo = jnp.where(t == 0, 0.0, xh_ref[...].astype(F32))
    cw = cw_ref[...]
    xc = cb_ref[...] + cw[CONV_W - 1:CONV_W, :] * x
    for k in range(1, CONV_W):
        xc = xc + cw[CONV_W - 1 - k:CONV_W - k, :] * _shift_rows(x, halo, k)

    gates = jnp.dot(xc.astype(BF16), gw_ref[...], preferred_element_type=F32) + gb_ref[...]
    r = _sigmoid(gates[:, :c])
    i = _sigmoid(gates[:, c:])
    log_a = (-LRU_C) * r * _softplus(-lam_ref[...])
    a = jnp.exp(log_a)
    u = jnp.sqrt(1.0 - jnp.exp(2.0 * log_a)) * (i * xc)

    row = lax.broadcasted_iota(jnp.int32, (tm, c), 0)
    s = 1
    while s < tm:
        if s % SUBLANES:
            a_sh = jnp.where(row >= s, pltpu.roll(a, s, axis=0), 1.0)
            u_sh = jnp.where(row >= s, pltpu.roll(u, s, axis=0), 0.0)
        else:
            a_sh = jnp.concatenate([jnp.ones((s, c), F32), a[:tm - s, :]], axis=0)
            u_sh = jnp.concatenate([jnp.zeros((s, c), F32), u[:tm - s, :]], axis=0)
        u = u + a * u_sh
        a = a * a_sh
        s *= 2
    h = u + a * carry[SUBLANES - 1:SUBLANES, :]
    carry[...] = h[tm - SUBLANES:, :]
    o_ref[...] = (h * _gelu_tanh(g_ref[...].astype(F32))).astype(o_ref.dtype)


def _lru_branch(z, prm, l):
    b, t, _ = z.shape
    tm = min(256, t)
    cb = COL_LRU_X // C
    hb = tm // HALO
    return pl.pallas_call(
        _lru_kernel,
        grid=(b, t // tm),
        in_specs=[
            pl.BlockSpec((None, tm, C), lambda i, j: (i, j, cb)),
            pl.BlockSpec((None, HALO, C), lambda i, j: (i, jnp.maximum(j * hb - 1, 0), cb)),
            pl.BlockSpec((None, tm, C), lambda i, j: (i, j, cb + 1)),
        ] + [_layer_spec(a, l) for a in prm],
        out_specs=pl.BlockSpec((None, tm, C), lambda i, j: (i, j, 0)),
        out_shape=jax.ShapeDtypeStruct((b, t, C), BF16),
        scratch_shapes=[pltpu.VMEM((SUBLANES, C), F32)],
        compiler_params=_cparams(("parallel", "arbitrary")),
        name="rglru",
    )(z, z, z, *prm)


S5_L = 8
S5_OCT = LANES // S5_GROUP


def _s5_kernel(u_ref, win_ref, bd_ref, wout_ref, lstep_ref, y_ref, uf, wintra, carry):
    t = pl.program_id(2)
    tm = u_ref.shape[0]
    nc = tm // S5_L
    ns = carry.shape[1]
    half = ns // 2

    @pl.when((pl.program_id(1) == 0) & (t == 0))
    def _():
        wintra[...] = jnp.zeros_like(wintra)
        for i in range(S5_L):
            for j in range(i, S5_L):
                wintra[i * LANES:(i + 1) * LANES, j * LANES:(j + 1) * LANES] = bd_ref[j - i]

    @pl.when(t == 0)
    def _():
        carry[...] = jnp.zeros_like(carry)

    for i in range(S5_L):
        uf[:, i * LANES:(i + 1) * LANES] = u_ref[pl.ds(i, nc, stride=S5_L), :].astype(BF16)
    ufv = uf[...]
    s = jnp.dot(ufv, win_ref[...], preferred_element_type=F32)

    def cmul(v, k):
        sw = jnp.concatenate([v[:, half:], v[:, :half]], axis=1)
        return v * lstep_ref[2 * k:2 * k + 1, :] + sw * lstep_ref[2 * k + 1:2 * k + 2, :]

    row = lax.broadcasted_iota(jnp.int32, (nc, ns), 0)
    cin = carry[0:1, :]
    s = s + jnp.where(row == 0, cmul(cin, 0), 0.0)
    k = 0
    st = 1
    while st < nc:
        sh = jnp.where(row >= st, pltpu.roll(s, st, axis=0), 0.0)
        s = s + cmul(sh, k)
        k += 1
        st *= 2
    carry[...] = jnp.broadcast_to(s[nc - 1:nc, :], carry.shape)
    sprev = jnp.where(row >= 1, pltpu.roll(s, 1, axis=0), cin)

    yf = jnp.dot(ufv, wintra[...], preferred_element_type=F32)
    yf = yf + jnp.dot(sprev.astype(BF16), wout_ref[...], preferred_element_type=F32)
    for j in range(S5_L):
        y_ref[pl.ds(j, nc, stride=S5_L), :] = yf[:, j * LANES:(j + 1) * LANES]


def _s5_tail_kernel(y_ref, u_ref, d_ref, w_ref, o_ref):
    c = u_ref.shape[1]
    y = _gelu_tanh(y_ref[...] + d_ref[...] * u_ref[...])
    z = jnp.dot(y.astype(BF16), w_ref[...], preferred_element_type=F32)
    o_ref[...] = (z[:, :c] * _sigmoid(z[:, c:])).astype(o_ref.dtype)


def _s5_tile(t):
    return min(2048, t)


def _s5_branch(u, tabs, d_skip, w_glu, l):
    win, bd, wout, lstep = tabs
    b, t, _ = u.shape
    tm = _s5_tile(t)
    nc = tm // S5_L
    noct = C // LANES
    ns = win.shape[-1]
    fl = S5_L * LANES

    def table(arr):
        zeros = (0,) * (arr.ndim - 2)
        return pl.BlockSpec((None, None) + arr.shape[2:], lambda q, i, j: (l, q) + zeros)

    y = pl.pallas_call(
        _s5_kernel,
        grid=(noct, b, t // tm),
        in_specs=[pl.BlockSpec((None, tm, LANES), lambda q, i, j: (i, j, q)),
                  table(win), table(bd), table(wout), table(lstep)],
        out_specs=pl.BlockSpec((None, tm, LANES), lambda q, i, j: (i, j, q)),
        out_shape=jax.ShapeDtypeStruct((b, t, C), F32),
        scratch_shapes=[pltpu.VMEM((nc, fl), BF16), pltpu.VMEM((fl, fl), BF16), pltpu.VMEM((SUBLANES, ns), F32)],
        compiler_params=_cparams(("arbitrary", "arbitrary", "arbitrary")),
        name="s5_ssm",
    )(u, win, bd, wout, lstep)

    tt = min(512, t)
    return pl.pallas_call(
        _s5_tail_kernel,
        grid=(b, t // tt),
        in_specs=[
            pl.BlockSpec((None, tt, C), lambda i, j: (i, j, 0)),
            pl.BlockSpec((None, tt, C), lambda i, j: (i, j, 0)),
            _layer_spec(d_skip, l),
            _layer_spec(w_glu, l),
        ],
        out_specs=pl.BlockSpec((None, tt, C), lambda i, j: (i, j, 0)),
        out_shape=jax.ShapeDtypeStruct((b, t, C), BF16),
        compiler_params=_cparams(("parallel", "parallel")),
        name="s5_glu",
    )(y, u, d_skip, w_glu)


def _prep_s5(a_re, a_im, b_re, b_im, c_re, c_im, log_dt, t):
    nl, g, p = a_re.shape
    m = S5_GROUP
    noct = g // S5_OCT
    nc = _s5_tile(t) // S5_L
    lam = lax.complex(a_re.astype(F32), a_im.astype(F32))
    dt = jnp.exp(log_dt.astype(F32))[..., None]
    ldt = lam * dt
    lam_bar = jnp.exp(ldt)
    b_bar = ((lam_bar - 1.0) / lam)[..., None] * lax.complex(b_re.astype(F32), b_im.astype(F32))
    cc = lax.complex(c_re.astype(F32), c_im.astype(F32))

    def powers(e):
        return jnp.exp(ldt[..., None] * e.astype(F32).reshape((1, 1, 1, -1)))

    steps = jnp.arange(S5_L)
    same = jnp.eye(S5_OCT, dtype=bool)

    def octet_bd(x):
        x = jnp.moveaxis(x, 2, 3)
        x = jnp.expand_dims(x, 6)
        x = jnp.where(same.reshape(1, 1, 1, S5_OCT, 1, 1, S5_OCT, 1), x, 0.0)
        nrow = int(np.prod(x.shape[2:5]))
        return x.astype(BF16).reshape(nl, noct, nrow, -1)

    kd = jnp.einsum("lgop,lgpd,lgpi->lgdio", cc, powers(steps), b_bar).real
    kd = kd.reshape(nl, noct, S5_OCT, S5_L, m, 1, m)
    bd = octet_bd(kd).reshape(nl, noct, S5_L, LANES, LANES)

    wi = powers(S5_L - 1 - steps)[..., None] * b_bar[:, :, :, None, :]
    wi = jnp.stack([wi.real, wi.imag], axis=2)
    wi = jnp.transpose(wi, (0, 1, 4, 5, 2, 3)).reshape(nl, noct, S5_OCT, S5_L, m, 2, p)
    win = octet_bd(wi)

    wo = cc[:, :, :, :, None] * powers(steps + 1)[:, :, None, :, :]
    wo = jnp.stack([wo.real, -wo.imag], axis=2)
    wo = jnp.transpose(wo, (0, 1, 2, 4, 5, 3)).reshape(nl, noct, S5_OCT, 2, p, S5_L, m)
    wout = octet_bd(wo)

    nsteps = max(int(math.log2(nc)), 1)
    pw = powers(S5_L * (2 ** jnp.arange(nsteps)))
    pw = jnp.transpose(pw, (0, 3, 1, 2)).reshape(nl, nsteps, noct, S5_OCT * p)
    pw = jnp.moveaxis(pw, 2, 1)
    ar = jnp.concatenate([pw.real, pw.real], axis=-1)
    ai = jnp.concatenate([-pw.imag, pw.imag], axis=-1)
    lstep = jnp.stack([ar, ai], axis=3).reshape(nl, noct, 2 * nsteps, -1)
    return win, bd, wout, lstep


RW_L = 64
HIGHEST = lax.Precision.HIGHEST


def _mm(a, b, dims=(((1,), (0,)), ((), ()))):
    return lax.dot_general(a.astype(BF16), b.astype(BF16), dims, preferred_element_type=F32)


def _mm_nt(a, b):
    return _mm(a, b, (((1,), (1,)), ((), ())))


def _mm_tn(a, b):
    return _mm(a, b, (((0,), (0,)), ((), ())))


def _mm_exact(a, b):
    return lax.dot_general(a, b, (((1,), (0,)), ((), ())), precision=HIGHEST, preferred_element_type=F32)


def _head_sum(x, m0):
    s0 = jnp.sum(jnp.where(m0, x, 0.0), axis=-1, keepdims=True)
    s1 = jnp.sum(jnp.where(m0, 0.0, x), axis=-1, keepdims=True)
    return jnp.where(m0, s0, s1)


def _rwkv_kernel(r_ref, k_ref, v_ref, lo_ref, rh_ref, kh_ref, vh_ref, loh_ref, mu_ref, mul_ref, w0_ref, a0_ref,
                 w2_ref, a2_ref, g2_ref, kk_ref, ka_ref, rk_ref, lng_ref, lnb_ref, o_ref, hstate):
    t = pl.program_id(1)
    tm, c = r_ref.shape
    npair = c // LANES
    ll = RW_L

    @pl.when(t == 0)
    def _():
        hstate[...] = jnp.zeros_like(hstate)

    def mix(ref, href, mu):
        p = ref[...].astype(F32)
        prev = _shift_rows(p, jnp.where(t == 0, 0.0, href[...].astype(F32)), 1)
        return p + (prev - p) * mu

    mu = mu_ref[...]
    r = mix(r_ref, rh_ref, mu[0:1, :])
    k = mix(k_ref, kh_ref, mu[1:2, :])
    v = mix(v_ref, vh_ref, mu[2:3, :])
    lo = mix(lo_ref, loh_ref, mul_ref[...])
    wlog = -_softplus(-(w0_ref[...] + _mm(jnp.tanh(lo), w2_ref[...]))) - 0.5
    lw = -jnp.exp(wlog)
    a = _sigmoid(a0_ref[...] + _mm(lo, a2_ref[...]))
    g = _mm(_sigmoid(lo), g2_ref[...])
    kk = k * kk_ref[...]
    k = k * (1.0 + (a - 1.0) * ka_ref[...])

    lane = lax.broadcasted_iota(jnp.int32, (ll, LANES), 1)
    m0 = lane < RWKV_HD
    ri = lax.broadcasted_iota(jnp.int32, (2 * ll, 2 * ll), 0)
    ci = lax.broadcasted_iota(jnp.int32, (2 * ll, 2 * ll), 1)
    same = (ri >= ll) == (ci >= ll)
    strict = same & (ri > ci)
    incl = same & (ri >= ci)
    eye = ri == ci
    tri = (lax.broadcasted_iota(jnp.int32, (ll, ll), 0) >= lax.broadcasted_iota(jnp.int32, (ll, ll), 1)).astype(F32)

    def st(x):
        return jnp.concatenate([jnp.where(m0, x, 0.0), jnp.where(m0, 0.0, x)], axis=0)

    nch = tm // ll
    pcs = [(ch, p) for ch in range(nch) for p in range(npair)]
    cum_all = [_mm_exact(tri, lw[ch * ll:(ch + 1) * ll, :]) for ch in range(nch)]
    pre = {}
    for (ch, p) in pcs:
        rows, cols = slice(ch * ll, (ch + 1) * ll), slice(p * LANES, (p + 1) * LANES)
        rp, kp, vp, ap, lwp, cum = r[rows, cols], k[rows, cols], v[rows, cols], a[rows, cols], lw[rows, cols], cum_all[ch][:, cols]
        kkp = kk[rows, cols]
        kkp = kkp * lax.rsqrt(_head_sum(kkp * kkp, m0) + 1e-12)
        cum_l = cum[ll - 1:ll, :]
        e_neg = jnp.exp(-cum)
        e_last = jnp.exp(cum_l - cum)
        kb = kkp * ap
        pre[ch, p] = dict(
            rows=rows, cols=cols, rp=rp, kp=kp, vp=vp,
            a_st=st(-kkp * jnp.exp(cum - lwp)).astype(BF16), r_st=st(rp * jnp.exp(cum)),
            v_st=st(vp).astype(BF16), b_st=st(kb * e_neg).astype(BF16), k_st=st(kp * e_neg).astype(BF16),
            bh_st=st(kb * e_last).astype(BF16), kh_st=st(kp * e_last).astype(BF16), g_l=jnp.exp(cum_l))

    gram = {pc: _mm_nt(jnp.concatenate([pre[pc]["a_st"], pre[pc]["r_st"].astype(BF16)], axis=0),
                       jnp.concatenate([pre[pc]["b_st"], pre[pc]["k_st"]], axis=0)) for pc in pcs}
    a_ab = {pc: jnp.where(strict, gram[pc][:2 * ll, :2 * ll], 0.0) for pc in pcs}
    a_ak = {pc: jnp.where(strict, gram[pc][:2 * ll, 2 * ll:], 0.0).astype(BF16) for pc in pcs}
    m_rb = {pc: jnp.where(incl, gram[pc][2 * ll:, :2 * ll], 0.0).astype(BF16) for pc in pcs}
    m_rk = {pc: jnp.where(incl, gram[pc][2 * ll:, 2 * ll:], 0.0).astype(BF16) for pc in pcs}
    akv = {pc: _mm(a_ak[pc], pre[pc]["v_st"]) for pc in pcs}
    tinv = {pc: jnp.where(eye, 1.0, a_ab[pc]) for pc in pcs}
    pw = {pc: a_ab[pc].astype(BF16) for pc in pcs}
    for _ in range(int(math.log2(ll)) - 1):
        pw = {pc: _mm(pw[pc], pw[pc]).astype(BF16) for pc in pcs}
        tinv = {pc: tinv[pc] + _mm(tinv[pc], pw[pc]) for pc in pcs}
    xa = {pc: _mm(tinv[pc], jnp.concatenate([pre[pc]["a_st"], akv[pc].astype(BF16)], axis=1)).astype(BF16)
          for pc in pcs}
    ma = {pc: _mm(m_rb[pc], xa[pc][:, :LANES]) for pc in pcs}
    y_loc = {pc: _mm(jnp.concatenate([m_rb[pc], m_rk[pc]], axis=1),
                     jnp.concatenate([xa[pc][:, LANES:], pre[pc]["v_st"]], axis=0)) for pc in pcs}
    ba = {pc: _mm_tn(pre[pc]["bh_st"], xa[pc][:, :LANES]) for pc in pcs}
    f_loc = {pc: _mm_tn(jnp.concatenate([pre[pc]["bh_st"], pre[pc]["kh_st"]], axis=0),
                        jnp.concatenate([xa[pc][:, LANES:], pre[pc]["v_st"]], axis=0)) for pc in pcs}
    for (ch, p) in pcs:
        pc = (ch, p)
        q = pre[pc]
        r_eff = q["r_st"] + ma[pc]
        g_eff = jnp.where(eye, q["g_l"], 0.0) + ba[pc]
        hs = _mm(jnp.concatenate([g_eff, r_eff], axis=0), hstate[p])
        hstate[p] = hs[:2 * ll, :] + f_loc[pc]
        y_st = hs[2 * ll:, :] + y_loc[pc]
        y = y_st[:ll, :] + y_st[ll:, :]
        rows, cols = q["rows"], q["cols"]
        mean = _head_sum(y, m0) * (1.0 / RWKV_HD)
        d = y - mean
        var = _head_sum(d * d, m0) * (1.0 / RWKV_HD)
        yn = d * lax.rsqrt(var + RWKV_LN_EPS) * lng_ref[:, cols] + lnb_ref[:, cols]
        bonus = _head_sum(q["rp"] * q["kp"] * rk_ref[:, cols], m0) * q["vp"]
        o_ref[rows, cols] = ((yn + bonus) * g[rows, cols]).astype(o_ref.dtype)


def _rwkv_branch(z, prm, l):
    b, t, _ = z.shape
    tm = min(4 * RW_L, t)
    hb = tm // HALO
    cr = COL_RW_R // C
    cl = COL_LORA // 256
    npair = C // LANES

    def tile(cb, w):
        return pl.BlockSpec((None, tm, w), lambda i, j: (i, j, cb))

    def halo(cb, w):
        return pl.BlockSpec((None, HALO, w), lambda i, j: (i, jnp.maximum(j * hb - 1, 0), cb))

    return pl.pallas_call(
        _rwkv_kernel,
        grid=(b, t // tm),
        in_specs=[tile(cr, C), tile(cr + 1, C), tile(cr + 2, C), tile(cl, 256),
                  halo(cr, C), halo(cr + 1, C), halo(cr + 2, C), halo(cl, 256)]
                 + [_layer_spec(a, l) for a in prm],
        out_specs=pl.BlockSpec((None, tm, C), lambda i, j: (i, j, 0)),
        out_shape=jax.ShapeDtypeStruct((b, t, C), BF16),
        scratch_shapes=[pltpu.VMEM((npair, LANES, LANES), F32)],
        compiler_params=_cparams(("parallel", "arbitrary")),
        name="rwkv7",
    )(z, z, z, z, z, z, z, z, *prm)


def _prep_rwkv(mu_rkv, mu_w, mu_a, mu_g, w0, w2, a0, a2, g2, k_k, k_a, r_k, lnx_g, lnx_b):
    nl = mu_rkv.shape[0]
    row = lambda x: x.reshape(nl, 1, -1).astype(F32)
    mul = jnp.concatenate([mu_w, mu_a, mu_g], axis=-1)
    zw = lambda n: jnp.zeros((nl, n, C), F32)
    w2p = jnp.concatenate([w2, zw(A_LORA + G_LORA)], axis=1).astype(BF16)
    a2p = jnp.concatenate([zw(W_LORA), a2, zw(G_LORA)], axis=1).astype(BF16)
    g2p = jnp.concatenate([zw(W_LORA + A_LORA), g2], axis=1).astype(BF16)
    return (mu_rkv.astype(F32), row(mul), row(w0), row(a0), w2p, a2p, g2p, row(k_k), row(k_a), row(r_k),
            row(lnx_g), row(lnx_b))


SLOT = LANES
ROT_HALF = QK_ROPE // 2
VT_ROWS = V_HD + 16
FLASH_AHEAD = 1
FLASH_QSPLIT = 1


def _mla_prep_kernel(cq_ref, ckv_ref, kr_ref, rc_ref, rs1_ref, rs2_ref, qn_ref, wq_ref, kvn_ref, wk_ref, wv_ref,
                     gq_ref, gk_ref, q_ref, k_ref, v_ref, v_scr):
    def rms(x, g, n):
        ms = jnp.sum(x * x, axis=-1, keepdims=True) * (1.0 / n)
        return x * lax.rsqrt(ms + EPS) * g

    def rope(x):
        return (x * rc_ref[...] + pltpu.roll(x, SLOT - ROT_HALF, axis=1) * rs1_ref[...]
                + pltpu.roll(x, ROT_HALF, axis=1) * rs2_ref[...])

    cq = rms(cq_ref[...].astype(F32), qn_ref[...], Q_LORA)
    ckv = rms(ckv_ref[...].astype(F32), kvn_ref[...], KV_LORA)
    qf = _mm(cq, wq_ref[...])
    kf = _mm(ckv, wk_ref[...])
    v_scr[...] = _mm(ckv, wv_ref[...])
    vt = jnp.transpose(v_scr[...]).astype(BF16)
    ones = jnp.ones((VT_ROWS - V_HD, vt.shape[1]), BF16)
    for h in range(MLA_HEADS):
        v_ref[h * VT_ROWS:h * VT_ROWS + V_HD, :] = vt[h * V_HD:(h + 1) * V_HD, :]
        v_ref[h * VT_ROWS + V_HD:(h + 1) * VT_ROWS, :] = ones
    kr = kr_ref[...].astype(F32)
    scale = QK_HD ** -0.5 * math.log2(math.e)
    for h in range(MLA_HEADS):
        cols = slice(h * SLOT, (h + 1) * SLOT)
        q = rope(rms(qf[:, cols], gq_ref[...], QK_HD)) * scale
        k = rope(rms(kf[:, cols] + kr, gk_ref[...], QK_HD))
        q_ref[:, cols] = q.astype(BF16)
        k_ref[:, cols] = k.astype(BF16)


def _flash_kernel(q_ref, k_ref, vt_ref, o_ref, m_scr, acc):
    i = pl.program_id(1)
    j = pl.program_id(2)
    bq = q_ref.shape[0]
    bk = k_ref.shape[0]

    @pl.when(j == 0)
    def _():
        m_scr[...] = jnp.full_like(m_scr, -jnp.inf)
        acc[...] = jnp.zeros_like(acc)

    def step(masked):
        if masked:
            causal = (lax.broadcasted_iota(jnp.int32, (bk, bq), 0) <= lax.broadcasted_iota(jnp.int32, (bk, bq), 1))
        qw = bq // FLASH_QSPLIT
        units = [(h, c) for h in range(MLA_HEADS) for c in range(FLASH_QSPLIT)]

        def scores(u):
            h, c = u
            cols = slice(h * SLOT, (h + 1) * SLOT)
            return _mm_nt(k_ref[:, cols], q_ref[c * qw:(c + 1) * qw, cols])

        pending = [scores(u) for u in units[:FLASH_AHEAD]]
        for n, (h, c) in enumerate(units):
            rows = slice(h * VT_ROWS, (h + 1) * VT_ROWS)
            qs = slice(c * qw, (c + 1) * qw)
            st = pending.pop(0)
            if n + FLASH_AHEAD < len(units):
                pending.append(scores(units[n + FLASH_AHEAD]))
            if masked:
                st = jnp.where(causal[:, qs], st, -jnp.inf)
            m_prev = m_scr[h:h + 1, qs]
            m_new = jnp.maximum(m_prev, jnp.max(st, axis=0, keepdims=True))
            alpha = jnp.exp2(m_prev - m_new)
            pt = jnp.exp2((st - m_new).astype(BF16))
            m_scr[h:h + 1, qs] = m_new
            acc[rows, qs] = acc[rows, qs] * alpha + _mm(vt_ref[rows, :], pt)

    @pl.when(j < i)
    def _():
        step(False)

    @pl.when(j == i)
    def _():
        step(True)
        outs = []
        for h in range(MLA_HEADS):
            r0 = h * VT_ROWS
            outs.append(acc[r0:r0 + V_HD, :] / acc[r0 + V_HD:r0 + V_HD + 1, :])
        o_ref[...] = jnp.transpose(jnp.concatenate(outs, axis=0)).astype(o_ref.dtype)


def _mla_branch(z, prm, rope, l):
    rc, rs1, rs2 = rope
    b, t, _ = z.shape
    tm = min(512, t)
    hs = MLA_HEADS * SLOT
    vr = MLA_HEADS * VT_ROWS

    def tab():
        return pl.BlockSpec((tm, SLOT), lambda i, j: (j, 0))

    q, k, vt = pl.pallas_call(
        _mla_prep_kernel,
        grid=(b, t // tm),
        in_specs=[pl.BlockSpec((None, tm, Q_LORA), lambda i, j: (i, j, COL_CQ // Q_LORA)),
                  pl.BlockSpec((None, tm, KV_LORA), lambda i, j: (i, j, COL_CKV // KV_LORA)),
                  pl.BlockSpec((None, tm, LANES), lambda i, j: (i, j, COL_KR // LANES)),
                  tab(), tab(), tab()] + [_layer_spec(a, l) for a in prm],
        out_specs=[pl.BlockSpec((None, tm, hs), lambda i, j: (i, j, 0)),
                   pl.BlockSpec((None, tm, hs), lambda i, j: (i, j, 0)),
                   pl.BlockSpec((None, vr, tm), lambda i, j: (i, 0, j))],
        out_shape=[jax.ShapeDtypeStruct((b, t, hs), BF16), jax.ShapeDtypeStruct((b, t, hs), BF16),
                   jax.ShapeDtypeStruct((b, vr, t), BF16)],
        scratch_shapes=[pltpu.VMEM((tm, C), F32)],
        compiler_params=_cparams(("parallel", "parallel")),
        name="mla_prep",
    )(z, z, z, rc, rs1, rs2, *prm)

    bq = min(512, t)
    nq = t // bq
    return pl.pallas_call(
        _flash_kernel,
        grid=(b, nq, nq),
        in_specs=[pl.BlockSpec((None, bq, hs), lambda bi, i, j: (bi, i, 0)),
                  pl.BlockSpec((None, bq, hs), lambda bi, i, j: (bi, jnp.minimum(j, i), 0)),
                  pl.BlockSpec((None, vr, bq), lambda bi, i, j: (bi, 0, jnp.minimum(j, i)))],
        out_specs=pl.BlockSpec((None, bq, C), lambda bi, i, j: (bi, i, 0)),
        out_shape=jax.ShapeDtypeStruct((b, t, C), BF16),
        scratch_shapes=[pltpu.VMEM((MLA_HEADS, bq), F32), pltpu.VMEM((vr, bq), F32)],
        compiler_params=_cparams(("parallel", "parallel", "arbitrary")),
        name="mla_attn",
    )(q, k, vt)


def _rope_tables(t):
    pos = jnp.arange(t, dtype=F32)
    inv_freq = ROPE_THETA ** (-jnp.arange(0, QK_ROPE, 2, dtype=F32) / QK_ROPE)
    ang = pos[:, None] * inv_freq[None, :]
    cos, sin = jnp.cos(ang), jnp.sin(ang)
    z = lambda n: jnp.zeros((t, n), F32)
    pad = SLOT - QK_HD
    rc = jnp.concatenate([jnp.ones((t, QK_NOPE), F32), cos, cos, z(pad)], axis=1)
    rs1 = jnp.concatenate([z(QK_NOPE), -sin, z(ROT_HALF + pad)], axis=1)
    rs2 = jnp.concatenate([z(QK_NOPE + ROT_HALF), sin, z(pad)], axis=1)
    return rc, rs1, rs2


def _prep_mla(q_norm, w_uq, kv_norm, w_ukv, qk_norm_q, qk_norm_k):
    nl = q_norm.shape[0]
    nh = MLA_HEADS
    pad = SLOT - QK_HD
    wq = w_uq.reshape(nl, Q_LORA, nh, QK_HD)
    wq = jnp.pad(wq, ((0, 0), (0, 0), (0, 0), (0, pad))).reshape(nl, Q_LORA, nh * SLOT).astype(BF16)
    wkv = w_ukv.reshape(nl, KV_LORA, nh, QK_NOPE + V_HD)
    wk = jnp.pad(wkv[..., :QK_NOPE], ((0, 0), (0, 0), (0, 0), (0, SLOT - QK_NOPE)))
    wk = wk.reshape(nl, KV_LORA, nh * SLOT).astype(BF16)
    wv = wkv[..., QK_NOPE:].reshape(nl, KV_LORA, nh * V_HD).astype(BF16)
    slot = lambda g: jnp.pad(g.astype(F32), ((0, 0), (0, pad))).reshape(nl, 1, SLOT)
    return (q_norm.reshape(nl, 1, -1).astype(F32), wq, kv_norm.reshape(nl, 1, -1).astype(F32), wk, wv,
            slot(qk_norm_q), slot(qk_norm_k))


def _merge_kernel(y0_ref, y1_ref, y2_ref, y3_ref, gl_ref, x_ref, wb_ref, wo_ref, o_ref):
    d = x_ref.shape[1]
    merged = None
    for n, y_ref in enumerate((y0_ref, y1_ref, y2_ref, y3_ref)):
        term = _mm(y_ref[...], wb_ref[n]) * _sigmoid(gl_ref[:, n * d:(n + 1) * d].astype(F32))
        merged = term if merged is None else merged + term
    o_ref[...] = x_ref[...] + _mm(merged, wo_ref[...])


def _merge(ys, z2, x2, wb, wo, l):
    n, d = x2.shape
    tm = min(256, n)
    ytile = pl.BlockSpec((tm, C), lambda i: (i, 0))
    return pl.pallas_call(
        _merge_kernel,
        grid=(n // tm,),
        in_specs=[ytile, ytile, ytile, ytile,
                  pl.BlockSpec((tm, 4 * d), lambda i: (i, COL_GATE)),
                  pl.BlockSpec((tm, d), lambda i: (i, 0)),
                  _layer_spec(wb, l), _layer_spec(wo, l)],
        out_specs=pl.BlockSpec((tm, d), lambda i: (i, 0)),
        out_shape=jax.ShapeDtypeStruct((n, d), F32),
        compiler_params=_cparams(("parallel",)),
        name="merge",
    )(*ys, z2, x2, wb, wo)


def _mlp_kernel(x_ref, g_ref, w1_ref, w2_ref, o_ref, h_scr, acc):
    j = pl.program_id(1)

    @pl.when(j == 0)
    def _():
        x = x_ref[...]
        ms = jnp.mean(x * x, axis=-1, keepdims=True)
        h_scr[...] = (x * lax.rsqrt(ms + EPS) * g_ref[...]).astype(BF16)
        acc[...] = x

    a = jnp.maximum(jnp.dot(h_scr[...], w1_ref[...], preferred_element_type=F32), 0.0)
    acc[...] += _mm(a * a, w2_ref[...])

    @pl.when(j == pl.num_programs(1) - 1)
    def _():
        o_ref[...] = acc[...]


def _mlp(x2, g, w1, w2, l):
    n, d = x2.shape
    dff = w1.shape[2]
    tm = min(1024, n)
    tf = 512
    return pl.pallas_call(
        _mlp_kernel,
        grid=(n // tm, dff // tf),
        in_specs=[pl.BlockSpec((tm, d), lambda i, j: (i, 0)),
                  pl.BlockSpec((None, 1, d), lambda i, j: (l, 0, 0)),
                  pl.BlockSpec((None, d, tf), lambda i, j: (l, 0, j)),
                  pl.BlockSpec((None, tf, d), lambda i, j: (l, j, 0))],
        out_specs=pl.BlockSpec((tm, d), lambda i, j: (i, 0)),
        out_shape=jax.ShapeDtypeStruct((n, d), F32),
        scratch_shapes=[pltpu.VMEM((tm, d), BF16), pltpu.VMEM((tm, d), F32)],
        compiler_params=_cparams(("parallel", "arbitrary")),
        name="mlp",
    )(x2, g, w1, w2)


def _prep_w_in(w_in):
    nl, d, _ = w_in.shape
    gate0 = 6 * C + W_LORA + A_LORA + G_LORA + Q_LORA + KV_LORA + QK_ROPE
    kr0 = gate0 - QK_ROPE
    parts = [
        w_in[:, :, gate0:],
        w_in[:, :, :kr0],
        jnp.zeros((nl, d, QK_NOPE), w_in.dtype),
        w_in[:, :, kr0:gate0],
        jnp.zeros((nl, d, D_INP - COL_KR - QK_NOPE - QK_ROPE), w_in.dtype),
    ]
    return jnp.concatenate(parts, axis=-1).astype(BF16)


def _prep_lru_gate(gate_w):
    nl, _, nh, hd, _ = gate_w.shape
    eye = jnp.eye(nh, dtype=gate_w.dtype)
    dense = jnp.einsum("lghij,hk->lghikj", gate_w, eye).reshape(nl, 2, nh * hd, nh * hd)
    return jnp.concatenate([dense[:, 0], dense[:, 1]], axis=-1).astype(BF16)


def kernel(x, norm_mix, w_in, lru_conv_w, lru_conv_b, lru_gate_w, lru_gate_b, lru_lambda, s5_a_re, s5_a_im, s5_b_re, s5_b_im, s5_c_re, s5_c_im, s5_d, s5_log_dt, s5_w_glu, rwkv_mu_rkv, rwkv_mu_w, rwkv_mu_a, rwkv_mu_g, rwkv_w0, rwkv_w2, rwkv_a0, rwkv_a2, rwkv_g2, rwkv_k_k, rwkv_k_a, rwkv_r_k, rwkv_lnx_g, rwkv_lnx_b, mla_q_norm, mla_w_uq, mla_kv_norm, mla_w_ukv, mla_qk_norm_q, mla_qk_norm_k, w_branch, w_out, norm_mlp, w_ff1, w_ff2):
    b, t, d = x.shape
    depth = w_in.shape[0]
    n = b * t
    row = lambda a: a.reshape(depth, 1, -1).astype(F32)

    w_in_p = _prep_w_in(w_in)
    lru_gw = _prep_lru_gate(lru_gate_w)
    s5_tabs = _prep_s5(s5_a_re, s5_a_im, s5_b_re, s5_b_im, s5_c_re, s5_c_im, s5_log_dt, t)
    rw_prm = _prep_rwkv(rwkv_mu_rkv, rwkv_mu_w, rwkv_mu_a, rwkv_mu_g, rwkv_w0, rwkv_w2, rwkv_a0, rwkv_a2, rwkv_g2,
                        rwkv_k_k, rwkv_k_a, rwkv_r_k, rwkv_lnx_g, rwkv_lnx_b)
    mla_prm = _prep_mla(mla_q_norm, mla_w_uq, mla_kv_norm, mla_w_ukv, mla_qk_norm_q, mla_qk_norm_k)
    rope = _rope_tables(t)
    g_mix, g_mlp = row(norm_mix), row(norm_mlp)
    conv_b, gate_b, lam, s5_dr = row(lru_conv_b), row(lru_gate_b), row(lru_lambda), row(s5_d)
    w_glu, wb, wo = s5_w_glu.astype(BF16), w_branch.astype(BF16), w_out.astype(BF16)
    w1, w2 = w_ff1.astype(BF16), w_ff2.astype(BF16)

    lru_prm = (lru_conv_w.astype(F32), conv_b, lru_gw, gate_b, lam)
    x2 = x.reshape(n, d)
    for l in range(depth):
        z2, u2 = _inproj(x2, g_mix, w_in_p, l)
        z = z2.reshape(b, t, -1)
        y_lru = _lru_branch(z, lru_prm, l)
        y_s5 = _s5_branch(u2.reshape(b, t, C), s5_tabs, s5_dr, w_glu, l)
        y_rw = _rwkv_branch(z, rw_prm, l)
        y_mla = _mla_branch(z, mla_prm, rope, l)
        ys = [y.reshape(n, C) for y in (y_lru, y_s5, y_rw, y_mla)]
        x2 = _merge(ys, z2, x2, wb, wo, l)
        x2 = _mlp(x2, g_mlp, w1, w2, l)
    return x2.reshape(b, t, d)
```

```python
import functools
import math

import jax
import jax.numpy as jnp
import numpy as np
from jax import lax
from jax.experimental import pallas as pl
from jax.experimental.pallas import tpu as pltpu

F32 = jnp.float32
BF16 = jnp.bfloat16

EPS = 1e-6
LRU_HEADS = 8
CONV_W = 4
LRU_C = 8.0
S5_GROUP = 16
S5_STATE = 64
RWKV_HD = 64
W_LORA, A_LORA, G_LORA = 64, 64, 128
RWKV_LN_EPS = 64e-5
MLA_HEADS = 8
QK_NOPE, QK_ROPE, V_HD = 64, 32, 64
QK_HD = QK_NOPE + QK_ROPE
Q_LORA, KV_LORA = 256, 128
ROPE_THETA = 10000.0

LANES = 128
SUBLANES = 8
VMEM_LIMIT = 56 * 1024 * 1024

D_MODEL = 1024
C = D_MODEL // 2
COL_GATE = 0
COL_LRU_X = 4 * D_MODEL
COL_LRU_G = COL_LRU_X + C
COL_S5_U = COL_LRU_G + C
COL_RW_R = COL_S5_U + C
COL_RW_K = COL_RW_R + C
COL_RW_V = COL_RW_K + C
COL_LORA = COL_RW_V + C
COL_CQ = COL_LORA + 256
COL_CKV = COL_CQ + Q_LORA
COL_KR = COL_CKV + KV_LORA
D_INP = 8 * D_MODEL


def _cparams(sem):
    return pltpu.CompilerParams(dimension_semantics=sem, vmem_limit_bytes=VMEM_LIMIT)


def _gelu_tanh(x):
    return 0.5 * x * (1.0 + jnp.tanh(math.sqrt(2.0 / math.pi) * (x + 0.044715 * (x * x * x))))


def _sigmoid(x):
    return 1.0 / (1.0 + jnp.exp(-x))


def _softplus(x):
    return jnp.maximum(x, 0.0) + jnp.log(1.0 + jnp.exp(-jnp.abs(x)))


HALO = 16


def _shift_rows(x, halo, k):
    xe = jnp.concatenate([halo, x], axis=0)
    n = x.shape[0]
    return xe[HALO - k:HALO - k + n, :]


def _layer_spec(arr, l):
    zeros = (0,) * (arr.ndim - 1)
    return pl.BlockSpec((None,) + arr.shape[1:], lambda *idx: (l,) + zeros)


def _inproj_kernel(x_ref, g_ref, w_ref, o_ref, u_ref, h_scr):
    j = pl.program_id(1)

    @pl.when(j == 0)
    def _():
        x = x_ref[...]
        ms = jnp.mean(x * x, axis=-1, keepdims=True)
        h_scr[...] = (x * lax.rsqrt(ms + EPS) * g_ref[...]).astype(BF16)

    y = jnp.dot(h_scr[...], w_ref[...], preferred_element_type=F32)
    o_ref[...] = y.astype(BF16)
    tn = w_ref.shape[1]
    u0 = COL_S5_U % tn

    @pl.when(j == COL_S5_U // tn)
    def _():
        u_ref[...] = y[:, u0:u0 + C]


INPROJ_TN = 1024


def _inproj(x2, g, w, l):
    n, d = x2.shape
    dn = w.shape[2]
    tm = min(1024, n)
    tn = INPROJ_TN
    assert COL_S5_U % tn + C <= tn and dn % tn == 0
    return pl.pallas_call(
        _inproj_kernel,
        grid=(n // tm, dn // tn),
        in_specs=[
            pl.BlockSpec((tm, d), lambda i, j: (i, 0)),
            pl.BlockSpec((None, 1, d), lambda i, j: (l, 0, 0)),
            pl.BlockSpec((None, d, tn), lambda i, j: (l, 0, j)),
        ],
        out_specs=[pl.BlockSpec((tm, tn), lambda i, j: (i, j)),
                   pl.BlockSpec((tm, C), lambda i, j: (i, 0))],
        out_shape=[jax.ShapeDtypeStruct((n, dn), BF16), jax.ShapeDtypeStruct((n, C), F32)],
        scratch_shapes=[pltpu.VMEM((tm, d), BF16)],
        compiler_params=_cparams(("parallel", "arbitrary")),
        name="inproj",
    )(x2, g, w)


def _lru_kernel(x_ref, xh_ref, g_ref, cw_ref, cb_ref, gw_ref, gb_ref, lam_ref, o_ref, carry):
    t = pl.program_id(1)
    tm, c = x_ref.shape

    @pl.when(t == 0)
    def _():
        carry[...] = jnp.zeros_like(carry)

    x = x_ref[...].astype(F32)
    halo = jnp.where(t == 0, 0.0, xh_ref[...].astype(F32))
    cw = cw_ref[...]
    xc = cb_ref[...] + cw[CONV_W - 1:CONV_W, :] * x
    for k in range(1, CONV_W):
        xc = xc + cw[CONV_W - 1 - k:CONV_W - k, :] * _shift_rows(x, halo, k)

    gates = jnp.dot(xc.astype(BF16), gw_ref[...], preferred_element_type=F32) + gb_ref[...]
    r = _sigmoid(gates[:, :c])
    i = _sigmoid(gates[:, c:])
    log_a = (-LRU_C) * r * _softplus(-lam_ref[...])
    a = jnp.exp(log_a)
    u = jnp.sqrt(1.0 - a * a) * (i * xc)

    sub = lax.broadcasted_iota(jnp.int32, (tm, c), 0) % SUBLANES
    s = 1
    while s < SUBLANES:
        a_sh = jnp.where(sub >= s, pltpu.roll(a, s, axis=0), 1.0)
        u_sh = jnp.where(sub >= s, pltpu.roll(u, s, axis=0), 0.0)
        u = u + a * u_sh
        a = a * a_sh
        s *= 2
    hp = carry[SUBLANES - 1:SUBLANES, :]
    blocks = []
    for blk in range(tm // SUBLANES):
        rows = slice(blk * SUBLANES, (blk + 1) * SUBLANES)
        hb = u[rows, :] + a[rows, :] * hp
        hp = hb[SUBLANES - 1:SUBLANES, :]
        blocks.append(hb)
    h = jnp.concatenate(blocks, axis=0)
    carry[...] = blocks[-1]
    o_ref[...] = (h * _gelu_tanh(g_ref[...].astype(F32))).astype(o_ref.dtype)


def _lru_branch(z, prm, l):
    b, t, _ = z.shape
    tm = min(256, t)
    cb = COL_LRU_X // C
    hb = tm // HALO
    return pl.pallas_call(
        _lru_kernel,
        grid=(b, t // tm),
        in_specs=[
            pl.BlockSpec((None, tm, C), lambda i, j: (i, j, cb)),
            pl.BlockSpec((None, HALO, C), lambda i, j: (i, jnp.maximum(j * hb - 1, 0), cb)),
            pl.BlockSpec((None, tm, C), lambda i, j: (i, j, cb + 1)),
        ] + [_layer_spec(a, l) for a in prm],
        out_specs=pl.BlockSpec((None, tm, C), lambda i, j: (i, j, 0)),
        out_shape=jax.ShapeDtypeStruct((b, t, C), BF16),
        scratch_shapes=[pltpu.VMEM((SUBLANES, C), F32)],
        compiler_params=_cparams(("parallel", "arbitrary")),
        name="rglru",
    )(z, z, z, *prm)


S5_L = 8
S5_OCT = LANES // S5_GROUP


def _s5_kernel(u_ref, win_ref, bd_ref, wout_ref, lstep_ref, lblk_ref, y_ref, uf, wintra, carry):
    t = pl.program_id(2)
    tm = u_ref.shape[0]
    nc = tm // S5_L
    ns = carry.shape[1]
    half = ns // 2

    @pl.when((pl.program_id(1) == 0) & (t == 0))
    def _():
        wintra[...] = jnp.zeros_like(wintra)
        for i in range(S5_L):
            for j in range(i, S5_L):
                wintra[i * LANES:(i + 1) * LANES, j * LANES:(j + 1) * LANES] = bd_ref[j - i]

    @pl.when(t == 0)
    def _():
        carry[...] = jnp.zeros_like(carry)

    for i in range(S5_L):
        uf[:, i * LANES:(i + 1) * LANES] = u_ref[pl.ds(i, nc, stride=S5_L), :].astype(BF16)
    ufv = uf[...]
    s = jnp.dot(ufv, win_ref[...], preferred_element_type=F32)

    def cmul(v, ar, ai):
        sw = jnp.concatenate([v[:, half:], v[:, :half]], axis=1)
        return v * ar + sw * ai

    row = lax.broadcasted_iota(jnp.int32, (nc, ns), 0)
    sub = row % SUBLANES
    k = 0
    st = 1
    while st < SUBLANES:
        sh = jnp.where(sub >= st, pltpu.roll(s, st, axis=0), 0.0)
        s = s + cmul(sh, lstep_ref[2 * k:2 * k + 1, :], lstep_ref[2 * k + 1:2 * k + 2, :])
        k += 1
        st *= 2
    cin = carry[0:1, :]
    prev = cin
    blocks = []
    for blk in range(nc // SUBLANES):
        sb = s[blk * SUBLANES:(blk + 1) * SUBLANES, :] + cmul(jnp.broadcast_to(prev, (SUBLANES, ns)),
                                                              lblk_ref[0], lblk_ref[1])
        prev = sb[SUBLANES - 1:SUBLANES, :]
        blocks.append(sb)
    s = jnp.concatenate(blocks, axis=0)
    carry[...] = jnp.broadcast_to(prev, carry.shape)
    sprev = jnp.where(row >= 1, pltpu.roll(s, 1, axis=0), cin)

    yf = jnp.dot(ufv, wintra[...], preferred_element_type=F32)
    yf = yf + jnp.dot(sprev.astype(BF16), wout_ref[...], preferred_element_type=F32)
    for j in range(S5_L):
        y_ref[pl.ds(j, nc, stride=S5_L), :] = yf[:, j * LANES:(j + 1) * LANES]


def _s5_tail_kernel(y_ref, u_ref, d_ref, w_ref, o_ref):
    c = u_ref.shape[1]
    y = _gelu_tanh(y_ref[...] + d_ref[...] * u_ref[...])
    z = jnp.dot(y.astype(BF16), w_ref[...], preferred_element_type=F32)
    o_ref[...] = (z[:, :c] * _sigmoid(z[:, c:])).astype(o_ref.dtype)


def _s5_tile(t):
    return min(2048, t)


def _s5_branch(u, tabs, d_skip, w_glu, l):
    win, bd, wout, lstep, lblk = tabs
    b, t, _ = u.shape
    tm = _s5_tile(t)
    nc = tm // S5_L
    noct = C // LANES
    ns = win.shape[-1]
    fl = S5_L * LANES

    def table(arr):
        zeros = (0,) * (arr.ndim - 2)
        return pl.BlockSpec((None, None) + arr.shape[2:], lambda q, i, j: (l, q) + zeros)

    y = pl.pallas_call(
        _s5_kernel,
        grid=(noct, b, t // tm),
        in_specs=[pl.BlockSpec((None, tm, LANES), lambda q, i, j: (i, j, q)),
                  table(win), table(bd), table(wout), table(lstep), table(lblk)],
        out_specs=pl.BlockSpec((None, tm, LANES), lambda q, i, j: (i, j, q)),
        out_shape=jax.ShapeDtypeStruct((b, t, C), F32),
        scratch_shapes=[pltpu.VMEM((nc, fl), BF16), pltpu.VMEM((fl, fl), BF16), pltpu.VMEM((SUBLANES, ns), F32)],
        compiler_params=_cparams(("arbitrary", "arbitrary", "arbitrary")),
        name="s5_ssm",
    )(u, win, bd, wout, lstep, lblk)

    tt = min(512, t)
    return pl.pallas_call(
        _s5_tail_kernel,
        grid=(b, t // tt),
        in_specs=[
            pl.BlockSpec((None, tt, C), lambda i, j: (i, j, 0)),
            pl.BlockSpec((None, tt, C), lambda i, j: (i, j, 0)),
            _layer_spec(d_skip, l),
            _layer_spec(w_glu, l),
        ],
        out_specs=pl.BlockSpec((None, tt, C), lambda i, j: (i, j, 0)),
        out_shape=jax.ShapeDtypeStruct((b, t, C), BF16),
        compiler_params=_cparams(("parallel", "parallel")),
        name="s5_glu",
    )(y, u, d_skip, w_glu)


def _prep_s5(a_re, a_im, b_re, b_im, c_re, c_im, log_dt, t):
    nl, g, p = a_re.shape
    m = S5_GROUP
    noct = g // S5_OCT
    nc = _s5_tile(t) // S5_L
    lam = lax.complex(a_re.astype(F32), a_im.astype(F32))
    dt = jnp.exp(log_dt.astype(F32))[..., None]
    ldt = lam * dt
    lam_bar = jnp.exp(ldt)
    b_bar = ((lam_bar - 1.0) / lam)[..., None] * lax.complex(b_re.astype(F32), b_im.astype(F32))
    cc = lax.complex(c_re.astype(F32), c_im.astype(F32))

    def powers(e):
        return jnp.exp(ldt[..., None] * e.astype(F32).reshape((1, 1, 1, -1)))

    steps = jnp.arange(S5_L)
    same = jnp.eye(S5_OCT, dtype=bool)

    def octet_bd(x):
        x = jnp.moveaxis(x, 2, 3)
        x = jnp.expand_dims(x, 6)
        x = jnp.where(same.reshape(1, 1, 1, S5_OCT, 1, 1, S5_OCT, 1), x, 0.0)
        nrow = int(np.prod(x.shape[2:5]))
        return x.astype(BF16).reshape(nl, noct, nrow, -1)

    kd = jnp.einsum("lgop,lgpd,lgpi->lgdio", cc, powers(steps), b_bar).real
    kd = kd.reshape(nl, noct, S5_OCT, S5_L, m, 1, m)
    bd = octet_bd(kd).reshape(nl, noct, S5_L, LANES, LANES)

    wi = powers(S5_L - 1 - steps)[..., None] * b_bar[:, :, :, None, :]
    wi = jnp.stack([wi.real, wi.imag], axis=2)
    wi = jnp.transpose(wi, (0, 1, 4, 5, 2, 3)).reshape(nl, noct, S5_OCT, S5_L, m, 2, p)
    win = octet_bd(wi)

    wo = cc[:, :, :, :, None] * powers(steps + 1)[:, :, None, :, :]
    wo = jnp.stack([wo.real, -wo.imag], axis=2)
    wo = jnp.transpose(wo, (0, 1, 2, 4, 5, 3)).reshape(nl, noct, S5_OCT, 2, p, S5_L, m)
    wout = octet_bd(wo)

    def state_mult(e):
        pw = powers(e)
        pw = jnp.transpose(pw, (0, 3, 1, 2)).reshape(nl, e.shape[0], noct, S5_OCT * p)
        pw = jnp.moveaxis(pw, 2, 1)
        return (jnp.concatenate([pw.real, pw.real], axis=-1), jnp.concatenate([-pw.imag, pw.imag], axis=-1))

    nsteps = int(math.log2(SUBLANES))
    ar, ai = state_mult(S5_L * (2 ** jnp.arange(nsteps)))
    lstep = jnp.stack([ar, ai], axis=3).reshape(nl, noct, 2 * nsteps, -1)
    lblk = jnp.stack(state_mult(S5_L * (jnp.arange(SUBLANES) + 1)), axis=2)
    return win, bd, wout, lstep, lblk


RW_L = 64
HIGHEST = lax.Precision.HIGHEST


def _mm(a, b, dims=(((1,), (0,)), ((), ()))):
    return lax.dot_general(a.astype(BF16), b.astype(BF16), dims, preferred_element_type=F32)


def _mm_nt(a, b):
    return _mm(a, b, (((1,), (1,)), ((), ())))


def _mm_tn(a, b):
    return _mm(a, b, (((0,), (0,)), ((), ())))


def _mm_exact(a, b):
    return lax.dot_general(a, b, (((1,), (0,)), ((), ())), precision=HIGHEST, preferred_element_type=F32)


def _head_sum(x, m0):
    s0 = jnp.sum(jnp.where(m0, x, 0.0), axis=-1, keepdims=True)
    s1 = jnp.sum(jnp.where(m0, 0.0, x), axis=-1, keepdims=True)
    return jnp.where(m0, s0, s1)


def _rwkv_kernel(r_ref, k_ref, v_ref, lo_ref, rh_ref, kh_ref, vh_ref, loh_ref, mu_ref, mul_ref, w0_ref, a0_ref,
                 w2_ref, a2_ref, g2_ref, kk_ref, ka_ref, rk_ref, lng_ref, lnb_ref, o_ref, hstate):
    t = pl.program_id(1)
    tm, c = r_ref.shape
    npair = c // LANES
    ll = RW_L

    @pl.when(t == 0)
    def _():
        hstate[...] = jnp.zeros_like(hstate)

    def mix(ref, href, mu):
        p = ref[...].astype(F32)
        prev = _shift_rows(p, jnp.where(t == 0, 0.0, href[...].astype(F32)), 1)
        return p + (prev - p) * mu

    mu = mu_ref[...]
    r = mix(r_ref, rh_ref, mu[0:1, :])
    k = mix(k_ref, kh_ref, mu[1:2, :])
    v = mix(v_ref, vh_ref, mu[2:3, :])
    lo = mix(lo_ref, loh_ref, mul_ref[...])
    wlog = -_softplus(-(w0_ref[...] + _mm(jnp.tanh(lo), w2_ref[...]))) - 0.5
    lw = -jnp.exp(wlog)
    a = _sigmoid(a0_ref[...] + _mm(lo, a2_ref[...]))
    g = _mm(_sigmoid(lo), g2_ref[...])
    kk = k * kk_ref[...]
    k = k * (1.0 + (a - 1.0) * ka_ref[...])

    lane = lax.broadcasted_iota(jnp.int32, (ll, LANES), 1)
    m0 = lane < RWKV_HD
    ri = lax.broadcasted_iota(jnp.int32, (2 * ll, 2 * ll), 0)
    ci = lax.broadcasted_iota(jnp.int32, (2 * ll, 2 * ll), 1)
    same = (ri >= ll) == (ci >= ll)
    strict = same & (ri > ci)
    incl = same & (ri >= ci)
    eye = ri == ci
    tri = (lax.broadcasted_iota(jnp.int32, (ll, ll), 0) >= lax.broadcasted_iota(jnp.int32, (ll, ll), 1)).astype(F32)

    def st(x):
        return jnp.concatenate([jnp.where(m0, x, 0.0), jnp.where(m0, 0.0, x)], axis=0)

    nch = tm // ll
    pcs = [(ch, p) for ch in range(nch) for p in range(npair)]
    cum_all = [_mm_exact(tri, lw[ch * ll:(ch + 1) * ll, :]) for ch in range(nch)]
    pre = {}
    for (ch, p) in pcs:
        rows, cols = slice(ch * ll, (ch + 1) * ll), slice(p * LANES, (p + 1) * LANES)
        rp, kp, vp, ap, lwp, cum = r[rows, cols], k[rows, cols], v[rows, cols], a[rows, cols], lw[rows, cols], cum_all[ch][:, cols]
        kkp = kk[rows, cols]
        kkp = kkp * lax.rsqrt(_head_sum(kkp * kkp, m0) + 1e-12)
        cum_l = cum[ll - 1:ll, :]
        e_neg = jnp.exp(-cum)
        e_last = jnp.exp(cum_l - cum)
        kb = kkp * ap
        pre[ch, p] = dict(
            rows=rows, cols=cols, rp=rp, kp=kp, vp=vp,
            a_st=st(-kkp * jnp.exp(cum - lwp)).astype(BF16), r_st=st(rp * jnp.exp(cum)),
            v_st=st(vp).astype(BF16), b_st=st(kb * e_neg).astype(BF16), k_st=st(kp * e_neg).astype(BF16),
            bh_st=st(kb * e_last).astype(BF16), kh_st=st(kp * e_last).astype(BF16), g_l=jnp.exp(cum_l))

    gram = {pc: _mm_nt(jnp.concatenate([pre[pc]["a_st"], pre[pc]["r_st"].astype(BF16)], axis=0),
                       jnp.concatenate([pre[pc]["b_st"], pre[pc]["k_st"]], axis=0)) for pc in pcs}
    a_ab = {pc: jnp.where(strict, gram[pc][:2 * ll, :2 * ll], 0.0) for pc in pcs}
    a_ak = {pc: jnp.where(strict, gram[pc][:2 * ll, 2 * ll:], 0.0).astype(BF16) for pc in pcs}
    m_rb = {pc: jnp.where(incl, gram[pc][2 * ll:, :2 * ll], 0.0).astype(BF16) for pc in pcs}
    m_rk = {pc: jnp.where(incl, gram[pc][2 * ll:, 2 * ll:], 0.0).astype(BF16) for pc in pcs}
    akv = {pc: _mm(a_ak[pc], pre[pc]["v_st"]) for pc in pcs}
    tinv = {pc: jnp.where(eye, 1.0, a_ab[pc]) for pc in pcs}
    pw = {pc: a_ab[pc].astype(BF16) for pc in pcs}
    for _ in range(int(math.log2(ll)) - 1):
        pw = {pc: _mm(pw[pc], pw[pc]).astype(BF16) for pc in pcs}
        tinv = {pc: tinv[pc] + _mm(tinv[pc], pw[pc]) for pc in pcs}
    xa = {pc: _mm(tinv[pc], jnp.concatenate([pre[pc]["a_st"], akv[pc].astype(BF16)], axis=1)).astype(BF16)
          for pc in pcs}
    ma = {pc: _mm(m_rb[pc], xa[pc][:, :LANES]) for pc in pcs}
    y_loc = {pc: _mm(jnp.concatenate([m_rb[pc], m_rk[pc]], axis=1),
                     jnp.concatenate([xa[pc][:, LANES:], pre[pc]["v_st"]], axis=0)) for pc in pcs}
    ba = {pc: _mm_tn(pre[pc]["bh_st"], xa[pc][:, :LANES]) for pc in pcs}
    f_loc = {pc: _mm_tn(jnp.concatenate([pre[pc]["bh_st"], pre[pc]["kh_st"]], axis=0),
                        jnp.concatenate([xa[pc][:, LANES:], pre[pc]["v_st"]], axis=0)) for pc in pcs}
    for (ch, p) in pcs:
        pc = (ch, p)
        q = pre[pc]
        r_eff = q["r_st"] + ma[pc]
        g_eff = jnp.where(eye, q["g_l"], 0.0) + ba[pc]
        hs = _mm(jnp.concatenate([g_eff, r_eff], axis=0), hstate[p])
        hstate[p] = hs[:2 * ll, :] + f_loc[pc]
        y_st = hs[2 * ll:, :] + y_loc[pc]
        y = y_st[:ll, :] + y_st[ll:, :]
        rows, cols = q["rows"], q["cols"]
        mean = _head_sum(y, m0) * (1.0 / RWKV_HD)
        d = y - mean
        var = _head_sum(d * d, m0) * (1.0 / RWKV_HD)
        yn = d * lax.rsqrt(var + RWKV_LN_EPS) * lng_ref[:, cols] + lnb_ref[:, cols]
        bonus = _head_sum(q["rp"] * q["kp"] * rk_ref[:, cols], m0) * q["vp"]
        o_ref[rows, cols] = ((yn + bonus) * g[rows, cols]).astype(o_ref.dtype)


def _rwkv_branch(z, prm, l):
    b, t, _ = z.shape
    tm = min(4 * RW_L, t)
    hb = tm // HALO
    cr = COL_RW_R // C
    cl = COL_LORA // 256
    npair = C // LANES

    def tile(cb, w):
        return pl.BlockSpec((None, tm, w), lambda i, j: (i, j, cb))

    def halo(cb, w):
        return pl.BlockSpec((None, HALO, w), lambda i, j: (i, jnp.maximum(j * hb - 1, 0), cb))

    return pl.pallas_call(
        _rwkv_kernel,
        grid=(b, t // tm),
        in_specs=[tile(cr, C), tile(cr + 1, C), tile(cr + 2, C), tile(cl, 256),
                  halo(cr, C), halo(cr + 1, C), halo(cr + 2, C), halo(cl, 256)]
                 + [_layer_spec(a, l) for a in prm],
        out_specs=pl.BlockSpec((None, tm, C), lambda i, j: (i, j, 0)),
        out_shape=jax.ShapeDtypeStruct((b, t, C), BF16),
        scratch_shapes=[pltpu.VMEM((npair, LANES, LANES), F32)],
        compiler_params=_cparams(("parallel", "arbitrary")),
        name="rwkv7",
    )(z, z, z, z, z, z, z, z, *prm)


def _prep_rwkv(mu_rkv, mu_w, mu_a, mu_g, w0, w2, a0, a2, g2, k_k, k_a, r_k, lnx_g, lnx_b):
    nl = mu_rkv.shape[0]
    row = lambda x: x.reshape(nl, 1, -1).astype(F32)
    mul = jnp.concatenate([mu_w, mu_a, mu_g], axis=-1)
    zw = lambda n: jnp.zeros((nl, n, C), F32)
    w2p = jnp.concatenate([w2, zw(A_LORA + G_LORA)], axis=1).astype(BF16)
    a2p = jnp.concatenate([zw(W_LORA), a2, zw(G_LORA)], axis=1).astype(BF16)
    g2p = jnp.concatenate([zw(W_LORA + A_LORA), g2], axis=1).astype(BF16)
    return (mu_rkv.astype(F32), row(mul), row(w0), row(a0), w2p, a2p, g2p, row(k_k), row(k_a), row(r_k),
            row(lnx_g), row(lnx_b))


SLOT = LANES
ROT_HALF = QK_ROPE // 2
VT_ROWS = V_HD + 16
FLASH_AHEAD = 1
FLASH_QSPLIT = 1


def _mla_prep_kernel(cq_ref, ckv_ref, kr_ref, rc_ref, rs1_ref, rs2_ref, qn_ref, wq_ref, kvn_ref, wk_ref, wv_ref,
                     gq_ref, gk_ref, q_ref, k_ref, v_ref, v_scr):
    def rms(x, g, n):
        ms = jnp.sum(x * x, axis=-1, keepdims=True) * (1.0 / n)
        return x * lax.rsqrt(ms + EPS) * g

    def rope(x):
        return (x * rc_ref[...] + pltpu.roll(x, SLOT - ROT_HALF, axis=1) * rs1_ref[...]
                + pltpu.roll(x, ROT_HALF, axis=1) * rs2_ref[...])

    cq = rms(cq_ref[...].astype(F32), qn_ref[...], Q_LORA)
    ckv = rms(ckv_ref[...].astype(F32), kvn_ref[...], KV_LORA)
    qf = _mm(cq, wq_ref[...])
    kf = _mm(ckv, wk_ref[...])
    v_scr[...] = _mm(ckv, wv_ref[...])
    vt = jnp.transpose(v_scr[...]).astype(BF16)
    ones = jnp.ones((VT_ROWS - V_HD, vt.shape[1]), BF16)
    for h in range(MLA_HEADS):
        v_ref[h * VT_ROWS:h * VT_ROWS + V_HD, :] = vt[h * V_HD:(h + 1) * V_HD, :]
        v_ref[h * VT_ROWS + V_HD:(h + 1) * VT_ROWS, :] = ones
    kr = kr_ref[...].astype(F32)
    scale = QK_HD ** -0.5 * math.log2(math.e)
    for h in range(MLA_HEADS):
        cols = slice(h * SLOT, (h + 1) * SLOT)
        q = rope(rms(qf[:, cols], gq_ref[...], QK_HD)) * scale
        k = rope(rms(kf[:, cols] + kr, gk_ref[...], QK_HD))
        q_ref[:, cols] = q.astype(BF16)
        k_ref[:, cols] = k.astype(BF16)


def _flash_kernel(qi_ref, kj_ref, q_ref, k_ref, vt_ref, o_ref, m_scr, acc):
    i = qi_ref[pl.program_id(1)]
    j = kj_ref[pl.program_id(1)]
    bq = q_ref.shape[0]
    bk = k_ref.shape[0]

    @pl.when(j == 0)
    def _():
        m_scr[...] = jnp.full_like(m_scr, -jnp.inf)
        acc[...] = jnp.zeros_like(acc)

    def step(masked):
        if masked:
            causal = (lax.broadcasted_iota(jnp.int32, (bk, bq), 0) <= lax.broadcasted_iota(jnp.int32, (bk, bq), 1))
        qw = bq // FLASH_QSPLIT
        units = [(h, c) for h in range(MLA_HEADS) for c in range(FLASH_QSPLIT)]

        def scores(u):
            h, c = u
            cols = slice(h * SLOT, (h + 1) * SLOT)
            return _mm_nt(k_ref[:, cols], q_ref[c * qw:(c + 1) * qw, cols])

        pending = [scores(u) for u in units[:FLASH_AHEAD]]
        for n, (h, c) in enumerate(units):
            rows = slice(h * VT_ROWS, (h + 1) * VT_ROWS)
            qs = slice(c * qw, (c + 1) * qw)
            st = pending.pop(0)
            if n + FLASH_AHEAD < len(units):
                pending.append(scores(units[n + FLASH_AHEAD]))
            if masked:
                st = jnp.where(causal[:, qs], st, -jnp.inf)
            m_prev = m_scr[h:h + 1, qs]
            m_new = jnp.maximum(m_prev, jnp.max(st, axis=0, keepdims=True))
            alpha = jnp.exp2(m_prev - m_new)
            pt = jnp.exp2((st - m_new).astype(BF16))
            m_scr[h:h + 1, qs] = m_new
            acc[rows, qs] = acc[rows, qs] * alpha + _mm(vt_ref[rows, :], pt)

    @pl.when(j < i)
    def _():
        step(False)

    @pl.when(j == i)
    def _():
        step(True)
        outs = []
        for h in range(MLA_HEADS):
            r0 = h * VT_ROWS
            outs.append(acc[r0:r0 + V_HD, :] / acc[r0 + V_HD:r0 + V_HD + 1, :])
        o_ref[...] = jnp.transpose(jnp.concatenate(outs, axis=0)).astype(o_ref.dtype)


def _mla_branch(z, prm, rope, l):
    rc, rs1, rs2 = rope
    b, t, _ = z.shape
    tm = min(512, t)
    hs = MLA_HEADS * SLOT
    vr = MLA_HEADS * VT_ROWS

    def tab():
        return pl.BlockSpec((tm, SLOT), lambda i, j: (j, 0))

    q, k, vt = pl.pallas_call(
        _mla_prep_kernel,
        grid=(b, t // tm),
        in_specs=[pl.BlockSpec((None, tm, Q_LORA), lambda i, j: (i, j, COL_CQ // Q_LORA)),
                  pl.BlockSpec((None, tm, KV_LORA), lambda i, j: (i, j, COL_CKV // KV_LORA)),
                  pl.BlockSpec((None, tm, LANES), lambda i, j: (i, j, COL_KR // LANES)),
                  tab(), tab(), tab()] + [_layer_spec(a, l) for a in prm],
        out_specs=[pl.BlockSpec((None, tm, hs), lambda i, j: (i, j, 0)),
                   pl.BlockSpec((None, tm, hs), lambda i, j: (i, j, 0)),
                   pl.BlockSpec((None, vr, tm), lambda i, j: (i, 0, j))],
        out_shape=[jax.ShapeDtypeStruct((b, t, hs), BF16), jax.ShapeDtypeStruct((b, t, hs), BF16),
                   jax.ShapeDtypeStruct((b, vr, t), BF16)],
        scratch_shapes=[pltpu.VMEM((tm, C), F32)],
        compiler_params=_cparams(("parallel", "parallel")),
        name="mla_prep",
    )(z, z, z, rc, rs1, rs2, *prm)

    bq = min(512, t)
    nq = t // bq
    tri = [(i, j) for i in range(nq) for j in range(i + 1)]
    qi = jnp.asarray([p[0] for p in tri], jnp.int32)
    kj = jnp.asarray([p[1] for p in tri], jnp.int32)
    return pl.pallas_call(
        _flash_kernel,
        grid_spec=pltpu.PrefetchScalarGridSpec(
            num_scalar_prefetch=2,
            grid=(b, len(tri)),
            in_specs=[pl.BlockSpec((None, bq, hs), lambda bi, s, qi, kj: (bi, qi[s], 0)),
                      pl.BlockSpec((None, bq, hs), lambda bi, s, qi, kj: (bi, kj[s], 0)),
                      pl.BlockSpec((None, vr, bq), lambda bi, s, qi, kj: (bi, 0, kj[s]))],
            out_specs=pl.BlockSpec((None, bq, C), lambda bi, s, qi, kj: (bi, qi[s], 0)),
            scratch_shapes=[pltpu.VMEM((MLA_HEADS, bq), F32), pltpu.VMEM((vr, bq), F32)]),
        out_shape=jax.ShapeDtypeStruct((b, t, C), BF16),
        compiler_params=_cparams(("parallel", "arbitrary")),
        name="mla_attn",
    )(qi, kj, q, k, vt)


def _rope_tables(t):
    pos = jnp.arange(t, dtype=F32)
    inv_freq = ROPE_THETA ** (-jnp.arange(0, QK_ROPE, 2, dtype=F32) / QK_ROPE)
    ang = pos[:, None] * inv_freq[None, :]
    cos, sin = jnp.cos(ang), jnp.sin(ang)
    z = lambda n: jnp.zeros((t, n), F32)
    pad = SLOT - QK_HD
    rc = jnp.concatenate([jnp.ones((t, QK_NOPE), F32), cos, cos, z(pad)], axis=1)
    rs1 = jnp.concatenate([z(QK_NOPE), -sin, z(ROT_HALF + pad)], axis=1)
    rs2 = jnp.concatenate([z(QK_NOPE + ROT_HALF), sin, z(pad)], axis=1)
    return rc, rs1, rs2


def _prep_mla(q_norm, w_uq, kv_norm, w_ukv, qk_norm_q, qk_norm_k):
    nl = q_norm.shape[0]
    nh = MLA_HEADS
    pad = SLOT - QK_HD
    wq = w_uq.reshape(nl, Q_LORA, nh, QK_HD)
    wq = jnp.pad(wq, ((0, 0), (0, 0), (0, 0), (0, pad))).reshape(nl, Q_LORA, nh * SLOT).astype(BF16)
    wkv = w_ukv.reshape(nl, KV_LORA, nh, QK_NOPE + V_HD)
    wk = jnp.pad(wkv[..., :QK_NOPE], ((0, 0), (0, 0), (0, 0), (0, SLOT - QK_NOPE)))
    wk = wk.reshape(nl, KV_LORA, nh * SLOT).astype(BF16)
    wv = wkv[..., QK_NOPE:].reshape(nl, KV_LORA, nh * V_HD).astype(BF16)
    slot = lambda g: jnp.pad(g.astype(F32), ((0, 0), (0, pad))).reshape(nl, 1, SLOT)
    return (q_norm.reshape(nl, 1, -1).astype(F32), wq, kv_norm.reshape(nl, 1, -1).astype(F32), wk, wv,
            slot(qk_norm_q), slot(qk_norm_k))


def _merge_kernel(y0_ref, y1_ref, y2_ref, y3_ref, gl_ref, x_ref, wb_ref, wo_ref, o_ref):
    d = x_ref.shape[1]
    merged = None
    for n, y_ref in enumerate((y0_ref, y1_ref, y2_ref, y3_ref)):
        term = _mm(y_ref[...], wb_ref[n]) * _sigmoid(gl_ref[:, n * d:(n + 1) * d].astype(F32))
        merged = term if merged is None else merged + term
    o_ref[...] = x_ref[...] + _mm(merged, wo_ref[...])


def _merge(ys, z2, x2, wb, wo, l):
    n, d = x2.shape
    tm = min(512, n)
    ytile = pl.BlockSpec((tm, C), lambda i: (i, 0))
    return pl.pallas_call(
        _merge_kernel,
        grid=(n // tm,),
        in_specs=[ytile, ytile, ytile, ytile,
                  pl.BlockSpec((tm, 4 * d), lambda i: (i, COL_GATE)),
                  pl.BlockSpec((tm, d), lambda i: (i, 0)),
                  _layer_spec(wb, l), _layer_spec(wo, l)],
        out_specs=pl.BlockSpec((tm, d), lambda i: (i, 0)),
        out_shape=jax.ShapeDtypeStruct((n, d), F32),
        compiler_params=_cparams(("parallel",)),
        name="merge",
    )(*ys, z2, x2, wb, wo)


def _mlp_kernel(x_ref, g_ref, w1_ref, w2_ref, o_ref, h_scr, acc):
    j = pl.program_id(1)

    @pl.when(j == 0)
    def _():
        x = x_ref[...]
        ms = jnp.mean(x * x, axis=-1, keepdims=True)
        h_scr[...] = (x * lax.rsqrt(ms + EPS) * g_ref[...]).astype(BF16)
        acc[...] = x

    a = jnp.maximum(jnp.dot(h_scr[...], w1_ref[...], preferred_element_type=F32), 0.0)
    acc[...] += _mm(a * a, w2_ref[...])

    @pl.when(j == pl.num_programs(1) - 1)
    def _():
        o_ref[...] = acc[...]


def _mlp(x2, g, w1, w2, l):
    n, d = x2.shape
    dff = w1.shape[2]
    tm = min(1024, n)
    tf = 512
    return pl.pallas_call(
        _mlp_kernel,
        grid=(n // tm, dff // tf),
        in_specs=[pl.BlockSpec((tm, d), lambda i, j: (i, 0)),
                  pl.BlockSpec((None, 1, d), lambda i, j: (l, 0, 0)),
                  pl.BlockSpec((None, d, tf), lambda i, j: (l, 0, j)),
                  pl.BlockSpec((None, tf, d), lambda i, j: (l, j, 0))],
        out_specs=pl.BlockSpec((tm, d), lambda i, j: (i, 0)),
        out_shape=jax.ShapeDtypeStruct((n, d), F32),
        scratch_shapes=[pltpu.VMEM((tm, d), BF16), pltpu.VMEM((tm, d), F32)],
        compiler_params=_cparams(("parallel", "arbitrary")),
        name="mlp",
    )(x2, g, w1, w2)


def _prep_w_in(w_in):
    nl, d, _ = w_in.shape
    gate0 = 6 * C + W_LORA + A_LORA + G_LORA + Q_LORA + KV_LORA + QK_ROPE
    kr0 = gate0 - QK_ROPE
    parts = [
        w_in[:, :, gate0:],
        w_in[:, :, :kr0],
        jnp.zeros((nl, d, QK_NOPE), w_in.dtype),
        w_in[:, :, kr0:gate0],
        jnp.zeros((nl, d, D_INP - COL_KR - QK_NOPE - QK_ROPE), w_in.dtype),
    ]
    return jnp.concatenate(parts, axis=-1).astype(BF16)


def _prep_lru_gate(gate_w):
    nl, _, nh, hd, _ = gate_w.shape
    eye = jnp.eye(nh, dtype=gate_w.dtype)
    dense = jnp.einsum("lghij,hk->lghikj", gate_w, eye).reshape(nl, 2, nh * hd, nh * hd)
    return jnp.concatenate([dense[:, 0], dense[:, 1]], axis=-1).astype(BF16)


def kernel(x, norm_mix, w_in, lru_conv_w, lru_conv_b, lru_gate_w, lru_gate_b, lru_lambda, s5_a_re, s5_a_im, s5_b_re, s5_b_im, s5_c_re, s5_c_im, s5_d, s5_log_dt, s5_w_glu, rwkv_mu_rkv, rwkv_mu_w, rwkv_mu_a, rwkv_mu_g, rwkv_w0, rwkv_w2, rwkv_a0, rwkv_a2, rwkv_g2, rwkv_k_k, rwkv_k_a, rwkv_r_k, rwkv_lnx_g, rwkv_lnx_b, mla_q_norm, mla_w_uq, mla_kv_norm, mla_w_ukv, mla_qk_norm_q, mla_qk_norm_k, w_branch, w_out, norm_mlp, w_ff1, w_ff2):
    b, t, d = x.shape
    depth = w_in.shape[0]
    n = b * t
    row = lambda a: a.reshape(depth, 1, -1).astype(F32)

    w_in_p = _prep_w_in(w_in)
    lru_gw = _prep_lru_gate(lru_gate_w)
    s5_tabs = _prep_s5(s5_a_re, s5_a_im, s5_b_re, s5_b_im, s5_c_re, s5_c_im, s5_log_dt, t)
    rw_prm = _prep_rwkv(rwkv_mu_rkv, rwkv_mu_w, rwkv_mu_a, rwkv_mu_g, rwkv_w0, rwkv_w2, rwkv_a0, rwkv_a2, rwkv_g2,
                        rwkv_k_k, rwkv_k_a, rwkv_r_k, rwkv_lnx_g, rwkv_lnx_b)
    mla_prm = _prep_mla(mla_q_norm, mla_w_uq, mla_kv_norm, mla_w_ukv, mla_qk_norm_q, mla_qk_norm_k)
    rope = _rope_tables(t)
    g_mix, g_mlp = row(norm_mix), row(norm_mlp)
    conv_b, gate_b, lam, s5_dr = row(lru_conv_b), row(lru_gate_b), row(lru_lambda), row(s5_d)
    w_glu, wb, wo = s5_w_glu.astype(BF16), w_branch.astype(BF16), w_out.astype(BF16)
    w1, w2 = w_ff1.astype(BF16), w_ff2.astype(BF16)

    lru_prm = (lru_conv_w.astype(F32), conv_b, lru_gw, gate_b, lam)
    x2 = x.reshape(n, d)
    for l in range(depth):
        z2, u2 = _inproj(x2, g_mix, w_in_p, l)
        z = z2.reshape(b, t, -1)
        y_lru = _lru_branch(z, lru_prm, l)
        y_s5 = _s5_branch(u2.reshape(b, t, C), s5_tabs, s5_dr, w_glu, l)
        y_rw = _rwkv_branch(z, rw_prm, l)
        y_mla = _mla_branch(z, mla_prm, rope, l)
        ys = [y.reshape(n, C) for y in (y_lru, y_s5, y_rw, y_mla)]
        x2 = _merge(ys, z2, x2, wb, wo, l)
        x2 = _mlp(x2, g_mlp, w1, w2, l)
    return x2.reshape(b, t, d)
```

```python
import functools
import math

import jax
import jax.numpy as jnp
import numpy as np
from jax import lax
from jax.experimental import pallas as pl
from jax.experimental.pallas import tpu as pltpu

F32 = jnp.float32
BF16 = jnp.bfloat16

EPS = 1e-6
LRU_HEADS = 8
CONV_W = 4
LRU_C = 8.0
S5_GROUP = 16
S5_STATE = 64
RWKV_HD = 64
W_LORA, A_LORA, G_LORA = 64, 64, 128
RWKV_LN_EPS = 64e-5
MLA_HEADS = 8
QK_NOPE, QK_ROPE, V_HD = 64, 32, 64
QK_HD = QK_NOPE + QK_ROPE
Q_LORA, KV_LORA = 256, 128
ROPE_THETA = 10000.0

LANES = 128
SUBLANES = 8
VMEM_LIMIT = 56 * 1024 * 1024

D_MODEL = 1024
C = D_MODEL // 2
COL_GATE = 0
COL_LRU_X = 4 * D_MODEL
COL_LRU_G = COL_LRU_X + C
COL_S5_U = COL_LRU_G + C
COL_RW_R = COL_S5_U + C
COL_RW_K = COL_RW_R + C
COL_RW_V = COL_RW_K + C
COL_LORA = COL_RW_V + C
COL_CQ = COL_LORA + 256
COL_CKV = COL_CQ + Q_LORA
COL_KR = COL_CKV + KV_LORA
D_INP = 8 * D_MODEL


def _cparams(sem):
    return pltpu.CompilerParams(dimension_semantics=sem, vmem_limit_bytes=VMEM_LIMIT)


def _gelu_tanh(x):
    return 0.5 * x * (1.0 + jnp.tanh(math.sqrt(2.0 / math.pi) * (x + 0.044715 * (x * x * x))))


def _sigmoid(x):
    return 1.0 / (1.0 + jnp.exp(-x))


def _softplus(x):
    return jnp.maximum(x, 0.0) + jnp.log(1.0 + jnp.exp(-jnp.abs(x)))


HALO = 16


def _shift_rows(x, halo, k):
    xe = jnp.concatenate([halo, x], axis=0)
    n = x.shape[0]
    return xe[HALO - k:HALO - k + n, :]


def _layer_spec(arr, l):
    zeros = (0,) * (arr.ndim - 1)
    return pl.BlockSpec((None,) + arr.shape[1:], lambda *idx: (l,) + zeros)


def _inproj_kernel(x_ref, g_ref, w_ref, o_ref, u_ref, h_scr):
    j = pl.program_id(1)

    @pl.when(j == 0)
    def _():
        x = x_ref[...]
        ms = jnp.mean(x * x, axis=-1, keepdims=True)
        h_scr[...] = (x * lax.rsqrt(ms + EPS) * g_ref[...]).astype(BF16)

    y = jnp.dot(h_scr[...], w_ref[...], preferred_element_type=F32)
    o_ref[...] = y.astype(BF16)
    tn = w_ref.shape[1]
    u0 = COL_S5_U % tn

    @pl.when(j == COL_S5_U // tn)
    def _():
        u_ref[...] = y[:, u0:u0 + C]


INPROJ_TN = 2048


def _inproj(x2, g, w, l):
    n, d = x2.shape
    dn = w.shape[2]
    tm = min(1024, n)
    tn = INPROJ_TN
    assert COL_S5_U % tn + C <= tn and dn % tn == 0
    return pl.pallas_call(
        _inproj_kernel,
        grid=(n // tm, dn // tn),
        in_specs=[
            pl.BlockSpec((tm, d), lambda i, j: (i, 0)),
            pl.BlockSpec((None, 1, d), lambda i, j: (l, 0, 0)),
            pl.BlockSpec((None, d, tn), lambda i, j: (l, 0, j)),
        ],
        out_specs=[pl.BlockSpec((tm, tn), lambda i, j: (i, j)),
                   pl.BlockSpec((tm, C), lambda i, j: (i, 0))],
        out_shape=[jax.ShapeDtypeStruct((n, dn), BF16), jax.ShapeDtypeStruct((n, C), F32)],
        scratch_shapes=[pltpu.VMEM((tm, d), BF16)],
        compiler_params=_cparams(("parallel", "arbitrary")),
        name="inproj",
    )(x2, g, w)


def _lru_kernel(x_ref, xh_ref, g_ref, cw_ref, cb_ref, gw_ref, gb_ref, lam_ref, o_ref, carry):
    t = pl.program_id(1)
    tm, c = x_ref.shape

    @pl.when(t == 0)
    def _():
        carry[...] = jnp.zeros_like(carry)

    x = x_ref[...].astype(F32)
    halo = jnp.where(t == 0, 0.0, xh_ref[...].astype(F32))
    cw = cw_ref[...]
    xc = cb_ref[...] + cw[CONV_W - 1:CONV_W, :] * x
    for k in range(1, CONV_W):
        xc = xc + cw[CONV_W - 1 - k:CONV_W - k, :] * _shift_rows(x, halo, k)

    gates = jnp.dot(xc.astype(BF16), gw_ref[...], preferred_element_type=F32) + gb_ref[...]
    r = _sigmoid(gates[:, :c])
    i = _sigmoid(gates[:, c:])
    log_a = (-LRU_C) * r * _softplus(-lam_ref[...])
    a = jnp.exp(log_a)
    u = jnp.sqrt(1.0 - a * a) * (i * xc)

    sub = lax.broadcasted_iota(jnp.int32, (tm, c), 0) % SUBLANES
    s = 1
    while s < SUBLANES:
        a_sh = jnp.where(sub >= s, pltpu.roll(a, s, axis=0), 1.0)
        u_sh = jnp.where(sub >= s, pltpu.roll(u, s, axis=0), 0.0)
        u = u + a * u_sh
        a = a * a_sh
        s *= 2
    hp = carry[SUBLANES - 1:SUBLANES, :]
    blocks = []
    for blk in range(tm // SUBLANES):
        rows = slice(blk * SUBLANES, (blk + 1) * SUBLANES)
        hb = u[rows, :] + a[rows, :] * hp
        hp = hb[SUBLANES - 1:SUBLANES, :]
        blocks.append(hb)
    h = jnp.concatenate(blocks, axis=0)
    carry[...] = blocks[-1]
    o_ref[...] = (h * _gelu_tanh(g_ref[...].astype(F32))).astype(o_ref.dtype)


def _lru_branch(z, prm, l):
    b, t, _ = z.shape
    tm = min(256, t)
    cb = COL_LRU_X // C
    hb = tm // HALO
    return pl.pallas_call(
        _lru_kernel,
        grid=(b, t // tm),
        in_specs=[
            pl.BlockSpec((None, tm, C), lambda i, j: (i, j, cb)),
            pl.BlockSpec((None, HALO, C), lambda i, j: (i, jnp.maximum(j * hb - 1, 0), cb)),
            pl.BlockSpec((None, tm, C), lambda i, j: (i, j, cb + 1)),
        ] + [_layer_spec(a, l) for a in prm],
        out_specs=pl.BlockSpec((None, tm, C), lambda i, j: (i, j, 0)),
        out_shape=jax.ShapeDtypeStruct((b, t, C), BF16),
        scratch_shapes=[pltpu.VMEM((SUBLANES, C), F32)],
        compiler_params=_cparams(("parallel", "arbitrary")),
        name="rglru",
    )(z, z, z, *prm)


S5_L = 8
S5_OCT = LANES // S5_GROUP


def _s5_kernel(u_ref, win_ref, bd_ref, wout_ref, lstep_ref, lblk_ref, y_ref, uf, wintra, carry):
    t = pl.program_id(2)
    tm = u_ref.shape[0]
    nc = tm // S5_L
    ns = carry.shape[1]
    half = ns // 2

    @pl.when((pl.program_id(1) == 0) & (t == 0))
    def _():
        wintra[...] = jnp.zeros_like(wintra)
        for i in range(S5_L):
            for j in range(i, S5_L):
                wintra[i * LANES:(i + 1) * LANES, j * LANES:(j + 1) * LANES] = bd_ref[j - i]

    @pl.when(t == 0)
    def _():
        carry[...] = jnp.zeros_like(carry)

    for i in range(S5_L):
        uf[:, i * LANES:(i + 1) * LANES] = u_ref[pl.ds(i, nc, stride=S5_L), :].astype(BF16)
    ufv = uf[...]
    s = jnp.dot(ufv, win_ref[...], preferred_element_type=F32)

    def cmul(v, ar, ai):
        sw = jnp.concatenate([v[:, half:], v[:, :half]], axis=1)
        return v * ar + sw * ai

    row = lax.broadcasted_iota(jnp.int32, (nc, ns), 0)
    sub = row % SUBLANES
    k = 0
    st = 1
    while st < SUBLANES:
        sh = jnp.where(sub >= st, pltpu.roll(s, st, axis=0), 0.0)
        s = s + cmul(sh, lstep_ref[2 * k:2 * k + 1, :], lstep_ref[2 * k + 1:2 * k + 2, :])
        k += 1
        st *= 2
    cin = carry[0:1, :]
    prev = cin
    blocks = []
    for blk in range(nc // SUBLANES):
        sb = s[blk * SUBLANES:(blk + 1) * SUBLANES, :] + cmul(jnp.broadcast_to(prev, (SUBLANES, ns)),
                                                              lblk_ref[0], lblk_ref[1])
        prev = sb[SUBLANES - 1:SUBLANES, :]
        blocks.append(sb)
    s = jnp.concatenate(blocks, axis=0)
    carry[...] = jnp.broadcast_to(prev, carry.shape)
    sprev = jnp.where(row >= 1, pltpu.roll(s, 1, axis=0), cin)

    yf = jnp.dot(ufv, wintra[...], preferred_element_type=F32)
    yf = yf + jnp.dot(sprev.astype(BF16), wout_ref[...], preferred_element_type=F32)
    for j in range(S5_L):
        y_ref[pl.ds(j, nc, stride=S5_L), :] = yf[:, j * LANES:(j + 1) * LANES]


def _s5_tail_kernel(y_ref, u_ref, d_ref, w_ref, o_ref):
    c = u_ref.shape[1]
    y = _gelu_tanh(y_ref[...] + d_ref[...] * u_ref[...])
    z = jnp.dot(y.astype(BF16), w_ref[...], preferred_element_type=F32)
    o_ref[...] = (z[:, :c] * _sigmoid(z[:, c:])).astype(o_ref.dtype)


def _s5_tile(t):
    return min(2048, t)


def _s5_branch(u, tabs, d_skip, w_glu, l):
    win, bd, wout, lstep, lblk = tabs
    b, t, _ = u.shape
    tm = _s5_tile(t)
    nc = tm // S5_L
    noct = C // LANES
    ns = win.shape[-1]
    fl = S5_L * LANES

    def table(arr):
        zeros = (0,) * (arr.ndim - 2)
        return pl.BlockSpec((None, None) + arr.shape[2:], lambda q, i, j: (l, q) + zeros)

    y = pl.pallas_call(
        _s5_kernel,
        grid=(noct, b, t // tm),
        in_specs=[pl.BlockSpec((None, tm, LANES), lambda q, i, j: (i, j, q)),
                  table(win), table(bd), table(wout), table(lstep), table(lblk)],
        out_specs=pl.BlockSpec((None, tm, LANES), lambda q, i, j: (i, j, q)),
        out_shape=jax.ShapeDtypeStruct((b, t, C), F32),
        scratch_shapes=[pltpu.VMEM((nc, fl), BF16), pltpu.VMEM((fl, fl), BF16), pltpu.VMEM((SUBLANES, ns), F32)],
        compiler_params=_cparams(("arbitrary", "arbitrary", "arbitrary")),
        name="s5_ssm",
    )(u, win, bd, wout, lstep, lblk)

    tt = min(512, t)
    return pl.pallas_call(
        _s5_tail_kernel,
        grid=(b, t // tt),
        in_specs=[
            pl.BlockSpec((None, tt, C), lambda i, j: (i, j, 0)),
            pl.BlockSpec((None, tt, C), lambda i, j: (i, j, 0)),
            _layer_spec(d_skip, l),
            _layer_spec(w_glu, l),
        ],
        out_specs=pl.BlockSpec((None, tt, C), lambda i, j: (i, j, 0)),
        out_shape=jax.ShapeDtypeStruct((b, t, C), BF16),
        compiler_params=_cparams(("parallel", "parallel")),
        name="s5_glu",
    )(y, u, d_skip, w_glu)


def _prep_s5(a_re, a_im, b_re, b_im, c_re, c_im, log_dt, t):
    nl, g, p = a_re.shape
    m = S5_GROUP
    noct = g // S5_OCT
    nc = _s5_tile(t) // S5_L
    lam = lax.complex(a_re.astype(F32), a_im.astype(F32))
    dt = jnp.exp(log_dt.astype(F32))[..., None]
    ldt = lam * dt
    lam_bar = jnp.exp(ldt)
    b_bar = ((lam_bar - 1.0) / lam)[..., None] * lax.complex(b_re.astype(F32), b_im.astype(F32))
    cc = lax.complex(c_re.astype(F32), c_im.astype(F32))

    def powers(e):
        return jnp.exp(ldt[..., None] * e.astype(F32).reshape((1, 1, 1, -1)))

    steps = jnp.arange(S5_L)
    same = jnp.eye(S5_OCT, dtype=bool)

    def octet_bd(x):
        x = jnp.moveaxis(x, 2, 3)
        x = jnp.expand_dims(x, 6)
        x = jnp.where(same.reshape(1, 1, 1, S5_OCT, 1, 1, S5_OCT, 1), x, 0.0)
        nrow = int(np.prod(x.shape[2:5]))
        return x.astype(BF16).reshape(nl, noct, nrow, -1)

    kd = jnp.einsum("lgop,lgpd,lgpi->lgdio", cc, powers(steps), b_bar).real
    kd = kd.reshape(nl, noct, S5_OCT, S5_L, m, 1, m)
    bd = octet_bd(kd).reshape(nl, noct, S5_L, LANES, LANES)

    wi = powers(S5_L - 1 - steps)[..., None] * b_bar[:, :, :, None, :]
    wi = jnp.stack([wi.real, wi.imag], axis=2)
    wi = jnp.transpose(wi, (0, 1, 4, 5, 2, 3)).reshape(nl, noct, S5_OCT, S5_L, m, 2, p)
    win = octet_bd(wi)

    wo = cc[:, :, :, :, None] * powers(steps + 1)[:, :, None, :, :]
    wo = jnp.stack([wo.real, -wo.imag], axis=2)
    wo = jnp.transpose(wo, (0, 1, 2, 4, 5, 3)).reshape(nl, noct, S5_OCT, 2, p, S5_L, m)
    wout = octet_bd(wo)

    def state_mult(e):
        pw = powers(e)
        pw = jnp.transpose(pw, (0, 3, 1, 2)).reshape(nl, e.shape[0], noct, S5_OCT * p)
        pw = jnp.moveaxis(pw, 2, 1)
        return (jnp.concatenate([pw.real, pw.real], axis=-1), jnp.concatenate([-pw.imag, pw.imag], axis=-1))

    nsteps = int(math.log2(SUBLANES))
    ar, ai = state_mult(S5_L * (2 ** jnp.arange(nsteps)))
    lstep = jnp.stack([ar, ai], axis=3).reshape(nl, noct, 2 * nsteps, -1)
    lblk = jnp.stack(state_mult(S5_L * (jnp.arange(SUBLANES) + 1)), axis=2)
    return win, bd, wout, lstep, lblk


RW_L = 64
HIGHEST = lax.Precision.HIGHEST


def _mm(a, b, dims=(((1,), (0,)), ((), ()))):
    return lax.dot_general(a.astype(BF16), b.astype(BF16), dims, preferred_element_type=F32)


def _mm_nt(a, b):
    return _mm(a, b, (((1,), (1,)), ((), ())))


def _mm_tn(a, b):
    return _mm(a, b, (((0,), (0,)), ((), ())))


def _mm_exact(a, b):
    return lax.dot_general(a, b, (((1,), (0,)), ((), ())), precision=HIGHEST, preferred_element_type=F32)


def _head_sum(x, m0):
    s0 = jnp.sum(jnp.where(m0, x, 0.0), axis=-1, keepdims=True)
    s1 = jnp.sum(jnp.where(m0, 0.0, x), axis=-1, keepdims=True)
    return jnp.where(m0, s0, s1)


def _rwkv_kernel(r_ref, k_ref, v_ref, lo_ref, rh_ref, kh_ref, vh_ref, loh_ref, mu_ref, mul_ref, w0_ref, a0_ref,
                 w2_ref, a2_ref, g2_ref, kk_ref, ka_ref, rk_ref, lng_ref, lnb_ref, o_ref, hstate):
    t = pl.program_id(1)
    tm, c = r_ref.shape
    npair = c // LANES
    ll = RW_L

    @pl.when(t == 0)
    def _():
        hstate[...] = jnp.zeros_like(hstate)

    def mix(ref, href, mu):
        p = ref[...].astype(F32)
        prev = _shift_rows(p, jnp.where(t == 0, 0.0, href[...].astype(F32)), 1)
        return p + (prev - p) * mu

    mu = mu_ref[...]
    r = mix(r_ref, rh_ref, mu[0:1, :])
    k = mix(k_ref, kh_ref, mu[1:2, :])
    v = mix(v_ref, vh_ref, mu[2:3, :])
    lo = mix(lo_ref, loh_ref, mul_ref[...])
    wlog = -_softplus(-(w0_ref[...] + _mm(jnp.tanh(lo), w2_ref[...]))) - 0.5
    lw = -jnp.exp(wlog)
    a = _sigmoid(a0_ref[...] + _mm(lo, a2_ref[...]))
    g = _mm(_sigmoid(lo), g2_ref[...])
    kk = k * kk_ref[...]
    k = k * (1.0 + (a - 1.0) * ka_ref[...])

    lane = lax.broadcasted_iota(jnp.int32, (ll, LANES), 1)
    m0 = lane < RWKV_HD
    ri = lax.broadcasted_iota(jnp.int32, (2 * ll, 2 * ll), 0)
    ci = lax.broadcasted_iota(jnp.int32, (2 * ll, 2 * ll), 1)
    same = (ri >= ll) == (ci >= ll)
    strict = same & (ri > ci)
    incl = same & (ri >= ci)
    eye = ri == ci
    tri = (lax.broadcasted_iota(jnp.int32, (ll, ll), 0) >= lax.broadcasted_iota(jnp.int32, (ll, ll), 1)).astype(F32)

    def st(x):
        return jnp.concatenate([jnp.where(m0, x, 0.0), jnp.where(m0, 0.0, x)], axis=0)

    nch = tm // ll
    pcs = [(ch, p) for ch in range(nch) for p in range(npair)]
    cum_all = [_mm_exact(tri, lw[ch * ll:(ch + 1) * ll, :]) for ch in range(nch)]
    pre = {}
    for (ch, p) in pcs:
        rows, cols = slice(ch * ll, (ch + 1) * ll), slice(p * LANES, (p + 1) * LANES)
        rp, kp, vp, ap, lwp, cum = r[rows, cols], k[rows, cols], v[rows, cols], a[rows, cols], lw[rows, cols], cum_all[ch][:, cols]
        kkp = kk[rows, cols]
        kkp = kkp * lax.rsqrt(_head_sum(kkp * kkp, m0) + 1e-12)
        cum_l = cum[ll - 1:ll, :]
        e_neg = jnp.exp(-cum)
        e_last = jnp.exp(cum_l - cum)
        kb = kkp * ap
        pre[ch, p] = dict(
            rows=rows, cols=cols, rp=rp, kp=kp, vp=vp,
            a_st=st(-kkp * jnp.exp(cum - lwp)).astype(BF16), r_st=st(rp * jnp.exp(cum)),
            v_st=st(vp).astype(BF16), b_st=st(kb * e_neg).astype(BF16), k_st=st(kp * e_neg).astype(BF16),
            bh_st=st(kb * e_last).astype(BF16), kh_st=st(kp * e_last).astype(BF16), g_l=jnp.exp(cum_l))

    gram = {pc: _mm_nt(jnp.concatenate([pre[pc]["a_st"], pre[pc]["r_st"].astype(BF16)], axis=0),
                       jnp.concatenate([pre[pc]["b_st"], pre[pc]["k_st"]], axis=0)) for pc in pcs}
    a_ab = {pc: jnp.where(strict, gram[pc][:2 * ll, :2 * ll], 0.0) for pc in pcs}
    a_ak = {pc: jnp.where(strict, gram[pc][:2 * ll, 2 * ll:], 0.0).astype(BF16) for pc in pcs}
    m_rb = {pc: jnp.where(incl, gram[pc][2 * ll:, :2 * ll], 0.0).astype(BF16) for pc in pcs}
    m_rk = {pc: jnp.where(incl, gram[pc][2 * ll:, 2 * ll:], 0.0).astype(BF16) for pc in pcs}
    akv = {pc: _mm(a_ak[pc], pre[pc]["v_st"]) for pc in pcs}
    tinv = {pc: jnp.where(eye, 1.0, a_ab[pc]) for pc in pcs}
    pw = {pc: a_ab[pc].astype(BF16) for pc in pcs}
    for _ in range(int(math.log2(ll)) - 1):
        pw = {pc: _mm(pw[pc], pw[pc]).astype(BF16) for pc in pcs}
        tinv = {pc: tinv[pc] + _mm(tinv[pc], pw[pc]) for pc in pcs}
    xa = {pc: _mm(tinv[pc], jnp.concatenate([pre[pc]["a_st"], akv[pc].astype(BF16)], axis=1)).astype(BF16)
          for pc in pcs}
    ma = {pc: _mm(m_rb[pc], xa[pc][:, :LANES]) for pc in pcs}
    y_loc = {pc: _mm(jnp.concatenate([m_rb[pc], m_rk[pc]], axis=1),
                     jnp.concatenate([xa[pc][:, LANES:], pre[pc]["v_st"]], axis=0)) for pc in pcs}
    ba = {pc: _mm_tn(pre[pc]["bh_st"], xa[pc][:, :LANES]) for pc in pcs}
    f_loc = {pc: _mm_tn(jnp.concatenate([pre[pc]["bh_st"], pre[pc]["kh_st"]], axis=0),
                        jnp.concatenate([xa[pc][:, LANES:], pre[pc]["v_st"]], axis=0)) for pc in pcs}
    for (ch, p) in pcs:
        pc = (ch, p)
        q = pre[pc]
        r_eff = q["r_st"] + ma[pc]
        g_eff = jnp.where(eye, q["g_l"], 0.0) + ba[pc]
        hs = _mm(jnp.concatenate([g_eff, r_eff], axis=0), hstate[p])
        hstate[p] = hs[:2 * ll, :] + f_loc[pc]
        y_st = hs[2 * ll:, :] + y_loc[pc]
        y = y_st[:ll, :] + y_st[ll:, :]
        rows, cols = q["rows"], q["cols"]
        mean = _head_sum(y, m0) * (1.0 / RWKV_HD)
        d = y - mean
        var = _head_sum(d * d, m0) * (1.0 / RWKV_HD)
        yn = d * lax.rsqrt(var + RWKV_LN_EPS) * lng_ref[:, cols] + lnb_ref[:, cols]
        bonus = _head_sum(q["rp"] * q["kp"] * rk_ref[:, cols], m0) * q["vp"]
        o_ref[rows, cols] = ((yn + bonus) * g[rows, cols]).astype(o_ref.dtype)


def _rwkv_branch(z, prm, l):
    b, t, _ = z.shape
    tm = min(4 * RW_L, t)
    hb = tm // HALO
    cr = COL_RW_R // C
    cl = COL_LORA // 256
    npair = C // LANES

    def tile(cb, w):
        return pl.BlockSpec((None, tm, w), lambda i, j: (i, j, cb))

    def halo(cb, w):
        return pl.BlockSpec((None, HALO, w), lambda i, j: (i, jnp.maximum(j * hb - 1, 0), cb))

    return pl.pallas_call(
        _rwkv_kernel,
        grid=(b, t // tm),
        in_specs=[tile(cr, C), tile(cr + 1, C), tile(cr + 2, C), tile(cl, 256),
                  halo(cr, C), halo(cr + 1, C), halo(cr + 2, C), halo(cl, 256)]
                 + [_layer_spec(a, l) for a in prm],
        out_specs=pl.BlockSpec((None, tm, C), lambda i, j: (i, j, 0)),
        out_shape=jax.ShapeDtypeStruct((b, t, C), BF16),
        scratch_shapes=[pltpu.VMEM((npair, LANES, LANES), F32)],
        compiler_params=_cparams(("parallel", "arbitrary")),
        name="rwkv7",
    )(z, z, z, z, z, z, z, z, *prm)


def _prep_rwkv(mu_rkv, mu_w, mu_a, mu_g, w0, w2, a0, a2, g2, k_k, k_a, r_k, lnx_g, lnx_b):
    nl = mu_rkv.shape[0]
    row = lambda x: x.reshape(nl, 1, -1).astype(F32)
    mul = jnp.concatenate([mu_w, mu_a, mu_g], axis=-1)
    zw = lambda n: jnp.zeros((nl, n, C), F32)
    w2p = jnp.concatenate([w2, zw(A_LORA + G_LORA)], axis=1).astype(BF16)
    a2p = jnp.concatenate([zw(W_LORA), a2, zw(G_LORA)], axis=1).astype(BF16)
    g2p = jnp.concatenate([zw(W_LORA + A_LORA), g2], axis=1).astype(BF16)
    return (mu_rkv.astype(F32), row(mul), row(w0), row(a0), w2p, a2p, g2p, row(k_k), row(k_a), row(r_k),
            row(lnx_g), row(lnx_b))


SLOT = LANES
ROT_HALF = QK_ROPE // 2
NOPE_A = SLOT // 2 - ROT_HALF


def _slot_source():
    src = np.full((SLOT,), QK_HD, np.int32)
    src[:ROT_HALF] = QK_NOPE + np.arange(ROT_HALF)
    src[ROT_HALF:SLOT // 2] = np.arange(NOPE_A)
    src[SLOT // 2:SLOT // 2 + ROT_HALF] = QK_NOPE + ROT_HALF + np.arange(ROT_HALF)
    src[SLOT // 2 + ROT_HALF:SLOT // 2 + ROT_HALF + QK_NOPE - NOPE_A] = NOPE_A + np.arange(QK_NOPE - NOPE_A)
    return src


def _to_slot(v):
    pad = jnp.concatenate([v, jnp.zeros(v.shape[:-1] + (1,), v.dtype)], axis=-1)
    return jnp.take(pad, jnp.asarray(_slot_source()), axis=-1)
VT_ROWS = V_HD + 16
FLASH_AHEAD = 1
FLASH_QSPLIT = 1


def _mla_prep_kernel(cq_ref, ckv_ref, kr_ref, tab_ref, qn_ref, wq_ref, kvn_ref, wk_ref, wv_ref,
                     q_ref, k_ref, v_ref, v_scr):
    def rms(x, g, n):
        ms = jnp.sum(x * x, axis=-1, keepdims=True) * (1.0 / n)
        return x * lax.rsqrt(ms + EPS) * g

    def norm_rope(x, c_tab, s_tab):
        ms = jnp.sum(x * x, axis=-1, keepdims=True) * (1.0 / QK_HD)
        return (x * c_tab + pltpu.roll(x, SLOT // 2, axis=1) * s_tab) * lax.rsqrt(ms + EPS)

    cq = rms(cq_ref[...].astype(F32), qn_ref[...], Q_LORA)
    ckv = rms(ckv_ref[...].astype(F32), kvn_ref[...], KV_LORA)
    qf = _mm(cq, wq_ref[...])
    kf = _mm(ckv, wk_ref[...])
    v_scr[...] = _mm(ckv, wv_ref[...])
    vt = jnp.transpose(v_scr[...]).astype(BF16)
    ones = jnp.ones((VT_ROWS - V_HD, vt.shape[1]), BF16)
    for h in range(MLA_HEADS):
        v_ref[h * VT_ROWS:h * VT_ROWS + V_HD, :] = vt[h * V_HD:(h + 1) * V_HD, :]
        v_ref[h * VT_ROWS + V_HD:(h + 1) * VT_ROWS, :] = ones
    kr = kr_ref[...].astype(F32)
    cq_tab, sq_tab, ck_tab, sk_tab = tab_ref[0], tab_ref[1], tab_ref[2], tab_ref[3]
    for h in range(MLA_HEADS):
        cols = slice(h * SLOT, (h + 1) * SLOT)
        q_ref[:, cols] = norm_rope(qf[:, cols], cq_tab, sq_tab).astype(BF16)
        k_ref[:, cols] = norm_rope(kf[:, cols] + kr, ck_tab, sk_tab).astype(BF16)


def _flash_kernel(qi_ref, kj_ref, q_ref, k_ref, vt_ref, o_ref, m_scr, acc):
    i = qi_ref[pl.program_id(1)]
    j = kj_ref[pl.program_id(1)]
    bq = q_ref.shape[0]
    bk = k_ref.shape[0]

    @pl.when(j == 0)
    def _():
        m_scr[...] = jnp.full_like(m_scr, -jnp.inf)
        acc[...] = jnp.zeros_like(acc)

    def step(masked):
        if masked:
            causal = (lax.broadcasted_iota(jnp.int32, (bk, bq), 0) <= lax.broadcasted_iota(jnp.int32, (bk, bq), 1))
        qw = bq // FLASH_QSPLIT
        units = [(h, c) for h in range(MLA_HEADS) for c in range(FLASH_QSPLIT)]

        def scores(u):
            h, c = u
            cols = slice(h * SLOT, (h + 1) * SLOT)
            return _mm_nt(k_ref[:, cols], q_ref[c * qw:(c + 1) * qw, cols])

        pending = [scores(u) for u in units[:FLASH_AHEAD]]
        for n, (h, c) in enumerate(units):
            rows = slice(h * VT_ROWS, (h + 1) * VT_ROWS)
            qs = slice(c * qw, (c + 1) * qw)
            st = pending.pop(0)
            if n + FLASH_AHEAD < len(units):
                pending.append(scores(units[n + FLASH_AHEAD]))
            if masked:
                st = jnp.where(causal[:, qs], st, -jnp.inf)
            m_prev = m_scr[h:h + 1, qs]
            m_new = jnp.maximum(m_prev, jnp.max(st, axis=0, keepdims=True))
            alpha = jnp.exp2(m_prev - m_new)
            pt = jnp.exp2((st - m_new).astype(BF16))
            m_scr[h:h + 1, qs] = m_new
            acc[rows, qs] = acc[rows, qs] * alpha + _mm(vt_ref[rows, :], pt)

    @pl.when(j < i)
    def _():
        step(False)

    @pl.when(j == i)
    def _():
        step(True)
        outs = []
        for h in range(MLA_HEADS):
            r0 = h * VT_ROWS
            outs.append(acc[r0:r0 + V_HD, :] / acc[r0 + V_HD:r0 + V_HD + 1, :])
        o_ref[...] = jnp.transpose(jnp.concatenate(outs, axis=0)).astype(o_ref.dtype)


def _mla_branch(z, prm, rope, l):
    b, t, _ = z.shape
    tm = min(512, t)
    hs = MLA_HEADS * SLOT
    vr = MLA_HEADS * VT_ROWS

    q, k, vt = pl.pallas_call(
        _mla_prep_kernel,
        grid=(b, t // tm),
        in_specs=[pl.BlockSpec((None, tm, Q_LORA), lambda i, j: (i, j, COL_CQ // Q_LORA)),
                  pl.BlockSpec((None, tm, KV_LORA), lambda i, j: (i, j, COL_CKV // KV_LORA)),
                  pl.BlockSpec((None, tm, LANES), lambda i, j: (i, j, COL_KR // LANES)),
                  pl.BlockSpec((None, 4, tm, SLOT), lambda i, j: (l, 0, j, 0))]
                 + [_layer_spec(a, l) for a in prm],
        out_specs=[pl.BlockSpec((None, tm, hs), lambda i, j: (i, j, 0)),
                   pl.BlockSpec((None, tm, hs), lambda i, j: (i, j, 0)),
                   pl.BlockSpec((None, vr, tm), lambda i, j: (i, 0, j))],
        out_shape=[jax.ShapeDtypeStruct((b, t, hs), BF16), jax.ShapeDtypeStruct((b, t, hs), BF16),
                   jax.ShapeDtypeStruct((b, vr, t), BF16)],
        scratch_shapes=[pltpu.VMEM((tm, C), F32)],
        compiler_params=_cparams(("parallel", "parallel")),
        name="mla_prep",
    )(z, z, z, rope, *prm)

    bq = min(512, t)
    nq = t // bq
    tri = [(i, j) for i in range(nq) for j in range(i + 1)]
    qi = jnp.asarray([p[0] for p in tri], jnp.int32)
    kj = jnp.asarray([p[1] for p in tri], jnp.int32)
    return pl.pallas_call(
        _flash_kernel,
        grid_spec=pltpu.PrefetchScalarGridSpec(
            num_scalar_prefetch=2,
            grid=(b, len(tri)),
            in_specs=[pl.BlockSpec((None, bq, hs), lambda bi, s, qi, kj: (bi, qi[s], 0)),
                      pl.BlockSpec((None, bq, hs), lambda bi, s, qi, kj: (bi, kj[s], 0)),
                      pl.BlockSpec((None, vr, bq), lambda bi, s, qi, kj: (bi, 0, kj[s]))],
            out_specs=pl.BlockSpec((None, bq, C), lambda bi, s, qi, kj: (bi, qi[s], 0)),
            scratch_shapes=[pltpu.VMEM((MLA_HEADS, bq), F32), pltpu.VMEM((vr, bq), F32)]),
        out_shape=jax.ShapeDtypeStruct((b, t, C), BF16),
        compiler_params=_cparams(("parallel", "arbitrary")),
        name="mla_attn",
    )(qi, kj, q, k, vt)


def _rope_tables(t, qk_norm_q, qk_norm_k):
    pos = jnp.arange(t, dtype=F32)
    inv_freq = ROPE_THETA ** (-jnp.arange(0, QK_ROPE, 2, dtype=F32) / QK_ROPE)
    ang = pos[:, None] * inv_freq[None, :]
    cos, sin = jnp.cos(ang), jnp.sin(ang)
    z = lambda n: jnp.zeros((t, n), F32)
    one = lambda n: jnp.ones((t, n), F32)
    tail = SLOT // 2 - ROT_HALF
    c_pos = jnp.concatenate([cos, one(NOPE_A), cos, one(QK_NOPE - NOPE_A), z(SLOT - QK_HD)], axis=1)
    s_pos = jnp.concatenate([-sin, z(NOPE_A), sin, z(tail)], axis=1)
    scale = QK_HD ** -0.5 * math.log2(math.e)

    def pair(g, mult):
        g = _to_slot(g.astype(F32)) * mult
        return [g[:, None, :] * c_pos[None], jnp.roll(g, SLOT // 2, axis=-1)[:, None, :] * s_pos[None]]

    return jnp.stack(pair(qk_norm_q, scale) + pair(qk_norm_k, 1.0), axis=1)


def _prep_mla(q_norm, w_uq, kv_norm, w_ukv):
    nl = q_norm.shape[0]
    nh = MLA_HEADS
    wq = _to_slot(w_uq.reshape(nl, Q_LORA, nh, QK_HD)).reshape(nl, Q_LORA, nh * SLOT).astype(BF16)
    wkv = w_ukv.reshape(nl, KV_LORA, nh, QK_NOPE + V_HD)
    wk = jnp.pad(wkv[..., :QK_NOPE], ((0, 0), (0, 0), (0, 0), (0, QK_ROPE)))
    wk = _to_slot(wk).reshape(nl, KV_LORA, nh * SLOT).astype(BF16)
    wv = wkv[..., QK_NOPE:].reshape(nl, KV_LORA, nh * V_HD).astype(BF16)
    return (q_norm.reshape(nl, 1, -1).astype(F32), wq, kv_norm.reshape(nl, 1, -1).astype(F32), wk, wv)


def _merge_kernel(y0_ref, y1_ref, y2_ref, y3_ref, gl_ref, x_ref, wb_ref, wo_ref, o_ref):
    d = x_ref.shape[1]
    merged = None
    for n, y_ref in enumerate((y0_ref, y1_ref, y2_ref, y3_ref)):
        term = _mm(y_ref[...], wb_ref[n]) * _sigmoid(gl_ref[:, n * d:(n + 1) * d].astype(F32))
        merged = term if merged is None else merged + term
    o_ref[...] = x_ref[...] + _mm(merged, wo_ref[...])


def _merge(ys, z2, x2, wb, wo, l):
    n, d = x2.shape
    tm = min(512, n)
    ytile = pl.BlockSpec((tm, C), lambda i: (i, 0))
    return pl.pallas_call(
        _merge_kernel,
        grid=(n // tm,),
        in_specs=[ytile, ytile, ytile, ytile,
                  pl.BlockSpec((tm, 4 * d), lambda i: (i, COL_GATE)),
                  pl.BlockSpec((tm, d), lambda i: (i, 0)),
                  _layer_spec(wb, l), _layer_spec(wo, l)],
        out_specs=pl.BlockSpec((tm, d), lambda i: (i, 0)),
        out_shape=jax.ShapeDtypeStruct((n, d), F32),
        compiler_params=_cparams(("parallel",)),
        name="merge",
    )(*ys, z2, x2, wb, wo)


def _mlp_kernel(x_ref, g_ref, w1_ref, w2_ref, o_ref, h_scr, acc):
    j = pl.program_id(1)

    @pl.when(j == 0)
    def _():
        x = x_ref[...]
        ms = jnp.mean(x * x, axis=-1, keepdims=True)
        h_scr[...] = (x * lax.rsqrt(ms + EPS) * g_ref[...]).astype(BF16)
        acc[...] = x

    a = jnp.maximum(jnp.dot(h_scr[...], w1_ref[...], preferred_element_type=F32), 0.0)
    acc[...] += _mm(a * a, w2_ref[...])

    @pl.when(j == pl.num_programs(1) - 1)
    def _():
        o_ref[...] = acc[...]


def _mlp(x2, g, w1, w2, l):
    n, d = x2.shape
    dff = w1.shape[2]
    tm = min(1024, n)
    tf = 1024
    return pl.pallas_call(
        _mlp_kernel,
        grid=(n // tm, dff // tf),
        in_specs=[pl.BlockSpec((tm, d), lambda i, j: (i, 0)),
                  pl.BlockSpec((None, 1, d), lambda i, j: (l, 0, 0)),
                  pl.BlockSpec((None, d, tf), lambda i, j: (l, 0, j)),
                  pl.BlockSpec((None, tf, d), lambda i, j: (l, j, 0))],
        out_specs=pl.BlockSpec((tm, d), lambda i, j: (i, 0)),
        out_shape=jax.ShapeDtypeStruct((n, d), F32),
        scratch_shapes=[pltpu.VMEM((tm, d), BF16), pltpu.VMEM((tm, d), F32)],
        compiler_params=_cparams(("parallel", "arbitrary")),
        name="mlp",
    )(x2, g, w1, w2)


def _prep_w_in(w_in):
    nl, d, _ = w_in.shape
    gate0 = 6 * C + W_LORA + A_LORA + G_LORA + Q_LORA + KV_LORA + QK_ROPE
    kr0 = gate0 - QK_ROPE
    parts = [
        w_in[:, :, gate0:],
        w_in[:, :, :kr0],
        w_in[:, :, kr0:kr0 + ROT_HALF],
        jnp.zeros((nl, d, NOPE_A), w_in.dtype),
        w_in[:, :, kr0 + ROT_HALF:gate0],
        jnp.zeros((nl, d, D_INP - COL_KR - SLOT // 2 - ROT_HALF), w_in.dtype),
    ]
    return jnp.concatenate(parts, axis=-1).astype(BF16)


def _prep_lru_gate(gate_w):
    nl, _, nh, hd, _ = gate_w.shape
    eye = jnp.eye(nh, dtype=gate_w.dtype)
    dense = jnp.einsum("lghij,hk->lghikj", gate_w, eye).reshape(nl, 2, nh * hd, nh * hd)
    return jnp.concatenate([dense[:, 0], dense[:, 1]], axis=-1).astype(BF16)


def kernel(x, norm_mix, w_in, lru_conv_w, lru_conv_b, lru_gate_w, lru_gate_b, lru_lambda, s5_a_re, s5_a_im, s5_b_re, s5_b_im, s5_c_re, s5_c_im, s5_d, s5_log_dt, s5_w_glu, rwkv_mu_rkv, rwkv_mu_w, rwkv_mu_a, rwkv_mu_g, rwkv_w0, rwkv_w2, rwkv_a0, rwkv_a2, rwkv_g2, rwkv_k_k, rwkv_k_a, rwkv_r_k, rwkv_lnx_g, rwkv_lnx_b, mla_q_norm, mla_w_uq, mla_kv_norm, mla_w_ukv, mla_qk_norm_q, mla_qk_norm_k, w_branch, w_out, norm_mlp, w_ff1, w_ff2):
    b, t, d = x.shape
    depth = w_in.shape[0]
    n = b * t
    row = lambda a: a.reshape(depth, 1, -1).astype(F32)

    w_in_p = _prep_w_in(w_in)
    lru_gw = _prep_lru_gate(lru_gate_w)
    s5_tabs = _prep_s5(s5_a_re, s5_a_im, s5_b_re, s5_b_im, s5_c_re, s5_c_im, s5_log_dt, t)
    rw_prm = _prep_rwkv(rwkv_mu_rkv, rwkv_mu_w, rwkv_mu_a, rwkv_mu_g, rwkv_w0, rwkv_w2, rwkv_a0, rwkv_a2, rwkv_g2,
                        rwkv_k_k, rwkv_k_a, rwkv_r_k, rwkv_lnx_g, rwkv_lnx_b)
    mla_prm = _prep_mla(mla_q_norm, mla_w_uq, mla_kv_norm, mla_w_ukv)
    rope = _rope_tables(t, mla_qk_norm_q, mla_qk_norm_k)
    g_mix, g_mlp = row(norm_mix), row(norm_mlp)
    conv_b, gate_b, lam, s5_dr = row(lru_conv_b), row(lru_gate_b), row(lru_lambda), row(s5_d)
    w_glu, wb, wo = s5_w_glu.astype(BF16), w_branch.astype(BF16), w_out.astype(BF16)
    w1, w2 = w_ff1.astype(BF16), w_ff2.astype(BF16)

    lru_prm = (lru_conv_w.astype(F32), conv_b, lru_gw, gate_b, lam)
    x2 = x.reshape(n, d)
    for l in range(depth):
        z2, u2 = _inproj(x2, g_mix, w_in_p, l)
        z = z2.reshape(b, t, -1)
        y_lru = _lru_branch(z, lru_prm, l)
        y_s5 = _s5_branch(u2.reshape(b, t, C), s5_tabs, s5_dr, w_glu, l)
        y_rw = _rwkv_branch(z, rw_prm, l)
        y_mla = _mla_branch(z, mla_prm, rope, l)
        ys = [y.reshape(n, C) for y in (y_lru, y_s5, y_rw, y_mla)]
        x2 = _merge(ys, z2, x2, wb, wo, l)
        x2 = _mlp(x2, g_mlp, w1, w2, l)
    return x2.reshape(b, t, d)
```

```python
import functools
import math

import jax
import jax.numpy as jnp
import numpy as np
from jax import lax
from jax.experimental import pallas as pl
from jax.experimental.pallas import tpu as pltpu

F32 = jnp.float32
BF16 = jnp.bfloat16

EPS = 1e-6
LRU_HEADS = 8
CONV_W = 4
LRU_C = 8.0
S5_GROUP = 16
S5_STATE = 64
RWKV_HD = 64
W_LORA, A_LORA, G_LORA = 64, 64, 128
RWKV_LN_EPS = 64e-5
MLA_HEADS = 8
QK_NOPE, QK_ROPE, V_HD = 64, 32, 64
QK_HD = QK_NOPE + QK_ROPE
Q_LORA, KV_LORA = 256, 128
ROPE_THETA = 10000.0

LANES = 128
SUBLANES = 8
VMEM_LIMIT = 56 * 1024 * 1024

D_MODEL = 1024
C = D_MODEL // 2
COL_GATE = 0
COL_LRU_X = 4 * D_MODEL
COL_LRU_G = COL_LRU_X + C
COL_S5_U = COL_LRU_G + C
COL_RW_R = COL_S5_U + C
COL_RW_K = COL_RW_R + C
COL_RW_V = COL_RW_K + C
COL_LORA = COL_RW_V + C
COL_CQ = COL_LORA + 256
COL_CKV = COL_CQ + Q_LORA
COL_KR = COL_CKV + KV_LORA
D_INP = 8 * D_MODEL


def _cparams(sem):
    return pltpu.CompilerParams(dimension_semantics=sem, vmem_limit_bytes=VMEM_LIMIT)


def _gelu_tanh(x):
    return 0.5 * x * (1.0 + jnp.tanh(math.sqrt(2.0 / math.pi) * (x + 0.044715 * (x * x * x))))


def _sigmoid(x):
    return 1.0 / (1.0 + jnp.exp(-x))


def _softplus(x):
    return jnp.maximum(x, 0.0) + jnp.log(1.0 + jnp.exp(-jnp.abs(x)))


HALO = 16


def _shift_rows(x, halo, k):
    xe = jnp.concatenate([halo, x], axis=0)
    n = x.shape[0]
    return xe[HALO - k:HALO - k + n, :]


def _layer_spec(arr, l):
    zeros = (0,) * (arr.ndim - 1)
    return pl.BlockSpec((None,) + arr.shape[1:], lambda *idx: (l,) + zeros)


def _inproj_kernel(x_ref, g_ref, w_ref, o_ref, u_ref, h_scr):
    j = pl.program_id(1)

    @pl.when(j == 0)
    def _():
        x = x_ref[...]
        ms = jnp.mean(x * x, axis=-1, keepdims=True)
        h_scr[...] = (x * lax.rsqrt(ms + EPS) * g_ref[...]).astype(BF16)

    y = jnp.dot(h_scr[...], w_ref[...], preferred_element_type=F32)
    o_ref[...] = y.astype(BF16)
    tn = w_ref.shape[1]
    u0 = COL_S5_U % tn

    @pl.when(j == COL_S5_U // tn)
    def _():
        u_ref[...] = y[:, u0:u0 + C]


INPROJ_TN = 2048


def _inproj(x2, g, w, l):
    n, d = x2.shape
    dn = w.shape[2]
    tm = min(1024, n)
    tn = INPROJ_TN
    assert COL_S5_U % tn + C <= tn and dn % tn == 0
    return pl.pallas_call(
        _inproj_kernel,
        grid=(n // tm, dn // tn),
        in_specs=[
            pl.BlockSpec((tm, d), lambda i, j: (i, 0)),
            pl.BlockSpec((None, 1, d), lambda i, j: (l, 0, 0)),
            pl.BlockSpec((None, d, tn), lambda i, j: (l, 0, j)),
        ],
        out_specs=[pl.BlockSpec((tm, tn), lambda i, j: (i, j)),
                   pl.BlockSpec((tm, C), lambda i, j: (i, 0))],
        out_shape=[jax.ShapeDtypeStruct((n, dn), BF16), jax.ShapeDtypeStruct((n, C), F32)],
        scratch_shapes=[pltpu.VMEM((tm, d), BF16)],
        compiler_params=_cparams(("parallel", "arbitrary")),
        name="inproj",
    )(x2, g, w)


def _lru_kernel(x_ref, xh_ref, g_ref, cw_ref, cb_ref, gw_ref, gb_ref, lam_ref, o_ref, carry):
    t = pl.program_id(1)
    tm, c = x_ref.shape

    @pl.when(t == 0)
    def _():
        carry[...] = jnp.zeros_like(carry)

    x = x_ref[...].astype(F32)
    halo = jnp.where(t == 0, 0.0, xh_ref[...].astype(F32))
    cw = cw_ref[...]
    xc = cb_ref[...] + cw[CONV_W - 1:CONV_W, :] * x
    for k in range(1, CONV_W):
        xc = xc + cw[CONV_W - 1 - k:CONV_W - k, :] * _shift_rows(x, halo, k)

    gates = jnp.dot(xc.astype(BF16), gw_ref[...], preferred_element_type=F32) + gb_ref[...]
    r = _sigmoid(gates[:, :c])
    i = _sigmoid(gates[:, c:])
    log_a = (-LRU_C) * r * _softplus(-lam_ref[...])
    a = jnp.exp(log_a)
    u = jnp.sqrt(1.0 - a * a) * (i * xc)

    sub = lax.broadcasted_iota(jnp.int32, (tm, c), 0) % SUBLANES
    s = 1
    while s < SUBLANES:
        a_sh = jnp.where(sub >= s, pltpu.roll(a, s, axis=0), 1.0)
        u_sh = jnp.where(sub >= s, pltpu.roll(u, s, axis=0), 0.0)
        u = u + a * u_sh
        a = a * a_sh
        s *= 2
    hp = carry[SUBLANES - 1:SUBLANES, :]
    blocks = []
    for blk in range(tm // SUBLANES):
        rows = slice(blk * SUBLANES, (blk + 1) * SUBLANES)
        hb = u[rows, :] + a[rows, :] * hp
        hp = hb[SUBLANES - 1:SUBLANES, :]
        blocks.append(hb)
    h = jnp.concatenate(blocks, axis=0)
    carry[...] = blocks[-1]
    o_ref[...] = (h * _gelu_tanh(g_ref[...].astype(F32))).astype(o_ref.dtype)


def _lru_branch(z, prm, l):
    b, t, _ = z.shape
    tm = min(512, t)
    cb = COL_LRU_X // C
    hb = tm // HALO
    return pl.pallas_call(
        _lru_kernel,
        grid=(b, t // tm),
        in_specs=[
            pl.BlockSpec((None, tm, C), lambda i, j: (i, j, cb)),
            pl.BlockSpec((None, HALO, C), lambda i, j: (i, jnp.maximum(j * hb - 1, 0), cb)),
            pl.BlockSpec((None, tm, C), lambda i, j: (i, j, cb + 1)),
        ] + [_layer_spec(a, l) for a in prm],
        out_specs=pl.BlockSpec((None, tm, C), lambda i, j: (i, j, 0)),
        out_shape=jax.ShapeDtypeStruct((b, t, C), BF16),
        scratch_shapes=[pltpu.VMEM((SUBLANES, C), F32)],
        compiler_params=_cparams(("parallel", "arbitrary")),
        name="rglru",
    )(z, z, z, *prm)


S5_L = 8
S5_OCT = LANES // S5_GROUP


def _s5_kernel(u_ref, win_ref, bd_ref, wout_ref, lstep_ref, lblk_ref, y_ref, uf, wintra, carry):
    t = pl.program_id(2)
    tm = u_ref.shape[0]
    nc = tm // S5_L
    ns = carry.shape[1]
    half = ns // 2

    @pl.when((pl.program_id(1) == 0) & (t == 0))
    def _():
        wintra[...] = jnp.zeros_like(wintra)
        for i in range(S5_L):
            for j in range(i, S5_L):
                wintra[i * LANES:(i + 1) * LANES, j * LANES:(j + 1) * LANES] = bd_ref[j - i]

    @pl.when(t == 0)
    def _():
        carry[...] = jnp.zeros_like(carry)

    for i in range(S5_L):
        uf[:, i * LANES:(i + 1) * LANES] = u_ref[pl.ds(i, nc, stride=S5_L), :].astype(BF16)
    ufv = uf[...]
    s = jnp.dot(ufv, win_ref[...], preferred_element_type=F32)

    def cmul(v, ar, ai):
        sw = jnp.concatenate([v[:, half:], v[:, :half]], axis=1)
        return v * ar + sw * ai

    row = lax.broadcasted_iota(jnp.int32, (nc, ns), 0)
    sub = row % SUBLANES
    k = 0
    st = 1
    while st < SUBLANES:
        sh = jnp.where(sub >= st, pltpu.roll(s, st, axis=0), 0.0)
        s = s + cmul(sh, lstep_ref[2 * k:2 * k + 1, :], lstep_ref[2 * k + 1:2 * k + 2, :])
        k += 1
        st *= 2
    cin = carry[0:1, :]
    prev = cin
    blocks = []
    for blk in range(nc // SUBLANES):
        sb = s[blk * SUBLANES:(blk + 1) * SUBLANES, :] + cmul(jnp.broadcast_to(prev, (SUBLANES, ns)),
                                                              lblk_ref[0], lblk_ref[1])
        prev = sb[SUBLANES - 1:SUBLANES, :]
        blocks.append(sb)
    s = jnp.concatenate(blocks, axis=0)
    carry[...] = jnp.broadcast_to(prev, carry.shape)
    sprev = jnp.where(row >= 1, pltpu.roll(s, 1, axis=0), cin)

    yf = jnp.dot(ufv, wintra[...], preferred_element_type=F32)
    yf = yf + jnp.dot(sprev.astype(BF16), wout_ref[...], preferred_element_type=F32)
    for j in range(S5_L):
        y_ref[pl.ds(j, nc, stride=S5_L), :] = yf[:, j * LANES:(j + 1) * LANES]


def _s5_tail_kernel(y_ref, u_ref, d_ref, w_ref, o_ref):
    c = u_ref.shape[1]
    y = _gelu_tanh(y_ref[...] + d_ref[...] * u_ref[...])
    z = jnp.dot(y.astype(BF16), w_ref[...], preferred_element_type=F32)
    o_ref[...] = (z[:, :c] * _sigmoid(z[:, c:])).astype(o_ref.dtype)


def _s5_tile(t):
    return min(4096, t)


def _s5_branch(u, tabs, d_skip, w_glu, l):
    win, bd, wout, lstep, lblk = tabs
    b, t, _ = u.shape
    tm = _s5_tile(t)
    nc = tm // S5_L
    noct = C // LANES
    ns = win.shape[-1]
    fl = S5_L * LANES

    def table(arr):
        zeros = (0,) * (arr.ndim - 2)
        return pl.BlockSpec((None, None) + arr.shape[2:], lambda q, i, j: (l, q) + zeros)

    y = pl.pallas_call(
        _s5_kernel,
        grid=(noct, b, t // tm),
        in_specs=[pl.BlockSpec((None, tm, LANES), lambda q, i, j: (i, j, q)),
                  table(win), table(bd), table(wout), table(lstep), table(lblk)],
        out_specs=pl.BlockSpec((None, tm, LANES), lambda q, i, j: (i, j, q)),
        out_shape=jax.ShapeDtypeStruct((b, t, C), F32),
        scratch_shapes=[pltpu.VMEM((nc, fl), BF16), pltpu.VMEM((fl, fl), BF16), pltpu.VMEM((SUBLANES, ns), F32)],
        compiler_params=_cparams(("arbitrary", "arbitrary", "arbitrary")),
        name="s5_ssm",
    )(u, win, bd, wout, lstep, lblk)

    tt = min(2048, t)
    return pl.pallas_call(
        _s5_tail_kernel,
        grid=(b, t // tt),
        in_specs=[
            pl.BlockSpec((None, tt, C), lambda i, j: (i, j, 0)),
            pl.BlockSpec((None, tt, C), lambda i, j: (i, j, 0)),
            _layer_spec(d_skip, l),
            _layer_spec(w_glu, l),
        ],
        out_specs=pl.BlockSpec((None, tt, C), lambda i, j: (i, j, 0)),
        out_shape=jax.ShapeDtypeStruct((b, t, C), BF16),
        compiler_params=_cparams(("parallel", "parallel")),
        name="s5_glu",
    )(y, u, d_skip, w_glu)


def _prep_s5(a_re, a_im, b_re, b_im, c_re, c_im, log_dt, t):
    nl, g, p = a_re.shape
    m = S5_GROUP
    noct = g // S5_OCT
    nc = _s5_tile(t) // S5_L
    lam = lax.complex(a_re.astype(F32), a_im.astype(F32))
    dt = jnp.exp(log_dt.astype(F32))[..., None]
    ldt = lam * dt
    lam_bar = jnp.exp(ldt)
    b_bar = ((lam_bar - 1.0) / lam)[..., None] * lax.complex(b_re.astype(F32), b_im.astype(F32))
    cc = lax.complex(c_re.astype(F32), c_im.astype(F32))

    def powers(e):
        return jnp.exp(ldt[..., None] * e.astype(F32).reshape((1, 1, 1, -1)))

    steps = jnp.arange(S5_L)
    same = jnp.eye(S5_OCT, dtype=bool)

    def octet_bd(x):
        x = jnp.moveaxis(x, 2, 3)
        x = jnp.expand_dims(x, 6)
        x = jnp.where(same.reshape(1, 1, 1, S5_OCT, 1, 1, S5_OCT, 1), x, 0.0)
        nrow = int(np.prod(x.shape[2:5]))
        return x.astype(BF16).reshape(nl, noct, nrow, -1)

    kd = jnp.einsum("lgop,lgpd,lgpi->lgdio", cc, powers(steps), b_bar).real
    kd = kd.reshape(nl, noct, S5_OCT, S5_L, m, 1, m)
    bd = octet_bd(kd).reshape(nl, noct, S5_L, LANES, LANES)

    wi = powers(S5_L - 1 - steps)[..., None] * b_bar[:, :, :, None, :]
    wi = jnp.stack([wi.real, wi.imag], axis=2)
    wi = jnp.transpose(wi, (0, 1, 4, 5, 2, 3)).reshape(nl, noct, S5_OCT, S5_L, m, 2, p)
    win = octet_bd(wi)

    wo = cc[:, :, :, :, None] * powers(steps + 1)[:, :, None, :, :]
    wo = jnp.stack([wo.real, -wo.imag], axis=2)
    wo = jnp.transpose(wo, (0, 1, 2, 4, 5, 3)).reshape(nl, noct, S5_OCT, 2, p, S5_L, m)
    wout = octet_bd(wo)

    def state_mult(e):
        pw = powers(e)
        pw = jnp.transpose(pw, (0, 3, 1, 2)).reshape(nl, e.shape[0], noct, S5_OCT * p)
        pw = jnp.moveaxis(pw, 2, 1)
        return (jnp.concatenate([pw.real, pw.real], axis=-1), jnp.concatenate([-pw.imag, pw.imag], axis=-1))

    nsteps = int(math.log2(SUBLANES))
    ar, ai = state_mult(S5_L * (2 ** jnp.arange(nsteps)))
    lstep = jnp.stack([ar, ai], axis=3).reshape(nl, noct, 2 * nsteps, -1)
    lblk = jnp.stack(state_mult(S5_L * (jnp.arange(SUBLANES) + 1)), axis=2)
    return win, bd, wout, lstep, lblk


RW_L = 64
HIGHEST = lax.Precision.HIGHEST


def _mm(a, b, dims=(((1,), (0,)), ((), ()))):
    return lax.dot_general(a.astype(BF16), b.astype(BF16), dims, preferred_element_type=F32)


def _mm_nt(a, b):
    return _mm(a, b, (((1,), (1,)), ((), ())))


def _mm_tn(a, b):
    return _mm(a, b, (((0,), (0,)), ((), ())))


def _mm_exact(a, b):
    return lax.dot_general(a, b, (((1,), (0,)), ((), ())), precision=HIGHEST, preferred_element_type=F32)


def _head_sum(x, m0):
    s0 = jnp.sum(jnp.where(m0, x, 0.0), axis=-1, keepdims=True)
    s1 = jnp.sum(jnp.where(m0, 0.0, x), axis=-1, keepdims=True)
    return jnp.where(m0, s0, s1)


def _rwkv_kernel(r_ref, k_ref, v_ref, lo_ref, rh_ref, kh_ref, vh_ref, loh_ref, mu_ref, mul_ref, w0_ref, a0_ref,
                 w2_ref, a2_ref, g2_ref, kk_ref, ka_ref, rk_ref, lng_ref, lnb_ref, o_ref, hstate):
    t = pl.program_id(1)
    tm, c = r_ref.shape
    npair = c // LANES
    ll = RW_L

    @pl.when(t == 0)
    def _():
        hstate[...] = jnp.zeros_like(hstate)

    def mix(ref, href, mu):
        p = ref[...].astype(F32)
        prev = _shift_rows(p, jnp.where(t == 0, 0.0, href[...].astype(F32)), 1)
        return p + (prev - p) * mu

    mu = mu_ref[...]
    r = mix(r_ref, rh_ref, mu[0:1, :])
    k = mix(k_ref, kh_ref, mu[1:2, :])
    v = mix(v_ref, vh_ref, mu[2:3, :])
    lo = mix(lo_ref, loh_ref, mul_ref[...])
    wlog = -_softplus(-(w0_ref[...] + _mm(jnp.tanh(lo), w2_ref[...]))) - 0.5
    lw = -jnp.exp(wlog)
    a = _sigmoid(a0_ref[...] + _mm(lo, a2_ref[...]))
    g = _mm(_sigmoid(lo), g2_ref[...])
    kk = k * kk_ref[...]
    k = k * (1.0 + (a - 1.0) * ka_ref[...])

    lane = lax.broadcasted_iota(jnp.int32, (ll, LANES), 1)
    m0 = lane < RWKV_HD
    ri = lax.broadcasted_iota(jnp.int32, (2 * ll, 2 * ll), 0)
    ci = lax.broadcasted_iota(jnp.int32, (2 * ll, 2 * ll), 1)
    same = (ri >= ll) == (ci >= ll)
    strict = same & (ri > ci)
    incl = same & (ri >= ci)
    eye = ri == ci
    tri = (lax.broadcasted_iota(jnp.int32, (ll, ll), 0) >= lax.broadcasted_iota(jnp.int32, (ll, ll), 1)).astype(F32)

    def st(x):
        return jnp.concatenate([jnp.where(m0, x, 0.0), jnp.where(m0, 0.0, x)], axis=0)

    nch = tm // ll
    pcs = [(ch, p) for ch in range(nch) for p in range(npair)]
    cum_all = [_mm_exact(tri, lw[ch * ll:(ch + 1) * ll, :]) for ch in range(nch)]
    pre = {}
    for (ch, p) in pcs:
        rows, cols = slice(ch * ll, (ch + 1) * ll), slice(p * LANES, (p + 1) * LANES)
        rp, kp, vp, ap, lwp, cum = r[rows, cols], k[rows, cols], v[rows, cols], a[rows, cols], lw[rows, cols], cum_all[ch][:, cols]
        kkp = kk[rows, cols]
        kkp = kkp * lax.rsqrt(_head_sum(kkp * kkp, m0) + 1e-12)
        cum_l = cum[ll - 1:ll, :]
        e_neg = jnp.exp(-cum)
        e_last = jnp.exp(cum_l - cum)
        kb = kkp * ap
        pre[ch, p] = dict(
            rows=rows, cols=cols, rp=rp, kp=kp, vp=vp,
            a_st=st(-kkp * jnp.exp(cum - lwp)).astype(BF16), r_st=st(rp * jnp.exp(cum)),
            v_st=st(vp).astype(BF16), b_st=st(kb * e_neg).astype(BF16), k_st=st(kp * e_neg).astype(BF16),
            bh_st=st(kb * e_last).astype(BF16), kh_st=st(kp * e_last).astype(BF16), g_l=jnp.exp(cum_l))

    gram = {pc: _mm_nt(jnp.concatenate([pre[pc]["a_st"], pre[pc]["r_st"].astype(BF16)], axis=0),
                       jnp.concatenate([pre[pc]["b_st"], pre[pc]["k_st"]], axis=0)) for pc in pcs}
    a_ab = {pc: jnp.where(strict, gram[pc][:2 * ll, :2 * ll], 0.0) for pc in pcs}
    a_ak = {pc: jnp.where(strict, gram[pc][:2 * ll, 2 * ll:], 0.0).astype(BF16) for pc in pcs}
    m_rb = {pc: jnp.where(incl, gram[pc][2 * ll:, :2 * ll], 0.0).astype(BF16) for pc in pcs}
    m_rk = {pc: jnp.where(incl, gram[pc][2 * ll:, 2 * ll:], 0.0).astype(BF16) for pc in pcs}
    akv = {pc: _mm(a_ak[pc], pre[pc]["v_st"]) for pc in pcs}
    tinv = {pc: jnp.where(eye, 1.0, a_ab[pc]) for pc in pcs}
    pw = {pc: a_ab[pc].astype(BF16) for pc in pcs}
    for _ in range(int(math.log2(ll)) - 1):
        pw = {pc: _mm(pw[pc], pw[pc]).astype(BF16) for pc in pcs}
        tinv = {pc: tinv[pc] + _mm(tinv[pc], pw[pc]) for pc in pcs}
    xa = {pc: _mm(tinv[pc], jnp.concatenate([pre[pc]["a_st"], akv[pc].astype(BF16)], axis=1)).astype(BF16)
          for pc in pcs}
    ma = {pc: _mm(m_rb[pc], xa[pc][:, :LANES]) for pc in pcs}
    y_loc = {pc: _mm(jnp.concatenate([m_rb[pc], m_rk[pc]], axis=1),
                     jnp.concatenate([xa[pc][:, LANES:], pre[pc]["v_st"]], axis=0)) for pc in pcs}
    ba = {pc: _mm_tn(pre[pc]["bh_st"], xa[pc][:, :LANES]) for pc in pcs}
    f_loc = {pc: _mm_tn(jnp.concatenate([pre[pc]["bh_st"], pre[pc]["kh_st"]], axis=0),
                        jnp.concatenate([xa[pc][:, LANES:], pre[pc]["v_st"]], axis=0)) for pc in pcs}
    for (ch, p) in pcs:
        pc = (ch, p)
        q = pre[pc]
        r_eff = q["r_st"] + ma[pc]
        g_eff = jnp.where(eye, q["g_l"], 0.0) + ba[pc]
        hs = _mm(jnp.concatenate([g_eff, r_eff], axis=0), hstate[p])
        hstate[p] = hs[:2 * ll, :] + f_loc[pc]
        y_st = hs[2 * ll:, :] + y_loc[pc]
        y = y_st[:ll, :] + y_st[ll:, :]
        rows, cols = q["rows"], q["cols"]
        mean = _head_sum(y, m0) * (1.0 / RWKV_HD)
        d = y - mean
        var = _head_sum(d * d, m0) * (1.0 / RWKV_HD)
        yn = d * lax.rsqrt(var + RWKV_LN_EPS) * lng_ref[:, cols] + lnb_ref[:, cols]
        bonus = _head_sum(q["rp"] * q["kp"] * rk_ref[:, cols], m0) * q["vp"]
        o_ref[rows, cols] = ((yn + bonus) * g[rows, cols]).astype(o_ref.dtype)


def _rwkv_branch(z, prm, l):
    b, t, _ = z.shape
    tm = min(4 * RW_L, t)
    hb = tm // HALO
    cr = COL_RW_R // C
    cl = COL_LORA // 256
    npair = C // LANES

    def tile(cb, w):
        return pl.BlockSpec((None, tm, w), lambda i, j: (i, j, cb))

    def halo(cb, w):
        return pl.BlockSpec((None, HALO, w), lambda i, j: (i, jnp.maximum(j * hb - 1, 0), cb))

    return pl.pallas_call(
        _rwkv_kernel,
        grid=(b, t // tm),
        in_specs=[tile(cr, C), tile(cr + 1, C), tile(cr + 2, C), tile(cl, 256),
                  halo(cr, C), halo(cr + 1, C), halo(cr + 2, C), halo(cl, 256)]
                 + [_layer_spec(a, l) for a in prm],
        out_specs=pl.BlockSpec((None, tm, C), lambda i, j: (i, j, 0)),
        out_shape=jax.ShapeDtypeStruct((b, t, C), BF16),
        scratch_shapes=[pltpu.VMEM((npair, LANES, LANES), F32)],
        compiler_params=_cparams(("parallel", "arbitrary")),
        name="rwkv7",
    )(z, z, z, z, z, z, z, z, *prm)


def _prep_rwkv(mu_rkv, mu_w, mu_a, mu_g, w0, w2, a0, a2, g2, k_k, k_a, r_k, lnx_g, lnx_b):
    nl = mu_rkv.shape[0]
    row = lambda x: x.reshape(nl, 1, -1).astype(F32)
    mul = jnp.concatenate([mu_w, mu_a, mu_g], axis=-1)
    zw = lambda n: jnp.zeros((nl, n, C), F32)
    w2p = jnp.concatenate([w2, zw(A_LORA + G_LORA)], axis=1).astype(BF16)
    a2p = jnp.concatenate([zw(W_LORA), a2, zw(G_LORA)], axis=1).astype(BF16)
    g2p = jnp.concatenate([zw(W_LORA + A_LORA), g2], axis=1).astype(BF16)
    return (mu_rkv.astype(F32), row(mul), row(w0), row(a0), w2p, a2p, g2p, row(k_k), row(k_a), row(r_k),
            row(lnx_g), row(lnx_b))


SLOT = LANES
ROT_HALF = QK_ROPE // 2
NOPE_A = SLOT // 2 - ROT_HALF


def _slot_source():
    src = np.full((SLOT,), QK_HD, np.int32)
    src[:ROT_HALF] = QK_NOPE + np.arange(ROT_HALF)
    src[ROT_HALF:SLOT // 2] = np.arange(NOPE_A)
    src[SLOT // 2:SLOT // 2 + ROT_HALF] = QK_NOPE + ROT_HALF + np.arange(ROT_HALF)
    src[SLOT // 2 + ROT_HALF:SLOT // 2 + ROT_HALF + QK_NOPE - NOPE_A] = NOPE_A + np.arange(QK_NOPE - NOPE_A)
    return src


def _to_slot(v):
    pad = jnp.concatenate([v, jnp.zeros(v.shape[:-1] + (1,), v.dtype)], axis=-1)
    return jnp.take(pad, jnp.asarray(_slot_source()), axis=-1)
VT_ROWS = V_HD + 16
FLASH_AHEAD = 1
FLASH_QSPLIT = 1


def _mla_prep_kernel(cq_ref, ckv_ref, kr_ref, tab_ref, qn_ref, wq_ref, kvn_ref, wk_ref, wv_ref,
                     q_ref, k_ref, v_ref, v_scr):
    def rms(x, g, n):
        ms = jnp.sum(x * x, axis=-1, keepdims=True) * (1.0 / n)
        return x * lax.rsqrt(ms + EPS) * g

    def norm_rope(x, c_tab, s_tab):
        ms = jnp.sum(x * x, axis=-1, keepdims=True) * (1.0 / QK_HD)
        return (x * c_tab + pltpu.roll(x, SLOT // 2, axis=1) * s_tab) * lax.rsqrt(ms + EPS)

    cq = rms(cq_ref[...].astype(F32), qn_ref[...], Q_LORA)
    ckv = rms(ckv_ref[...].astype(F32), kvn_ref[...], KV_LORA)
    qf = _mm(cq, wq_ref[...])
    kf = _mm(ckv, wk_ref[...])
    v_scr[...] = _mm(ckv, wv_ref[...])
    vt = jnp.transpose(v_scr[...]).astype(BF16)
    ones = jnp.ones((VT_ROWS - V_HD, vt.shape[1]), BF16)
    for h in range(MLA_HEADS):
        v_ref[h * VT_ROWS:h * VT_ROWS + V_HD, :] = vt[h * V_HD:(h + 1) * V_HD, :]
        v_ref[h * VT_ROWS + V_HD:(h + 1) * VT_ROWS, :] = ones
    kr = kr_ref[...].astype(F32)
    cq_tab, sq_tab, ck_tab, sk_tab = tab_ref[0], tab_ref[1], tab_ref[2], tab_ref[3]
    for h in range(MLA_HEADS):
        cols = slice(h * SLOT, (h + 1) * SLOT)
        q_ref[:, cols] = norm_rope(qf[:, cols], cq_tab, sq_tab).astype(BF16)
        k_ref[:, cols] = norm_rope(kf[:, cols] + kr, ck_tab, sk_tab).astype(BF16)


def _flash_kernel(qi_ref, kj_ref, q_ref, k_ref, vt_ref, o_ref, m_scr, acc):
    i = qi_ref[pl.program_id(1)]
    j = kj_ref[pl.program_id(1)]
    bq = q_ref.shape[0]
    bk = k_ref.shape[0]

    @pl.when(j == 0)
    def _():
        m_scr[...] = jnp.full_like(m_scr, -jnp.inf)
        acc[...] = jnp.zeros_like(acc)

    def step(masked):
        if masked:
            causal = (lax.broadcasted_iota(jnp.int32, (bk, bq), 0) <= lax.broadcasted_iota(jnp.int32, (bk, bq), 1))
        qw = bq // FLASH_QSPLIT
        units = [(h, c) for h in range(MLA_HEADS) for c in range(FLASH_QSPLIT)]

        def scores(u):
            h, c = u
            cols = slice(h * SLOT, (h + 1) * SLOT)
            return _mm_nt(k_ref[:, cols], q_ref[c * qw:(c + 1) * qw, cols])

        pending = [scores(u) for u in units[:FLASH_AHEAD]]
        for n, (h, c) in enumerate(units):
            rows = slice(h * VT_ROWS, (h + 1) * VT_ROWS)
            qs = slice(c * qw, (c + 1) * qw)
            st = pending.pop(0)
            if n + FLASH_AHEAD < len(units):
                pending.append(scores(units[n + FLASH_AHEAD]))
            if masked:
                st = jnp.where(causal[:, qs], st, -jnp.inf)
            m_prev = m_scr[h:h + 1, qs]
            m_new = jnp.maximum(m_prev, jnp.max(st, axis=0, keepdims=True))
            alpha = jnp.exp2(m_prev - m_new)
            pt = jnp.exp2((st - m_new).astype(BF16))
            m_scr[h:h + 1, qs] = m_new
            acc[rows, qs] = acc[rows, qs] * alpha + _mm(vt_ref[rows, :], pt)

    @pl.when(j < i)
    def _():
        step(False)

    @pl.when(j == i)
    def _():
        step(True)
        outs = []
        for h in range(MLA_HEADS):
            r0 = h * VT_ROWS
            outs.append(acc[r0:r0 + V_HD, :] / acc[r0 + V_HD:r0 + V_HD + 1, :])
        o_ref[...] = jnp.transpose(jnp.concatenate(outs, axis=0)).astype(o_ref.dtype)


def _mla_branch(z, prm, rope, l):
    b, t, _ = z.shape
    tm = min(1024, t)
    hs = MLA_HEADS * SLOT
    vr = MLA_HEADS * VT_ROWS

    q, k, vt = pl.pallas_call(
        _mla_prep_kernel,
        grid=(b, t // tm),
        in_specs=[pl.BlockSpec((None, tm, Q_LORA), lambda i, j: (i, j, COL_CQ // Q_LORA)),
                  pl.BlockSpec((None, tm, KV_LORA), lambda i, j: (i, j, COL_CKV // KV_LORA)),
                  pl.BlockSpec((None, tm, LANES), lambda i, j: (i, j, COL_KR // LANES)),
                  pl.BlockSpec((None, 4, tm, SLOT), lambda i, j: (l, 0, j, 0))]
                 + [_layer_spec(a, l) for a in prm],
        out_specs=[pl.BlockSpec((None, tm, hs), lambda i, j: (i, j, 0)),
                   pl.BlockSpec((None, tm, hs), lambda i, j: (i, j, 0)),
                   pl.BlockSpec((None, vr, tm), lambda i, j: (i, 0, j))],
        out_shape=[jax.ShapeDtypeStruct((b, t, hs), BF16), jax.ShapeDtypeStruct((b, t, hs), BF16),
                   jax.ShapeDtypeStruct((b, vr, t), BF16)],
        scratch_shapes=[pltpu.VMEM((tm, C), F32)],
        compiler_params=_cparams(("parallel", "parallel")),
        name="mla_prep",
    )(z, z, z, rope, *prm)

    bq = min(512, t)
    nq = t // bq
    tri = [(i, j) for i in range(nq) for j in range(i + 1)]
    qi = jnp.asarray([p[0] for p in tri], jnp.int32)
    kj = jnp.asarray([p[1] for p in tri], jnp.int32)
    return pl.pallas_call(
        _flash_kernel,
        grid_spec=pltpu.PrefetchScalarGridSpec(
            num_scalar_prefetch=2,
            grid=(b, len(tri)),
            in_specs=[pl.BlockSpec((None, bq, hs), lambda bi, s, qi, kj: (bi, qi[s], 0)),
                      pl.BlockSpec((None, bq, hs), lambda bi, s, qi, kj: (bi, kj[s], 0)),
                      pl.BlockSpec((None, vr, bq), lambda bi, s, qi, kj: (bi, 0, kj[s]))],
            out_specs=pl.BlockSpec((None, bq, C), lambda bi, s, qi, kj: (bi, qi[s], 0)),
            scratch_shapes=[pltpu.VMEM((MLA_HEADS, bq), F32), pltpu.VMEM((vr, bq), F32)]),
        out_shape=jax.ShapeDtypeStruct((b, t, C), BF16),
        compiler_params=_cparams(("parallel", "arbitrary")),
        name="mla_attn",
    )(qi, kj, q, k, vt)


def _rope_tables(t, qk_norm_q, qk_norm_k):
    pos = jnp.arange(t, dtype=F32)
    inv_freq = ROPE_THETA ** (-jnp.arange(0, QK_ROPE, 2, dtype=F32) / QK_ROPE)
    ang = pos[:, None] * inv_freq[None, :]
    cos, sin = jnp.cos(ang), jnp.sin(ang)
    z = lambda n: jnp.zeros((t, n), F32)
    one = lambda n: jnp.ones((t, n), F32)
    tail = SLOT // 2 - ROT_HALF
    c_pos = jnp.concatenate([cos, one(NOPE_A), cos, one(QK_NOPE - NOPE_A), z(SLOT - QK_HD)], axis=1)
    s_pos = jnp.concatenate([-sin, z(NOPE_A), sin, z(tail)], axis=1)
    scale = QK_HD ** -0.5 * math.log2(math.e)

    def pair(g, mult):
        g = _to_slot(g.astype(F32)) * mult
        return [g[:, None, :] * c_pos[None], jnp.roll(g, SLOT // 2, axis=-1)[:, None, :] * s_pos[None]]

    return jnp.stack(pair(qk_norm_q, scale) + pair(qk_norm_k, 1.0), axis=1)


def _prep_mla(q_norm, w_uq, kv_norm, w_ukv):
    nl = q_norm.shape[0]
    nh = MLA_HEADS
    wq = _to_slot(w_uq.reshape(nl, Q_LORA, nh, QK_HD)).reshape(nl, Q_LORA, nh * SLOT).astype(BF16)
    wkv = w_ukv.reshape(nl, KV_LORA, nh, QK_NOPE + V_HD)
    wk = jnp.pad(wkv[..., :QK_NOPE], ((0, 0), (0, 0), (0, 0), (0, QK_ROPE)))
    wk = _to_slot(wk).reshape(nl, KV_LORA, nh * SLOT).astype(BF16)
    wv = wkv[..., QK_NOPE:].reshape(nl, KV_LORA, nh * V_HD).astype(BF16)
    return (q_norm.reshape(nl, 1, -1).astype(F32), wq, kv_norm.reshape(nl, 1, -1).astype(F32), wk, wv)


def _merge_kernel(y0_ref, y1_ref, y2_ref, y3_ref, gl_ref, x_ref, wb_ref, wo_ref, o_ref):
    d = x_ref.shape[1]
    merged = None
    for n, y_ref in enumerate((y0_ref, y1_ref, y2_ref, y3_ref)):
        term = _mm(y_ref[...], wb_ref[n]) * _sigmoid(gl_ref[:, n * d:(n + 1) * d].astype(F32))
        merged = term if merged is None else merged + term
    o_ref[...] = x_ref[...] + _mm(merged, wo_ref[...])


def _merge(ys, z2, x2, wb, wo, l):
    n, d = x2.shape
    tm = min(512, n)
    ytile = pl.BlockSpec((tm, C), lambda i: (i, 0))
    return pl.pallas_call(
        _merge_kernel,
        grid=(n // tm,),
        in_specs=[ytile, ytile, ytile, ytile,
                  pl.BlockSpec((tm, 4 * d), lambda i: (i, COL_GATE)),
                  pl.BlockSpec((tm, d), lambda i: (i, 0)),
                  _layer_spec(wb, l), _layer_spec(wo, l)],
        out_specs=pl.BlockSpec((tm, d), lambda i: (i, 0)),
        out_shape=jax.ShapeDtypeStruct((n, d), F32),
        compiler_params=_cparams(("parallel",)),
        name="merge",
    )(*ys, z2, x2, wb, wo)


def _mlp_kernel(x_ref, g_ref, w1_ref, w2_ref, o_ref, h_scr, acc):
    j = pl.program_id(1)

    @pl.when(j == 0)
    def _():
        x = x_ref[...]
        ms = jnp.mean(x * x, axis=-1, keepdims=True)
        h_scr[...] = (x * lax.rsqrt(ms + EPS) * g_ref[...]).astype(BF16)
        acc[...] = x

    a = jnp.maximum(jnp.dot(h_scr[...], w1_ref[...], preferred_element_type=F32), 0.0)
    acc[...] += _mm(a * a, w2_ref[...])

    @pl.when(j == pl.num_programs(1) - 1)
    def _():
        o_ref[...] = acc[...]


def _mlp(x2, g, w1, w2, l):
    n, d = x2.shape
    dff = w1.shape[2]
    tm = min(1024, n)
    tf = 2048
    return pl.pallas_call(
        _mlp_kernel,
        grid=(n // tm, dff // tf),
        in_specs=[pl.BlockSpec((tm, d), lambda i, j: (i, 0)),
                  pl.BlockSpec((None, 1, d), lambda i, j: (l, 0, 0)),
                  pl.BlockSpec((None, d, tf), lambda i, j: (l, 0, j)),
                  pl.BlockSpec((None, tf, d), lambda i, j: (l, j, 0))],
        out_specs=pl.BlockSpec((tm, d), lambda i, j: (i, 0)),
        out_shape=jax.ShapeDtypeStruct((n, d), F32),
        scratch_shapes=[pltpu.VMEM((tm, d), BF16), pltpu.VMEM((tm, d), F32)],
        compiler_params=_cparams(("parallel", "arbitrary")),
        name="mlp",
    )(x2, g, w1, w2)


def _prep_w_in(w_in):
    nl, d, _ = w_in.shape
    gate0 = 6 * C + W_LORA + A_LORA + G_LORA + Q_LORA + KV_LORA + QK_ROPE
    kr0 = gate0 - QK_ROPE
    parts = [
        w_in[:, :, gate0:],
        w_in[:, :, :kr0],
        w_in[:, :, kr0:kr0 + ROT_HALF],
        jnp.zeros((nl, d, NOPE_A), w_in.dtype),
        w_in[:, :, kr0 + ROT_HALF:gate0],
        jnp.zeros((nl, d, D_INP - COL_KR - SLOT // 2 - ROT_HALF), w_in.dtype),
    ]
    return jnp.concatenate(parts, axis=-1).astype(BF16)


def _prep_lru_gate(gate_w):
    nl, _, nh, hd, _ = gate_w.shape
    eye = jnp.eye(nh, dtype=gate_w.dtype)
    dense = jnp.einsum("lghij,hk->lghikj", gate_w, eye).reshape(nl, 2, nh * hd, nh * hd)
    return jnp.concatenate([dense[:, 0], dense[:, 1]], axis=-1).astype(BF16)


def kernel(x, norm_mix, w_in, lru_conv_w, lru_conv_b, lru_gate_w, lru_gate_b, lru_lambda, s5_a_re, s5_a_im, s5_b_re, s5_b_im, s5_c_re, s5_c_im, s5_d, s5_log_dt, s5_w_glu, rwkv_mu_rkv, rwkv_mu_w, rwkv_mu_a, rwkv_mu_g, rwkv_w0, rwkv_w2, rwkv_a0, rwkv_a2, rwkv_g2, rwkv_k_k, rwkv_k_a, rwkv_r_k, rwkv_lnx_g, rwkv_lnx_b, mla_q_norm, mla_w_uq, mla_kv_norm, mla_w_ukv, mla_qk_norm_q, mla_qk_norm_k, w_branch, w_out, norm_mlp, w_ff1, w_ff2):
    b, t, d = x.shape
    depth = w_in.shape[0]
    n = b * t
    row = lambda a: a.reshape(depth, 1, -1).astype(F32)

    w_in_p = _prep_w_in(w_in)
    lru_gw = _prep_lru_gate(lru_gate_w)
    s5_tabs = _prep_s5(s5_a_re, s5_a_im, s5_b_re, s5_b_im, s5_c_re, s5_c_im, s5_log_dt, t)
    rw_prm = _prep_rwkv(rwkv_mu_rkv, rwkv_mu_w, rwkv_mu_a, rwkv_mu_g, rwkv_w0, rwkv_w2, rwkv_a0, rwkv_a2, rwkv_g2,
                        rwkv_k_k, rwkv_k_a, rwkv_r_k, rwkv_lnx_g, rwkv_lnx_b)
    mla_prm = _prep_mla(mla_q_norm, mla_w_uq, mla_kv_norm, mla_w_ukv)
    rope = _rope_tables(t, mla_qk_norm_q, mla_qk_norm_k)
    g_mix, g_mlp = row(norm_mix), row(norm_mlp)
    conv_b, gate_b, lam, s5_dr = row(lru_conv_b), row(lru_gate_b), row(lru_lambda), row(s5_d)
    w_glu, wb, wo = s5_w_glu.astype(BF16), w_branch.astype(BF16), w_out.astype(BF16)
    w1, w2 = w_ff1.astype(BF16), w_ff2.astype(BF16)

    lru_prm = (lru_conv_w.astype(F32), conv_b, lru_gw, gate_b, lam)
    x2 = x.reshape(n, d)
    for l in range(depth):
        z2, u2 = _inproj(x2, g_mix, w_in_p, l)
        z = z2.reshape(b, t, -1)
        y_lru = _lru_branch(z, lru_prm, l)
        y_s5 = _s5_branch(u2.reshape(b, t, C), s5_tabs, s5_dr, w_glu, l)
        y_rw = _rwkv_branch(z, rw_prm, l)
        y_mla = _mla_branch(z, mla_prm, rope, l)
        ys = [y.reshape(n, C) for y in (y_lru, y_s5, y_rw, y_mla)]
        x2 = _merge(ys, z2, x2, wb, wo, l)
        x2 = _mlp(x2, g_mlp, w1, w2, l)
    return x2.reshape(b, t, d)
```

```python
import functools
import math

import jax
import jax.numpy as jnp
import numpy as np
from jax import lax
from jax.experimental import pallas as pl
from jax.experimental.pallas import tpu as pltpu

F32 = jnp.float32
BF16 = jnp.bfloat16

EPS = 1e-6
LRU_HEADS = 8
CONV_W = 4
LRU_C = 8.0
S5_GROUP = 16
S5_STATE = 64
RWKV_HD = 64
W_LORA, A_LORA, G_LORA = 64, 64, 128
RWKV_LN_EPS = 64e-5
MLA_HEADS = 8
QK_NOPE, QK_ROPE, V_HD = 64, 32, 64
QK_HD = QK_NOPE + QK_ROPE
Q_LORA, KV_LORA = 256, 128
ROPE_THETA = 10000.0

LANES = 128
SUBLANES = 8
VMEM_LIMIT = 56 * 1024 * 1024

D_MODEL = 1024
C = D_MODEL // 2
COL_GATE = 0
COL_LRU_X = 4 * D_MODEL
COL_LRU_G = COL_LRU_X + C
COL_S5_U = COL_LRU_G + C
COL_RW_R = COL_S5_U + C
COL_RW_K = COL_RW_R + C
COL_RW_V = COL_RW_K + C
COL_LORA = COL_RW_V + C
COL_CQ = COL_LORA + 256
COL_CKV = COL_CQ + Q_LORA
COL_KR = COL_CKV + KV_LORA
D_INP = 8 * D_MODEL


def _cparams(sem):
    return pltpu.CompilerParams(dimension_semantics=sem, vmem_limit_bytes=VMEM_LIMIT)


def _gelu_tanh(x):
    return 0.5 * x * (1.0 + jnp.tanh(math.sqrt(2.0 / math.pi) * (x + 0.044715 * (x * x * x))))


def _sigmoid(x):
    return 1.0 / (1.0 + jnp.exp(-x))


def _softplus(x):
    return jnp.maximum(x, 0.0) + jnp.log(1.0 + jnp.exp(-jnp.abs(x)))


HALO = 16


def _shift_rows(x, halo, k):
    xe = jnp.concatenate([halo, x], axis=0)
    n = x.shape[0]
    return xe[HALO - k:HALO - k + n, :]


def _layer_spec(arr, l):
    zeros = (0,) * (arr.ndim - 1)
    return pl.BlockSpec((None,) + arr.shape[1:], lambda *idx: (l,) + zeros)


def _inproj_kernel(x_ref, g_ref, w_ref, o_ref, u_ref, h_scr):
    j = pl.program_id(1)

    @pl.when(j == 0)
    def _():
        x = x_ref[...]
        ms = jnp.mean(x * x, axis=-1, keepdims=True)
        h_scr[...] = (x * lax.rsqrt(ms + EPS) * g_ref[...]).astype(BF16)

    y = jnp.dot(h_scr[...], w_ref[...], preferred_element_type=F32)
    o_ref[...] = y.astype(BF16)
    tn = w_ref.shape[1]
    u0 = COL_S5_U % tn

    @pl.when(j == COL_S5_U // tn)
    def _():
        u_ref[...] = y[:, u0:u0 + C]


INPROJ_TN = 2048


def _inproj(x2, g, w, l):
    n, d = x2.shape
    dn = w.shape[2]
    tm = min(1024, n)
    tn = INPROJ_TN
    assert COL_S5_U % tn + C <= tn and dn % tn == 0
    return pl.pallas_call(
        _inproj_kernel,
        grid=(n // tm, dn // tn),
        in_specs=[
            pl.BlockSpec((tm, d), lambda i, j: (i, 0)),
            pl.BlockSpec((None, 1, d), lambda i, j: (l, 0, 0)),
            pl.BlockSpec((None, d, tn), lambda i, j: (l, 0, j)),
        ],
        out_specs=[pl.BlockSpec((tm, tn), lambda i, j: (i, j)),
                   pl.BlockSpec((tm, C), lambda i, j: (i, 0))],
        out_shape=[jax.ShapeDtypeStruct((n, dn), BF16), jax.ShapeDtypeStruct((n, C), F32)],
        scratch_shapes=[pltpu.VMEM((tm, d), BF16)],
        compiler_params=_cparams(("parallel", "arbitrary")),
        name="inproj",
    )(x2, g, w)


def _lru_kernel(x_ref, xh_ref, g_ref, cw_ref, cb_ref, gw_ref, gb_ref, lam_ref, o_ref, carry):
    t = pl.program_id(1)
    tm, c = x_ref.shape

    @pl.when(t == 0)
    def _():
        carry[...] = jnp.zeros_like(carry)

    x = x_ref[...].astype(F32)
    halo = jnp.where(t == 0, 0.0, xh_ref[...].astype(F32))
    cw = cw_ref[...]
    xc = cb_ref[...] + cw[CONV_W - 1:CONV_W, :] * x
    for k in range(1, CONV_W):
        xc = xc + cw[CONV_W - 1 - k:CONV_W - k, :] * _shift_rows(x, halo, k)

    gates = jnp.dot(xc.astype(BF16), gw_ref[...], preferred_element_type=F32) + gb_ref[...]
    r = _sigmoid(gates[:, :c])
    i = _sigmoid(gates[:, c:])
    log_a = (-LRU_C) * r * _softplus(-lam_ref[...])
    a = jnp.exp(log_a)
    u = jnp.sqrt(1.0 - a * a) * (i * xc)

    sub = lax.broadcasted_iota(jnp.int32, (tm, c), 0) % SUBLANES
    s = 1
    while s < SUBLANES:
        a_sh = jnp.where(sub >= s, pltpu.roll(a, s, axis=0), 1.0)
        u_sh = jnp.where(sub >= s, pltpu.roll(u, s, axis=0), 0.0)
        u = u + a * u_sh
        a = a * a_sh
        s *= 2
    hp = carry[SUBLANES - 1:SUBLANES, :]
    blocks = []
    for blk in range(tm // SUBLANES):
        rows = slice(blk * SUBLANES, (blk + 1) * SUBLANES)
        hb = u[rows, :] + a[rows, :] * hp
        hp = hb[SUBLANES - 1:SUBLANES, :]
        blocks.append(hb)
    h = jnp.concatenate(blocks, axis=0)
    carry[...] = blocks[-1]
    o_ref[...] = (h * _gelu_tanh(g_ref[...].astype(F32))).astype(o_ref.dtype)


def _lru_branch(z, prm, l):
    b, t, _ = z.shape
    tm = min(256, t)
    cb = COL_LRU_X // C
    hb = tm // HALO
    return pl.pallas_call(
        _lru_kernel,
        grid=(b, t // tm),
        in_specs=[
            pl.BlockSpec((None, tm, C), lambda i, j: (i, j, cb)),
            pl.BlockSpec((None, HALO, C), lambda i, j: (i, jnp.maximum(j * hb - 1, 0), cb)),
            pl.BlockSpec((None, tm, C), lambda i, j: (i, j, cb + 1)),
        ] + [_layer_spec(a, l) for a in prm],
        out_specs=pl.BlockSpec((None, tm, C), lambda i, j: (i, j, 0)),
        out_shape=jax.ShapeDtypeStruct((b, t, C), BF16),
        scratch_shapes=[pltpu.VMEM((SUBLANES, C), F32)],
        compiler_params=_cparams(("parallel", "arbitrary")),
        name="rglru",
    )(z, z, z, *prm)


S5_L = 8
S5_OCT = LANES // S5_GROUP


def _s5_kernel(u_ref, win_ref, bd_ref, wout_ref, lstep_ref, lblk_ref, y_ref, uf, wintra, carry):
    t = pl.program_id(2)
    tm = u_ref.shape[0]
    nc = tm // S5_L
    ns = carry.shape[1]
    half = ns // 2

    @pl.when((pl.program_id(1) == 0) & (t == 0))
    def _():
        wintra[...] = jnp.zeros_like(wintra)
        for i in range(S5_L):
            for j in range(i, S5_L):
                wintra[i * LANES:(i + 1) * LANES, j * LANES:(j + 1) * LANES] = bd_ref[j - i]

    @pl.when(t == 0)
    def _():
        carry[...] = jnp.zeros_like(carry)

    for i in range(S5_L):
        uf[:, i * LANES:(i + 1) * LANES] = u_ref[pl.ds(i, nc, stride=S5_L), :].astype(BF16)
    ufv = uf[...]
    s = jnp.dot(ufv, win_ref[...], preferred_element_type=F32)

    def cmul(v, ar, ai):
        sw = jnp.concatenate([v[:, half:], v[:, :half]], axis=1)
        return v * ar + sw * ai

    row = lax.broadcasted_iota(jnp.int32, (nc, ns), 0)
    sub = row % SUBLANES
    k = 0
    st = 1
    while st < SUBLANES:
        sh = jnp.where(sub >= st, pltpu.roll(s, st, axis=0), 0.0)
        s = s + cmul(sh, lstep_ref[2 * k:2 * k + 1, :], lstep_ref[2 * k + 1:2 * k + 2, :])
        k += 1
        st *= 2
    cin = carry[0:1, :]
    prev = cin
    blocks = []
    for blk in range(nc // SUBLANES):
        sb = s[blk * SUBLANES:(blk + 1) * SUBLANES, :] + cmul(jnp.broadcast_to(prev, (SUBLANES, ns)),
                                                              lblk_ref[0], lblk_ref[1])
        prev = sb[SUBLANES - 1:SUBLANES, :]
        blocks.append(sb)
    s = jnp.concatenate(blocks, axis=0)
    carry[...] = jnp.broadcast_to(prev, carry.shape)
    sprev = jnp.where(row >= 1, pltpu.roll(s, 1, axis=0), cin)

    yf = jnp.dot(ufv, wintra[...], preferred_element_type=F32)
    yf = yf + jnp.dot(sprev.astype(BF16), wout_ref[...], preferred_element_type=F32)
    for j in range(S5_L):
        y_ref[pl.ds(j, nc, stride=S5_L), :] = yf[:, j * LANES:(j + 1) * LANES]


def _s5_tail_kernel(y_ref, u_ref, d_ref, w_ref, o_ref):
    c = u_ref.shape[1]
    y = _gelu_tanh(y_ref[...] + d_ref[...] * u_ref[...])
    z = jnp.dot(y.astype(BF16), w_ref[...], preferred_element_type=F32)
    o_ref[...] = (z[:, :c] * _sigmoid(z[:, c:])).astype(o_ref.dtype)


def _s5_tile(t):
    return min(4096, t)


def _s5_branch(u, tabs, d_skip, w_glu, l):
    win, bd, wout, lstep, lblk = tabs
    b, t, _ = u.shape
    tm = _s5_tile(t)
    nc = tm // S5_L
    noct = C // LANES
    ns = win.shape[-1]
    fl = S5_L * LANES

    def table(arr):
        zeros = (0,) * (arr.ndim - 2)
        return pl.BlockSpec((None, None) + arr.shape[2:], lambda q, i, j: (l, q) + zeros)

    y = pl.pallas_call(
        _s5_kernel,
        grid=(noct, b, t // tm),
        in_specs=[pl.BlockSpec((None, tm, LANES), lambda q, i, j: (i, j, q)),
                  table(win), table(bd), table(wout), table(lstep), table(lblk)],
        out_specs=pl.BlockSpec((None, tm, LANES), lambda q, i, j: (i, j, q)),
        out_shape=jax.ShapeDtypeStruct((b, t, C), F32),
        scratch_shapes=[pltpu.VMEM((nc, fl), BF16), pltpu.VMEM((fl, fl), BF16), pltpu.VMEM((SUBLANES, ns), F32)],
        compiler_params=_cparams(("arbitrary", "arbitrary", "arbitrary")),
        name="s5_ssm",
    )(u, win, bd, wout, lstep, lblk)

    tt = min(2048, t)
    return pl.pallas_call(
        _s5_tail_kernel,
        grid=(b, t // tt),
        in_specs=[
            pl.BlockSpec((None, tt, C), lambda i, j: (i, j, 0)),
            pl.BlockSpec((None, tt, C), lambda i, j: (i, j, 0)),
            _layer_spec(d_skip, l),
            _layer_spec(w_glu, l),
        ],
        out_specs=pl.BlockSpec((None, tt, C), lambda i, j: (i, j, 0)),
        out_shape=jax.ShapeDtypeStruct((b, t, C), BF16),
        compiler_params=_cparams(("parallel", "parallel")),
        name="s5_glu",
    )(y, u, d_skip, w_glu)


def _prep_s5(a_re, a_im, b_re, b_im, c_re, c_im, log_dt, t):
    nl, g, p = a_re.shape
    m = S5_GROUP
    noct = g // S5_OCT
    nc = _s5_tile(t) // S5_L
    lam = lax.complex(a_re.astype(F32), a_im.astype(F32))
    dt = jnp.exp(log_dt.astype(F32))[..., None]
    ldt = lam * dt
    lam_bar = jnp.exp(ldt)
    b_bar = ((lam_bar - 1.0) / lam)[..., None] * lax.complex(b_re.astype(F32), b_im.astype(F32))
    cc = lax.complex(c_re.astype(F32), c_im.astype(F32))

    def powers(e):
        return jnp.exp(ldt[..., None] * e.astype(F32).reshape((1, 1, 1, -1)))

    steps = jnp.arange(S5_L)

    def octet_bd(x, row_inner, col_pre, col_post):
        ncol = col_pre * S5_OCT * col_post
        cidx = np.arange(ncol)
        src = (cidx // (S5_OCT * col_post)) * col_post + cidx % col_post
        expand = jnp.asarray(np.arange(col_pre * col_post)[:, None] == src[None, :], F32)
        ridx = np.arange(x.shape[2])
        keep = jnp.asarray(((ridx // row_inner) % S5_OCT)[:, None] == ((cidx // col_post) % S5_OCT)[None, :])
        return jnp.where(keep, jnp.einsum("lqrc,cn->lqrn", x.astype(BF16), expand.astype(BF16),
                                          preferred_element_type=F32), 0.0).astype(BF16)

    kd = jnp.einsum("lgop,lgpd,lgpi->lgdio", cc, powers(steps), b_bar).real
    kd = kd.reshape(nl, noct, S5_OCT, S5_L, m, m)
    kd = jnp.moveaxis(kd, 3, 2).reshape(nl, noct, S5_L * LANES, m)
    bd = octet_bd(kd, m, 1, m).reshape(nl, noct, S5_L, LANES, LANES)

    wi = powers(S5_L - 1 - steps)[..., None] * b_bar[:, :, :, None, :]
    wi = jnp.stack([wi.real, wi.imag], axis=2)
    wi = wi.reshape(nl, noct, S5_OCT, 2, p, S5_L, m)
    wi = jnp.transpose(wi, (0, 1, 5, 2, 6, 3, 4)).reshape(nl, noct, S5_L * LANES, 2 * p)
    win = octet_bd(wi, m, 2, p)

    wo = cc[:, :, :, :, None] * powers(steps + 1)[:, :, None, :, :]
    wo = jnp.stack([wo.real, -wo.imag], axis=2)
    wo = wo.reshape(nl, noct, S5_OCT, 2, m, p, S5_L)
    wo = jnp.transpose(wo, (0, 1, 3, 2, 5, 6, 4)).reshape(nl, noct, 2 * S5_OCT * p, S5_L * m)
    wout = octet_bd(wo, p, S5_L, m)

    def state_mult(e):
        pw = powers(e)
        pw = jnp.transpose(pw, (0, 3, 1, 2)).reshape(nl, e.shape[0], noct, S5_OCT * p)
        pw = jnp.moveaxis(pw, 2, 1)
        return (jnp.concatenate([pw.real, pw.real], axis=-1), jnp.concatenate([-pw.imag, pw.imag], axis=-1))

    nsteps = int(math.log2(SUBLANES))
    ar, ai = state_mult(S5_L * (2 ** jnp.arange(nsteps)))
    lstep = jnp.stack([ar, ai], axis=3).reshape(nl, noct, 2 * nsteps, -1)
    lblk = jnp.stack(state_mult(S5_L * (jnp.arange(SUBLANES) + 1)), axis=2)
    return win, bd, wout, lstep, lblk


RW_L = 64
HIGHEST = lax.Precision.HIGHEST


def _mm(a, b, dims=(((1,), (0,)), ((), ()))):
    return lax.dot_general(a.astype(BF16), b.astype(BF16), dims, preferred_element_type=F32)


def _mm_nt(a, b):
    return _mm(a, b, (((1,), (1,)), ((), ())))


def _mm_tn(a, b):
    return _mm(a, b, (((0,), (0,)), ((), ())))


def _mm_exact(a, b):
    return lax.dot_general(a, b, (((1,), (0,)), ((), ())), precision=HIGHEST, preferred_element_type=F32)


def _head_sum(x, m0):
    s0 = jnp.sum(jnp.where(m0, x, 0.0), axis=-1, keepdims=True)
    s1 = jnp.sum(jnp.where(m0, 0.0, x), axis=-1, keepdims=True)
    return jnp.where(m0, s0, s1)


def _rwkv_kernel(r_ref, k_ref, v_ref, lo_ref, rh_ref, kh_ref, vh_ref, loh_ref, mu_ref, mul_ref, w0_ref, a0_ref,
                 w2_ref, a2_ref, g2_ref, kk_ref, ka_ref, rk_ref, lng_ref, lnb_ref, o_ref, hstate):
    t = pl.program_id(1)
    tm, c = r_ref.shape
    npair = c // LANES
    ll = RW_L

    @pl.when(t == 0)
    def _():
        hstate[...] = jnp.zeros_like(hstate)

    def mix(ref, href, mu):
        p = ref[...].astype(F32)
        prev = _shift_rows(p, jnp.where(t == 0, 0.0, href[...].astype(F32)), 1)
        return p + (prev - p) * mu

    mu = mu_ref[...]
    r = mix(r_ref, rh_ref, mu[0:1, :])
    k = mix(k_ref, kh_ref, mu[1:2, :])
    v = mix(v_ref, vh_ref, mu[2:3, :])
    lo = mix(lo_ref, loh_ref, mul_ref[...])
    wlog = -_softplus(-(w0_ref[...] + _mm(jnp.tanh(lo), w2_ref[...]))) - 0.5
    lw = -jnp.exp(wlog)
    a = _sigmoid(a0_ref[...] + _mm(lo, a2_ref[...]))
    g = _mm(_sigmoid(lo), g2_ref[...])
    kk = k * kk_ref[...]
    k = k * (1.0 + (a - 1.0) * ka_ref[...])

    lane = lax.broadcasted_iota(jnp.int32, (ll, LANES), 1)
    m0 = lane < RWKV_HD
    ri = lax.broadcasted_iota(jnp.int32, (2 * ll, 2 * ll), 0)
    ci = lax.broadcasted_iota(jnp.int32, (2 * ll, 2 * ll), 1)
    same = (ri >= ll) == (ci >= ll)
    strict = same & (ri > ci)
    incl = same & (ri >= ci)
    eye = ri == ci
    tri = (lax.broadcasted_iota(jnp.int32, (ll, ll), 0) >= lax.broadcasted_iota(jnp.int32, (ll, ll), 1)).astype(F32)

    def st(x):
        return jnp.concatenate([jnp.where(m0, x, 0.0), jnp.where(m0, 0.0, x)], axis=0)

    nch = tm // ll
    pcs = [(ch, p) for ch in range(nch) for p in range(npair)]
    cum_all = [_mm_exact(tri, lw[ch * ll:(ch + 1) * ll, :]) for ch in range(nch)]
    pre = {}
    for (ch, p) in pcs:
        rows, cols = slice(ch * ll, (ch + 1) * ll), slice(p * LANES, (p + 1) * LANES)
        rp, kp, vp, ap, lwp, cum = r[rows, cols], k[rows, cols], v[rows, cols], a[rows, cols], lw[rows, cols], cum_all[ch][:, cols]
        kkp = kk[rows, cols]
        kkp = kkp * lax.rsqrt(_head_sum(kkp * kkp, m0) + 1e-12)
        cum_l = cum[ll - 1:ll, :]
        e_neg = jnp.exp(-cum)
        e_last = jnp.exp(cum_l - cum)
        kb = kkp * ap
        pre[ch, p] = dict(
            rows=rows, cols=cols, rp=rp, kp=kp, vp=vp,
            a_st=st(-kkp * jnp.exp(cum - lwp)).astype(BF16), r_st=st(rp * jnp.exp(cum)),
            v_st=st(vp).astype(BF16), b_st=st(kb * e_neg).astype(BF16), k_st=st(kp * e_neg).astype(BF16),
            bh_st=st(kb * e_last).astype(BF16), kh_st=st(kp * e_last).astype(BF16), g_l=jnp.exp(cum_l))

    gram = {pc: _mm_nt(jnp.concatenate([pre[pc]["a_st"], pre[pc]["r_st"].astype(BF16)], axis=0),
                       jnp.concatenate([pre[pc]["b_st"], pre[pc]["k_st"]], axis=0)) for pc in pcs}
    a_ab = {pc: jnp.where(strict, gram[pc][:2 * ll, :2 * ll], 0.0) for pc in pcs}
    a_ak = {pc: jnp.where(strict, gram[pc][:2 * ll, 2 * ll:], 0.0).astype(BF16) for pc in pcs}
    m_rb = {pc: jnp.where(incl, gram[pc][2 * ll:, :2 * ll], 0.0).astype(BF16) for pc in pcs}
    m_rk = {pc: jnp.where(incl, gram[pc][2 * ll:, 2 * ll:], 0.0).astype(BF16) for pc in pcs}
    akv = {pc: _mm(a_ak[pc], pre[pc]["v_st"]) for pc in pcs}
    tinv = {pc: jnp.where(eye, 1.0, a_ab[pc]) for pc in pcs}
    pw = {pc: a_ab[pc].astype(BF16) for pc in pcs}
    for _ in range(int(math.log2(ll)) - 1):
        pw = {pc: _mm(pw[pc], pw[pc]).astype(BF16) for pc in pcs}
        tinv = {pc: tinv[pc] + _mm(tinv[pc], pw[pc]) for pc in pcs}
    xa = {pc: _mm(tinv[pc], jnp.concatenate([pre[pc]["a_st"], akv[pc].astype(BF16)], axis=1)).astype(BF16)
          for pc in pcs}
    ma = {pc: _mm(m_rb[pc], xa[pc][:, :LANES]) for pc in pcs}
    y_loc = {pc: _mm(jnp.concatenate([m_rb[pc], m_rk[pc]], axis=1),
                     jnp.concatenate([xa[pc][:, LANES:], pre[pc]["v_st"]], axis=0)) for pc in pcs}
    ba = {pc: _mm_tn(pre[pc]["bh_st"], xa[pc][:, :LANES]) for pc in pcs}
    f_loc = {pc: _mm_tn(jnp.concatenate([pre[pc]["bh_st"], pre[pc]["kh_st"]], axis=0),
                        jnp.concatenate([xa[pc][:, LANES:], pre[pc]["v_st"]], axis=0)) for pc in pcs}
    for (ch, p) in pcs:
        pc = (ch, p)
        q = pre[pc]
        r_eff = q["r_st"] + ma[pc]
        g_eff = jnp.where(eye, q["g_l"], 0.0) + ba[pc]
        hs = _mm(jnp.concatenate([g_eff, r_eff], axis=0), hstate[p])
        hstate[p] = hs[:2 * ll, :] + f_loc[pc]
        y_st = hs[2 * ll:, :] + y_loc[pc]
        y = y_st[:ll, :] + y_st[ll:, :]
        rows, cols = q["rows"], q["cols"]
        mean = _head_sum(y, m0) * (1.0 / RWKV_HD)
        d = y - mean
        var = _head_sum(d * d, m0) * (1.0 / RWKV_HD)
        yn = d * lax.rsqrt(var + RWKV_LN_EPS) * lng_ref[:, cols] + lnb_ref[:, cols]
        bonus = _head_sum(q["rp"] * q["kp"] * rk_ref[:, cols], m0) * q["vp"]
        o_ref[rows, cols] = ((yn + bonus) * g[rows, cols]).astype(o_ref.dtype)


def _rwkv_branch(z, prm, l):
    b, t, _ = z.shape
    tm = min(4 * RW_L, t)
    hb = tm // HALO
    cr = COL_RW_R // C
    cl = COL_LORA // 256
    npair = C // LANES

    def tile(cb, w):
        return pl.BlockSpec((None, tm, w), lambda i, j: (i, j, cb))

    def halo(cb, w):
        return pl.BlockSpec((None, HALO, w), lambda i, j: (i, jnp.maximum(j * hb - 1, 0), cb))

    return pl.pallas_call(
        _rwkv_kernel,
        grid=(b, t // tm),
        in_specs=[tile(cr, C), tile(cr + 1, C), tile(cr + 2, C), tile(cl, 256),
                  halo(cr, C), halo(cr + 1, C), halo(cr + 2, C), halo(cl, 256)]
                 + [_layer_spec(a, l) for a in prm],
        out_specs=pl.BlockSpec((None, tm, C), lambda i, j: (i, j, 0)),
        out_shape=jax.ShapeDtypeStruct((b, t, C), BF16),
        scratch_shapes=[pltpu.VMEM((npair, LANES, LANES), F32)],
        compiler_params=_cparams(("parallel", "arbitrary")),
        name="rwkv7",
    )(z, z, z, z, z, z, z, z, *prm)


def _prep_rwkv(mu_rkv, mu_w, mu_a, mu_g, w0, w2, a0, a2, g2, k_k, k_a, r_k, lnx_g, lnx_b):
    nl = mu_rkv.shape[0]
    row = lambda x: x.reshape(nl, 1, -1).astype(F32)
    mul = jnp.concatenate([mu_w, mu_a, mu_g], axis=-1)
    zw = lambda n: jnp.zeros((nl, n, C), F32)
    w2p = jnp.concatenate([w2, zw(A_LORA + G_LORA)], axis=1).astype(BF16)
    a2p = jnp.concatenate([zw(W_LORA), a2, zw(G_LORA)], axis=1).astype(BF16)
    g2p = jnp.concatenate([zw(W_LORA + A_LORA), g2], axis=1).astype(BF16)
    return (mu_rkv.astype(F32), row(mul), row(w0), row(a0), w2p, a2p, g2p, row(k_k), row(k_a), row(r_k),
            row(lnx_g), row(lnx_b))


SLOT = LANES
ROT_HALF = QK_ROPE // 2
NOPE_A = SLOT // 2 - ROT_HALF


def _slot_source():
    src = np.full((SLOT,), QK_HD, np.int32)
    src[:ROT_HALF] = QK_NOPE + np.arange(ROT_HALF)
    src[ROT_HALF:SLOT // 2] = np.arange(NOPE_A)
    src[SLOT // 2:SLOT // 2 + ROT_HALF] = QK_NOPE + ROT_HALF + np.arange(ROT_HALF)
    src[SLOT // 2 + ROT_HALF:SLOT // 2 + ROT_HALF + QK_NOPE - NOPE_A] = NOPE_A + np.arange(QK_NOPE - NOPE_A)
    return src


def _to_slot(v):
    pad = jnp.concatenate([v, jnp.zeros(v.shape[:-1] + (1,), v.dtype)], axis=-1)
    return jnp.take(pad, jnp.asarray(_slot_source()), axis=-1)
VT_ROWS = V_HD + 16
FLASH_AHEAD = 1
FLASH_QSPLIT = 1


def _mla_prep_kernel(cq_ref, ckv_ref, kr_ref, tab_ref, qn_ref, wq_ref, kvn_ref, wk_ref, wv_ref,
                     q_ref, k_ref, v_ref, v_scr):
    def rms(x, g, n):
        ms = jnp.sum(x * x, axis=-1, keepdims=True) * (1.0 / n)
        return x * lax.rsqrt(ms + EPS) * g

    def norm_rope(x, c_tab, s_tab):
        ms = jnp.sum(x * x, axis=-1, keepdims=True) * (1.0 / QK_HD)
        return (x * c_tab + pltpu.roll(x, SLOT // 2, axis=1) * s_tab) * lax.rsqrt(ms + EPS)

    cq = rms(cq_ref[...].astype(F32), qn_ref[...], Q_LORA)
    ckv = rms(ckv_ref[...].astype(F32), kvn_ref[...], KV_LORA)
    qf = _mm(cq, wq_ref[...])
    kf = _mm(ckv, wk_ref[...])
    v_scr[...] = _mm(ckv, wv_ref[...])
    vt = jnp.transpose(v_scr[...]).astype(BF16)
    ones = jnp.ones((VT_ROWS - V_HD, vt.shape[1]), BF16)
    for h in range(MLA_HEADS):
        v_ref[h * VT_ROWS:h * VT_ROWS + V_HD, :] = vt[h * V_HD:(h + 1) * V_HD, :]
        v_ref[h * VT_ROWS + V_HD:(h + 1) * VT_ROWS, :] = ones
    kr = kr_ref[...].astype(F32)
    cq_tab, sq_tab, ck_tab, sk_tab = tab_ref[0], tab_ref[1], tab_ref[2], tab_ref[3]
    for h in range(MLA_HEADS):
        cols = slice(h * SLOT, (h + 1) * SLOT)
        q_ref[:, cols] = norm_rope(qf[:, cols], cq_tab, sq_tab).astype(BF16)
        k_ref[:, cols] = norm_rope(kf[:, cols] + kr, ck_tab, sk_tab).astype(BF16)


def _flash_kernel(qi_ref, kj_ref, q_ref, k_ref, vt_ref, o_ref, m_scr, acc):
    i = qi_ref[pl.program_id(1)]
    j = kj_ref[pl.program_id(1)]
    bq = q_ref.shape[0]
    bk = k_ref.shape[0]

    @pl.when(j == 0)
    def _():
        m_scr[...] = jnp.full_like(m_scr, -jnp.inf)
        acc[...] = jnp.zeros_like(acc)

    def step(masked):
        if masked:
            causal = (lax.broadcasted_iota(jnp.int32, (bk, bq), 0) <= lax.broadcasted_iota(jnp.int32, (bk, bq), 1))
        qw = bq // FLASH_QSPLIT
        units = [(h, c) for h in range(MLA_HEADS) for c in range(FLASH_QSPLIT)]

        def scores(u):
            h, c = u
            cols = slice(h * SLOT, (h + 1) * SLOT)
            return _mm_nt(k_ref[:, cols], q_ref[c * qw:(c + 1) * qw, cols])

        pending = [scores(u) for u in units[:FLASH_AHEAD]]
        for n, (h, c) in enumerate(units):
            rows = slice(h * VT_ROWS, (h + 1) * VT_ROWS)
            qs = slice(c * qw, (c + 1) * qw)
            st = pending.pop(0)
            if n + FLASH_AHEAD < len(units):
                pending.append(scores(units[n + FLASH_AHEAD]))
            if masked:
                st = jnp.where(causal[:, qs], st, -jnp.inf)
            m_prev = m_scr[h:h + 1, qs]
            m_new = jnp.maximum(m_prev, jnp.max(st, axis=0, keepdims=True))
            alpha = jnp.exp2(m_prev - m_new)
            pt = jnp.exp2(st - m_new)
            m_scr[h:h + 1, qs] = m_new
            acc[rows, qs] = acc[rows, qs] * alpha + _mm(vt_ref[rows, :], pt)

    @pl.when(j < i)
    def _():
        step(False)

    @pl.when(j == i)
    def _():
        step(True)
        outs = []
        for h in range(MLA_HEADS):
            r0 = h * VT_ROWS
            outs.append(acc[r0:r0 + V_HD, :] / acc[r0 + V_HD:r0 + V_HD + 1, :])
        o_ref[...] = jnp.transpose(jnp.concatenate(outs, axis=0)).astype(o_ref.dtype)


def _mla_branch(z, prm, rope, l):
    b, t, _ = z.shape
    tm = min(1024, t)
    hs = MLA_HEADS * SLOT
    vr = MLA_HEADS * VT_ROWS

    q, k, vt = pl.pallas_call(
        _mla_prep_kernel,
        grid=(b, t // tm),
        in_specs=[pl.BlockSpec((None, tm, Q_LORA), lambda i, j: (i, j, COL_CQ // Q_LORA)),
                  pl.BlockSpec((None, tm, KV_LORA), lambda i, j: (i, j, COL_CKV // KV_LORA)),
                  pl.BlockSpec((None, tm, LANES), lambda i, j: (i, j, COL_KR // LANES)),
                  pl.BlockSpec((None, 4, tm, SLOT), lambda i, j: (l, 0, j, 0))]
                 + [_layer_spec(a, l) for a in prm],
        out_specs=[pl.BlockSpec((None, tm, hs), lambda i, j: (i, j, 0)),
                   pl.BlockSpec((None, tm, hs), lambda i, j: (i, j, 0)),
                   pl.BlockSpec((None, vr, tm), lambda i, j: (i, 0, j))],
        out_shape=[jax.ShapeDtypeStruct((b, t, hs), BF16), jax.ShapeDtypeStruct((b, t, hs), BF16),
                   jax.ShapeDtypeStruct((b, vr, t), BF16)],
        scratch_shapes=[pltpu.VMEM((tm, C), F32)],
        compiler_params=_cparams(("parallel", "parallel")),
        name="mla_prep",
    )(z, z, z, rope, *prm)

    bq = min(512, t)
    nq = t // bq
    tri = [(i, j) for i in range(nq) for j in range(i + 1)]
    qi = jnp.asarray([p[0] for p in tri], jnp.int32)
    kj = jnp.asarray([p[1] for p in tri], jnp.int32)
    return pl.pallas_call(
        _flash_kernel,
        grid_spec=pltpu.PrefetchScalarGridSpec(
            num_scalar_prefetch=2,
            grid=(b, len(tri)),
            in_specs=[pl.BlockSpec((None, bq, hs), lambda bi, s, qi, kj: (bi, qi[s], 0)),
                      pl.BlockSpec((None, bq, hs), lambda bi, s, qi, kj: (bi, kj[s], 0)),
                      pl.BlockSpec((None, vr, bq), lambda bi, s, qi, kj: (bi, 0, kj[s]))],
            out_specs=pl.BlockSpec((None, bq, C), lambda bi, s, qi, kj: (bi, qi[s], 0)),
            scratch_shapes=[pltpu.VMEM((MLA_HEADS, bq), F32), pltpu.VMEM((vr, bq), F32)]),
        out_shape=jax.ShapeDtypeStruct((b, t, C), BF16),
        compiler_params=_cparams(("parallel", "arbitrary")),
        name="mla_attn",
    )(qi, kj, q, k, vt)


def _rope_tables(t, qk_norm_q, qk_norm_k):
    pos = jnp.arange(t, dtype=F32)
    inv_freq = ROPE_THETA ** (-jnp.arange(0, QK_ROPE, 2, dtype=F32) / QK_ROPE)
    ang = pos[:, None] * inv_freq[None, :]
    cos, sin = jnp.cos(ang), jnp.sin(ang)
    z = lambda n: jnp.zeros((t, n), F32)
    one = lambda n: jnp.ones((t, n), F32)
    tail = SLOT // 2 - ROT_HALF
    c_pos = jnp.concatenate([cos, one(NOPE_A), cos, one(QK_NOPE - NOPE_A), z(SLOT - QK_HD)], axis=1)
    s_pos = jnp.concatenate([-sin, z(NOPE_A), sin, z(tail)], axis=1)
    scale = QK_HD ** -0.5 * math.log2(math.e)

    def pair(g, mult):
        g = _to_slot(g.astype(F32)) * mult
        return [g[:, None, :] * c_pos[None], jnp.roll(g, SLOT // 2, axis=-1)[:, None, :] * s_pos[None]]

    return jnp.stack(pair(qk_norm_q, scale) + pair(qk_norm_k, 1.0), axis=1)


def _prep_mla(q_norm, w_uq, kv_norm, w_ukv):
    nl = q_norm.shape[0]
    nh = MLA_HEADS
    wq = _to_slot(w_uq.reshape(nl, Q_LORA, nh, QK_HD)).reshape(nl, Q_LORA, nh * SLOT).astype(BF16)
    wkv = w_ukv.reshape(nl, KV_LORA, nh, QK_NOPE + V_HD)
    wk = jnp.pad(wkv[..., :QK_NOPE], ((0, 0), (0, 0), (0, 0), (0, QK_ROPE)))
    wk = _to_slot(wk).reshape(nl, KV_LORA, nh * SLOT).astype(BF16)
    wv = wkv[..., QK_NOPE:].reshape(nl, KV_LORA, nh * V_HD).astype(BF16)
    return (q_norm.reshape(nl, 1, -1).astype(F32), wq, kv_norm.reshape(nl, 1, -1).astype(F32), wk, wv)


def _merge_kernel(y0_ref, y1_ref, y2_ref, y3_ref, gl_ref, x_ref, wb_ref, wo_ref, o_ref):
    d = x_ref.shape[1]
    merged = None
    for n, y_ref in enumerate((y0_ref, y1_ref, y2_ref, y3_ref)):
        term = _mm(y_ref[...], wb_ref[n]) * _sigmoid(gl_ref[:, n * d:(n + 1) * d].astype(F32))
        merged = term if merged is None else merged + term
    o_ref[...] = x_ref[...] + _mm(merged, wo_ref[...])


def _merge(ys, z2, x2, wb, wo, l):
    n, d = x2.shape
    tm = min(512, n)
    ytile = pl.BlockSpec((tm, C), lambda i: (i, 0))
    return pl.pallas_call(
        _merge_kernel,
        grid=(n // tm,),
        in_specs=[ytile, ytile, ytile, ytile,
                  pl.BlockSpec((tm, 4 * d), lambda i: (i, COL_GATE)),
                  pl.BlockSpec((tm, d), lambda i: (i, 0)),
                  _layer_spec(wb, l), _layer_spec(wo, l)],
        out_specs=pl.BlockSpec((tm, d), lambda i: (i, 0)),
        out_shape=jax.ShapeDtypeStruct((n, d), F32),
        compiler_params=_cparams(("parallel",)),
        name="merge",
    )(*ys, z2, x2, wb, wo)


def _mlp_kernel(x_ref, g_ref, w1_ref, w2_ref, o_ref, h_scr, acc):
    j = pl.program_id(1)

    @pl.when(j == 0)
    def _():
        x = x_ref[...]
        ms = jnp.mean(x * x, axis=-1, keepdims=True)
        h_scr[...] = (x * lax.rsqrt(ms + EPS) * g_ref[...]).astype(BF16)
        acc[...] = x

    a = jnp.maximum(jnp.dot(h_scr[...], w1_ref[...], preferred_element_type=F32), 0.0)
    acc[...] += _mm(a * a, w2_ref[...])

    @pl.when(j == pl.num_programs(1) - 1)
    def _():
        o_ref[...] = acc[...]


def _mlp(x2, g, w1, w2, l):
    n, d = x2.shape
    dff = w1.shape[2]
    tm = min(1024, n)
    tf = 2048
    return pl.pallas_call(
        _mlp_kernel,
        grid=(n // tm, dff // tf),
        in_specs=[pl.BlockSpec((tm, d), lambda i, j: (i, 0)),
                  pl.BlockSpec((None, 1, d), lambda i, j: (l, 0, 0)),
                  pl.BlockSpec((None, d, tf), lambda i, j: (l, 0, j)),
                  pl.BlockSpec((None, tf, d), lambda i, j: (l, j, 0))],
        out_specs=pl.BlockSpec((tm, d), lambda i, j: (i, 0)),
        out_shape=jax.ShapeDtypeStruct((n, d), F32),
        scratch_shapes=[pltpu.VMEM((tm, d), BF16), pltpu.VMEM((tm, d), F32)],
        compiler_params=_cparams(("parallel", "arbitrary")),
        name="mlp",
    )(x2, g, w1, w2)


def _prep_w_in(w_in):
    nl, d, _ = w_in.shape
    gate0 = 6 * C + W_LORA + A_LORA + G_LORA + Q_LORA + KV_LORA + QK_ROPE
    kr0 = gate0 - QK_ROPE
    parts = [
        w_in[:, :, gate0:],
        w_in[:, :, :kr0],
        w_in[:, :, kr0:kr0 + ROT_HALF],
        jnp.zeros((nl, d, NOPE_A), w_in.dtype),
        w_in[:, :, kr0 + ROT_HALF:gate0],
        jnp.zeros((nl, d, D_INP - COL_KR - SLOT // 2 - ROT_HALF), w_in.dtype),
    ]
    return jnp.concatenate(parts, axis=-1).astype(BF16)


def _prep_lru_gate(gate_w):
    nl, _, nh, hd, _ = gate_w.shape
    eye = jnp.eye(nh, dtype=gate_w.dtype)
    dense = jnp.einsum("lghij,hk->lghikj", gate_w, eye).reshape(nl, 2, nh * hd, nh * hd)
    return jnp.concatenate([dense[:, 0], dense[:, 1]], axis=-1).astype(BF16)


def kernel(x, norm_mix, w_in, lru_conv_w, lru_conv_b, lru_gate_w, lru_gate_b, lru_lambda, s5_a_re, s5_a_im, s5_b_re, s5_b_im, s5_c_re, s5_c_im, s5_d, s5_log_dt, s5_w_glu, rwkv_mu_rkv, rwkv_mu_w, rwkv_mu_a, rwkv_mu_g, rwkv_w0, rwkv_w2, rwkv_a0, rwkv_a2, rwkv_g2, rwkv_k_k, rwkv_k_a, rwkv_r_k, rwkv_lnx_g, rwkv_lnx_b, mla_q_norm, mla_w_uq, mla_kv_norm, mla_w_ukv, mla_qk_norm_q, mla_qk_norm_k, w_branch, w_out, norm_mlp, w_ff1, w_ff2):
    b, t, d = x.shape
    depth = w_in.shape[0]
    n = b * t
    row = lambda a: a.reshape(depth, 1, -1).astype(F32)

    w_in_p = _prep_w_in(w_in)
    lru_gw = _prep_lru_gate(lru_gate_w)
    s5_tabs = _prep_s5(s5_a_re, s5_a_im, s5_b_re, s5_b_im, s5_c_re, s5_c_im, s5_log_dt, t)
    rw_prm = _prep_rwkv(rwkv_mu_rkv, rwkv_mu_w, rwkv_mu_a, rwkv_mu_g, rwkv_w0, rwkv_w2, rwkv_a0, rwkv_a2, rwkv_g2,
                        rwkv_k_k, rwkv_k_a, rwkv_r_k, rwkv_lnx_g, rwkv_lnx_b)
    mla_prm = _prep_mla(mla_q_norm, mla_w_uq, mla_kv_norm, mla_w_ukv)
    rope = _rope_tables(t, mla_qk_norm_q, mla_qk_norm_k)
    g_mix, g_mlp = row(norm_mix), row(norm_mlp)
    conv_b, gate_b, lam, s5_dr = row(lru_conv_b), row(lru_gate_b), row(lru_lambda), row(s5_d)
    w_glu, wb, wo = s5_w_glu.astype(BF16), w_branch.astype(BF16), w_out.astype(BF16)
    w1, w2 = w_ff1.astype(BF16), w_ff2.astype(BF16)

    lru_prm = (lru_conv_w.astype(F32), conv_b, lru_gw, gate_b, lam)
    x2 = x.reshape(n, d)
    for l in range(depth):
        z2, u2 = _inproj(x2, g_mix, w_in_p, l)
        z = z2.reshape(b, t, -1)
        y_lru = _lru_branch(z, lru_prm, l)
        y_s5 = _s5_branch(u2.reshape(b, t, C), s5_tabs, s5_dr, w_glu, l)
        y_rw = _rwkv_branch(z, rw_prm, l)
        y_mla = _mla_branch(z, mla_prm, rope, l)
        ys = [y.reshape(n, C) for y in (y_lru, y_s5, y_rw, y_mla)]
        x2 = _merge(ys, z2, x2, wb, wo, l)
        x2 = _mlp(x2, g_mlp, w1, w2, l)
    return x2.reshape(b, t, d)
```

```python
import functools
import math

import jax
import jax.numpy as jnp
import numpy as np
from jax import lax
from jax.experimental import pallas as pl
from jax.experimental.pallas import tpu as pltpu

F32 = jnp.float32
BF16 = jnp.bfloat16

EPS = 1e-6
LRU_HEADS = 8
CONV_W = 4
LRU_C = 8.0
S5_GROUP = 16
S5_STATE = 64
RWKV_HD = 64
W_LORA, A_LORA, G_LORA = 64, 64, 128
RWKV_LN_EPS = 64e-5
MLA_HEADS = 8
QK_NOPE, QK_ROPE, V_HD = 64, 32, 64
QK_HD = QK_NOPE + QK_ROPE
Q_LORA, KV_LORA = 256, 128
ROPE_THETA = 10000.0

LANES = 128
SUBLANES = 8
VMEM_LIMIT = 56 * 1024 * 1024

D_MODEL = 1024
C = D_MODEL // 2
COL_GATE = 0
COL_LRU_X = 4 * D_MODEL
COL_LRU_G = COL_LRU_X + C
COL_S5_U = COL_LRU_G + C
COL_RW_R = COL_S5_U + C
COL_RW_K = COL_RW_R + C
COL_RW_V = COL_RW_K + C
COL_LORA = COL_RW_V + C
COL_CQ = COL_LORA + 256
COL_CKV = COL_CQ + Q_LORA
COL_KR = COL_CKV + KV_LORA
D_INP = 8 * D_MODEL


def _cparams(sem):
    return pltpu.CompilerParams(dimension_semantics=sem, vmem_limit_bytes=VMEM_LIMIT)


def _gelu_tanh(x):
    return 0.5 * x * (1.0 + jnp.tanh(math.sqrt(2.0 / math.pi) * (x + 0.044715 * (x * x * x))))


def _sigmoid(x):
    return 1.0 / (1.0 + jnp.exp(-x))


def _softplus(x):
    return jnp.maximum(x, 0.0) + jnp.log(1.0 + jnp.exp(-jnp.abs(x)))


HALO = 16


def _shift_rows(x, halo, k):
    xe = jnp.concatenate([halo, x], axis=0)
    n = x.shape[0]
    return xe[HALO - k:HALO - k + n, :]


def _layer_spec(arr, l):
    zeros = (0,) * (arr.ndim - 1)
    return pl.BlockSpec((None,) + arr.shape[1:], lambda *idx: (l,) + zeros)


def _inproj_kernel(x_ref, g_ref, w_ref, o_ref, u_ref, h_scr):
    j = pl.program_id(1)

    @pl.when(j == 0)
    def _():
        x = x_ref[...]
        ms = jnp.mean(x * x, axis=-1, keepdims=True)
        h_scr[...] = (x * lax.rsqrt(ms + EPS) * g_ref[...]).astype(BF16)

    y = jnp.dot(h_scr[...], w_ref[...], preferred_element_type=F32)
    o_ref[...] = y.astype(BF16)
    tn = w_ref.shape[1]
    u0 = COL_S5_U % tn

    @pl.when(j == COL_S5_U // tn)
    def _():
        u_ref[...] = y[:, u0:u0 + C]


INPROJ_TN = 2048


def _inproj(x2, g, w, l):
    n, d = x2.shape
    dn = w.shape[2]
    tm = min(1024, n)
    tn = INPROJ_TN
    assert COL_S5_U % tn + C <= tn and dn % tn == 0
    return pl.pallas_call(
        _inproj_kernel,
        grid=(n // tm, dn // tn),
        in_specs=[
            pl.BlockSpec((tm, d), lambda i, j: (i, 0)),
            pl.BlockSpec((None, 1, d), lambda i, j: (l, 0, 0)),
            pl.BlockSpec((None, d, tn), lambda i, j: (l, 0, j)),
        ],
        out_specs=[pl.BlockSpec((tm, tn), lambda i, j: (i, j)),
                   pl.BlockSpec((tm, C), lambda i, j: (i, 0))],
        out_shape=[jax.ShapeDtypeStruct((n, dn), BF16), jax.ShapeDtypeStruct((n, C), F32)],
        scratch_shapes=[pltpu.VMEM((tm, d), BF16)],
        compiler_params=_cparams(("parallel", "arbitrary")),
        name="inproj",
    )(x2, g, w)


def _lru_kernel(x_ref, xh_ref, g_ref, cw_ref, cb_ref, gw_ref, gb_ref, lam_ref, o_ref, carry):
    t = pl.program_id(1)
    tm, c = x_ref.shape

    @pl.when(t == 0)
    def _():
        carry[...] = jnp.zeros_like(carry)

    x = x_ref[...].astype(F32)
    halo = jnp.where(t == 0, 0.0, xh_ref[...].astype(F32))
    cw = cw_ref[...]
    xc = cb_ref[...] + cw[CONV_W - 1:CONV_W, :] * x
    for k in range(1, CONV_W):
        xc = xc + cw[CONV_W - 1 - k:CONV_W - k, :] * _shift_rows(x, halo, k)

    gates = jnp.dot(xc.astype(BF16), gw_ref[...], preferred_element_type=F32) + gb_ref[...]
    r = _sigmoid(gates[:, :c])
    i = _sigmoid(gates[:, c:])
    log_a = (-LRU_C) * r * _softplus(-lam_ref[...])
    a = jnp.exp(log_a)
    u = jnp.sqrt(1.0 - a * a) * (i * xc)

    sub = lax.broadcasted_iota(jnp.int32, (tm, c), 0) % SUBLANES
    s = 1
    while s < SUBLANES:
        a_sh = jnp.where(sub >= s, pltpu.roll(a, s, axis=0), 1.0)
        u_sh = jnp.where(sub >= s, pltpu.roll(u, s, axis=0), 0.0)
        u = u + a * u_sh
        a = a * a_sh
        s *= 2
    hp = carry[SUBLANES - 1:SUBLANES, :]
    blocks = []
    for blk in range(tm // SUBLANES):
        rows = slice(blk * SUBLANES, (blk + 1) * SUBLANES)
        hb = u[rows, :] + a[rows, :] * hp
        hp = hb[SUBLANES - 1:SUBLANES, :]
        blocks.append(hb)
    h = jnp.concatenate(blocks, axis=0)
    carry[...] = blocks[-1]
    o_ref[...] = (h * _gelu_tanh(g_ref[...].astype(F32))).astype(o_ref.dtype)


def _lru_branch(z, prm, l):
    b, t, _ = z.shape
    tm = min(256, t)
    cb = COL_LRU_X // C
    hb = tm // HALO
    return pl.pallas_call(
        _lru_kernel,
        grid=(b, t // tm),
        in_specs=[
            pl.BlockSpec((None, tm, C), lambda i, j: (i, j, cb)),
            pl.BlockSpec((None, HALO, C), lambda i, j: (i, jnp.maximum(j * hb - 1, 0), cb)),
            pl.BlockSpec((None, tm, C), lambda i, j: (i, j, cb + 1)),
        ] + [_layer_spec(a, l) for a in prm],
        out_specs=pl.BlockSpec((None, tm, C), lambda i, j: (i, j, 0)),
        out_shape=jax.ShapeDtypeStruct((b, t, C), BF16),
        scratch_shapes=[pltpu.VMEM((SUBLANES, C), F32)],
        compiler_params=_cparams(("parallel", "arbitrary")),
        name="rglru",
    )(z, z, z, *prm)


S5_L = 8
S5_OCT = LANES // S5_GROUP


def _s5_kernel(u_ref, win_ref, bd_ref, wout_ref, lstep_ref, lblk_ref, y_ref, uf, wintra, carry):
    t = pl.program_id(2)
    tm = u_ref.shape[0]
    nc = tm // S5_L
    ns = carry.shape[1]
    half = ns // 2

    @pl.when((pl.program_id(1) == 0) & (t == 0))
    def _():
        wintra[...] = jnp.zeros_like(wintra)
        for i in range(S5_L):
            for j in range(i, S5_L):
                wintra[i * LANES:(i + 1) * LANES, j * LANES:(j + 1) * LANES] = bd_ref[j - i]

    @pl.when(t == 0)
    def _():
        carry[...] = jnp.zeros_like(carry)

    for i in range(S5_L):
        uf[:, i * LANES:(i + 1) * LANES] = u_ref[pl.ds(i, nc, stride=S5_L), :].astype(BF16)
    ufv = uf[...]
    s = jnp.dot(ufv, win_ref[...], preferred_element_type=F32)

    def cmul(v, ar, ai):
        sw = jnp.concatenate([v[:, half:], v[:, :half]], axis=1)
        return v * ar + sw * ai

    row = lax.broadcasted_iota(jnp.int32, (nc, ns), 0)
    sub = row % SUBLANES
    k = 0
    st = 1
    while st < SUBLANES:
        sh = jnp.where(sub >= st, pltpu.roll(s, st, axis=0), 0.0)
        s = s + cmul(sh, lstep_ref[2 * k:2 * k + 1, :], lstep_ref[2 * k + 1:2 * k + 2, :])
        k += 1
        st *= 2
    cin = carry[0:1, :]
    prev = cin
    blocks = []
    for blk in range(nc // SUBLANES):
        sb = s[blk * SUBLANES:(blk + 1) * SUBLANES, :] + cmul(jnp.broadcast_to(prev, (SUBLANES, ns)),
                                                              lblk_ref[0], lblk_ref[1])
        prev = sb[SUBLANES - 1:SUBLANES, :]
        blocks.append(sb)
    s = jnp.concatenate(blocks, axis=0)
    carry[...] = jnp.broadcast_to(prev, carry.shape)
    sprev = jnp.where(row >= 1, pltpu.roll(s, 1, axis=0), cin)

    yf = jnp.dot(ufv, wintra[...], preferred_element_type=F32)
    yf = yf + jnp.dot(sprev.astype(BF16), wout_ref[...], preferred_element_type=F32)
    for j in range(S5_L):
        y_ref[pl.ds(j, nc, stride=S5_L), :] = yf[:, j * LANES:(j + 1) * LANES]


def _s5_tail_kernel(y_ref, u_ref, d_ref, w_ref, o_ref):
    c = u_ref.shape[1]
    y = _gelu_tanh(y_ref[...] + d_ref[...] * u_ref[...])
    z = jnp.dot(y.astype(BF16), w_ref[...], preferred_element_type=F32)
    o_ref[...] = (z[:, :c] * _sigmoid(z[:, c:])).astype(o_ref.dtype)


def _s5_tile(t):
    return min(4096, t)


def _s5_branch(u, tabs, d_skip, w_glu, l):
    win, bd, wout, lstep, lblk = tabs
    b, t, _ = u.shape
    tm = _s5_tile(t)
    nc = tm // S5_L
    noct = C // LANES
    ns = win.shape[-1]
    fl = S5_L * LANES

    def table(arr):
        zeros = (0,) * (arr.ndim - 2)
        return pl.BlockSpec((None, None) + arr.shape[2:], lambda q, i, j: (l, q) + zeros)

    y = pl.pallas_call(
        _s5_kernel,
        grid=(noct, b, t // tm),
        in_specs=[pl.BlockSpec((None, tm, LANES), lambda q, i, j: (i, j, q)),
                  table(win), table(bd), table(wout), table(lstep), table(lblk)],
        out_specs=pl.BlockSpec((None, tm, LANES), lambda q, i, j: (i, j, q)),
        out_shape=jax.ShapeDtypeStruct((b, t, C), F32),
        scratch_shapes=[pltpu.VMEM((nc, fl), BF16), pltpu.VMEM((fl, fl), BF16), pltpu.VMEM((SUBLANES, ns), F32)],
        compiler_params=_cparams(("arbitrary", "arbitrary", "arbitrary")),
        name="s5_ssm",
    )(u, win, bd, wout, lstep, lblk)

    tt = min(2048, t)
    return pl.pallas_call(
        _s5_tail_kernel,
        grid=(b, t // tt),
        in_specs=[
            pl.BlockSpec((None, tt, C), lambda i, j: (i, j, 0)),
            pl.BlockSpec((None, tt, C), lambda i, j: (i, j, 0)),
            _layer_spec(d_skip, l),
            _layer_spec(w_glu, l),
        ],
        out_specs=pl.BlockSpec((None, tt, C), lambda i, j: (i, j, 0)),
        out_shape=jax.ShapeDtypeStruct((b, t, C), BF16),
        compiler_params=_cparams(("parallel", "parallel")),
        name="s5_glu",
    )(y, u, d_skip, w_glu)


def _prep_s5(a_re, a_im, b_re, b_im, c_re, c_im, log_dt, t):
    nl, g, p = a_re.shape
    m = S5_GROUP
    noct = g // S5_OCT
    nc = _s5_tile(t) // S5_L
    lam = lax.complex(a_re.astype(F32), a_im.astype(F32))
    dt = jnp.exp(log_dt.astype(F32))[..., None]
    ldt = lam * dt
    lam_bar = jnp.exp(ldt)
    b_bar = ((lam_bar - 1.0) / lam)[..., None] * lax.complex(b_re.astype(F32), b_im.astype(F32))
    cc = lax.complex(c_re.astype(F32), c_im.astype(F32))

    def powers(e):
        return jnp.exp(ldt[..., None] * e.astype(F32).reshape((1, 1, 1, -1)))

    steps = jnp.arange(S5_L)

    def octet_bd(x, row_inner, col_pre, col_post):
        ncol = col_pre * S5_OCT * col_post
        cidx = np.arange(ncol)
        src = (cidx // (S5_OCT * col_post)) * col_post + cidx % col_post
        expand = jnp.asarray(np.arange(col_pre * col_post)[:, None] == src[None, :], F32)
        ridx = np.arange(x.shape[2])
        keep = jnp.asarray(((ridx // row_inner) % S5_OCT)[:, None] == ((cidx // col_post) % S5_OCT)[None, :])
        return jnp.where(keep, jnp.einsum("lqrc,cn->lqrn", x.astype(BF16), expand.astype(BF16),
                                          preferred_element_type=F32), 0.0).astype(BF16)

    kd = jnp.einsum("lgop,lgpd,lgpi->lgdio", cc, powers(steps), b_bar).real
    kd = kd.reshape(nl, noct, S5_OCT, S5_L, m, m)
    kd = jnp.moveaxis(kd, 3, 2).reshape(nl, noct, S5_L * LANES, m)
    bd = octet_bd(kd, m, 1, m).reshape(nl, noct, S5_L, LANES, LANES)

    wi = powers(S5_L - 1 - steps)[..., None] * b_bar[:, :, :, None, :]
    wi = jnp.stack([wi.real, wi.imag], axis=2)
    wi = wi.reshape(nl, noct, S5_OCT, 2, p, S5_L, m)
    wi = jnp.transpose(wi, (0, 1, 5, 2, 6, 3, 4)).reshape(nl, noct, S5_L * LANES, 2 * p)
    win = octet_bd(wi, m, 2, p)

    wo = cc[:, :, :, :, None] * powers(steps + 1)[:, :, None, :, :]
    wo = jnp.stack([wo.real, -wo.imag], axis=2)
    wo = wo.reshape(nl, noct, S5_OCT, 2, m, p, S5_L)
    wo = jnp.transpose(wo, (0, 1, 3, 2, 5, 6, 4)).reshape(nl, noct, 2 * S5_OCT * p, S5_L * m)
    wout = octet_bd(wo, p, S5_L, m)

    def state_mult(e):
        pw = powers(e)
        pw = jnp.transpose(pw, (0, 3, 1, 2)).reshape(nl, e.shape[0], noct, S5_OCT * p)
        pw = jnp.moveaxis(pw, 2, 1)
        return (jnp.concatenate([pw.real, pw.real], axis=-1), jnp.concatenate([-pw.imag, pw.imag], axis=-1))

    nsteps = int(math.log2(SUBLANES))
    ar, ai = state_mult(S5_L * (2 ** jnp.arange(nsteps)))
    lstep = jnp.stack([ar, ai], axis=3).reshape(nl, noct, 2 * nsteps, -1)
    lblk = jnp.stack(state_mult(S5_L * (jnp.arange(SUBLANES) + 1)), axis=2)
    return win, bd, wout, lstep, lblk


RW_L = 64
HIGHEST = lax.Precision.HIGHEST


def _mm(a, b, dims=(((1,), (0,)), ((), ()))):
    return lax.dot_general(a.astype(BF16), b.astype(BF16), dims, preferred_element_type=F32)


def _mm_nt(a, b):
    return _mm(a, b, (((1,), (1,)), ((), ())))


def _mm_tn(a, b):
    return _mm(a, b, (((0,), (0,)), ((), ())))


def _mm_exact(a, b):
    return lax.dot_general(a, b, (((1,), (0,)), ((), ())), precision=HIGHEST, preferred_element_type=F32)


def _head_sum(x, m0):
    s0 = jnp.sum(jnp.where(m0, x, 0.0), axis=-1, keepdims=True)
    s1 = jnp.sum(jnp.where(m0, 0.0, x), axis=-1, keepdims=True)
    return jnp.where(m0, s0, s1)


def _rwkv_kernel(r_ref, k_ref, v_ref, lo_ref, rh_ref, kh_ref, vh_ref, loh_ref, mu_ref, mul_ref, w0_ref, a0_ref,
                 w2_ref, a2_ref, g2_ref, kk_ref, ka_ref, rk_ref, lng_ref, lnb_ref, o_ref, hstate):
    t = pl.program_id(1)
    tm, c = r_ref.shape
    npair = c // LANES
    ll = RW_L

    @pl.when(t == 0)
    def _():
        hstate[...] = jnp.zeros_like(hstate)

    def mix(ref, href, mu):
        p = ref[...].astype(F32)
        prev = _shift_rows(p, jnp.where(t == 0, 0.0, href[...].astype(F32)), 1)
        return p + (prev - p) * mu

    mu = mu_ref[...]
    r = mix(r_ref, rh_ref, mu[0:1, :])
    k = mix(k_ref, kh_ref, mu[1:2, :])
    v = mix(v_ref, vh_ref, mu[2:3, :])
    lo = mix(lo_ref, loh_ref, mul_ref[...])
    wlog = -_softplus(-(w0_ref[...] + _mm(jnp.tanh(lo), w2_ref[...]))) - 0.5
    lw = -jnp.exp(wlog)
    a = _sigmoid(a0_ref[...] + _mm(lo, a2_ref[...]))
    g = _mm(_sigmoid(lo), g2_ref[...])
    kk = k * kk_ref[...]
    k = k * (1.0 + (a - 1.0) * ka_ref[...])

    lane = lax.broadcasted_iota(jnp.int32, (ll, LANES), 1)
    m0 = lane < RWKV_HD
    ri = lax.broadcasted_iota(jnp.int32, (2 * ll, 2 * ll), 0)
    ci = lax.broadcasted_iota(jnp.int32, (2 * ll, 2 * ll), 1)
    same = (ri >= ll) == (ci >= ll)
    strict = same & (ri > ci)
    incl = same & (ri >= ci)
    eye = ri == ci
    tri = (lax.broadcasted_iota(jnp.int32, (ll, ll), 0) >= lax.broadcasted_iota(jnp.int32, (ll, ll), 1)).astype(F32)

    def st(x):
        return jnp.concatenate([jnp.where(m0, x, 0.0), jnp.where(m0, 0.0, x)], axis=0)

    nch = tm // ll
    pcs = [(ch, p) for ch in range(nch) for p in range(npair)]
    cum_all = [_mm_exact(tri, lw[ch * ll:(ch + 1) * ll, :]) for ch in range(nch)]
    pre = {}
    for (ch, p) in pcs:
        rows, cols = slice(ch * ll, (ch + 1) * ll), slice(p * LANES, (p + 1) * LANES)
        rp, kp, vp, ap, lwp, cum = r[rows, cols], k[rows, cols], v[rows, cols], a[rows, cols], lw[rows, cols], cum_all[ch][:, cols]
        kkp = kk[rows, cols]
        kkp = kkp * lax.rsqrt(_head_sum(kkp * kkp, m0) + 1e-12)
        cum_l = cum[ll - 1:ll, :]
        e_neg = jnp.exp(-cum)
        e_last = jnp.exp(cum_l - cum)
        kb = kkp * ap
        pre[ch, p] = dict(
            rows=rows, cols=cols, rp=rp, kp=kp, vp=vp,
            a_st=st(-kkp * jnp.exp(cum - lwp)).astype(BF16), r_st=st(rp * jnp.exp(cum)),
            v_st=st(vp).astype(BF16), b_st=st(kb * e_neg).astype(BF16), k_st=st(kp * e_neg).astype(BF16),
            bh_st=st(kb * e_last).astype(BF16), kh_st=st(kp * e_last).astype(BF16), g_l=jnp.exp(cum_l))

    gram = {pc: _mm_nt(jnp.concatenate([pre[pc]["a_st"], pre[pc]["r_st"].astype(BF16)], axis=0),
                       jnp.concatenate([pre[pc]["b_st"], pre[pc]["k_st"]], axis=0)) for pc in pcs}
    a_ab = {pc: jnp.where(strict, gram[pc][:2 * ll, :2 * ll], 0.0) for pc in pcs}
    a_ak = {pc: jnp.where(strict, gram[pc][:2 * ll, 2 * ll:], 0.0).astype(BF16) for pc in pcs}
    m_rb = {pc: jnp.where(incl, gram[pc][2 * ll:, :2 * ll], 0.0).astype(BF16) for pc in pcs}
    m_rk = {pc: jnp.where(incl, gram[pc][2 * ll:, 2 * ll:], 0.0).astype(BF16) for pc in pcs}
    akv = {pc: _mm(a_ak[pc], pre[pc]["v_st"]) for pc in pcs}
    tinv = {pc: jnp.where(eye, 1.0, a_ab[pc]) for pc in pcs}
    pw = {pc: a_ab[pc].astype(BF16) for pc in pcs}
    for _ in range(int(math.log2(ll)) - 1):
        pw = {pc: _mm(pw[pc], pw[pc]).astype(BF16) for pc in pcs}
        tinv = {pc: tinv[pc] + _mm(tinv[pc], pw[pc]) for pc in pcs}
    xa = {pc: _mm(tinv[pc], jnp.concatenate([pre[pc]["a_st"], akv[pc].astype(BF16)], axis=1)).astype(BF16)
          for pc in pcs}
    ma = {pc: _mm(m_rb[pc], xa[pc][:, :LANES]) for pc in pcs}
    y_loc = {pc: _mm(jnp.concatenate([m_rb[pc], m_rk[pc]], axis=1),
                     jnp.concatenate([xa[pc][:, LANES:], pre[pc]["v_st"]], axis=0)) for pc in pcs}
    ba = {pc: _mm_tn(pre[pc]["bh_st"], xa[pc][:, :LANES]) for pc in pcs}
    f_loc = {pc: _mm_tn(jnp.concatenate([pre[pc]["bh_st"], pre[pc]["kh_st"]], axis=0),
                        jnp.concatenate([xa[pc][:, LANES:], pre[pc]["v_st"]], axis=0)) for pc in pcs}
    for (ch, p) in pcs:
        pc = (ch, p)
        q = pre[pc]
        r_eff = q["r_st"] + ma[pc]
        g_eff = jnp.where(eye, q["g_l"], 0.0) + ba[pc]
        hs = _mm(jnp.concatenate([g_eff, r_eff], axis=0), hstate[p])
        hstate[p] = hs[:2 * ll, :] + f_loc[pc]
        y_st = hs[2 * ll:, :] + y_loc[pc]
        y = y_st[:ll, :] + y_st[ll:, :]
        rows, cols = q["rows"], q["cols"]
        mean = _head_sum(y, m0) * (1.0 / RWKV_HD)
        d = y - mean
        var = _head_sum(d * d, m0) * (1.0 / RWKV_HD)
        yn = d * lax.rsqrt(var + RWKV_LN_EPS) * lng_ref[:, cols] + lnb_ref[:, cols]
        bonus = _head_sum(q["rp"] * q["kp"] * rk_ref[:, cols], m0) * q["vp"]
        o_ref[rows, cols] = ((yn + bonus) * g[rows, cols]).astype(o_ref.dtype)


def _rwkv_branch(z, prm, l):
    b, t, _ = z.shape
    tm = min(4 * RW_L, t)
    hb = tm // HALO
    cr = COL_RW_R // C
    cl = COL_LORA // 256
    npair = C // LANES

    def tile(cb, w):
        return pl.BlockSpec((None, tm, w), lambda i, j: (i, j, cb))

    def halo(cb, w):
        return pl.BlockSpec((None, HALO, w), lambda i, j: (i, jnp.maximum(j * hb - 1, 0), cb))

    return pl.pallas_call(
        _rwkv_kernel,
        grid=(b, t // tm),
        in_specs=[tile(cr, C), tile(cr + 1, C), tile(cr + 2, C), tile(cl, 256),
                  halo(cr, C), halo(cr + 1, C), halo(cr + 2, C), halo(cl, 256)]
                 + [_layer_spec(a, l) for a in prm],
        out_specs=pl.BlockSpec((None, tm, C), lambda i, j: (i, j, 0)),
        out_shape=jax.ShapeDtypeStruct((b, t, C), BF16),
        scratch_shapes=[pltpu.VMEM((npair, LANES, LANES), F32)],
        compiler_params=_cparams(("parallel", "arbitrary")),
        name="rwkv7",
    )(z, z, z, z, z, z, z, z, *prm)


def _prep_rwkv(mu_rkv, mu_w, mu_a, mu_g, w0, w2, a0, a2, g2, k_k, k_a, r_k, lnx_g, lnx_b):
    nl = mu_rkv.shape[0]
    row = lambda x: x.reshape(nl, 1, -1).astype(F32)
    mul = jnp.concatenate([mu_w, mu_a, mu_g], axis=-1)
    zw = lambda n: jnp.zeros((nl, n, C), F32)
    w2p = jnp.concatenate([w2, zw(A_LORA + G_LORA)], axis=1).astype(BF16)
    a2p = jnp.concatenate([zw(W_LORA), a2, zw(G_LORA)], axis=1).astype(BF16)
    g2p = jnp.concatenate([zw(W_LORA + A_LORA), g2], axis=1).astype(BF16)
    return (mu_rkv.astype(F32), row(mul), row(w0), row(a0), w2p, a2p, g2p, row(k_k), row(k_a), row(r_k),
            row(lnx_g), row(lnx_b))


SLOT = LANES
ROT_HALF = QK_ROPE // 2
NOPE_A = SLOT // 2 - ROT_HALF


def _slot_source():
    src = np.full((SLOT,), QK_HD, np.int32)
    src[:ROT_HALF] = QK_NOPE + np.arange(ROT_HALF)
    src[ROT_HALF:SLOT // 2] = np.arange(NOPE_A)
    src[SLOT // 2:SLOT // 2 + ROT_HALF] = QK_NOPE + ROT_HALF + np.arange(ROT_HALF)
    src[SLOT // 2 + ROT_HALF:SLOT // 2 + ROT_HALF + QK_NOPE - NOPE_A] = NOPE_A + np.arange(QK_NOPE - NOPE_A)
    return src


def _to_slot(v):
    pad = jnp.concatenate([v, jnp.zeros(v.shape[:-1] + (1,), v.dtype)], axis=-1)
    return jnp.take(pad, jnp.asarray(_slot_source()), axis=-1)
VT_ROWS = V_HD + 16
FLASH_AHEAD = 1
FLASH_QSPLIT = 1


FAST_BOUND = 40.0


def _mla_prep_kernel(cq_ref, ckv_ref, kr_ref, tab_ref, qn_ref, wq_ref, kvn_ref, wk_ref, wv_ref,
                     q_ref, k_ref, v_ref, v_scr):
    def rms(x, g, n):
        ms = jnp.sum(x * x, axis=-1, keepdims=True) * (1.0 / n)
        return x * lax.rsqrt(ms + EPS) * g

    def norm_rope(x, c_tab, s_tab):
        ms = jnp.sum(x * x, axis=-1, keepdims=True) * (1.0 / QK_HD)
        return (x * c_tab + pltpu.roll(x, SLOT // 2, axis=1) * s_tab) * lax.rsqrt(ms + EPS)

    cq = rms(cq_ref[...].astype(F32), qn_ref[...], Q_LORA)
    ckv = rms(ckv_ref[...].astype(F32), kvn_ref[...], KV_LORA)
    qf = _mm(cq, wq_ref[...])
    kf = _mm(ckv, wk_ref[...])
    v_scr[...] = _mm(ckv, wv_ref[...])
    vt = jnp.transpose(v_scr[...]).astype(BF16)
    ones = jnp.ones((VT_ROWS - V_HD, vt.shape[1]), BF16)
    for h in range(MLA_HEADS):
        v_ref[h * VT_ROWS:h * VT_ROWS + V_HD, :] = vt[h * V_HD:(h + 1) * V_HD, :]
        v_ref[h * VT_ROWS + V_HD:(h + 1) * VT_ROWS, :] = ones
    kr = kr_ref[...].astype(F32)
    cq_tab, sq_tab, ck_tab, sk_tab = tab_ref[0], tab_ref[1], tab_ref[2], tab_ref[3]
    for h in range(MLA_HEADS):
        cols = slice(h * SLOT, (h + 1) * SLOT)
        q_ref[:, cols] = norm_rope(qf[:, cols], cq_tab, sq_tab).astype(BF16)
        k_ref[:, cols] = norm_rope(kf[:, cols] + kr, ck_tab, sk_tab).astype(BF16)


def _flash_kernel(qi_ref, kj_ref, q_ref, k_ref, vt_ref, o_ref, acc, m_scr=None):
    online = m_scr is not None
    i = qi_ref[pl.program_id(1)]
    j = kj_ref[pl.program_id(1)]
    bq = q_ref.shape[0]
    bk = k_ref.shape[0]

    @pl.when(j == 0)
    def _():
        if online:
            m_scr[...] = jnp.full_like(m_scr, -jnp.inf)
        acc[...] = jnp.zeros_like(acc)

    def step(masked):
        if masked:
            causal = (lax.broadcasted_iota(jnp.int32, (bk, bq), 0) <= lax.broadcasted_iota(jnp.int32, (bk, bq), 1))
        qw = bq // FLASH_QSPLIT
        units = [(h, c) for h in range(MLA_HEADS) for c in range(FLASH_QSPLIT)]

        def scores(u):
            h, c = u
            cols = slice(h * SLOT, (h + 1) * SLOT)
            return _mm_nt(k_ref[:, cols], q_ref[c * qw:(c + 1) * qw, cols])

        pending = [scores(u) for u in units[:FLASH_AHEAD]]
        for n, (h, c) in enumerate(units):
            rows = slice(h * VT_ROWS, (h + 1) * VT_ROWS)
            qs = slice(c * qw, (c + 1) * qw)
            st = pending.pop(0)
            if n + FLASH_AHEAD < len(units):
                pending.append(scores(units[n + FLASH_AHEAD]))
            if masked:
                st = jnp.where(causal[:, qs], st, -jnp.inf)
            if online:
                m_prev = m_scr[h:h + 1, qs]
                m_new = jnp.maximum(m_prev, jnp.max(st, axis=0, keepdims=True))
                alpha = jnp.exp2(m_prev - m_new)
                pt = jnp.exp2(st - m_new)
                m_scr[h:h + 1, qs] = m_new
                acc[rows, qs] = acc[rows, qs] * alpha + _mm(vt_ref[rows, :], pt)
            else:
                acc[rows, qs] += _mm(vt_ref[rows, :], jnp.exp2(st))

    @pl.when(j < i)
    def _():
        step(False)

    @pl.when(j == i)
    def _():
        step(True)
        outs = []
        for h in range(MLA_HEADS):
            r0 = h * VT_ROWS
            outs.append(acc[r0:r0 + V_HD, :] / acc[r0 + V_HD:r0 + V_HD + 1, :])
        o_ref[...] = jnp.transpose(jnp.concatenate(outs, axis=0)).astype(o_ref.dtype)


def _score_bound(qk_norm_q, qk_norm_k):
    scale = QK_HD ** -0.5 * math.log2(math.e)
    gq = jnp.max(jnp.abs(qk_norm_q.astype(F32)), axis=-1)
    gk = jnp.max(jnp.abs(qk_norm_k.astype(F32)), axis=-1)
    return 1.01 * QK_HD * scale * gq * gk


def _mla_branch(z, prm, rope, score_bound, l):
    b, t, _ = z.shape
    tm = min(1024, t)
    hs = MLA_HEADS * SLOT
    vr = MLA_HEADS * VT_ROWS

    q, k, vt = pl.pallas_call(
        _mla_prep_kernel,
        grid=(b, t // tm),
        in_specs=[pl.BlockSpec((None, tm, Q_LORA), lambda i, j: (i, j, COL_CQ // Q_LORA)),
                  pl.BlockSpec((None, tm, KV_LORA), lambda i, j: (i, j, COL_CKV // KV_LORA)),
                  pl.BlockSpec((None, tm, LANES), lambda i, j: (i, j, COL_KR // LANES)),
                  pl.BlockSpec((None, 4, tm, SLOT), lambda i, j: (l, 0, j, 0))]
                 + [_layer_spec(a, l) for a in prm],
        out_specs=[pl.BlockSpec((None, tm, hs), lambda i, j: (i, j, 0)),
                   pl.BlockSpec((None, tm, hs), lambda i, j: (i, j, 0)),
                   pl.BlockSpec((None, vr, tm), lambda i, j: (i, 0, j))],
        out_shape=[jax.ShapeDtypeStruct((b, t, hs), BF16), jax.ShapeDtypeStruct((b, t, hs), BF16),
                   jax.ShapeDtypeStruct((b, vr, t), BF16)],
        scratch_shapes=[pltpu.VMEM((tm, C), F32)],
        compiler_params=_cparams(("parallel", "parallel")),
        name="mla_prep",
    )(z, z, z, rope, *prm)

    bq = min(512, t)
    nq = t // bq
    tri = [(i, j) for i in range(nq) for j in range(i + 1)]
    qi = jnp.asarray([p[0] for p in tri], jnp.int32)
    kj = jnp.asarray([p[1] for p in tri], jnp.int32)

    def attend(online):
        scratch = [pltpu.VMEM((vr, bq), F32)] + ([pltpu.VMEM((MLA_HEADS, bq), F32)] if online else [])
        return pl.pallas_call(
            _flash_kernel,
            grid_spec=pltpu.PrefetchScalarGridSpec(
                num_scalar_prefetch=2,
                grid=(b, len(tri)),
                in_specs=[pl.BlockSpec((None, bq, hs), lambda bi, s, qi, kj: (bi, qi[s], 0)),
                          pl.BlockSpec((None, bq, hs), lambda bi, s, qi, kj: (bi, kj[s], 0)),
                          pl.BlockSpec((None, vr, bq), lambda bi, s, qi, kj: (bi, 0, kj[s]))],
                out_specs=pl.BlockSpec((None, bq, C), lambda bi, s, qi, kj: (bi, qi[s], 0)),
                scratch_shapes=scratch),
            out_shape=jax.ShapeDtypeStruct((b, t, C), BF16),
            compiler_params=_cparams(("parallel", "arbitrary")),
            name="mla_attn_online" if online else "mla_attn",
        )(qi, kj, q, k, vt)

    return lax.cond(score_bound <= FAST_BOUND, lambda: attend(False), lambda: attend(True))


def _rope_tables(t, qk_norm_q, qk_norm_k):
    pos = jnp.arange(t, dtype=F32)
    inv_freq = ROPE_THETA ** (-jnp.arange(0, QK_ROPE, 2, dtype=F32) / QK_ROPE)
    ang = pos[:, None] * inv_freq[None, :]
    cos, sin = jnp.cos(ang), jnp.sin(ang)
    z = lambda n: jnp.zeros((t, n), F32)
    one = lambda n: jnp.ones((t, n), F32)
    tail = SLOT // 2 - ROT_HALF
    c_pos = jnp.concatenate([cos, one(NOPE_A), cos, one(QK_NOPE - NOPE_A), z(SLOT - QK_HD)], axis=1)
    s_pos = jnp.concatenate([-sin, z(NOPE_A), sin, z(tail)], axis=1)
    scale = QK_HD ** -0.5 * math.log2(math.e)

    def pair(g, mult):
        g = _to_slot(g.astype(F32)) * mult
        return [g[:, None, :] * c_pos[None], jnp.roll(g, SLOT // 2, axis=-1)[:, None, :] * s_pos[None]]

    return jnp.stack(pair(qk_norm_q, scale) + pair(qk_norm_k, 1.0), axis=1)


def _prep_mla(q_norm, w_uq, kv_norm, w_ukv):
    nl = q_norm.shape[0]
    nh = MLA_HEADS
    wq = _to_slot(w_uq.reshape(nl, Q_LORA, nh, QK_HD)).reshape(nl, Q_LORA, nh * SLOT).astype(BF16)
    wkv = w_ukv.reshape(nl, KV_LORA, nh, QK_NOPE + V_HD)
    wk = jnp.pad(wkv[..., :QK_NOPE], ((0, 0), (0, 0), (0, 0), (0, QK_ROPE)))
    wk = _to_slot(wk).reshape(nl, KV_LORA, nh * SLOT).astype(BF16)
    wv = wkv[..., QK_NOPE:].reshape(nl, KV_LORA, nh * V_HD).astype(BF16)
    return (q_norm.reshape(nl, 1, -1).astype(F32), wq, kv_norm.reshape(nl, 1, -1).astype(F32), wk, wv)


def _merge_kernel(y0_ref, y1_ref, y2_ref, y3_ref, gl_ref, x_ref, wb_ref, wo_ref, o_ref):
    d = x_ref.shape[1]
    merged = None
    for n, y_ref in enumerate((y0_ref, y1_ref, y2_ref, y3_ref)):
        term = _mm(y_ref[...], wb_ref[n]) * _sigmoid(gl_ref[:, n * d:(n + 1) * d].astype(F32))
        merged = term if merged is None else merged + term
    o_ref[...] = x_ref[...] + _mm(merged, wo_ref[...])


def _merge(ys, z2, x2, wb, wo, l):
    n, d = x2.shape
    tm = min(512, n)
    ytile = pl.BlockSpec((tm, C), lambda i: (i, 0))
    return pl.pallas_call(
        _merge_kernel,
        grid=(n // tm,),
        in_specs=[ytile, ytile, ytile, ytile,
                  pl.BlockSpec((tm, 4 * d), lambda i: (i, COL_GATE)),
                  pl.BlockSpec((tm, d), lambda i: (i, 0)),
                  _layer_spec(wb, l), _layer_spec(wo, l)],
        out_specs=pl.BlockSpec((tm, d), lambda i: (i, 0)),
        out_shape=jax.ShapeDtypeStruct((n, d), F32),
        compiler_params=_cparams(("parallel",)),
        name="merge",
    )(*ys, z2, x2, wb, wo)


def _mlp_kernel(x_ref, g_ref, w1_ref, w2_ref, o_ref, h_scr, acc):
    j = pl.program_id(1)

    @pl.when(j == 0)
    def _():
        x = x_ref[...]
        ms = jnp.mean(x * x, axis=-1, keepdims=True)
        h_scr[...] = (x * lax.rsqrt(ms + EPS) * g_ref[...]).astype(BF16)
        acc[...] = x

    a = jnp.maximum(jnp.dot(h_scr[...], w1_ref[...], preferred_element_type=F32), 0.0)
    acc[...] += _mm(a * a, w2_ref[...])

    @pl.when(j == pl.num_programs(1) - 1)
    def _():
        o_ref[...] = acc[...]


def _mlp(x2, g, w1, w2, l):
    n, d = x2.shape
    dff = w1.shape[2]
    tm = min(1024, n)
    tf = 2048
    return pl.pallas_call(
        _mlp_kernel,
        grid=(n // tm, dff // tf),
        in_specs=[pl.BlockSpec((tm, d), lambda i, j: (i, 0)),
                  pl.BlockSpec((None, 1, d), lambda i, j: (l, 0, 0)),
                  pl.BlockSpec((None, d, tf), lambda i, j: (l, 0, j)),
                  pl.BlockSpec((None, tf, d), lambda i, j: (l, j, 0))],
        out_specs=pl.BlockSpec((tm, d), lambda i, j: (i, 0)),
        out_shape=jax.ShapeDtypeStruct((n, d), F32),
        scratch_shapes=[pltpu.VMEM((tm, d), BF16), pltpu.VMEM((tm, d), F32)],
        compiler_params=_cparams(("parallel", "arbitrary")),
        name="mlp",
    )(x2, g, w1, w2)


def _prep_w_in(w_in):
    nl, d, _ = w_in.shape
    gate0 = 6 * C + W_LORA + A_LORA + G_LORA + Q_LORA + KV_LORA + QK_ROPE
    kr0 = gate0 - QK_ROPE
    parts = [
        w_in[:, :, gate0:],
        w_in[:, :, :kr0],
        w_in[:, :, kr0:kr0 + ROT_HALF],
        jnp.zeros((nl, d, NOPE_A), w_in.dtype),
        w_in[:, :, kr0 + ROT_HALF:gate0],
        jnp.zeros((nl, d, D_INP - COL_KR - SLOT // 2 - ROT_HALF), w_in.dtype),
    ]
    return jnp.concatenate(parts, axis=-1).astype(BF16)


def _prep_lru_gate(gate_w):
    nl, _, nh, hd, _ = gate_w.shape
    eye = jnp.eye(nh, dtype=gate_w.dtype)
    dense = jnp.einsum("lghij,hk->lghikj", gate_w, eye).reshape(nl, 2, nh * hd, nh * hd)
    return jnp.concatenate([dense[:, 0], dense[:, 1]], axis=-1).astype(BF16)


def kernel(x, norm_mix, w_in, lru_conv_w, lru_conv_b, lru_gate_w, lru_gate_b, lru_lambda, s5_a_re, s5_a_im, s5_b_re, s5_b_im, s5_c_re, s5_c_im, s5_d, s5_log_dt, s5_w_glu, rwkv_mu_rkv, rwkv_mu_w, rwkv_mu_a, rwkv_mu_g, rwkv_w0, rwkv_w2, rwkv_a0, rwkv_a2, rwkv_g2, rwkv_k_k, rwkv_k_a, rwkv_r_k, rwkv_lnx_g, rwkv_lnx_b, mla_q_norm, mla_w_uq, mla_kv_norm, mla_w_ukv, mla_qk_norm_q, mla_qk_norm_k, w_branch, w_out, norm_mlp, w_ff1, w_ff2):
    b, t, d = x.shape
    depth = w_in.shape[0]
    n = b * t
    row = lambda a: a.reshape(depth, 1, -1).astype(F32)

    w_in_p = _prep_w_in(w_in)
    lru_gw = _prep_lru_gate(lru_gate_w)
    s5_tabs = _prep_s5(s5_a_re, s5_a_im, s5_b_re, s5_b_im, s5_c_re, s5_c_im, s5_log_dt, t)
    rw_prm = _prep_rwkv(rwkv_mu_rkv, rwkv_mu_w, rwkv_mu_a, rwkv_mu_g, rwkv_w0, rwkv_w2, rwkv_a0, rwkv_a2, rwkv_g2,
                        rwkv_k_k, rwkv_k_a, rwkv_r_k, rwkv_lnx_g, rwkv_lnx_b)
    mla_prm = _prep_mla(mla_q_norm, mla_w_uq, mla_kv_norm, mla_w_ukv)
    rope = _rope_tables(t, mla_qk_norm_q, mla_qk_norm_k)
    score_bound = _score_bound(mla_qk_norm_q, mla_qk_norm_k)
    g_mix, g_mlp = row(norm_mix), row(norm_mlp)
    conv_b, gate_b, lam, s5_dr = row(lru_conv_b), row(lru_gate_b), row(lru_lambda), row(s5_d)
    w_glu, wb, wo = s5_w_glu.astype(BF16), w_branch.astype(BF16), w_out.astype(BF16)
    w1, w2 = w_ff1.astype(BF16), w_ff2.astype(BF16)

    lru_prm = (lru_conv_w.astype(F32), conv_b, lru_gw, gate_b, lam)
    x2 = x.reshape(n, d)
    for l in range(depth):
        z2, u2 = _inproj(x2, g_mix, w_in_p, l)
        z = z2.reshape(b, t, -1)
        y_lru = _lru_branch(z, lru_prm, l)
        y_s5 = _s5_branch(u2.reshape(b, t, C), s5_tabs, s5_dr, w_glu, l)
        y_rw = _rwkv_branch(z, rw_prm, l)
        y_mla = _mla_branch(z, mla_prm, rope, score_bound[l], l)
        ys = [y.reshape(n, C) for y in (y_lru, y_s5, y_rw, y_mla)]
        x2 = _merge(ys, z2, x2, wb, wo, l)
        x2 = _mlp(x2, g_mlp, w1, w2, l)
    return x2.reshape(b, t, d)
```

```python
import functools
import math

import jax
import jax.numpy as jnp
import numpy as np
from jax import lax
from jax.experimental import pallas as pl
from jax.experimental.pallas import tpu as pltpu

F32 = jnp.float32
BF16 = jnp.bfloat16

EPS = 1e-6
LRU_HEADS = 8
CONV_W = 4
LRU_C = 8.0
S5_GROUP = 16
S5_STATE = 64
RWKV_HD = 64
W_LORA, A_LORA, G_LORA = 64, 64, 128
RWKV_LN_EPS = 64e-5
MLA_HEADS = 8
QK_NOPE, QK_ROPE, V_HD = 64, 32, 64
QK_HD = QK_NOPE + QK_ROPE
Q_LORA, KV_LORA = 256, 128
ROPE_THETA = 10000.0

LANES = 128
SUBLANES = 8
VMEM_LIMIT = 56 * 1024 * 1024

D_MODEL = 1024
C = D_MODEL // 2
COL_GATE = 0
COL_LRU_X = 4 * D_MODEL
COL_LRU_G = COL_LRU_X + C
COL_S5_U = COL_LRU_G + C
COL_RW_R = COL_S5_U + C
COL_RW_K = COL_RW_R + C
COL_RW_V = COL_RW_K + C
COL_LORA = COL_RW_V + C
COL_CQ = COL_LORA + 256
COL_CKV = COL_CQ + Q_LORA
COL_KR = COL_CKV + KV_LORA
D_INP = 8 * D_MODEL


def _cparams(sem):
    return pltpu.CompilerParams(dimension_semantics=sem, vmem_limit_bytes=VMEM_LIMIT)


def _gelu_tanh(x):
    return 0.5 * x * (1.0 + jnp.tanh(math.sqrt(2.0 / math.pi) * (x + 0.044715 * (x * x * x))))


def _sigmoid(x):
    return 1.0 / (1.0 + jnp.exp(-x))


def _softplus(x):
    return jnp.maximum(x, 0.0) + jnp.log(1.0 + jnp.exp(-jnp.abs(x)))


HALO = 16


def _shift_rows(x, halo, k):
    xe = jnp.concatenate([halo, x], axis=0)
    n = x.shape[0]
    return xe[HALO - k:HALO - k + n, :]


def _layer_spec(arr, l):
    zeros = (0,) * (arr.ndim - 1)
    return pl.BlockSpec((None,) + arr.shape[1:], lambda *idx: (l,) + zeros)


def _inproj_kernel(x_ref, g_ref, w_ref, o_ref, u_ref, h_scr):
    j = pl.program_id(1)

    @pl.when(j == 0)
    def _():
        x = x_ref[...]
        ms = jnp.mean(x * x, axis=-1, keepdims=True)
        h_scr[...] = (x * lax.rsqrt(ms + EPS) * g_ref[...]).astype(BF16)

    y = jnp.dot(h_scr[...], w_ref[...], preferred_element_type=F32)
    o_ref[...] = y.astype(BF16)
    tn = w_ref.shape[1]
    u0 = COL_S5_U % tn

    @pl.when(j == COL_S5_U // tn)
    def _():
        u_ref[...] = y[:, u0:u0 + C]


INPROJ_TN = 2048


def _inproj(x2, g, w, l):
    n, d = x2.shape
    dn = w.shape[2]
    tm = min(1024, n)
    tn = INPROJ_TN
    assert COL_S5_U % tn + C <= tn and dn % tn == 0
    return pl.pallas_call(
        _inproj_kernel,
        grid=(n // tm, dn // tn),
        in_specs=[
            pl.BlockSpec((tm, d), lambda i, j: (i, 0)),
            pl.BlockSpec((None, 1, d), lambda i, j: (l, 0, 0)),
            pl.BlockSpec((None, d, tn), lambda i, j: (l, 0, j)),
        ],
        out_specs=[pl.BlockSpec((tm, tn), lambda i, j: (i, j)),
                   pl.BlockSpec((tm, C), lambda i, j: (i, 0))],
        out_shape=[jax.ShapeDtypeStruct((n, dn), BF16), jax.ShapeDtypeStruct((n, C), F32)],
        scratch_shapes=[pltpu.VMEM((tm, d), BF16)],
        compiler_params=_cparams(("parallel", "arbitrary")),
        name="inproj",
    )(x2, g, w)


def _lru_kernel(x_ref, xh_ref, g_ref, cw_ref, cb_ref, gw_ref, gb_ref, lam_ref, o_ref, carry):
    t = pl.program_id(1)
    tm, c = x_ref.shape

    @pl.when(t == 0)
    def _():
        carry[...] = jnp.zeros_like(carry)

    x = x_ref[...].astype(F32)
    halo = jnp.where(t == 0, 0.0, xh_ref[...].astype(F32))
    cw = cw_ref[...]
    xc = cb_ref[...] + cw[CONV_W - 1:CONV_W, :] * x
    for k in range(1, CONV_W):
        xc = xc + cw[CONV_W - 1 - k:CONV_W - k, :] * _shift_rows(x, halo, k)

    gates = jnp.dot(xc.astype(BF16), gw_ref[...], preferred_element_type=F32) + gb_ref[...]
    r = _sigmoid(gates[:, :c])
    i = _sigmoid(gates[:, c:])
    log_a = (-LRU_C) * r * _softplus(-lam_ref[...])
    a = jnp.exp(log_a)
    u = jnp.sqrt(1.0 - a * a) * (i * xc)

    sub = lax.broadcasted_iota(jnp.int32, (tm, c), 0) % SUBLANES
    s = 1
    while s < SUBLANES:
        a_sh = jnp.where(sub >= s, pltpu.roll(a, s, axis=0), 1.0)
        u_sh = jnp.where(sub >= s, pltpu.roll(u, s, axis=0), 0.0)
        u = u + a * u_sh
        a = a * a_sh
        s *= 2
    hp = carry[SUBLANES - 1:SUBLANES, :]
    blocks = []
    for blk in range(tm // SUBLANES):
        rows = slice(blk * SUBLANES, (blk + 1) * SUBLANES)
        hb = u[rows, :] + a[rows, :] * hp
        hp = hb[SUBLANES - 1:SUBLANES, :]
        blocks.append(hb)
    h = jnp.concatenate(blocks, axis=0)
    carry[...] = blocks[-1]
    o_ref[...] = (h * _gelu_tanh(g_ref[...].astype(F32))).astype(o_ref.dtype)


def _lru_branch(z, prm, l):
    b, t, _ = z.shape
    tm = min(256, t)
    cb = COL_LRU_X // C
    hb = tm // HALO
    return pl.pallas_call(
        _lru_kernel,
        grid=(b, t // tm),
        in_specs=[
            pl.BlockSpec((None, tm, C), lambda i, j: (i, j, cb)),
            pl.BlockSpec((None, HALO, C), lambda i, j: (i, jnp.maximum(j * hb - 1, 0), cb)),
            pl.BlockSpec((None, tm, C), lambda i, j: (i, j, cb + 1)),
        ] + [_layer_spec(a, l) for a in prm],
        out_specs=pl.BlockSpec((None, tm, C), lambda i, j: (i, j, 0)),
        out_shape=jax.ShapeDtypeStruct((b, t, C), BF16),
        scratch_shapes=[pltpu.VMEM((SUBLANES, C), F32)],
        compiler_params=_cparams(("parallel", "arbitrary")),
        name="rglru",
    )(z, z, z, *prm)


S5_L = 8
S5_OCT = LANES // S5_GROUP


def _s5_kernel(u_ref, win_ref, bd_ref, wout_ref, lstep_ref, lblk_ref, y_ref, uf, wintra, carry):
    t = pl.program_id(2)
    tm = u_ref.shape[0]
    nc = tm // S5_L
    ns = carry.shape[1]
    half = ns // 2

    @pl.when((pl.program_id(1) == 0) & (t == 0))
    def _():
        wintra[...] = jnp.zeros_like(wintra)
        for i in range(S5_L):
            for j in range(i, S5_L):
                wintra[i * LANES:(i + 1) * LANES, j * LANES:(j + 1) * LANES] = bd_ref[j - i]

    @pl.when(t == 0)
    def _():
        carry[...] = jnp.zeros_like(carry)

    for i in range(S5_L):
        uf[:, i * LANES:(i + 1) * LANES] = u_ref[pl.ds(i, nc, stride=S5_L), :].astype(BF16)
    ufv = uf[...]
    s = jnp.dot(ufv, win_ref[...], preferred_element_type=F32)

    def cmul(v, ar, ai):
        sw = jnp.concatenate([v[:, half:], v[:, :half]], axis=1)
        return v * ar + sw * ai

    row = lax.broadcasted_iota(jnp.int32, (nc, ns), 0)
    sub = row % SUBLANES
    k = 0
    st = 1
    while st < SUBLANES:
        sh = jnp.where(sub >= st, pltpu.roll(s, st, axis=0), 0.0)
        s = s + cmul(sh, lstep_ref[2 * k:2 * k + 1, :], lstep_ref[2 * k + 1:2 * k + 2, :])
        k += 1
        st *= 2
    cin = carry[0:1, :]
    prev = cin
    blocks = []
    for blk in range(nc // SUBLANES):
        sb = s[blk * SUBLANES:(blk + 1) * SUBLANES, :] + cmul(jnp.broadcast_to(prev, (SUBLANES, ns)),
                                                              lblk_ref[0], lblk_ref[1])
        prev = sb[SUBLANES - 1:SUBLANES, :]
        blocks.append(sb)
    s = jnp.concatenate(blocks, axis=0)
    carry[...] = jnp.broadcast_to(prev, carry.shape)
    sprev = jnp.where(row >= 1, pltpu.roll(s, 1, axis=0), cin)

    yf = jnp.dot(ufv, wintra[...], preferred_element_type=F32)
    yf = yf + jnp.dot(sprev.astype(BF16), wout_ref[...], preferred_element_type=F32)
    for j in range(S5_L):
        y_ref[pl.ds(j, nc, stride=S5_L), :] = yf[:, j * LANES:(j + 1) * LANES]


def _s5_tail_kernel(y_ref, u_ref, d_ref, w_ref, o_ref):
    c = u_ref.shape[1]
    y = _gelu_tanh(y_ref[...] + d_ref[...] * u_ref[...])
    z = jnp.dot(y.astype(BF16), w_ref[...], preferred_element_type=F32)
    o_ref[...] = (z[:, :c] * _sigmoid(z[:, c:])).astype(o_ref.dtype)


def _s5_tile(t):
    return min(4096, t)


def _s5_branch(u, tabs, d_skip, w_glu, l):
    win, bd, wout, lstep, lblk = tabs
    b, t, _ = u.shape
    tm = _s5_tile(t)
    nc = tm // S5_L
    noct = C // LANES
    ns = win.shape[-1]
    fl = S5_L * LANES

    def table(arr):
        zeros = (0,) * (arr.ndim - 2)
        return pl.BlockSpec((None, None) + arr.shape[2:], lambda q, i, j: (l, q) + zeros)

    y = pl.pallas_call(
        _s5_kernel,
        grid=(noct, b, t // tm),
        in_specs=[pl.BlockSpec((None, tm, LANES), lambda q, i, j: (i, j, q)),
                  table(win), table(bd), table(wout), table(lstep), table(lblk)],
        out_specs=pl.BlockSpec((None, tm, LANES), lambda q, i, j: (i, j, q)),
        out_shape=jax.ShapeDtypeStruct((b, t, C), F32),
        scratch_shapes=[pltpu.VMEM((nc, fl), BF16), pltpu.VMEM((fl, fl), BF16), pltpu.VMEM((SUBLANES, ns), F32)],
        compiler_params=_cparams(("arbitrary", "arbitrary", "arbitrary")),
        name="s5_ssm",
    )(u, win, bd, wout, lstep, lblk)

    tt = min(2048, t)
    return pl.pallas_call(
        _s5_tail_kernel,
        grid=(b, t // tt),
        in_specs=[
            pl.BlockSpec((None, tt, C), lambda i, j: (i, j, 0)),
            pl.BlockSpec((None, tt, C), lambda i, j: (i, j, 0)),
            _layer_spec(d_skip, l),
            _layer_spec(w_glu, l),
        ],
        out_specs=pl.BlockSpec((None, tt, C), lambda i, j: (i, j, 0)),
        out_shape=jax.ShapeDtypeStruct((b, t, C), BF16),
        compiler_params=_cparams(("parallel", "parallel")),
        name="s5_glu",
    )(y, u, d_skip, w_glu)


def _prep_s5(a_re, a_im, b_re, b_im, c_re, c_im, log_dt, t):
    nl, g, p = a_re.shape
    m = S5_GROUP
    noct = g // S5_OCT
    nc = _s5_tile(t) // S5_L
    lam = lax.complex(a_re.astype(F32), a_im.astype(F32))
    dt = jnp.exp(log_dt.astype(F32))[..., None]
    ldt = lam * dt
    lam_bar = jnp.exp(ldt)
    b_bar = ((lam_bar - 1.0) / lam)[..., None] * lax.complex(b_re.astype(F32), b_im.astype(F32))
    cc = lax.complex(c_re.astype(F32), c_im.astype(F32))

    def powers(e):
        return jnp.exp(ldt[..., None] * e.astype(F32).reshape((1, 1, 1, -1)))

    steps = jnp.arange(S5_L)

    def octet_bd(x, row_inner, col_pre, col_post):
        ncol = col_pre * S5_OCT * col_post
        cidx = np.arange(ncol)
        src = (cidx // (S5_OCT * col_post)) * col_post + cidx % col_post
        expand = jnp.asarray(np.arange(col_pre * col_post)[:, None] == src[None, :], F32)
        ridx = np.arange(x.shape[2])
        keep = jnp.asarray(((ridx // row_inner) % S5_OCT)[:, None] == ((cidx // col_post) % S5_OCT)[None, :])
        return jnp.where(keep, jnp.einsum("lqrc,cn->lqrn", x.astype(BF16), expand.astype(BF16),
                                          preferred_element_type=F32), 0.0).astype(BF16)

    kd = jnp.einsum("lgop,lgpd,lgpi->lgdio", cc, powers(steps), b_bar).real
    kd = kd.reshape(nl, noct, S5_OCT, S5_L, m, m)
    kd = jnp.moveaxis(kd, 3, 2).reshape(nl, noct, S5_L * LANES, m)
    bd = octet_bd(kd, m, 1, m).reshape(nl, noct, S5_L, LANES, LANES)

    wi = powers(S5_L - 1 - steps)[..., None] * b_bar[:, :, :, None, :]
    wi = jnp.stack([wi.real, wi.imag], axis=2)
    wi = wi.reshape(nl, noct, S5_OCT, 2, p, S5_L, m)
    wi = jnp.transpose(wi, (0, 1, 5, 2, 6, 3, 4)).reshape(nl, noct, S5_L * LANES, 2 * p)
    win = octet_bd(wi, m, 2, p)

    wo = cc[:, :, :, :, None] * powers(steps + 1)[:, :, None, :, :]
    wo = jnp.stack([wo.real, -wo.imag], axis=2)
    wo = wo.reshape(nl, noct, S5_OCT, 2, m, p, S5_L)
    wo = jnp.transpose(wo, (0, 1, 3, 2, 5, 6, 4)).reshape(nl, noct, 2 * S5_OCT * p, S5_L * m)
    wout = octet_bd(wo, p, S5_L, m)

    def state_mult(e):
        pw = powers(e)
        pw = jnp.transpose(pw, (0, 3, 1, 2)).reshape(nl, e.shape[0], noct, S5_OCT * p)
        pw = jnp.moveaxis(pw, 2, 1)
        return (jnp.concatenate([pw.real, pw.real], axis=-1), jnp.concatenate([-pw.imag, pw.imag], axis=-1))

    nsteps = int(math.log2(SUBLANES))
    ar, ai = state_mult(S5_L * (2 ** jnp.arange(nsteps)))
    lstep = jnp.stack([ar, ai], axis=3).reshape(nl, noct, 2 * nsteps, -1)
    lblk = jnp.stack(state_mult(S5_L * (jnp.arange(SUBLANES) + 1)), axis=2)
    return win, bd, wout, lstep, lblk


RW_L = 64
HIGHEST = lax.Precision.HIGHEST


def _mm(a, b, dims=(((1,), (0,)), ((), ()))):
    return lax.dot_general(a.astype(BF16), b.astype(BF16), dims, preferred_element_type=F32)


def _mm_nt(a, b):
    return _mm(a, b, (((1,), (1,)), ((), ())))


def _mm_tn(a, b):
    return _mm(a, b, (((0,), (0,)), ((), ())))


def _mm_exact(a, b):
    return lax.dot_general(a, b, (((1,), (0,)), ((), ())), precision=HIGHEST, preferred_element_type=F32)


def _head_sum(x, m0):
    s0 = jnp.sum(jnp.where(m0, x, 0.0), axis=-1, keepdims=True)
    s1 = jnp.sum(jnp.where(m0, 0.0, x), axis=-1, keepdims=True)
    return jnp.where(m0, s0, s1)


def _rwkv_kernel(r_ref, k_ref, v_ref, lo_ref, rh_ref, kh_ref, vh_ref, loh_ref, mu_ref, mul_ref, w0_ref, a0_ref,
                 w2_ref, a2_ref, g2_ref, kk_ref, ka_ref, rk_ref, lng_ref, lnb_ref, o_ref, hstate):
    t = pl.program_id(1)
    tm, c = r_ref.shape
    npair = c // LANES
    ll = RW_L

    @pl.when(t == 0)
    def _():
        hstate[...] = jnp.zeros_like(hstate)

    def mix(ref, href, mu):
        p = ref[...].astype(F32)
        prev = _shift_rows(p, jnp.where(t == 0, 0.0, href[...].astype(F32)), 1)
        return p + (prev - p) * mu

    mu = mu_ref[...]
    r = mix(r_ref, rh_ref, mu[0:1, :])
    k = mix(k_ref, kh_ref, mu[1:2, :])
    v = mix(v_ref, vh_ref, mu[2:3, :])
    lo = mix(lo_ref, loh_ref, mul_ref[...])
    wlog = -_softplus(-(w0_ref[...] + _mm(jnp.tanh(lo), w2_ref[...]))) - 0.5
    lw = -jnp.exp(wlog)
    a = _sigmoid(a0_ref[...] + _mm(lo, a2_ref[...]))
    g = _mm(_sigmoid(lo), g2_ref[...])
    kk = k * kk_ref[...]
    k = k * (1.0 + (a - 1.0) * ka_ref[...])

    lane = lax.broadcasted_iota(jnp.int32, (ll, LANES), 1)
    m0 = lane < RWKV_HD
    ri = lax.broadcasted_iota(jnp.int32, (2 * ll, 2 * ll), 0)
    ci = lax.broadcasted_iota(jnp.int32, (2 * ll, 2 * ll), 1)
    same = (ri >= ll) == (ci >= ll)
    strict = same & (ri > ci)
    incl = same & (ri >= ci)
    eye = ri == ci
    tri = (lax.broadcasted_iota(jnp.int32, (ll, ll), 0) >= lax.broadcasted_iota(jnp.int32, (ll, ll), 1)).astype(F32)

    def st(x):
        return jnp.concatenate([jnp.where(m0, x, 0.0), jnp.where(m0, 0.0, x)], axis=0)

    nch = tm // ll
    pcs = [(ch, p) for ch in range(nch) for p in range(npair)]
    cum_all = [_mm_exact(tri, lw[ch * ll:(ch + 1) * ll, :]) for ch in range(nch)]
    pre = {}
    for (ch, p) in pcs:
        rows, cols = slice(ch * ll, (ch + 1) * ll), slice(p * LANES, (p + 1) * LANES)
        rp, kp, vp, ap, lwp, cum = r[rows, cols], k[rows, cols], v[rows, cols], a[rows, cols], lw[rows, cols], cum_all[ch][:, cols]
        kkp = kk[rows, cols]
        kkp = kkp * lax.rsqrt(_head_sum(kkp * kkp, m0) + 1e-12)
        cum_l = cum[ll - 1:ll, :]
        e_neg = jnp.exp(-cum)
        e_last = jnp.exp(cum_l - cum)
        kb = kkp * ap
        stb = lambda x: st(x.astype(BF16))
        pre[ch, p] = dict(
            rows=rows, cols=cols, rp=rp, kp=kp, vp=vp,
            a_st=stb(-kkp * jnp.exp(cum - lwp)), r_st=st(rp * jnp.exp(cum)),
            v_st=stb(vp), b_st=stb(kb * e_neg), k_st=stb(kp * e_neg),
            bh_st=stb(kb * e_last), kh_st=stb(kp * e_last), g_l=jnp.exp(cum_l))

    gram = {pc: _mm_nt(jnp.concatenate([pre[pc]["a_st"], pre[pc]["r_st"].astype(BF16)], axis=0),
                       jnp.concatenate([pre[pc]["b_st"], pre[pc]["k_st"]], axis=0)) for pc in pcs}
    a_ab = {pc: jnp.where(strict, gram[pc][:2 * ll, :2 * ll], 0.0) for pc in pcs}
    a_ak = {pc: jnp.where(strict, gram[pc][:2 * ll, 2 * ll:], 0.0).astype(BF16) for pc in pcs}
    m_rb = {pc: jnp.where(incl, gram[pc][2 * ll:, :2 * ll], 0.0).astype(BF16) for pc in pcs}
    m_rk = {pc: jnp.where(incl, gram[pc][2 * ll:, 2 * ll:], 0.0).astype(BF16) for pc in pcs}
    akv = {pc: _mm(a_ak[pc], pre[pc]["v_st"]) for pc in pcs}
    tinv = {pc: jnp.where(eye, 1.0, a_ab[pc]) for pc in pcs}
    pw = {pc: a_ab[pc].astype(BF16) for pc in pcs}
    for _ in range(int(math.log2(ll)) - 1):
        pw = {pc: _mm(pw[pc], pw[pc]).astype(BF16) for pc in pcs}
        tinv = {pc: tinv[pc] + _mm(tinv[pc], pw[pc]) for pc in pcs}
    xa = {pc: _mm(tinv[pc], jnp.concatenate([pre[pc]["a_st"], akv[pc].astype(BF16)], axis=1)).astype(BF16)
          for pc in pcs}
    ma = {pc: _mm(m_rb[pc], xa[pc][:, :LANES]) for pc in pcs}
    y_loc = {pc: _mm(jnp.concatenate([m_rb[pc], m_rk[pc]], axis=1),
                     jnp.concatenate([xa[pc][:, LANES:], pre[pc]["v_st"]], axis=0)) for pc in pcs}
    ba = {pc: _mm_tn(pre[pc]["bh_st"], xa[pc][:, :LANES]) for pc in pcs}
    f_loc = {pc: _mm_tn(jnp.concatenate([pre[pc]["bh_st"], pre[pc]["kh_st"]], axis=0),
                        jnp.concatenate([xa[pc][:, LANES:], pre[pc]["v_st"]], axis=0)) for pc in pcs}
    for (ch, p) in pcs:
        pc = (ch, p)
        q = pre[pc]
        r_eff = q["r_st"] + ma[pc]
        g_eff = jnp.where(eye, q["g_l"], 0.0) + ba[pc]
        hs = _mm(jnp.concatenate([g_eff, r_eff], axis=0), hstate[p])
        hstate[p] = hs[:2 * ll, :] + f_loc[pc]
        y_st = hs[2 * ll:, :] + y_loc[pc]
        y = y_st[:ll, :] + y_st[ll:, :]
        rows, cols = q["rows"], q["cols"]
        mean = _head_sum(y, m0) * (1.0 / RWKV_HD)
        d = y - mean
        var = _head_sum(d * d, m0) * (1.0 / RWKV_HD)
        yn = d * lax.rsqrt(var + RWKV_LN_EPS) * lng_ref[:, cols] + lnb_ref[:, cols]
        bonus = _head_sum(q["rp"] * q["kp"] * rk_ref[:, cols], m0) * q["vp"]
        o_ref[rows, cols] = ((yn + bonus) * g[rows, cols]).astype(o_ref.dtype)


def _rwkv_branch(z, prm, l):
    b, t, _ = z.shape
    tm = min(8 * RW_L, t)
    hb = tm // HALO
    cr = COL_RW_R // C
    cl = COL_LORA // 256
    npair = C // LANES

    def tile(cb, w):
        return pl.BlockSpec((None, tm, w), lambda i, j: (i, j, cb))

    def halo(cb, w):
        return pl.BlockSpec((None, HALO, w), lambda i, j: (i, jnp.maximum(j * hb - 1, 0), cb))

    return pl.pallas_call(
        _rwkv_kernel,
        grid=(b, t // tm),
        in_specs=[tile(cr, C), tile(cr + 1, C), tile(cr + 2, C), tile(cl, 256),
                  halo(cr, C), halo(cr + 1, C), halo(cr + 2, C), halo(cl, 256)]
                 + [_layer_spec(a, l) for a in prm],
        out_specs=pl.BlockSpec((None, tm, C), lambda i, j: (i, j, 0)),
        out_shape=jax.ShapeDtypeStruct((b, t, C), BF16),
        scratch_shapes=[pltpu.VMEM((npair, LANES, LANES), F32)],
        compiler_params=_cparams(("parallel", "arbitrary")),
        name="rwkv7",
    )(z, z, z, z, z, z, z, z, *prm)


def _prep_rwkv(mu_rkv, mu_w, mu_a, mu_g, w0, w2, a0, a2, g2, k_k, k_a, r_k, lnx_g, lnx_b):
    nl = mu_rkv.shape[0]
    row = lambda x: x.reshape(nl, 1, -1).astype(F32)
    mul = jnp.concatenate([mu_w, mu_a, mu_g], axis=-1)
    zw = lambda n: jnp.zeros((nl, n, C), F32)
    w2p = jnp.concatenate([w2, zw(A_LORA + G_LORA)], axis=1).astype(BF16)
    a2p = jnp.concatenate([zw(W_LORA), a2, zw(G_LORA)], axis=1).astype(BF16)
    g2p = jnp.concatenate([zw(W_LORA + A_LORA), g2], axis=1).astype(BF16)
    return (mu_rkv.astype(F32), row(mul), row(w0), row(a0), w2p, a2p, g2p, row(k_k), row(k_a), row(r_k),
            row(lnx_g), row(lnx_b))


SLOT = LANES
ROT_HALF = QK_ROPE // 2
NOPE_A = SLOT // 2 - ROT_HALF


def _slot_source():
    src = np.full((SLOT,), QK_HD, np.int32)
    src[:ROT_HALF] = QK_NOPE + np.arange(ROT_HALF)
    src[ROT_HALF:SLOT // 2] = np.arange(NOPE_A)
    src[SLOT // 2:SLOT // 2 + ROT_HALF] = QK_NOPE + ROT_HALF + np.arange(ROT_HALF)
    src[SLOT // 2 + ROT_HALF:SLOT // 2 + ROT_HALF + QK_NOPE - NOPE_A] = NOPE_A + np.arange(QK_NOPE - NOPE_A)
    return src


def _to_slot(v):
    pad = jnp.concatenate([v, jnp.zeros(v.shape[:-1] + (1,), v.dtype)], axis=-1)
    return jnp.take(pad, jnp.asarray(_slot_source()), axis=-1)
VT_ROWS = V_HD + 16
FLASH_AHEAD = 1
FLASH_QSPLIT = 1


FAST_BOUND = 40.0


def _mla_prep_kernel(cq_ref, ckv_ref, kr_ref, tab_ref, qn_ref, wq_ref, kvn_ref, wk_ref, wv_ref,
                     q_ref, k_ref, v_ref, v_scr):
    def rms(x, g, n):
        ms = jnp.sum(x * x, axis=-1, keepdims=True) * (1.0 / n)
        return x * lax.rsqrt(ms + EPS) * g

    def norm_rope(x, c_tab, s_tab):
        ms = jnp.sum(x * x, axis=-1, keepdims=True) * (1.0 / QK_HD)
        return (x * c_tab + pltpu.roll(x, SLOT // 2, axis=1) * s_tab) * lax.rsqrt(ms + EPS)

    cq = rms(cq_ref[...].astype(F32), qn_ref[...], Q_LORA)
    ckv = rms(ckv_ref[...].astype(F32), kvn_ref[...], KV_LORA)
    qf = _mm(cq, wq_ref[...])
    kf = _mm(ckv, wk_ref[...])
    v_scr[...] = _mm(ckv, wv_ref[...])
    vt = jnp.transpose(v_scr[...]).astype(BF16)
    ones = jnp.ones((VT_ROWS - V_HD, vt.shape[1]), BF16)
    for h in range(MLA_HEADS):
        v_ref[h * VT_ROWS:h * VT_ROWS + V_HD, :] = vt[h * V_HD:(h + 1) * V_HD, :]
        v_ref[h * VT_ROWS + V_HD:(h + 1) * VT_ROWS, :] = ones
    kr = kr_ref[...].astype(F32)
    cq_tab, sq_tab, ck_tab, sk_tab = tab_ref[0], tab_ref[1], tab_ref[2], tab_ref[3]
    for h in range(MLA_HEADS):
        cols = slice(h * SLOT, (h + 1) * SLOT)
        q_ref[:, cols] = norm_rope(qf[:, cols], cq_tab, sq_tab).astype(BF16)
        k_ref[:, cols] = norm_rope(kf[:, cols] + kr, ck_tab, sk_tab).astype(BF16)


def _flash_kernel(qi_ref, kj_ref, q_ref, k_ref, vt_ref, o_ref, acc, m_scr=None):
    online = m_scr is not None
    i = qi_ref[pl.program_id(1)]
    j = kj_ref[pl.program_id(1)]
    bq = q_ref.shape[0]
    bk = k_ref.shape[0]

    @pl.when(j == 0)
    def _():
        if online:
            m_scr[...] = jnp.full_like(m_scr, -jnp.inf)
        acc[...] = jnp.zeros_like(acc)

    def step(masked):
        if masked:
            causal = (lax.broadcasted_iota(jnp.int32, (bk, bq), 0) <= lax.broadcasted_iota(jnp.int32, (bk, bq), 1))
        qw = bq // FLASH_QSPLIT
        units = [(h, c) for h in range(MLA_HEADS) for c in range(FLASH_QSPLIT)]

        def scores(u):
            h, c = u
            cols = slice(h * SLOT, (h + 1) * SLOT)
            return _mm_nt(k_ref[:, cols], q_ref[c * qw:(c + 1) * qw, cols])

        pending = [scores(u) for u in units[:FLASH_AHEAD]]
        for n, (h, c) in enumerate(units):
            rows = slice(h * VT_ROWS, (h + 1) * VT_ROWS)
            qs = slice(c * qw, (c + 1) * qw)
            st = pending.pop(0)
            if n + FLASH_AHEAD < len(units):
                pending.append(scores(units[n + FLASH_AHEAD]))
            if masked:
                st = jnp.where(causal[:, qs], st, -jnp.inf)
            if online:
                m_prev = m_scr[h:h + 1, qs]
                m_new = jnp.maximum(m_prev, jnp.max(st, axis=0, keepdims=True))
                alpha = jnp.exp2(m_prev - m_new)
                pt = jnp.exp2(st - m_new)
                m_scr[h:h + 1, qs] = m_new
                acc[rows, qs] = acc[rows, qs] * alpha + _mm(vt_ref[rows, :], pt)
            else:
                acc[rows, qs] += _mm(vt_ref[rows, :], jnp.exp2(st))

    @pl.when(j < i)
    def _():
        step(False)

    @pl.when(j == i)
    def _():
        step(True)
        outs = []
        for h in range(MLA_HEADS):
            r0 = h * VT_ROWS
            outs.append(acc[r0:r0 + V_HD, :] / acc[r0 + V_HD:r0 + V_HD + 1, :])
        o_ref[...] = jnp.transpose(jnp.concatenate(outs, axis=0)).astype(o_ref.dtype)


def _score_bound(qk_norm_q, qk_norm_k):
    scale = QK_HD ** -0.5 * math.log2(math.e)
    gq = jnp.max(jnp.abs(qk_norm_q.astype(F32)), axis=-1)
    gk = jnp.max(jnp.abs(qk_norm_k.astype(F32)), axis=-1)
    return 1.01 * QK_HD * scale * gq * gk


def _mla_branch(z, prm, rope, score_bound, l):
    b, t, _ = z.shape
    tm = min(1024, t)
    hs = MLA_HEADS * SLOT
    vr = MLA_HEADS * VT_ROWS

    q, k, vt = pl.pallas_call(
        _mla_prep_kernel,
        grid=(b, t // tm),
        in_specs=[pl.BlockSpec((None, tm, Q_LORA), lambda i, j: (i, j, COL_CQ // Q_LORA)),
                  pl.BlockSpec((None, tm, KV_LORA), lambda i, j: (i, j, COL_CKV // KV_LORA)),
                  pl.BlockSpec((None, tm, LANES), lambda i, j: (i, j, COL_KR // LANES)),
                  pl.BlockSpec((None, 4, tm, SLOT), lambda i, j: (l, 0, j, 0))]
                 + [_layer_spec(a, l) for a in prm],
        out_specs=[pl.BlockSpec((None, tm, hs), lambda i, j: (i, j, 0)),
                   pl.BlockSpec((None, tm, hs), lambda i, j: (i, j, 0)),
                   pl.BlockSpec((None, vr, tm), lambda i, j: (i, 0, j))],
        out_shape=[jax.ShapeDtypeStruct((b, t, hs), BF16), jax.ShapeDtypeStruct((b, t, hs), BF16),
                   jax.ShapeDtypeStruct((b, vr, t), BF16)],
        scratch_shapes=[pltpu.VMEM((tm, C), F32)],
        compiler_params=_cparams(("parallel", "parallel")),
        name="mla_prep",
    )(z, z, z, rope, *prm)

    bq = min(512, t)
    nq = t // bq
    tri = [(i, j) for i in range(nq) for j in range(i + 1)]
    qi = jnp.asarray([p[0] for p in tri], jnp.int32)
    kj = jnp.asarray([p[1] for p in tri], jnp.int32)

    def attend(online):
        scratch = [pltpu.VMEM((vr, bq), F32)] + ([pltpu.VMEM((MLA_HEADS, bq), F32)] if online else [])
        return pl.pallas_call(
            _flash_kernel,
            grid_spec=pltpu.PrefetchScalarGridSpec(
                num_scalar_prefetch=2,
                grid=(b, len(tri)),
                in_specs=[pl.BlockSpec((None, bq, hs), lambda bi, s, qi, kj: (bi, qi[s], 0)),
                          pl.BlockSpec((None, bq, hs), lambda bi, s, qi, kj: (bi, kj[s], 0)),
                          pl.BlockSpec((None, vr, bq), lambda bi, s, qi, kj: (bi, 0, kj[s]))],
                out_specs=pl.BlockSpec((None, bq, C), lambda bi, s, qi, kj: (bi, qi[s], 0)),
                scratch_shapes=scratch),
            out_shape=jax.ShapeDtypeStruct((b, t, C), BF16),
            compiler_params=_cparams(("parallel", "arbitrary")),
            name="mla_attn_online" if online else "mla_attn",
        )(qi, kj, q, k, vt)

    return lax.cond(score_bound <= FAST_BOUND, lambda: attend(False), lambda: attend(True))


def _rope_tables(t, qk_norm_q, qk_norm_k):
    pos = jnp.arange(t, dtype=F32)
    inv_freq = ROPE_THETA ** (-jnp.arange(0, QK_ROPE, 2, dtype=F32) / QK_ROPE)
    ang = pos[:, None] * inv_freq[None, :]
    cos, sin = jnp.cos(ang), jnp.sin(ang)
    z = lambda n: jnp.zeros((t, n), F32)
    one = lambda n: jnp.ones((t, n), F32)
    tail = SLOT // 2 - ROT_HALF
    c_pos = jnp.concatenate([cos, one(NOPE_A), cos, one(QK_NOPE - NOPE_A), z(SLOT - QK_HD)], axis=1)
    s_pos = jnp.concatenate([-sin, z(NOPE_A), sin, z(tail)], axis=1)
    scale = QK_HD ** -0.5 * math.log2(math.e)

    def pair(g, mult):
        g = _to_slot(g.astype(F32)) * mult
        return [g[:, None, :] * c_pos[None], jnp.roll(g, SLOT // 2, axis=-1)[:, None, :] * s_pos[None]]

    return jnp.stack(pair(qk_norm_q, scale) + pair(qk_norm_k, 1.0), axis=1)


def _prep_mla(q_norm, w_uq, kv_norm, w_ukv):
    nl = q_norm.shape[0]
    nh = MLA_HEADS
    wq = _to_slot(w_uq.reshape(nl, Q_LORA, nh, QK_HD)).reshape(nl, Q_LORA, nh * SLOT).astype(BF16)
    wkv = w_ukv.reshape(nl, KV_LORA, nh, QK_NOPE + V_HD)
    wk = jnp.pad(wkv[..., :QK_NOPE], ((0, 0), (0, 0), (0, 0), (0, QK_ROPE)))
    wk = _to_slot(wk).reshape(nl, KV_LORA, nh * SLOT).astype(BF16)
    wv = wkv[..., QK_NOPE:].reshape(nl, KV_LORA, nh * V_HD).astype(BF16)
    return (q_norm.reshape(nl, 1, -1).astype(F32), wq, kv_norm.reshape(nl, 1, -1).astype(F32), wk, wv)


def _merge_kernel(y0_ref, y1_ref, y2_ref, y3_ref, gl_ref, x_ref, wb_ref, wo_ref, o_ref):
    d = x_ref.shape[1]
    merged = None
    for n, y_ref in enumerate((y0_ref, y1_ref, y2_ref, y3_ref)):
        term = _mm(y_ref[...], wb_ref[n]) * _sigmoid(gl_ref[:, n * d:(n + 1) * d].astype(F32))
        merged = term if merged is None else merged + term
    o_ref[...] = x_ref[...] + _mm(merged, wo_ref[...])


def _merge(ys, z2, x2, wb, wo, l):
    n, d = x2.shape
    tm = min(512, n)
    ytile = pl.BlockSpec((tm, C), lambda i: (i, 0))
    return pl.pallas_call(
        _merge_kernel,
        grid=(n // tm,),
        in_specs=[ytile, ytile, ytile, ytile,
                  pl.BlockSpec((tm, 4 * d), lambda i: (i, COL_GATE)),
                  pl.BlockSpec((tm, d), lambda i: (i, 0)),
                  _layer_spec(wb, l), _layer_spec(wo, l)],
        out_specs=pl.BlockSpec((tm, d), lambda i: (i, 0)),
        out_shape=jax.ShapeDtypeStruct((n, d), F32),
        compiler_params=_cparams(("parallel",)),
        name="merge",
    )(*ys, z2, x2, wb, wo)


def _mlp_kernel(x_ref, g_ref, w1_ref, w2_ref, o_ref, h_scr, acc):
    j = pl.program_id(1)

    @pl.when(j == 0)
    def _():
        x = x_ref[...]
        ms = jnp.mean(x * x, axis=-1, keepdims=True)
        h_scr[...] = (x * lax.rsqrt(ms + EPS) * g_ref[...]).astype(BF16)
        acc[...] = x

    a = jnp.maximum(jnp.dot(h_scr[...], w1_ref[...], preferred_element_type=F32), 0.0)
    acc[...] += _mm(a * a, w2_ref[...])

    @pl.when(j == pl.num_programs(1) - 1)
    def _():
        o_ref[...] = acc[...]


def _mlp(x2, g, w1, w2, l):
    n, d = x2.shape
    dff = w1.shape[2]
    tm = min(1024, n)
    tf = 2048
    return pl.pallas_call(
        _mlp_kernel,
        grid=(n // tm, dff // tf),
        in_specs=[pl.BlockSpec((tm, d), lambda i, j: (i, 0)),
                  pl.BlockSpec((None, 1, d), lambda i, j: (l, 0, 0)),
                  pl.BlockSpec((None, d, tf), lambda i, j: (l, 0, j)),
                  pl.BlockSpec((None, tf, d), lambda i, j: (l, j, 0))],
        out_specs=pl.BlockSpec((tm, d), lambda i, j: (i, 0)),
        out_shape=jax.ShapeDtypeStruct((n, d), F32),
        scratch_shapes=[pltpu.VMEM((tm, d), BF16), pltpu.VMEM((tm, d), F32)],
        compiler_params=_cparams(("parallel", "arbitrary")),
        name="mlp",
    )(x2, g, w1, w2)


def _prep_w_in(w_in):
    nl, d, _ = w_in.shape
    gate0 = 6 * C + W_LORA + A_LORA + G_LORA + Q_LORA + KV_LORA + QK_ROPE
    kr0 = gate0 - QK_ROPE
    parts = [
        w_in[:, :, gate0:],
        w_in[:, :, :kr0],
        w_in[:, :, kr0:kr0 + ROT_HALF],
        jnp.zeros((nl, d, NOPE_A), w_in.dtype),
        w_in[:, :, kr0 + ROT_HALF:gate0],
        jnp.zeros((nl, d, D_INP - COL_KR - SLOT // 2 - ROT_HALF), w_in.dtype),
    ]
    return jnp.concatenate(parts, axis=-1).astype(BF16)


def _prep_lru_gate(gate_w):
    nl, _, nh, hd, _ = gate_w.shape
    eye = jnp.eye(nh, dtype=gate_w.dtype)
    dense = jnp.einsum("lghij,hk->lghikj", gate_w, eye).reshape(nl, 2, nh * hd, nh * hd)
    return jnp.concatenate([dense[:, 0], dense[:, 1]], axis=-1).astype(BF16)


def kernel(x, norm_mix, w_in, lru_conv_w, lru_conv_b, lru_gate_w, lru_gate_b, lru_lambda, s5_a_re, s5_a_im, s5_b_re, s5_b_im, s5_c_re, s5_c_im, s5_d, s5_log_dt, s5_w_glu, rwkv_mu_rkv, rwkv_mu_w, rwkv_mu_a, rwkv_mu_g, rwkv_w0, rwkv_w2, rwkv_a0, rwkv_a2, rwkv_g2, rwkv_k_k, rwkv_k_a, rwkv_r_k, rwkv_lnx_g, rwkv_lnx_b, mla_q_norm, mla_w_uq, mla_kv_norm, mla_w_ukv, mla_qk_norm_q, mla_qk_norm_k, w_branch, w_out, norm_mlp, w_ff1, w_ff2):
    b, t, d = x.shape
    depth = w_in.shape[0]
    n = b * t
    row = lambda a: a.reshape(depth, 1, -1).astype(F32)

    w_in_p = _prep_w_in(w_in)
    lru_gw = _prep_lru_gate(lru_gate_w)
    s5_tabs = _prep_s5(s5_a_re, s5_a_im, s5_b_re, s5_b_im, s5_c_re, s5_c_im, s5_log_dt, t)
    rw_prm = _prep_rwkv(rwkv_mu_rkv, rwkv_mu_w, rwkv_mu_a, rwkv_mu_g, rwkv_w0, rwkv_w2, rwkv_a0, rwkv_a2, rwkv_g2,
                        rwkv_k_k, rwkv_k_a, rwkv_r_k, rwkv_lnx_g, rwkv_lnx_b)
    mla_prm = _prep_mla(mla_q_norm, mla_w_uq, mla_kv_norm, mla_w_ukv)
    rope = _rope_tables(t, mla_qk_norm_q, mla_qk_norm_k)
    score_bound = _score_bound(mla_qk_norm_q, mla_qk_norm_k)
    g_mix, g_mlp = row(norm_mix), row(norm_mlp)
    conv_b, gate_b, lam, s5_dr = row(lru_conv_b), row(lru_gate_b), row(lru_lambda), row(s5_d)
    w_glu, wb, wo = s5_w_glu.astype(BF16), w_branch.astype(BF16), w_out.astype(BF16)
    w1, w2 = w_ff1.astype(BF16), w_ff2.astype(BF16)

    lru_prm = (lru_conv_w.astype(F32), conv_b, lru_gw, gate_b, lam)
    x2 = x.reshape(n, d)
    for l in range(depth):
        z2, u2 = _inproj(x2, g_mix, w_in_p, l)
        z = z2.reshape(b, t, -1)
        y_lru = _lru_branch(z, lru_prm, l)
        y_s5 = _s5_branch(u2.reshape(b, t, C), s5_tabs, s5_dr, w_glu, l)
        y_rw = _rwkv_branch(z, rw_prm, l)
        y_mla = _mla_branch(z, mla_prm, rope, score_bound[l], l)
        ys = [y.reshape(n, C) for y in (y_lru, y_s5, y_rw, y_mla)]
        x2 = _merge(ys, z2, x2, wb, wo, l)
        x2 = _mlp(x2, g_mlp, w1, w2, l)
    return x2.reshape(b, t, d)
```

```python
import functools
import math

import jax
import jax.numpy as jnp
import numpy as np
from jax import lax
from jax.experimental import pallas as pl
from jax.experimental.pallas import tpu as pltpu

F32 = jnp.float32
BF16 = jnp.bfloat16

EPS = 1e-6
LRU_HEADS = 8
CONV_W = 4
LRU_C = 8.0
S5_GROUP = 16
S5_STATE = 64
RWKV_HD = 64
W_LORA, A_LORA, G_LORA = 64, 64, 128
RWKV_LN_EPS = 64e-5
MLA_HEADS = 8
QK_NOPE, QK_ROPE, V_HD = 64, 32, 64
QK_HD = QK_NOPE + QK_ROPE
Q_LORA, KV_LORA = 256, 128
ROPE_THETA = 10000.0

LANES = 128
SUBLANES = 8
VMEM_LIMIT = 56 * 1024 * 1024

D_MODEL = 1024
C = D_MODEL // 2
COL_GATE = 0
COL_LRU_X = 4 * D_MODEL
COL_LRU_G = COL_LRU_X + C
COL_S5_U = COL_LRU_G + C
COL_RW_R = COL_S5_U + C
COL_RW_K = COL_RW_R + C
COL_RW_V = COL_RW_K + C
COL_LORA = COL_RW_V + C
COL_CQ = COL_LORA + 256
COL_CKV = COL_CQ + Q_LORA
COL_KR = COL_CKV + KV_LORA
D_INP = 8 * D_MODEL


def _cparams(sem):
    return pltpu.CompilerParams(dimension_semantics=sem, vmem_limit_bytes=VMEM_LIMIT)


def _gelu_tanh(x):
    return 0.5 * x * (1.0 + jnp.tanh(math.sqrt(2.0 / math.pi) * (x + 0.044715 * (x * x * x))))


def _sigmoid(x):
    return 1.0 / (1.0 + jnp.exp(-x))


def _softplus(x):
    return jnp.maximum(x, 0.0) + jnp.log(1.0 + jnp.exp(-jnp.abs(x)))


HALO = 16


def _shift_rows(x, halo, k):
    xe = jnp.concatenate([halo, x], axis=0)
    n = x.shape[0]
    return xe[HALO - k:HALO - k + n, :]


def _layer_spec(arr, l):
    zeros = (0,) * (arr.ndim - 1)
    return pl.BlockSpec((None,) + arr.shape[1:], lambda *idx: (l,) + zeros)


def _inproj_kernel(x_ref, g_ref, w_ref, o_ref, u_ref, h_scr):
    j = pl.program_id(1)

    @pl.when(j == 0)
    def _():
        x = x_ref[...]
        ms = jnp.mean(x * x, axis=-1, keepdims=True)
        h_scr[...] = (x * lax.rsqrt(ms + EPS) * g_ref[...]).astype(BF16)

    y = jnp.dot(h_scr[...], w_ref[...], preferred_element_type=F32)
    o_ref[...] = y.astype(BF16)
    tn = w_ref.shape[1]
    u0 = COL_S5_U % tn

    @pl.when(j == COL_S5_U // tn)
    def _():
        u_ref[...] = y[:, u0:u0 + C]


INPROJ_TN = 2048


def _inproj(x2, g, w, l):
    n, d = x2.shape
    dn = w.shape[2]
    tm = min(1024, n)
    tn = INPROJ_TN
    assert COL_S5_U % tn + C <= tn and dn % tn == 0
    return pl.pallas_call(
        _inproj_kernel,
        grid=(n // tm, dn // tn),
        in_specs=[
            pl.BlockSpec((tm, d), lambda i, j: (i, 0)),
            pl.BlockSpec((None, 1, d), lambda i, j: (l, 0, 0)),
            pl.BlockSpec((None, d, tn), lambda i, j: (l, 0, j)),
        ],
        out_specs=[pl.BlockSpec((tm, tn), lambda i, j: (i, j)),
                   pl.BlockSpec((tm, C), lambda i, j: (i, 0))],
        out_shape=[jax.ShapeDtypeStruct((n, dn), BF16), jax.ShapeDtypeStruct((n, C), F32)],
        scratch_shapes=[pltpu.VMEM((tm, d), BF16)],
        compiler_params=_cparams(("parallel", "arbitrary")),
        name="inproj",
    )(x2, g, w)


def _lru_kernel(x_ref, xh_ref, g_ref, cw_ref, cb_ref, gw_ref, gb_ref, lam_ref, o_ref, carry):
    t = pl.program_id(1)
    tm, c = x_ref.shape

    @pl.when(t == 0)
    def _():
        carry[...] = jnp.zeros_like(carry)

    x = x_ref[...].astype(F32)
    halo = jnp.where(t == 0, 0.0, xh_ref[...].astype(F32))
    cw = cw_ref[...]
    xc = cb_ref[...] + cw[CONV_W - 1:CONV_W, :] * x
    for k in range(1, CONV_W):
        xc = xc + cw[CONV_W - 1 - k:CONV_W - k, :] * _shift_rows(x, halo, k)

    gates = jnp.dot(xc.astype(BF16), gw_ref[...], preferred_element_type=F32) + gb_ref[...]
    r = _sigmoid(gates[:, :c])
    i = _sigmoid(gates[:, c:])
    log_a = (-LRU_C) * r * _softplus(-lam_ref[...])
    a = jnp.exp(log_a)
    u = jnp.sqrt(1.0 - a * a) * (i * xc)

    sub = lax.broadcasted_iota(jnp.int32, (tm, c), 0) % SUBLANES
    s = 1
    while s < SUBLANES:
        a_sh = jnp.where(sub >= s, pltpu.roll(a, s, axis=0), 1.0)
        u_sh = jnp.where(sub >= s, pltpu.roll(u, s, axis=0), 0.0)
        u = u + a * u_sh
        a = a * a_sh
        s *= 2
    hp = carry[SUBLANES - 1:SUBLANES, :]
    blocks = []
    for blk in range(tm // SUBLANES):
        rows = slice(blk * SUBLANES, (blk + 1) * SUBLANES)
        hb = u[rows, :] + a[rows, :] * hp
        hp = hb[SUBLANES - 1:SUBLANES, :]
        blocks.append(hb)
    h = jnp.concatenate(blocks, axis=0)
    carry[...] = blocks[-1]
    o_ref[...] = (h * _gelu_tanh(g_ref[...].astype(F32))).astype(o_ref.dtype)


def _lru_branch(z, prm, l):
    b, t, _ = z.shape
    tm = min(256, t)
    cb = COL_LRU_X // C
    hb = tm // HALO
    return pl.pallas_call(
        _lru_kernel,
        grid=(b, t // tm),
        in_specs=[
            pl.BlockSpec((None, tm, C), lambda i, j: (i, j, cb)),
            pl.BlockSpec((None, HALO, C), lambda i, j: (i, jnp.maximum(j * hb - 1, 0), cb)),
            pl.BlockSpec((None, tm, C), lambda i, j: (i, j, cb + 1)),
        ] + [_layer_spec(a, l) for a in prm],
        out_specs=pl.BlockSpec((None, tm, C), lambda i, j: (i, j, 0)),
        out_shape=jax.ShapeDtypeStruct((b, t, C), BF16),
        scratch_shapes=[pltpu.VMEM((SUBLANES, C), F32)],
        compiler_params=_cparams(("parallel", "arbitrary")),
        name="rglru",
    )(z, z, z, *prm)


S5_L = 8
S5_OCT = LANES // S5_GROUP


def _s5_kernel(u_ref, win_ref, bd_ref, wout_ref, lstep_ref, lblk_ref, y_ref, uf, wintra, carry):
    t = pl.program_id(2)
    tm = u_ref.shape[0]
    nc = tm // S5_L
    ns = carry.shape[1]
    half = ns // 2

    @pl.when((pl.program_id(1) == 0) & (t == 0))
    def _():
        wintra[...] = jnp.zeros_like(wintra)
        for i in range(S5_L):
            for j in range(i, S5_L):
                wintra[i * LANES:(i + 1) * LANES, j * LANES:(j + 1) * LANES] = bd_ref[j - i]

    @pl.when(t == 0)
    def _():
        carry[...] = jnp.zeros_like(carry)

    for i in range(S5_L):
        uf[:, i * LANES:(i + 1) * LANES] = u_ref[pl.ds(i, nc, stride=S5_L), :].astype(BF16)
    ufv = uf[...]
    s = jnp.dot(ufv, win_ref[...], preferred_element_type=F32)

    def cmul(v, ar, ai):
        sw = jnp.concatenate([v[:, half:], v[:, :half]], axis=1)
        return v * ar + sw * ai

    row = lax.broadcasted_iota(jnp.int32, (nc, ns), 0)
    sub = row % SUBLANES
    k = 0
    st = 1
    while st < SUBLANES:
        sh = jnp.where(sub >= st, pltpu.roll(s, st, axis=0), 0.0)
        s = s + cmul(sh, lstep_ref[2 * k:2 * k + 1, :], lstep_ref[2 * k + 1:2 * k + 2, :])
        k += 1
        st *= 2
    cin = carry[0:1, :]
    prev = cin
    blocks = []
    for blk in range(nc // SUBLANES):
        sb = s[blk * SUBLANES:(blk + 1) * SUBLANES, :] + cmul(jnp.broadcast_to(prev, (SUBLANES, ns)),
                                                              lblk_ref[0], lblk_ref[1])
        prev = sb[SUBLANES - 1:SUBLANES, :]
        blocks.append(sb)
    s = jnp.concatenate(blocks, axis=0)
    carry[...] = jnp.broadcast_to(prev, carry.shape)
    sprev = jnp.where(row >= 1, pltpu.roll(s, 1, axis=0), cin)

    yf = jnp.dot(ufv, wintra[...], preferred_element_type=F32)
    yf = yf + jnp.dot(sprev.astype(BF16), wout_ref[...], preferred_element_type=F32)
    for j in range(S5_L):
        y_ref[pl.ds(j, nc, stride=S5_L), :] = yf[:, j * LANES:(j + 1) * LANES]


def _s5_tail_kernel(y_ref, u_ref, d_ref, w_ref, o_ref):
    c = u_ref.shape[1]
    y = _gelu_tanh(y_ref[...] + d_ref[...] * u_ref[...])
    z = jnp.dot(y.astype(BF16), w_ref[...], preferred_element_type=F32)
    o_ref[...] = (z[:, :c] * _sigmoid(z[:, c:])).astype(o_ref.dtype)


def _s5_tile(t):
    return min(4096, t)


def _s5_branch(u, tabs, d_skip, w_glu, l):
    win, bd, wout, lstep, lblk = tabs
    b, t, _ = u.shape
    tm = _s5_tile(t)
    nc = tm // S5_L
    noct = C // LANES
    ns = win.shape[-1]
    fl = S5_L * LANES

    def table(arr):
        zeros = (0,) * (arr.ndim - 2)
        return pl.BlockSpec((None, None) + arr.shape[2:], lambda q, i, j: (l, q) + zeros)

    y = pl.pallas_call(
        _s5_kernel,
        grid=(noct, b, t // tm),
        in_specs=[pl.BlockSpec((None, tm, LANES), lambda q, i, j: (i, j, q)),
                  table(win), table(bd), table(wout), table(lstep), table(lblk)],
        out_specs=pl.BlockSpec((None, tm, LANES), lambda q, i, j: (i, j, q)),
        out_shape=jax.ShapeDtypeStruct((b, t, C), F32),
        scratch_shapes=[pltpu.VMEM((nc, fl), BF16), pltpu.VMEM((fl, fl), BF16), pltpu.VMEM((SUBLANES, ns), F32)],
        compiler_params=_cparams(("arbitrary", "arbitrary", "arbitrary")),
        name="s5_ssm",
    )(u, win, bd, wout, lstep, lblk)

    tt = min(2048, t)
    return pl.pallas_call(
        _s5_tail_kernel,
        grid=(b, t // tt),
        in_specs=[
            pl.BlockSpec((None, tt, C), lambda i, j: (i, j, 0)),
            pl.BlockSpec((None, tt, C), lambda i, j: (i, j, 0)),
            _layer_spec(d_skip, l),
            _layer_spec(w_glu, l),
        ],
        out_specs=pl.BlockSpec((None, tt, C), lambda i, j: (i, j, 0)),
        out_shape=jax.ShapeDtypeStruct((b, t, C), BF16),
        compiler_params=_cparams(("parallel", "parallel")),
        name="s5_glu",
    )(y, u, d_skip, w_glu)


def _prep_s5(a_re, a_im, b_re, b_im, c_re, c_im, log_dt, t):
    nl, g, p = a_re.shape
    m = S5_GROUP
    noct = g // S5_OCT
    nc = _s5_tile(t) // S5_L
    lam = lax.complex(a_re.astype(F32), a_im.astype(F32))
    dt = jnp.exp(log_dt.astype(F32))[..., None]
    ldt = lam * dt
    lam_bar = jnp.exp(ldt)
    b_bar = ((lam_bar - 1.0) / lam)[..., None] * lax.complex(b_re.astype(F32), b_im.astype(F32))
    cc = lax.complex(c_re.astype(F32), c_im.astype(F32))

    def powers(e):
        return jnp.exp(ldt[..., None] * e.astype(F32).reshape((1, 1, 1, -1)))

    steps = jnp.arange(S5_L)

    def octet_bd(x, row_inner, col_pre, col_post):
        ncol = col_pre * S5_OCT * col_post
        cidx = np.arange(ncol)
        src = (cidx // (S5_OCT * col_post)) * col_post + cidx % col_post
        expand = jnp.asarray(np.arange(col_pre * col_post)[:, None] == src[None, :], F32)
        ridx = np.arange(x.shape[2])
        keep = jnp.asarray(((ridx // row_inner) % S5_OCT)[:, None] == ((cidx // col_post) % S5_OCT)[None, :])
        return jnp.where(keep, jnp.einsum("lqrc,cn->lqrn", x.astype(BF16), expand.astype(BF16),
                                          preferred_element_type=F32), 0.0).astype(BF16)

    kd = jnp.einsum("lgop,lgpd,lgpi->lgdio", cc, powers(steps), b_bar).real
    kd = kd.reshape(nl, noct, S5_OCT, S5_L, m, m)
    kd = jnp.moveaxis(kd, 3, 2).reshape(nl, noct, S5_L * LANES, m)
    bd = octet_bd(kd, m, 1, m).reshape(nl, noct, S5_L, LANES, LANES)

    wi = powers(S5_L - 1 - steps)[..., None] * b_bar[:, :, :, None, :]
    wi = jnp.stack([wi.real, wi.imag], axis=2)
    wi = wi.reshape(nl, noct, S5_OCT, 2, p, S5_L, m)
    wi = jnp.transpose(wi, (0, 1, 5, 2, 6, 3, 4)).reshape(nl, noct, S5_L * LANES, 2 * p)
    win = octet_bd(wi, m, 2, p)

    wo = cc[:, :, :, :, None] * powers(steps + 1)[:, :, None, :, :]
    wo = jnp.stack([wo.real, -wo.imag], axis=2)
    wo = wo.reshape(nl, noct, S5_OCT, 2, m, p, S5_L)
    wo = jnp.transpose(wo, (0, 1, 3, 2, 5, 6, 4)).reshape(nl, noct, 2 * S5_OCT * p, S5_L * m)
    wout = octet_bd(wo, p, S5_L, m)

    def state_mult(e):
        pw = powers(e)
        pw = jnp.transpose(pw, (0, 3, 1, 2)).reshape(nl, e.shape[0], noct, S5_OCT * p)
        pw = jnp.moveaxis(pw, 2, 1)
        return (jnp.concatenate([pw.real, pw.real], axis=-1), jnp.concatenate([-pw.imag, pw.imag], axis=-1))

    nsteps = int(math.log2(SUBLANES))
    ar, ai = state_mult(S5_L * (2 ** jnp.arange(nsteps)))
    lstep = jnp.stack([ar, ai], axis=3).reshape(nl, noct, 2 * nsteps, -1)
    lblk = jnp.stack(state_mult(S5_L * (jnp.arange(SUBLANES) + 1)), axis=2)
    return win, bd, wout, lstep, lblk


RW_L = 64
HIGHEST = lax.Precision.HIGHEST


def _mm(a, b, dims=(((1,), (0,)), ((), ()))):
    return lax.dot_general(a.astype(BF16), b.astype(BF16), dims, preferred_element_type=F32)


def _mm_nt(a, b):
    return _mm(a, b, (((1,), (1,)), ((), ())))


def _mm_tn(a, b):
    return _mm(a, b, (((0,), (0,)), ((), ())))


def _mm_exact(a, b):
    return lax.dot_general(a, b, (((1,), (0,)), ((), ())), precision=HIGHEST, preferred_element_type=F32)


def _head_sum(x, m0):
    s0 = jnp.sum(jnp.where(m0, x, 0.0), axis=-1, keepdims=True)
    s1 = jnp.sum(jnp.where(m0, 0.0, x), axis=-1, keepdims=True)
    return jnp.where(m0, s0, s1)


def _rwkv_kernel(r_ref, k_ref, v_ref, lo_ref, rh_ref, kh_ref, vh_ref, loh_ref, mu_ref, mul_ref, w0_ref, a0_ref,
                 w2_ref, a2_ref, g2_ref, kk_ref, ka_ref, rk_ref, lng_ref, lnb_ref, o_ref, hstate):
    t = pl.program_id(1)
    tm, c = r_ref.shape
    npair = c // LANES
    ll = RW_L

    @pl.when(t == 0)
    def _():
        hstate[...] = jnp.zeros_like(hstate)

    def mix(ref, href, mu):
        p = ref[...].astype(F32)
        prev = _shift_rows(p, jnp.where(t == 0, 0.0, href[...].astype(F32)), 1)
        return p + (prev - p) * mu

    mu = mu_ref[...]
    r = mix(r_ref, rh_ref, mu[0:1, :])
    k = mix(k_ref, kh_ref, mu[1:2, :])
    v = mix(v_ref, vh_ref, mu[2:3, :])
    lo = mix(lo_ref, loh_ref, mul_ref[...])
    wlog = -_softplus(-(w0_ref[...] + _mm(jnp.tanh(lo), w2_ref[...]))) - 0.5
    lw = -jnp.exp(wlog)
    a = _sigmoid(a0_ref[...] + _mm(lo, a2_ref[...]))
    g = _mm(_sigmoid(lo), g2_ref[...])
    kk = k * kk_ref[...]
    k = k * (1.0 + (a - 1.0) * ka_ref[...])

    lane = lax.broadcasted_iota(jnp.int32, (ll, LANES), 1)
    m0 = lane < RWKV_HD
    ri = lax.broadcasted_iota(jnp.int32, (2 * ll, 2 * ll), 0)
    ci = lax.broadcasted_iota(jnp.int32, (2 * ll, 2 * ll), 1)
    same = (ri >= ll) == (ci >= ll)
    strict = same & (ri > ci)
    incl = same & (ri >= ci)
    eye = ri == ci
    tri = (lax.broadcasted_iota(jnp.int32, (ll, ll), 0) >= lax.broadcasted_iota(jnp.int32, (ll, ll), 1)).astype(F32)

    def st(x):
        return jnp.concatenate([jnp.where(m0, x, 0.0), jnp.where(m0, 0.0, x)], axis=0)

    nch = tm // ll
    pcs = [(ch, p) for ch in range(nch) for p in range(npair)]
    cum_all = [_mm_exact(tri, lw[ch * ll:(ch + 1) * ll, :]) for ch in range(nch)]
    pre = {}
    for (ch, p) in pcs:
        rows, cols = slice(ch * ll, (ch + 1) * ll), slice(p * LANES, (p + 1) * LANES)
        rp, kp, vp, ap, lwp, cum = r[rows, cols], k[rows, cols], v[rows, cols], a[rows, cols], lw[rows, cols], cum_all[ch][:, cols]
        kkp = kk[rows, cols]
        kkp = kkp * lax.rsqrt(_head_sum(kkp * kkp, m0) + 1e-12)
        cum_l = cum[ll - 1:ll, :]
        e_neg = jnp.exp(-cum)
        e_last = jnp.exp(cum_l - cum)
        kb = kkp * ap
        stb = lambda x: st(x.astype(BF16))
        pre[ch, p] = dict(
            rows=rows, cols=cols, rp=rp, kp=kp, vp=vp,
            a_st=stb(-kkp * jnp.exp(cum - lwp)), r_st=st(rp * jnp.exp(cum)),
            v_st=stb(vp), b_st=stb(kb * e_neg), k_st=stb(kp * e_neg),
            bh_st=stb(kb * e_last), kh_st=stb(kp * e_last), g_l=jnp.exp(cum_l))

    gram = {pc: _mm_nt(jnp.concatenate([pre[pc]["a_st"], pre[pc]["r_st"].astype(BF16)], axis=0),
                       jnp.concatenate([pre[pc]["b_st"], pre[pc]["k_st"]], axis=0)) for pc in pcs}
    a_ab = {pc: jnp.where(strict, gram[pc][:2 * ll, :2 * ll], 0.0) for pc in pcs}
    a_ak = {pc: jnp.where(strict, gram[pc][:2 * ll, 2 * ll:], 0.0).astype(BF16) for pc in pcs}
    m_rb = {pc: jnp.where(incl, gram[pc][2 * ll:, :2 * ll], 0.0).astype(BF16) for pc in pcs}
    m_rk = {pc: jnp.where(incl, gram[pc][2 * ll:, 2 * ll:], 0.0).astype(BF16) for pc in pcs}
    akv = {pc: _mm(a_ak[pc], pre[pc]["v_st"]) for pc in pcs}
    tinv = {pc: jnp.where(eye, 1.0, a_ab[pc]) for pc in pcs}
    pw = {pc: a_ab[pc].astype(BF16) for pc in pcs}
    for _ in range(int(math.log2(ll)) - 1):
        pw = {pc: _mm(pw[pc], pw[pc]).astype(BF16) for pc in pcs}
        tinv = {pc: tinv[pc] + _mm(tinv[pc], pw[pc]) for pc in pcs}
    xa = {pc: _mm(tinv[pc], jnp.concatenate([pre[pc]["a_st"], akv[pc].astype(BF16)], axis=1)).astype(BF16)
          for pc in pcs}
    ma = {pc: _mm(m_rb[pc], xa[pc][:, :LANES]) for pc in pcs}
    y_loc = {pc: _mm(jnp.concatenate([m_rb[pc], m_rk[pc]], axis=1),
                     jnp.concatenate([xa[pc][:, LANES:], pre[pc]["v_st"]], axis=0)) for pc in pcs}
    ba = {pc: _mm_tn(pre[pc]["bh_st"], xa[pc][:, :LANES]) for pc in pcs}
    f_loc = {pc: _mm_tn(jnp.concatenate([pre[pc]["bh_st"], pre[pc]["kh_st"]], axis=0),
                        jnp.concatenate([xa[pc][:, LANES:], pre[pc]["v_st"]], axis=0)) for pc in pcs}
    for (ch, p) in pcs:
        pc = (ch, p)
        q = pre[pc]
        r_eff = q["r_st"] + ma[pc]
        g_eff = jnp.where(eye, q["g_l"], 0.0) + ba[pc]
        hs = _mm(jnp.concatenate([g_eff, r_eff], axis=0), hstate[p])
        hstate[p] = hs[:2 * ll, :] + f_loc[pc]
        y_st = hs[2 * ll:, :] + y_loc[pc]
        y = y_st[:ll, :] + y_st[ll:, :]
        rows, cols = q["rows"], q["cols"]
        mean = _head_sum(y, m0) * (1.0 / RWKV_HD)
        d = y - mean
        var = _head_sum(d * d, m0) * (1.0 / RWKV_HD)
        yn = d * lax.rsqrt(var + RWKV_LN_EPS) * lng_ref[:, cols] + lnb_ref[:, cols]
        bonus = _head_sum(q["rp"] * q["kp"] * rk_ref[:, cols], m0) * q["vp"]
        o_ref[rows, cols] = ((yn + bonus) * g[rows, cols]).astype(o_ref.dtype)


def _rwkv_branch(z, prm, l):
    b, t, _ = z.shape
    tm = min(8 * RW_L, t)
    hb = tm // HALO
    cr = COL_RW_R // C
    cl = COL_LORA // 256
    npair = C // LANES

    def tile(cb, w):
        return pl.BlockSpec((None, tm, w), lambda i, j: (i, j, cb))

    def halo(cb, w):
        return pl.BlockSpec((None, HALO, w), lambda i, j: (i, jnp.maximum(j * hb - 1, 0), cb))

    return pl.pallas_call(
        _rwkv_kernel,
        grid=(b, t // tm),
        in_specs=[tile(cr, C), tile(cr + 1, C), tile(cr + 2, C), tile(cl, 256),
                  halo(cr, C), halo(cr + 1, C), halo(cr + 2, C), halo(cl, 256)]
                 + [_layer_spec(a, l) for a in prm],
        out_specs=pl.BlockSpec((None, tm, C), lambda i, j: (i, j, 0)),
        out_shape=jax.ShapeDtypeStruct((b, t, C), BF16),
        scratch_shapes=[pltpu.VMEM((npair, LANES, LANES), F32)],
        compiler_params=_cparams(("parallel", "arbitrary")),
        name="rwkv7",
    )(z, z, z, z, z, z, z, z, *prm)


def _prep_rwkv(mu_rkv, mu_w, mu_a, mu_g, w0, w2, a0, a2, g2, k_k, k_a, r_k, lnx_g, lnx_b):
    nl = mu_rkv.shape[0]
    row = lambda x: x.reshape(nl, 1, -1).astype(F32)
    mul = jnp.concatenate([mu_w, mu_a, mu_g], axis=-1)
    zw = lambda n: jnp.zeros((nl, n, C), F32)
    w2p = jnp.concatenate([w2, zw(A_LORA + G_LORA)], axis=1).astype(BF16)
    a2p = jnp.concatenate([zw(W_LORA), a2, zw(G_LORA)], axis=1).astype(BF16)
    g2p = jnp.concatenate([zw(W_LORA + A_LORA), g2], axis=1).astype(BF16)
    return (mu_rkv.astype(F32), row(mul), row(w0), row(a0), w2p, a2p, g2p, row(k_k), row(k_a), row(r_k),
            row(lnx_g), row(lnx_b))


SLOT = LANES
ROT_HALF = QK_ROPE // 2
NOPE_A = SLOT // 2 - ROT_HALF


def _slot_source():
    src = np.full((SLOT,), QK_HD, np.int32)
    src[:ROT_HALF] = QK_NOPE + np.arange(ROT_HALF)
    src[ROT_HALF:SLOT // 2] = np.arange(NOPE_A)
    src[SLOT // 2:SLOT // 2 + ROT_HALF] = QK_NOPE + ROT_HALF + np.arange(ROT_HALF)
    src[SLOT // 2 + ROT_HALF:SLOT // 2 + ROT_HALF + QK_NOPE - NOPE_A] = NOPE_A + np.arange(QK_NOPE - NOPE_A)
    return src


def _to_slot(v):
    pad = jnp.concatenate([v, jnp.zeros(v.shape[:-1] + (1,), v.dtype)], axis=-1)
    return jnp.take(pad, jnp.asarray(_slot_source()), axis=-1)
VT_ROWS = V_HD + 16
FLASH_NB = 4


FAST_BOUND = 40.0


def _mla_prep_kernel(cq_ref, ckv_ref, kr_ref, tab_ref, qn_ref, wq_ref, kvn_ref, wk_ref, wv_ref,
                     q_ref, k_ref, v_ref, v_scr):
    def rms(x, g, n):
        ms = jnp.sum(x * x, axis=-1, keepdims=True) * (1.0 / n)
        return x * lax.rsqrt(ms + EPS) * g

    def norm_rope(x, c_tab, s_tab):
        ms = jnp.sum(x * x, axis=-1, keepdims=True) * (1.0 / QK_HD)
        return (x * c_tab + pltpu.roll(x, SLOT // 2, axis=1) * s_tab) * lax.rsqrt(ms + EPS)

    cq = rms(cq_ref[...].astype(F32), qn_ref[...], Q_LORA)
    ckv = rms(ckv_ref[...].astype(F32), kvn_ref[...], KV_LORA)
    qf = _mm(cq, wq_ref[...])
    kf = _mm(ckv, wk_ref[...])
    v_scr[...] = _mm(ckv, wv_ref[...])
    vt = jnp.transpose(v_scr[...]).astype(BF16)
    ones = jnp.ones((VT_ROWS - V_HD, vt.shape[1]), BF16)
    for h in range(MLA_HEADS):
        v_ref[h * VT_ROWS:h * VT_ROWS + V_HD, :] = vt[h * V_HD:(h + 1) * V_HD, :]
        v_ref[h * VT_ROWS + V_HD:(h + 1) * VT_ROWS, :] = ones
    kr = kr_ref[...].astype(F32)
    cq_tab, sq_tab, ck_tab, sk_tab = tab_ref[0], tab_ref[1], tab_ref[2], tab_ref[3]
    for h in range(MLA_HEADS):
        cols = slice(h * SLOT, (h + 1) * SLOT)
        q_ref[:, cols] = norm_rope(qf[:, cols], cq_tab, sq_tab).astype(BF16)
        k_ref[:, cols] = norm_rope(kf[:, cols] + kr, ck_tab, sk_tab).astype(BF16)


def _flash_kernel(qi_ref, kj_ref, q_ref, k_ref, vt_ref, o_ref, acc, m_scr=None):
    online = m_scr is not None
    i = qi_ref[pl.program_id(1)]
    j = kj_ref[pl.program_id(1)]
    nb, bq, _ = q_ref.shape
    bk = k_ref.shape[1]

    @pl.when(j == 0)
    def _():
        if online:
            m_scr[...] = jnp.full_like(m_scr, -jnp.inf)
        acc[...] = jnp.zeros_like(acc)

    def step(bb, masked):
        if masked:
            causal = (lax.broadcasted_iota(jnp.int32, (bk, bq), 0) <= lax.broadcasted_iota(jnp.int32, (bk, bq), 1))

        def scores(h):
            cols = slice(h * SLOT, (h + 1) * SLOT)
            return _mm_nt(k_ref[bb, :, cols], q_ref[bb, :, cols])

        st_next = scores(0)
        for h in range(MLA_HEADS):
            rows = slice(h * VT_ROWS, (h + 1) * VT_ROWS)
            st = st_next
            if h + 1 < MLA_HEADS:
                st_next = scores(h + 1)
            if masked:
                st = jnp.where(causal, st, -jnp.inf)
            if online:
                m_prev = m_scr[bb, h:h + 1, :]
                m_new = jnp.maximum(m_prev, jnp.max(st, axis=0, keepdims=True))
                alpha = jnp.exp2(m_prev - m_new)
                pt = jnp.exp2(st - m_new)
                m_scr[bb, h:h + 1, :] = m_new
                acc[bb, rows, :] = acc[bb, rows, :] * alpha + _mm(vt_ref[bb, rows, :], pt)
            else:
                acc[bb, rows, :] += _mm(vt_ref[bb, rows, :], jnp.exp2(st))

    def finish(bb):
        outs = []
        for h in range(MLA_HEADS):
            r0 = h * VT_ROWS
            outs.append(acc[bb, r0:r0 + V_HD, :] / acc[bb, r0 + V_HD:r0 + V_HD + 1, :])
        o_ref[bb] = jnp.transpose(jnp.concatenate(outs, axis=0)).astype(o_ref.dtype)

    @pl.when(j < i)
    def _():
        @pl.loop(0, nb)
        def _(bb):
            step(bb, False)

    @pl.when(j == i)
    def _():
        @pl.loop(0, nb)
        def _(bb):
            step(bb, True)
            finish(bb)


def _score_bound(qk_norm_q, qk_norm_k):
    scale = QK_HD ** -0.5 * math.log2(math.e)
    gq = jnp.max(jnp.abs(qk_norm_q.astype(F32)), axis=-1)
    gk = jnp.max(jnp.abs(qk_norm_k.astype(F32)), axis=-1)
    return 1.01 * QK_HD * scale * gq * gk


def _mla_branch(z, prm, rope, score_bound, l):
    b, t, _ = z.shape
    tm = min(1024, t)
    hs = MLA_HEADS * SLOT
    vr = MLA_HEADS * VT_ROWS

    q, k, vt = pl.pallas_call(
        _mla_prep_kernel,
        grid=(b, t // tm),
        in_specs=[pl.BlockSpec((None, tm, Q_LORA), lambda i, j: (i, j, COL_CQ // Q_LORA)),
                  pl.BlockSpec((None, tm, KV_LORA), lambda i, j: (i, j, COL_CKV // KV_LORA)),
                  pl.BlockSpec((None, tm, LANES), lambda i, j: (i, j, COL_KR // LANES)),
                  pl.BlockSpec((None, 4, tm, SLOT), lambda i, j: (l, 0, j, 0))]
                 + [_layer_spec(a, l) for a in prm],
        out_specs=[pl.BlockSpec((None, tm, hs), lambda i, j: (i, j, 0)),
                   pl.BlockSpec((None, tm, hs), lambda i, j: (i, j, 0)),
                   pl.BlockSpec((None, vr, tm), lambda i, j: (i, 0, j))],
        out_shape=[jax.ShapeDtypeStruct((b, t, hs), BF16), jax.ShapeDtypeStruct((b, t, hs), BF16),
                   jax.ShapeDtypeStruct((b, vr, t), BF16)],
        scratch_shapes=[pltpu.VMEM((tm, C), F32)],
        compiler_params=_cparams(("parallel", "parallel")),
        name="mla_prep",
    )(z, z, z, rope, *prm)

    bq = min(512, t)
    nq = t // bq
    tri = [(i, j) for i in range(nq) for j in range(i + 1)]
    qi = jnp.asarray([p[0] for p in tri], jnp.int32)
    kj = jnp.asarray([p[1] for p in tri], jnp.int32)

    nb = math.gcd(b, FLASH_NB)

    def attend(online):
        scratch = [pltpu.VMEM((nb, vr, bq), F32)] + ([pltpu.VMEM((nb, MLA_HEADS, bq), F32)] if online else [])
        return pl.pallas_call(
            _flash_kernel,
            grid_spec=pltpu.PrefetchScalarGridSpec(
                num_scalar_prefetch=2,
                grid=(b // nb, len(tri)),
                in_specs=[pl.BlockSpec((nb, bq, hs), lambda bi, s, qi, kj: (bi, qi[s], 0)),
                          pl.BlockSpec((nb, bq, hs), lambda bi, s, qi, kj: (bi, kj[s], 0)),
                          pl.BlockSpec((nb, vr, bq), lambda bi, s, qi, kj: (bi, 0, kj[s]))],
                out_specs=pl.BlockSpec((nb, bq, C), lambda bi, s, qi, kj: (bi, qi[s], 0)),
                scratch_shapes=scratch),
            out_shape=jax.ShapeDtypeStruct((b, t, C), BF16),
            compiler_params=_cparams(("parallel", "arbitrary")),
            name="mla_attn_online" if online else "mla_attn",
        )(qi, kj, q, k, vt)

    return lax.cond(score_bound <= FAST_BOUND, lambda: attend(False), lambda: attend(True))


def _rope_tables(t, qk_norm_q, qk_norm_k):
    pos = jnp.arange(t, dtype=F32)
    inv_freq = ROPE_THETA ** (-jnp.arange(0, QK_ROPE, 2, dtype=F32) / QK_ROPE)
    ang = pos[:, None] * inv_freq[None, :]
    cos, sin = jnp.cos(ang), jnp.sin(ang)
    z = lambda n: jnp.zeros((t, n), F32)
    one = lambda n: jnp.ones((t, n), F32)
    tail = SLOT // 2 - ROT_HALF
    c_pos = jnp.concatenate([cos, one(NOPE_A), cos, one(QK_NOPE - NOPE_A), z(SLOT - QK_HD)], axis=1)
    s_pos = jnp.concatenate([-sin, z(NOPE_A), sin, z(tail)], axis=1)
    scale = QK_HD ** -0.5 * math.log2(math.e)

    def pair(g, mult):
        g = _to_slot(g.astype(F32)) * mult
        return [g[:, None, :] * c_pos[None], jnp.roll(g, SLOT // 2, axis=-1)[:, None, :] * s_pos[None]]

    return jnp.stack(pair(qk_norm_q, scale) + pair(qk_norm_k, 1.0), axis=1)


def _prep_mla(q_norm, w_uq, kv_norm, w_ukv):
    nl = q_norm.shape[0]
    nh = MLA_HEADS
    wq = _to_slot(w_uq.reshape(nl, Q_LORA, nh, QK_HD)).reshape(nl, Q_LORA, nh * SLOT).astype(BF16)
    wkv = w_ukv.reshape(nl, KV_LORA, nh, QK_NOPE + V_HD)
    wk = jnp.pad(wkv[..., :QK_NOPE], ((0, 0), (0, 0), (0, 0), (0, QK_ROPE)))
    wk = _to_slot(wk).reshape(nl, KV_LORA, nh * SLOT).astype(BF16)
    wv = wkv[..., QK_NOPE:].reshape(nl, KV_LORA, nh * V_HD).astype(BF16)
    return (q_norm.reshape(nl, 1, -1).astype(F32), wq, kv_norm.reshape(nl, 1, -1).astype(F32), wk, wv)


def _merge_kernel(y0_ref, y1_ref, y2_ref, y3_ref, gl_ref, x_ref, wb_ref, wo_ref, o_ref):
    d = x_ref.shape[1]
    merged = None
    for n, y_ref in enumerate((y0_ref, y1_ref, y2_ref, y3_ref)):
        term = _mm(y_ref[...], wb_ref[n]) * _sigmoid(gl_ref[:, n * d:(n + 1) * d].astype(F32))
        merged = term if merged is None else merged + term
    o_ref[...] = x_ref[...] + _mm(merged, wo_ref[...])


def _merge(ys, z2, x2, wb, wo, l):
    n, d = x2.shape
    tm = min(512, n)
    ytile = pl.BlockSpec((tm, C), lambda i: (i, 0))
    return pl.pallas_call(
        _merge_kernel,
        grid=(n // tm,),
        in_specs=[ytile, ytile, ytile, ytile,
                  pl.BlockSpec((tm, 4 * d), lambda i: (i, COL_GATE)),
                  pl.BlockSpec((tm, d), lambda i: (i, 0)),
                  _layer_spec(wb, l), _layer_spec(wo, l)],
        out_specs=pl.BlockSpec((tm, d), lambda i: (i, 0)),
        out_shape=jax.ShapeDtypeStruct((n, d), F32),
        compiler_params=_cparams(("parallel",)),
        name="merge",
    )(*ys, z2, x2, wb, wo)


def _mlp_kernel(x_ref, g_ref, w1_ref, w2_ref, o_ref, h_scr, acc):
    j = pl.program_id(1)

    @pl.when(j == 0)
    def _():
        x = x_ref[...]
        ms = jnp.mean(x * x, axis=-1, keepdims=True)
        h_scr[...] = (x * lax.rsqrt(ms + EPS) * g_ref[...]).astype(BF16)
        acc[...] = x

    a = jnp.maximum(jnp.dot(h_scr[...], w1_ref[...], preferred_element_type=F32), 0.0)
    acc[...] += _mm(a * a, w2_ref[...])

    @pl.when(j == pl.num_programs(1) - 1)
    def _():
        o_ref[...] = acc[...]


def _mlp(x2, g, w1, w2, l):
    n, d = x2.shape
    dff = w1.shape[2]
    tm = min(1024, n)
    tf = 2048
    return pl.pallas_call(
        _mlp_kernel,
        grid=(n // tm, dff // tf),
        in_specs=[pl.BlockSpec((tm, d), lambda i, j: (i, 0)),
                  pl.BlockSpec((None, 1, d), lambda i, j: (l, 0, 0)),
                  pl.BlockSpec((None, d, tf), lambda i, j: (l, 0, j)),
                  pl.BlockSpec((None, tf, d), lambda i, j: (l, j, 0))],
        out_specs=pl.BlockSpec((tm, d), lambda i, j: (i, 0)),
        out_shape=jax.ShapeDtypeStruct((n, d), F32),
        scratch_shapes=[pltpu.VMEM((tm, d), BF16), pltpu.VMEM((tm, d), F32)],
        compiler_params=_cparams(("parallel", "arbitrary")),
        name="mlp",
    )(x2, g, w1, w2)


def _prep_w_in(w_in):
    nl, d, _ = w_in.shape
    gate0 = 6 * C + W_LORA + A_LORA + G_LORA + Q_LORA + KV_LORA + QK_ROPE
    kr0 = gate0 - QK_ROPE
    parts = [
        w_in[:, :, gate0:],
        w_in[:, :, :kr0],
        w_in[:, :, kr0:kr0 + ROT_HALF],
        jnp.zeros((nl, d, NOPE_A), w_in.dtype),
        w_in[:, :, kr0 + ROT_HALF:gate0],
        jnp.zeros((nl, d, D_INP - COL_KR - SLOT // 2 - ROT_HALF), w_in.dtype),
    ]
    return jnp.concatenate(parts, axis=-1).astype(BF16)


def _prep_lru_gate(gate_w):
    nl, _, nh, hd, _ = gate_w.shape
    eye = jnp.eye(nh, dtype=gate_w.dtype)
    dense = jnp.einsum("lghij,hk->lghikj", gate_w, eye).reshape(nl, 2, nh * hd, nh * hd)
    return jnp.concatenate([dense[:, 0], dense[:, 1]], axis=-1).astype(BF16)


def kernel(x, norm_mix, w_in, lru_conv_w, lru_conv_b, lru_gate_w, lru_gate_b, lru_lambda, s5_a_re, s5_a_im, s5_b_re, s5_b_im, s5_c_re, s5_c_im, s5_d, s5_log_dt, s5_w_glu, rwkv_mu_rkv, rwkv_mu_w, rwkv_mu_a, rwkv_mu_g, rwkv_w0, rwkv_w2, rwkv_a0, rwkv_a2, rwkv_g2, rwkv_k_k, rwkv_k_a, rwkv_r_k, rwkv_lnx_g, rwkv_lnx_b, mla_q_norm, mla_w_uq, mla_kv_norm, mla_w_ukv, mla_qk_norm_q, mla_qk_norm_k, w_branch, w_out, norm_mlp, w_ff1, w_ff2):
    b, t, d = x.shape
    depth = w_in.shape[0]
    n = b * t
    row = lambda a: a.reshape(depth, 1, -1).astype(F32)

    w_in_p = _prep_w_in(w_in)
    lru_gw = _prep_lru_gate(lru_gate_w)
    s5_tabs = _prep_s5(s5_a_re, s5_a_im, s5_b_re, s5_b_im, s5_c_re, s5_c_im, s5_log_dt, t)
    rw_prm = _prep_rwkv(rwkv_mu_rkv, rwkv_mu_w, rwkv_mu_a, rwkv_mu_g, rwkv_w0, rwkv_w2, rwkv_a0, rwkv_a2, rwkv_g2,
                        rwkv_k_k, rwkv_k_a, rwkv_r_k, rwkv_lnx_g, rwkv_lnx_b)
    mla_prm = _prep_mla(mla_q_norm, mla_w_uq, mla_kv_norm, mla_w_ukv)
    rope = _rope_tables(t, mla_qk_norm_q, mla_qk_norm_k)
    score_bound = _score_bound(mla_qk_norm_q, mla_qk_norm_k)
    g_mix, g_mlp = row(norm_mix), row(norm_mlp)
    conv_b, gate_b, lam, s5_dr = row(lru_conv_b), row(lru_gate_b), row(lru_lambda), row(s5_d)
    w_glu, wb, wo = s5_w_glu.astype(BF16), w_branch.astype(BF16), w_out.astype(BF16)
    w1, w2 = w_ff1.astype(BF16), w_ff2.astype(BF16)

    lru_prm = (lru_conv_w.astype(F32), conv_b, lru_gw, gate_b, lam)
    x2 = x.reshape(n, d)
    for l in range(depth):
        z2, u2 = _inproj(x2, g_mix, w_in_p, l)
        z = z2.reshape(b, t, -1)
        y_lru = _lru_branch(z, lru_prm, l)
        y_s5 = _s5_branch(u2.reshape(b, t, C), s5_tabs, s5_dr, w_glu, l)
        y_rw = _rwkv_branch(z, rw_prm, l)
        y_mla = _mla_branch(z, mla_prm, rope, score_bound[l], l)
        ys = [y.reshape(n, C) for y in (y_lru, y_s5, y_rw, y_mla)]
        x2 = _merge(ys, z2, x2, wb, wo, l)
        x2 = _mlp(x2, g_mlp, w1, w2, l)
    return x2.reshape(b, t, d)
```

```python
import math

import jax
import jax.numpy as jnp
import numpy as np
from jax import lax
from jax.experimental import pallas as pl
from jax.experimental.pallas import tpu as pltpu

F32 = jnp.float32
BF16 = jnp.bfloat16

EPS = 1e-6
CONV_W = 4
LRU_C = 8.0
S5_GROUP = 16
RWKV_HD = 64
W_LORA, A_LORA, G_LORA = 64, 64, 128
RWKV_LN_EPS = 64e-5
MLA_HEADS = 8
QK_NOPE, QK_ROPE, V_HD = 64, 32, 64
QK_HD = QK_NOPE + QK_ROPE
Q_LORA, KV_LORA = 256, 128
ROPE_THETA = 10000.0

LANES = 128
SUBLANES = 8
VMEM_LIMIT = 56 * 1024 * 1024

D_MODEL = 1024
C = D_MODEL // 2
COL_GATE = 0
COL_LRU_X = 4 * D_MODEL
COL_LRU_G = COL_LRU_X + C
COL_S5_U = COL_LRU_G + C
COL_RW_R = COL_S5_U + C
COL_RW_K = COL_RW_R + C
COL_RW_V = COL_RW_K + C
COL_LORA = COL_RW_V + C
COL_CQ = COL_LORA + 256
COL_CKV = COL_CQ + Q_LORA
COL_KR = COL_CKV + KV_LORA
D_INP = 8 * D_MODEL


def _cparams(sem):
    return pltpu.CompilerParams(dimension_semantics=sem, vmem_limit_bytes=VMEM_LIMIT)


def _gelu_tanh(x):
    return 0.5 * x * (1.0 + jnp.tanh(math.sqrt(2.0 / math.pi) * (x + 0.044715 * (x * x * x))))


def _sigmoid(x):
    return 1.0 / (1.0 + jnp.exp(-x))


def _softplus(x):
    return jnp.maximum(x, 0.0) + jnp.log(1.0 + jnp.exp(-jnp.abs(x)))


HALO = 16


def _shift_rows(x, halo, k):
    xe = jnp.concatenate([halo, x], axis=0)
    n = x.shape[0]
    return xe[HALO - k:HALO - k + n, :]


def _layer_spec(arr, l):
    zeros = (0,) * (arr.ndim - 1)
    return pl.BlockSpec((None,) + arr.shape[1:], lambda *idx: (l,) + zeros)


def _inproj_kernel(x_ref, g_ref, wg_ref, wr_ref, o_ref, u_ref, h_scr):
    j = pl.program_id(1)
    tn = wg_ref.shape[1]
    ng = COL_LRU_X // tn
    u_step, u0 = divmod(COL_S5_U, tn)

    @pl.when(j == 0)
    def _():
        x = x_ref[...]
        ms = jnp.mean(x * x, axis=-1, keepdims=True)
        h_scr[...] = (x * lax.rsqrt(ms + EPS) * g_ref[...]).astype(BF16)

    @pl.when(j < ng)
    def _():
        o_ref[...] = jnp.dot(h_scr[...], wg_ref[...], preferred_element_type=F32).astype(BF16)

    @pl.when(j >= ng)
    def _():
        y = jnp.dot(h_scr[...], wr_ref[...], preferred_element_type=F32)
        o_ref[...] = y.astype(BF16)

        @pl.when(j == u_step)
        def _():
            u_ref[...] = y[:, u0:u0 + C]


INPROJ_TN = 2048


def _inproj(x2, g, w_gate, w_rest, l):
    n, d = x2.shape
    tm = min(1024, n)
    tn = INPROJ_TN
    ng = COL_LRU_X // tn
    nr = w_rest.shape[2] // tn
    assert w_gate.shape[2] == ng * tn and COL_S5_U % tn + C <= tn
    return pl.pallas_call(
        _inproj_kernel,
        grid=(n // tm, ng + nr),
        in_specs=[
            pl.BlockSpec((tm, d), lambda i, j: (i, 0)),
            pl.BlockSpec((None, 1, d), lambda i, j: (l, 0, 0)),
            pl.BlockSpec((None, d, tn), lambda i, j: (l, 0, jnp.minimum(j, ng - 1))),
            pl.BlockSpec((None, d, tn), lambda i, j: (l, 0, jnp.maximum(j - ng, 0))),
        ],
        out_specs=[pl.BlockSpec((tm, tn), lambda i, j: (i, j)),
                   pl.BlockSpec((tm, C), lambda i, j: (i, 0))],
        out_shape=[jax.ShapeDtypeStruct((n, D_INP), BF16), jax.ShapeDtypeStruct((n, C), F32)],
        scratch_shapes=[pltpu.VMEM((tm, d), BF16)],
        compiler_params=_cparams(("parallel", "arbitrary")),
        name="inproj",
    )(x2, g, w_gate, w_rest)


def _lru_kernel(x_ref, xh_ref, g_ref, cw_ref, cb_ref, gw_ref, gb_ref, lam_ref, o_ref, carry):
    t = pl.program_id(1)
    tm, c = x_ref.shape

    @pl.when(t == 0)
    def _():
        carry[...] = jnp.zeros_like(carry)

    x = x_ref[...].astype(F32)
    halo = jnp.where(t == 0, 0.0, xh_ref[...].astype(F32))
    cw = cw_ref[...]
    xc = cb_ref[...] + cw[CONV_W - 1:CONV_W, :] * x
    for k in range(1, CONV_W):
        xc = xc + cw[CONV_W - 1 - k:CONV_W - k, :] * _shift_rows(x, halo, k)

    gates = jnp.dot(xc.astype(BF16), gw_ref[...], preferred_element_type=F32) + gb_ref[...]
    r = _sigmoid(gates[:, :c])
    i = _sigmoid(gates[:, c:])
    log_a = (-LRU_C) * r * _softplus(-lam_ref[...])
    a = jnp.exp(log_a)
    u = jnp.sqrt(1.0 - a * a) * (i * xc)

    sub = lax.broadcasted_iota(jnp.int32, (tm, c), 0) % SUBLANES
    s = 1
    while s < SUBLANES:
        a_sh = jnp.where(sub >= s, pltpu.roll(a, s, axis=0), 1.0)
        u_sh = jnp.where(sub >= s, pltpu.roll(u, s, axis=0), 0.0)
        u = u + a * u_sh
        a = a * a_sh
        s *= 2
    hp = carry[SUBLANES - 1:SUBLANES, :]
    blocks = []
    for blk in range(tm // SUBLANES):
        rows = slice(blk * SUBLANES, (blk + 1) * SUBLANES)
        hb = u[rows, :] + a[rows, :] * hp
        hp = hb[SUBLANES - 1:SUBLANES, :]
        blocks.append(hb)
    h = jnp.concatenate(blocks, axis=0)
    carry[...] = blocks[-1]
    o_ref[...] = (h * _gelu_tanh(g_ref[...].astype(F32))).astype(o_ref.dtype)


def _lru_branch(z, prm, l):
    b, t, _ = z.shape
    tm = min(256, t)
    cb = COL_LRU_X // C
    hb = tm // HALO
    return pl.pallas_call(
        _lru_kernel,
        grid=(b, t // tm),
        in_specs=[
            pl.BlockSpec((None, tm, C), lambda i, j: (i, j, cb)),
            pl.BlockSpec((None, HALO, C), lambda i, j: (i, jnp.maximum(j * hb - 1, 0), cb)),
            pl.BlockSpec((None, tm, C), lambda i, j: (i, j, cb + 1)),
        ] + [_layer_spec(a, l) for a in prm],
        out_specs=pl.BlockSpec((None, tm, C), lambda i, j: (i, j, 0)),
        out_shape=jax.ShapeDtypeStruct((b, t, C), BF16),
        scratch_shapes=[pltpu.VMEM((SUBLANES, C), F32)],
        compiler_params=_cparams(("parallel", "arbitrary")),
        name="rglru",
    )(z, z, z, *prm)


S5_L = 8
S5_OCT = LANES // S5_GROUP


def _s5_kernel(u_ref, win_ref, bd_ref, wout_ref, lstep_ref, lblk_ref, y_ref, uf, wintra, carry):
    t = pl.program_id(2)
    tm = u_ref.shape[0]
    nc = tm // S5_L
    ns = carry.shape[1]
    half = ns // 2

    @pl.when((pl.program_id(1) == 0) & (t == 0))
    def _():
        wintra[...] = jnp.zeros_like(wintra)
        for i in range(S5_L):
            for j in range(i, S5_L):
                wintra[i * LANES:(i + 1) * LANES, j * LANES:(j + 1) * LANES] = bd_ref[j - i]

    @pl.when(t == 0)
    def _():
        carry[...] = jnp.zeros_like(carry)

    for i in range(S5_L):
        uf[:, i * LANES:(i + 1) * LANES] = u_ref[pl.ds(i, nc, stride=S5_L), :].astype(BF16)
    ufv = uf[...]
    s = jnp.dot(ufv, win_ref[...], preferred_element_type=F32)

    def cmul(v, ar, ai):
        sw = jnp.concatenate([v[:, half:], v[:, :half]], axis=1)
        return v * ar + sw * ai

    row = lax.broadcasted_iota(jnp.int32, (nc, ns), 0)
    sub = row % SUBLANES
    k = 0
    st = 1
    while st < SUBLANES:
        sh = jnp.where(sub >= st, pltpu.roll(s, st, axis=0), 0.0)
        s = s + cmul(sh, lstep_ref[2 * k:2 * k + 1, :], lstep_ref[2 * k + 1:2 * k + 2, :])
        k += 1
        st *= 2
    cin = carry[0:1, :]
    prev = cin
    blocks = []
    for blk in range(nc // SUBLANES):
        sb = s[blk * SUBLANES:(blk + 1) * SUBLANES, :] + cmul(jnp.broadcast_to(prev, (SUBLANES, ns)),
                                                              lblk_ref[0], lblk_ref[1])
        prev = sb[SUBLANES - 1:SUBLANES, :]
        blocks.append(sb)
    s = jnp.concatenate(blocks, axis=0)
    carry[...] = jnp.broadcast_to(prev, carry.shape)
    sprev = jnp.where(row >= 1, pltpu.roll(s, 1, axis=0), cin)

    yf = jnp.dot(ufv, wintra[...], preferred_element_type=F32)
    yf = yf + jnp.dot(sprev.astype(BF16), wout_ref[...], preferred_element_type=F32)
    for j in range(S5_L):
        y_ref[pl.ds(j, nc, stride=S5_L), :] = yf[:, j * LANES:(j + 1) * LANES]


def _s5_tail_kernel(y_ref, u_ref, d_ref, w_ref, o_ref):
    c = u_ref.shape[1]
    y = _gelu_tanh(y_ref[...] + d_ref[...] * u_ref[...])
    z = jnp.dot(y.astype(BF16), w_ref[...], preferred_element_type=F32)
    o_ref[...] = (z[:, :c] * _sigmoid(z[:, c:])).astype(o_ref.dtype)


def _s5_tile(t):
    return min(4096, t)


def _s5_branch(u, tabs, d_skip, w_glu, l):
    win, bd, wout, lstep, lblk = tabs
    b, t, _ = u.shape
    tm = _s5_tile(t)
    nc = tm // S5_L
    noct = C // LANES
    ns = win.shape[-1]
    fl = S5_L * LANES

    def table(arr):
        zeros = (0,) * (arr.ndim - 2)
        return pl.BlockSpec((None, None) + arr.shape[2:], lambda q, i, j: (l, q) + zeros)

    y = pl.pallas_call(
        _s5_kernel,
        grid=(noct, b, t // tm),
        in_specs=[pl.BlockSpec((None, tm, LANES), lambda q, i, j: (i, j, q)),
                  table(win), table(bd), table(wout), table(lstep), table(lblk)],
        out_specs=pl.BlockSpec((None, tm, LANES), lambda q, i, j: (i, j, q)),
        out_shape=jax.ShapeDtypeStruct((b, t, C), F32),
        scratch_shapes=[pltpu.VMEM((nc, fl), BF16), pltpu.VMEM((fl, fl), BF16), pltpu.VMEM((SUBLANES, ns), F32)],
        compiler_params=_cparams(("arbitrary", "arbitrary", "arbitrary")),
        name="s5_ssm",
    )(u, win, bd, wout, lstep, lblk)

    tt = min(2048, t)
    return pl.pallas_call(
        _s5_tail_kernel,
        grid=(b, t // tt),
        in_specs=[
            pl.BlockSpec((None, tt, C), lambda i, j: (i, j, 0)),
            pl.BlockSpec((None, tt, C), lambda i, j: (i, j, 0)),
            _layer_spec(d_skip, l),
            _layer_spec(w_glu, l),
        ],
        out_specs=pl.BlockSpec((None, tt, C), lambda i, j: (i, j, 0)),
        out_shape=jax.ShapeDtypeStruct((b, t, C), BF16),
        compiler_params=_cparams(("parallel", "parallel")),
        name="s5_glu",
    )(y, u, d_skip, w_glu)


def _prep_s5(a_re, a_im, b_re, b_im, c_re, c_im, log_dt):
    nl, g, p = a_re.shape
    m = S5_GROUP
    noct = g // S5_OCT
    lam = lax.complex(a_re.astype(F32), a_im.astype(F32))
    dt = jnp.exp(log_dt.astype(F32))[..., None]
    ldt = lam * dt
    lam_bar = jnp.exp(ldt)
    b_bar = ((lam_bar - 1.0) / lam)[..., None] * lax.complex(b_re.astype(F32), b_im.astype(F32))
    cc = lax.complex(c_re.astype(F32), c_im.astype(F32))

    def powers(e):
        return jnp.exp(ldt[..., None] * e.astype(F32).reshape((1, 1, 1, -1)))

    steps = jnp.arange(S5_L)

    def octet_bd(x, row_inner, col_pre, col_post):
        ncol = col_pre * S5_OCT * col_post
        cidx = np.arange(ncol)
        src = (cidx // (S5_OCT * col_post)) * col_post + cidx % col_post
        expand = jnp.asarray(np.arange(col_pre * col_post)[:, None] == src[None, :], F32)
        ridx = np.arange(x.shape[2])
        keep = jnp.asarray(((ridx // row_inner) % S5_OCT)[:, None] == ((cidx // col_post) % S5_OCT)[None, :])
        return jnp.where(keep, jnp.einsum("lqrc,cn->lqrn", x.astype(BF16), expand.astype(BF16),
                                          preferred_element_type=F32), 0.0).astype(BF16)

    kd = jnp.einsum("lgop,lgpd,lgpi->lgdio", cc, powers(steps), b_bar).real
    kd = kd.reshape(nl, noct, S5_OCT, S5_L, m, m)
    kd = jnp.moveaxis(kd, 3, 2).reshape(nl, noct, S5_L * LANES, m)
    bd = octet_bd(kd, m, 1, m).reshape(nl, noct, S5_L, LANES, LANES)

    wi = powers(S5_L - 1 - steps)[..., None] * b_bar[:, :, :, None, :]
    wi = jnp.stack([wi.real, wi.imag], axis=2)
    wi = wi.reshape(nl, noct, S5_OCT, 2, p, S5_L, m)
    wi = jnp.transpose(wi, (0, 1, 5, 2, 6, 3, 4)).reshape(nl, noct, S5_L * LANES, 2 * p)
    win = octet_bd(wi, m, 2, p)

    wo = cc[:, :, :, :, None] * powers(steps + 1)[:, :, None, :, :]
    wo = jnp.stack([wo.real, -wo.imag], axis=2)
    wo = wo.reshape(nl, noct, S5_OCT, 2, m, p, S5_L)
    wo = jnp.transpose(wo, (0, 1, 3, 2, 5, 6, 4)).reshape(nl, noct, 2 * S5_OCT * p, S5_L * m)
    wout = octet_bd(wo, p, S5_L, m)

    def state_mult(e):
        pw = powers(e)
        pw = jnp.transpose(pw, (0, 3, 1, 2)).reshape(nl, e.shape[0], noct, S5_OCT * p)
        pw = jnp.moveaxis(pw, 2, 1)
        return (jnp.concatenate([pw.real, pw.real], axis=-1), jnp.concatenate([-pw.imag, pw.imag], axis=-1))

    nsteps = int(math.log2(SUBLANES))
    ar, ai = state_mult(S5_L * (2 ** jnp.arange(nsteps)))
    lstep = jnp.stack([ar, ai], axis=3).reshape(nl, noct, 2 * nsteps, -1)
    lblk = jnp.stack(state_mult(S5_L * (jnp.arange(SUBLANES) + 1)), axis=2)
    return win, bd, wout, lstep, lblk


RW_L = 64
HIGHEST = lax.Precision.HIGHEST


def _mm(a, b, dims=(((1,), (0,)), ((), ()))):
    return lax.dot_general(a.astype(BF16), b.astype(BF16), dims, preferred_element_type=F32)


def _mm_nt(a, b):
    return _mm(a, b, (((1,), (1,)), ((), ())))


def _mm_tn(a, b):
    return _mm(a, b, (((0,), (0,)), ((), ())))


def _mm_exact(a, b):
    return lax.dot_general(a, b, (((1,), (0,)), ((), ())), precision=HIGHEST, preferred_element_type=F32)


def _head_sum(x, m0):
    s0 = jnp.sum(jnp.where(m0, x, 0.0), axis=-1, keepdims=True)
    s1 = jnp.sum(jnp.where(m0, 0.0, x), axis=-1, keepdims=True)
    return jnp.where(m0, s0, s1)


def _rwkv_kernel(r_ref, k_ref, v_ref, lo_ref, rh_ref, kh_ref, vh_ref, loh_ref, mu_ref, mul_ref, w0_ref, a0_ref,
                 w2_ref, a2_ref, g2_ref, kk_ref, ka_ref, rk_ref, lng_ref, lnb_ref, o_ref, hstate):
    t = pl.program_id(1)
    tm, c = r_ref.shape
    npair = c // LANES
    ll = RW_L

    @pl.when(t == 0)
    def _():
        hstate[...] = jnp.zeros_like(hstate)

    def mix(ref, href, mu):
        p = ref[...].astype(F32)
        prev = _shift_rows(p, jnp.where(t == 0, 0.0, href[...].astype(F32)), 1)
        return p + (prev - p) * mu

    mu = mu_ref[...]
    r = mix(r_ref, rh_ref, mu[0:1, :])
    k = mix(k_ref, kh_ref, mu[1:2, :])
    v = mix(v_ref, vh_ref, mu[2:3, :])
    lo = mix(lo_ref, loh_ref, mul_ref[...])
    wlog = -_softplus(-(w0_ref[...] + _mm(jnp.tanh(lo), w2_ref[...]))) - 0.5
    lw = -jnp.exp(wlog)
    a = _sigmoid(a0_ref[...] + _mm(lo, a2_ref[...]))
    g = _mm(_sigmoid(lo), g2_ref[...])
    kk = k * kk_ref[...]
    k = k * (1.0 + (a - 1.0) * ka_ref[...])

    lane = lax.broadcasted_iota(jnp.int32, (ll, LANES), 1)
    m0 = lane < RWKV_HD
    ri = lax.broadcasted_iota(jnp.int32, (2 * ll, 2 * ll), 0)
    ci = lax.broadcasted_iota(jnp.int32, (2 * ll, 2 * ll), 1)
    same = (ri >= ll) == (ci >= ll)
    strict = same & (ri > ci)
    incl = same & (ri >= ci)
    eye = ri == ci
    tri = (lax.broadcasted_iota(jnp.int32, (ll, ll), 0) >= lax.broadcasted_iota(jnp.int32, (ll, ll), 1)).astype(F32)

    def st(x):
        return jnp.concatenate([jnp.where(m0, x, 0.0), jnp.where(m0, 0.0, x)], axis=0)

    nch = tm // ll
    pcs = [(ch, p) for ch in range(nch) for p in range(npair)]
    cum_all = [_mm_exact(tri, lw[ch * ll:(ch + 1) * ll, :]) for ch in range(nch)]
    pre = {}
    for (ch, p) in pcs:
        rows, cols = slice(ch * ll, (ch + 1) * ll), slice(p * LANES, (p + 1) * LANES)
        rp, kp, vp, ap, lwp, cum = r[rows, cols], k[rows, cols], v[rows, cols], a[rows, cols], lw[rows, cols], cum_all[ch][:, cols]
        kkp = kk[rows, cols]
        kkp = kkp * lax.rsqrt(_head_sum(kkp * kkp, m0) + 1e-12)
        cum_l = cum[ll - 1:ll, :]
        e_neg = jnp.exp(-cum)
        e_last = jnp.exp(cum_l - cum)
        kb = kkp * ap
        stb = lambda x: st(x.astype(BF16))
        pre[ch, p] = dict(
            rows=rows, cols=cols, rp=rp, kp=kp, vp=vp,
            a_st=stb(-kkp * jnp.exp(cum - lwp)), r_st=st(rp * jnp.exp(cum)),
            v_st=stb(vp), b_st=stb(kb * e_neg), k_st=stb(kp * e_neg),
            bh_st=stb(kb * e_last), kh_st=stb(kp * e_last), g_l=jnp.exp(cum_l))

    gram = {pc: _mm_nt(jnp.concatenate([pre[pc]["a_st"], pre[pc]["r_st"].astype(BF16)], axis=0),
                       jnp.concatenate([pre[pc]["b_st"], pre[pc]["k_st"]], axis=0)) for pc in pcs}
    a_ab = {pc: jnp.where(strict, gram[pc][:2 * ll, :2 * ll], 0.0) for pc in pcs}
    a_ak = {pc: jnp.where(strict, gram[pc][:2 * ll, 2 * ll:], 0.0).astype(BF16) for pc in pcs}
    m_rb = {pc: jnp.where(incl, gram[pc][2 * ll:, :2 * ll], 0.0).astype(BF16) for pc in pcs}
    m_rk = {pc: jnp.where(incl, gram[pc][2 * ll:, 2 * ll:], 0.0).astype(BF16) for pc in pcs}
    akv = {pc: _mm(a_ak[pc], pre[pc]["v_st"]) for pc in pcs}
    tinv = {pc: jnp.where(eye, 1.0, a_ab[pc]) for pc in pcs}
    pw = {pc: a_ab[pc].astype(BF16) for pc in pcs}
    for _ in range(int(math.log2(ll)) - 1):
        pw = {pc: _mm(pw[pc], pw[pc]).astype(BF16) for pc in pcs}
        tinv = {pc: tinv[pc] + _mm(tinv[pc], pw[pc]) for pc in pcs}
    xa = {pc: _mm(tinv[pc], jnp.concatenate([pre[pc]["a_st"], akv[pc].astype(BF16)], axis=1)).astype(BF16)
          for pc in pcs}
    ma = {pc: _mm(m_rb[pc], xa[pc][:, :LANES]) for pc in pcs}
    y_loc = {pc: _mm(jnp.concatenate([m_rb[pc], m_rk[pc]], axis=1),
                     jnp.concatenate([xa[pc][:, LANES:], pre[pc]["v_st"]], axis=0)) for pc in pcs}
    ba = {pc: _mm_tn(pre[pc]["bh_st"], xa[pc][:, :LANES]) for pc in pcs}
    f_loc = {pc: _mm_tn(jnp.concatenate([pre[pc]["bh_st"], pre[pc]["kh_st"]], axis=0),
                        jnp.concatenate([xa[pc][:, LANES:], pre[pc]["v_st"]], axis=0)) for pc in pcs}
    for (ch, p) in pcs:
        pc = (ch, p)
        q = pre[pc]
        r_eff = q["r_st"] + ma[pc]
        g_eff = jnp.where(eye, q["g_l"], 0.0) + ba[pc]
        hs = _mm(jnp.concatenate([g_eff, r_eff], axis=0), hstate[p])
        hstate[p] = hs[:2 * ll, :] + f_loc[pc]
        y_st = hs[2 * ll:, :] + y_loc[pc]
        y = y_st[:ll, :] + y_st[ll:, :]
        rows, cols = q["rows"], q["cols"]
        mean = _head_sum(y, m0) * (1.0 / RWKV_HD)
        d = y - mean
        var = _head_sum(d * d, m0) * (1.0 / RWKV_HD)
        yn = d * lax.rsqrt(var + RWKV_LN_EPS) * lng_ref[:, cols] + lnb_ref[:, cols]
        bonus = _head_sum(q["rp"] * q["kp"] * rk_ref[:, cols], m0) * q["vp"]
        o_ref[rows, cols] = ((yn + bonus) * g[rows, cols]).astype(o_ref.dtype)


def _rwkv_branch(z, prm, l):
    b, t, _ = z.shape
    tm = min(8 * RW_L, t)
    hb = tm // HALO
    cr = COL_RW_R // C
    cl = COL_LORA // 256
    npair = C // LANES

    def tile(cb, w):
        return pl.BlockSpec((None, tm, w), lambda i, j: (i, j, cb))

    def halo(cb, w):
        return pl.BlockSpec((None, HALO, w), lambda i, j: (i, jnp.maximum(j * hb - 1, 0), cb))

    return pl.pallas_call(
        _rwkv_kernel,
        grid=(b, t // tm),
        in_specs=[tile(cr, C), tile(cr + 1, C), tile(cr + 2, C), tile(cl, 256),
                  halo(cr, C), halo(cr + 1, C), halo(cr + 2, C), halo(cl, 256)]
                 + [_layer_spec(a, l) for a in prm],
        out_specs=pl.BlockSpec((None, tm, C), lambda i, j: (i, j, 0)),
        out_shape=jax.ShapeDtypeStruct((b, t, C), BF16),
        scratch_shapes=[pltpu.VMEM((npair, LANES, LANES), F32)],
        compiler_params=_cparams(("parallel", "arbitrary")),
        name="rwkv7",
    )(z, z, z, z, z, z, z, z, *prm)


def _prep_rwkv(mu_rkv, mu_w, mu_a, mu_g, w0, w2, a0, a2, g2, k_k, k_a, r_k, lnx_g, lnx_b):
    nl = mu_rkv.shape[0]
    row = lambda x: x.reshape(nl, 1, -1).astype(F32)
    mul = jnp.concatenate([mu_w, mu_a, mu_g], axis=-1)
    zw = lambda n: jnp.zeros((nl, n, C), F32)
    w2p = jnp.concatenate([w2, zw(A_LORA + G_LORA)], axis=1).astype(BF16)
    a2p = jnp.concatenate([zw(W_LORA), a2, zw(G_LORA)], axis=1).astype(BF16)
    g2p = jnp.concatenate([zw(W_LORA + A_LORA), g2], axis=1).astype(BF16)
    return (mu_rkv.astype(F32), row(mul), row(w0), row(a0), w2p, a2p, g2p, row(k_k), row(k_a), row(r_k),
            row(lnx_g), row(lnx_b))


SLOT = LANES
ROT_HALF = QK_ROPE // 2
NOPE_A = SLOT // 2 - ROT_HALF


def _slot_source():
    src = np.full((SLOT,), QK_HD, np.int32)
    src[:ROT_HALF] = QK_NOPE + np.arange(ROT_HALF)
    src[ROT_HALF:SLOT // 2] = np.arange(NOPE_A)
    src[SLOT // 2:SLOT // 2 + ROT_HALF] = QK_NOPE + ROT_HALF + np.arange(ROT_HALF)
    src[SLOT // 2 + ROT_HALF:SLOT // 2 + ROT_HALF + QK_NOPE - NOPE_A] = NOPE_A + np.arange(QK_NOPE - NOPE_A)
    return src


def _to_slot(v):
    pad = jnp.concatenate([v, jnp.zeros(v.shape[:-1] + (1,), v.dtype)], axis=-1)
    return jnp.take(pad, jnp.asarray(_slot_source()), axis=-1)
VT_ROWS = V_HD + 16
FLASH_NB = 4


FAST_BOUND = 40.0


def _mla_prep_kernel(cq_ref, ckv_ref, kr_ref, tab_ref, qn_ref, wq_ref, kvn_ref, wk_ref, wv_ref,
                     q_ref, k_ref, v_ref, v_scr):
    def rms(x, g, n):
        ms = jnp.sum(x * x, axis=-1, keepdims=True) * (1.0 / n)
        return x * lax.rsqrt(ms + EPS) * g

    def norm_rope(x, c_tab, s_tab):
        ms = jnp.sum(x * x, axis=-1, keepdims=True) * (1.0 / QK_HD)
        return (x * c_tab + pltpu.roll(x, SLOT // 2, axis=1) * s_tab) * lax.rsqrt(ms + EPS)

    cq = rms(cq_ref[...].astype(F32), qn_ref[...], Q_LORA)
    ckv = rms(ckv_ref[...].astype(F32), kvn_ref[...], KV_LORA)
    qf = _mm(cq, wq_ref[...])
    kf = _mm(ckv, wk_ref[...])
    v_scr[...] = _mm(ckv, wv_ref[...])
    vt = jnp.transpose(v_scr[...]).astype(BF16)
    ones = jnp.ones((VT_ROWS - V_HD, vt.shape[1]), BF16)
    for h in range(MLA_HEADS):
        v_ref[h * VT_ROWS:h * VT_ROWS + V_HD, :] = vt[h * V_HD:(h + 1) * V_HD, :]
        v_ref[h * VT_ROWS + V_HD:(h + 1) * VT_ROWS, :] = ones
    kr = kr_ref[...].astype(F32)
    cq_tab, sq_tab, ck_tab, sk_tab = tab_ref[0], tab_ref[1], tab_ref[2], tab_ref[3]
    for h in range(MLA_HEADS):
        cols = slice(h * SLOT, (h + 1) * SLOT)
        q_ref[:, cols] = norm_rope(qf[:, cols], cq_tab, sq_tab).astype(BF16)
        k_ref[:, cols] = norm_rope(kf[:, cols] + kr, ck_tab, sk_tab).astype(BF16)


def _flash_kernel(qi_ref, kj_ref, q_ref, k_ref, vt_ref, o_ref, acc, m_scr=None):
    online = m_scr is not None
    i = qi_ref[pl.program_id(1)]
    j = kj_ref[pl.program_id(1)]
    nb, bq, _ = q_ref.shape
    bk = k_ref.shape[1]

    @pl.when(j == 0)
    def _():
        if online:
            m_scr[...] = jnp.full_like(m_scr, -jnp.inf)
        acc[...] = jnp.zeros_like(acc)

    def step(bb, masked):
        if masked:
            causal = (lax.broadcasted_iota(jnp.int32, (bk, bq), 0) <= lax.broadcasted_iota(jnp.int32, (bk, bq), 1))

        def scores(h):
            cols = slice(h * SLOT, (h + 1) * SLOT)
            return _mm_nt(k_ref[bb, :, cols], q_ref[bb, :, cols])

        st_next = scores(0)
        for h in range(MLA_HEADS):
            rows = slice(h * VT_ROWS, (h + 1) * VT_ROWS)
            st = st_next
            if h + 1 < MLA_HEADS:
                st_next = scores(h + 1)
            if masked:
                st = jnp.where(causal, st, -jnp.inf)
            if online:
                m_prev = m_scr[bb, h:h + 1, :]
                m_new = jnp.maximum(m_prev, jnp.max(st, axis=0, keepdims=True))
                alpha = jnp.exp2(m_prev - m_new)
                pt = jnp.exp2(st - m_new)
                m_scr[bb, h:h + 1, :] = m_new
                acc[bb, rows, :] = acc[bb, rows, :] * alpha + _mm(vt_ref[bb, rows, :], pt)
            else:
                acc[bb, rows, :] += _mm(vt_ref[bb, rows, :], jnp.exp2(st))

    def finish(bb):
        outs = []
        for h in range(MLA_HEADS):
            r0 = h * VT_ROWS
            outs.append(acc[bb, r0:r0 + V_HD, :] / acc[bb, r0 + V_HD:r0 + V_HD + 1, :])
        o_ref[bb] = jnp.transpose(jnp.concatenate(outs, axis=0)).astype(o_ref.dtype)

    @pl.when(j < i)
    def _():
        @pl.loop(0, nb)
        def _(bb):
            step(bb, False)

    @pl.when(j == i)
    def _():
        @pl.loop(0, nb)
        def _(bb):
            step(bb, True)
            finish(bb)


def _score_bound(qk_norm_q, qk_norm_k):
    scale = QK_HD ** -0.5 * math.log2(math.e)
    gq = jnp.max(jnp.abs(qk_norm_q.astype(F32)), axis=-1)
    gk = jnp.max(jnp.abs(qk_norm_k.astype(F32)), axis=-1)
    return 1.01 * QK_HD * scale * gq * gk


def _mla_branch(z, prm, rope, score_bound, l):
    b, t, _ = z.shape
    tm = min(1024, t)
    hs = MLA_HEADS * SLOT
    vr = MLA_HEADS * VT_ROWS

    q, k, vt = pl.pallas_call(
        _mla_prep_kernel,
        grid=(b, t // tm),
        in_specs=[pl.BlockSpec((None, tm, Q_LORA), lambda i, j: (i, j, COL_CQ // Q_LORA)),
                  pl.BlockSpec((None, tm, KV_LORA), lambda i, j: (i, j, COL_CKV // KV_LORA)),
                  pl.BlockSpec((None, tm, LANES), lambda i, j: (i, j, COL_KR // LANES)),
                  pl.BlockSpec((None, 4, tm, SLOT), lambda i, j: (l, 0, j, 0))]
                 + [_layer_spec(a, l) for a in prm],
        out_specs=[pl.BlockSpec((None, tm, hs), lambda i, j: (i, j, 0)),
                   pl.BlockSpec((None, tm, hs), lambda i, j: (i, j, 0)),
                   pl.BlockSpec((None, vr, tm), lambda i, j: (i, 0, j))],
        out_shape=[jax.ShapeDtypeStruct((b, t, hs), BF16), jax.ShapeDtypeStruct((b, t, hs), BF16),
                   jax.ShapeDtypeStruct((b, vr, t), BF16)],
        scratch_shapes=[pltpu.VMEM((tm, C), F32)],
        compiler_params=_cparams(("parallel", "parallel")),
        name="mla_prep",
    )(z, z, z, rope, *prm)

    bq = min(512, t)
    nq = t // bq
    tri = [(i, j) for i in range(nq) for j in range(i + 1)]
    qi = jnp.asarray([p[0] for p in tri], jnp.int32)
    kj = jnp.asarray([p[1] for p in tri], jnp.int32)

    nb = math.gcd(b, FLASH_NB)

    def attend(online):
        scratch = [pltpu.VMEM((nb, vr, bq), F32)] + ([pltpu.VMEM((nb, MLA_HEADS, bq), F32)] if online else [])
        return pl.pallas_call(
            _flash_kernel,
            grid_spec=pltpu.PrefetchScalarGridSpec(
                num_scalar_prefetch=2,
                grid=(b // nb, len(tri)),
                in_specs=[pl.BlockSpec((nb, bq, hs), lambda bi, s, qi, kj: (bi, qi[s], 0)),
                          pl.BlockSpec((nb, bq, hs), lambda bi, s, qi, kj: (bi, kj[s], 0)),
                          pl.BlockSpec((nb, vr, bq), lambda bi, s, qi, kj: (bi, 0, kj[s]))],
                out_specs=pl.BlockSpec((nb, bq, C), lambda bi, s, qi, kj: (bi, qi[s], 0)),
                scratch_shapes=scratch),
            out_shape=jax.ShapeDtypeStruct((b, t, C), BF16),
            compiler_params=_cparams(("parallel", "arbitrary")),
            name="mla_attn_online" if online else "mla_attn",
        )(qi, kj, q, k, vt)

    return lax.cond(score_bound <= FAST_BOUND, lambda: attend(False), lambda: attend(True))


def _rope_tables(t, qk_norm_q, qk_norm_k):
    pos = jnp.arange(t, dtype=F32)
    inv_freq = ROPE_THETA ** (-jnp.arange(0, QK_ROPE, 2, dtype=F32) / QK_ROPE)
    ang = pos[:, None] * inv_freq[None, :]
    cos, sin = jnp.cos(ang), jnp.sin(ang)
    z = lambda n: jnp.zeros((t, n), F32)
    one = lambda n: jnp.ones((t, n), F32)
    tail = SLOT // 2 - ROT_HALF
    c_pos = jnp.concatenate([cos, one(NOPE_A), cos, one(QK_NOPE - NOPE_A), z(SLOT - QK_HD)], axis=1)
    s_pos = jnp.concatenate([-sin, z(NOPE_A), sin, z(tail)], axis=1)
    scale = QK_HD ** -0.5 * math.log2(math.e)

    def pair(g, mult):
        g = _to_slot(g.astype(F32)) * mult
        return [g[:, None, :] * c_pos[None], jnp.roll(g, SLOT // 2, axis=-1)[:, None, :] * s_pos[None]]

    return jnp.stack(pair(qk_norm_q, scale) + pair(qk_norm_k, 1.0), axis=1)


def _prep_mla(q_norm, w_uq, kv_norm, w_ukv):
    nl = q_norm.shape[0]
    nh = MLA_HEADS
    wq = _to_slot(w_uq.reshape(nl, Q_LORA, nh, QK_HD)).reshape(nl, Q_LORA, nh * SLOT).astype(BF16)
    wkv = w_ukv.reshape(nl, KV_LORA, nh, QK_NOPE + V_HD)
    wk = jnp.pad(wkv[..., :QK_NOPE], ((0, 0), (0, 0), (0, 0), (0, QK_ROPE)))
    wk = _to_slot(wk).reshape(nl, KV_LORA, nh * SLOT).astype(BF16)
    wv = wkv[..., QK_NOPE:].reshape(nl, KV_LORA, nh * V_HD).astype(BF16)
    return (q_norm.reshape(nl, 1, -1).astype(F32), wq, kv_norm.reshape(nl, 1, -1).astype(F32), wk, wv)


def _merge_kernel(y0_ref, y1_ref, y2_ref, y3_ref, gl_ref, x_ref, wb_ref, wo_ref, o_ref):
    d = x_ref.shape[1]
    merged = None
    for n, y_ref in enumerate((y0_ref, y1_ref, y2_ref, y3_ref)):
        term = _mm(y_ref[...], wb_ref[n]) * _sigmoid(gl_ref[:, n * d:(n + 1) * d].astype(F32))
        merged = term if merged is None else merged + term
    o_ref[...] = x_ref[...] + _mm(merged, wo_ref[...])


def _merge(ys, z2, x2, wb, wo, l):
    n, d = x2.shape
    tm = min(512, n)
    ytile = pl.BlockSpec((tm, C), lambda i: (i, 0))
    return pl.pallas_call(
        _merge_kernel,
        grid=(n // tm,),
        in_specs=[ytile, ytile, ytile, ytile,
                  pl.BlockSpec((tm, 4 * d), lambda i: (i, COL_GATE)),
                  pl.BlockSpec((tm, d), lambda i: (i, 0)),
                  _layer_spec(wb, l), _layer_spec(wo, l)],
        out_specs=pl.BlockSpec((tm, d), lambda i: (i, 0)),
        out_shape=jax.ShapeDtypeStruct((n, d), F32),
        compiler_params=_cparams(("parallel",)),
        name="merge",
    )(*ys, z2, x2, wb, wo)


def _mlp_kernel(x_ref, g_ref, w1_ref, w2_ref, o_ref, h_scr, acc):
    j = pl.program_id(1)

    @pl.when(j == 0)
    def _():
        x = x_ref[...]
        ms = jnp.mean(x * x, axis=-1, keepdims=True)
        h_scr[...] = (x * lax.rsqrt(ms + EPS) * g_ref[...]).astype(BF16)
        acc[...] = x

    a = jnp.maximum(jnp.dot(h_scr[...], w1_ref[...], preferred_element_type=F32), 0.0)
    acc[...] += _mm(a * a, w2_ref[...])

    @pl.when(j == pl.num_programs(1) - 1)
    def _():
        o_ref[...] = acc[...]


def _mlp(x2, g, w1, w2, l):
    n, d = x2.shape
    dff = w1.shape[2]
    tm = min(1024, n)
    tf = 2048
    return pl.pallas_call(
        _mlp_kernel,
        grid=(n // tm, dff // tf),
        in_specs=[pl.BlockSpec((tm, d), lambda i, j: (i, 0)),
                  pl.BlockSpec((None, 1, d), lambda i, j: (l, 0, 0)),
                  pl.BlockSpec((None, d, tf), lambda i, j: (l, 0, j)),
                  pl.BlockSpec((None, tf, d), lambda i, j: (l, j, 0))],
        out_specs=pl.BlockSpec((tm, d), lambda i, j: (i, 0)),
        out_shape=jax.ShapeDtypeStruct((n, d), F32),
        scratch_shapes=[pltpu.VMEM((tm, d), BF16), pltpu.VMEM((tm, d), F32)],
        compiler_params=_cparams(("parallel", "arbitrary")),
        name="mlp",
    )(x2, g, w1, w2)


def _prep_w_in(w_in):
    nl, d, _ = w_in.shape
    gate0 = 6 * C + W_LORA + A_LORA + G_LORA + Q_LORA + KV_LORA + QK_ROPE
    kr0 = gate0 - QK_ROPE
    parts = [
        w_in[:, :, :kr0],
        w_in[:, :, kr0:kr0 + ROT_HALF],
        jnp.zeros((nl, d, NOPE_A), w_in.dtype),
        w_in[:, :, kr0 + ROT_HALF:gate0],
        jnp.zeros((nl, d, D_INP - COL_KR - SLOT // 2 - ROT_HALF), w_in.dtype),
    ]
    return w_in[:, :, gate0:].astype(BF16), jnp.concatenate(parts, axis=-1).astype(BF16)


def _prep_lru_gate(gate_w):
    nl, _, nh, hd, _ = gate_w.shape
    eye = jnp.eye(nh, dtype=gate_w.dtype)
    dense = jnp.einsum("lghij,hk->lghikj", gate_w, eye).reshape(nl, 2, nh * hd, nh * hd)
    return jnp.concatenate([dense[:, 0], dense[:, 1]], axis=-1).astype(BF16)


def kernel(x, norm_mix, w_in, lru_conv_w, lru_conv_b, lru_gate_w, lru_gate_b, lru_lambda, s5_a_re, s5_a_im, s5_b_re, s5_b_im, s5_c_re, s5_c_im, s5_d, s5_log_dt, s5_w_glu, rwkv_mu_rkv, rwkv_mu_w, rwkv_mu_a, rwkv_mu_g, rwkv_w0, rwkv_w2, rwkv_a0, rwkv_a2, rwkv_g2, rwkv_k_k, rwkv_k_a, rwkv_r_k, rwkv_lnx_g, rwkv_lnx_b, mla_q_norm, mla_w_uq, mla_kv_norm, mla_w_ukv, mla_qk_norm_q, mla_qk_norm_k, w_branch, w_out, norm_mlp, w_ff1, w_ff2):
    b, t, d = x.shape
    depth = w_in.shape[0]
    n = b * t
    row = lambda a: a.reshape(depth, 1, -1).astype(F32)

    w_gate, w_rest = _prep_w_in(w_in)
    lru_gw = _prep_lru_gate(lru_gate_w)
    s5_tabs = _prep_s5(s5_a_re, s5_a_im, s5_b_re, s5_b_im, s5_c_re, s5_c_im, s5_log_dt)
    rw_prm = _prep_rwkv(rwkv_mu_rkv, rwkv_mu_w, rwkv_mu_a, rwkv_mu_g, rwkv_w0, rwkv_w2, rwkv_a0, rwkv_a2, rwkv_g2,
                        rwkv_k_k, rwkv_k_a, rwkv_r_k, rwkv_lnx_g, rwkv_lnx_b)
    mla_prm = _prep_mla(mla_q_norm, mla_w_uq, mla_kv_norm, mla_w_ukv)
    rope = _rope_tables(t, mla_qk_norm_q, mla_qk_norm_k)
    score_bound = _score_bound(mla_qk_norm_q, mla_qk_norm_k)
    g_mix, g_mlp = row(norm_mix), row(norm_mlp)
    conv_b, gate_b, lam, s5_dr = row(lru_conv_b), row(lru_gate_b), row(lru_lambda), row(s5_d)
    w_glu, wb, wo = s5_w_glu.astype(BF16), w_branch.astype(BF16), w_out.astype(BF16)
    w1, w2 = w_ff1.astype(BF16), w_ff2.astype(BF16)

    lru_prm = (lru_conv_w.astype(F32), conv_b, lru_gw, gate_b, lam)
    x2 = x.reshape(n, d)
    for l in range(depth):
        z2, u2 = _inproj(x2, g_mix, w_gate, w_rest, l)
        z = z2.reshape(b, t, -1)
        y_lru = _lru_branch(z, lru_prm, l)
        y_s5 = _s5_branch(u2.reshape(b, t, C), s5_tabs, s5_dr, w_glu, l)
        y_rw = _rwkv_branch(z, rw_prm, l)
        y_mla = _mla_branch(z, mla_prm, rope, score_bound[l], l)
        ys = [y.reshape(n, C) for y in (y_lru, y_s5, y_rw, y_mla)]
        x2 = _merge(ys, z2, x2, wb, wo, l)
        x2 = _mlp(x2, g_mlp, w1, w2, l)
    return x2.reshape(b, t, d)
```

```python
import math

import jax
import jax.numpy as jnp
import numpy as np
from jax import lax
from jax.experimental import pallas as pl
from jax.experimental.pallas import tpu as pltpu

F32 = jnp.float32
BF16 = jnp.bfloat16

EPS = 1e-6
CONV_W = 4
LRU_C = 8.0
S5_GROUP = 16
RWKV_HD = 64
W_LORA, A_LORA, G_LORA = 64, 64, 128
RWKV_LN_EPS = 64e-5
MLA_HEADS = 8
QK_NOPE, QK_ROPE, V_HD = 64, 32, 64
QK_HD = QK_NOPE + QK_ROPE
Q_LORA, KV_LORA = 256, 128
ROPE_THETA = 10000.0

LANES = 128
SUBLANES = 8
VMEM_LIMIT = 56 * 1024 * 1024

D_MODEL = 1024
C = D_MODEL // 2
COL_GATE = 0
COL_LRU_X = 4 * D_MODEL
COL_LRU_G = COL_LRU_X + C
COL_S5_U = COL_LRU_G + C
COL_RW_R = COL_S5_U + C
COL_RW_K = COL_RW_R + C
COL_RW_V = COL_RW_K + C
COL_LORA = COL_RW_V + C
COL_CQ = COL_LORA + 256
COL_CKV = COL_CQ + Q_LORA
COL_KR = COL_CKV + KV_LORA
D_INP = 8 * D_MODEL


def _cparams(sem):
    return pltpu.CompilerParams(dimension_semantics=sem, vmem_limit_bytes=VMEM_LIMIT)


def _gelu_tanh(x):
    return 0.5 * x * (1.0 + jnp.tanh(math.sqrt(2.0 / math.pi) * (x + 0.044715 * (x * x * x))))


def _sigmoid(x):
    return 1.0 / (1.0 + jnp.exp(-x))


def _softplus(x):
    return jnp.maximum(x, 0.0) + jnp.log(1.0 + jnp.exp(-jnp.abs(x)))


HALO = 16


def _shift_rows(x, halo, k):
    xe = jnp.concatenate([halo, x], axis=0)
    n = x.shape[0]
    return xe[HALO - k:HALO - k + n, :]


def _layer_spec(arr, l):
    zeros = (0,) * (arr.ndim - 1)
    return pl.BlockSpec((None,) + arr.shape[1:], lambda *idx: (l,) + zeros)


def _inproj_kernel(x_ref, g_ref, w_ref, o_ref, u_ref, h_scr):
    j = pl.program_id(1)

    @pl.when(j == 0)
    def _():
        x = x_ref[...]
        ms = jnp.mean(x * x, axis=-1, keepdims=True)
        h_scr[...] = (x * lax.rsqrt(ms + EPS) * g_ref[...]).astype(BF16)

    y = jnp.dot(h_scr[...], w_ref[...], preferred_element_type=F32)
    o_ref[...] = y.astype(BF16)
    tn = w_ref.shape[1]
    u0 = COL_S5_U % tn

    @pl.when(j == COL_S5_U // tn)
    def _():
        u_ref[...] = y[:, u0:u0 + C]


INPROJ_TN = 2048


def _inproj(x2, g, w, l):
    n, d = x2.shape
    dn = w.shape[2]
    tm = min(1024, n)
    tn = INPROJ_TN
    assert COL_S5_U % tn + C <= tn and dn % tn == 0
    return pl.pallas_call(
        _inproj_kernel,
        grid=(n // tm, dn // tn),
        in_specs=[
            pl.BlockSpec((tm, d), lambda i, j: (i, 0)),
            pl.BlockSpec((None, 1, d), lambda i, j: (l, 0, 0)),
            pl.BlockSpec((None, d, tn), lambda i, j: (l, 0, j)),
        ],
        out_specs=[pl.BlockSpec((tm, tn), lambda i, j: (i, j)),
                   pl.BlockSpec((tm, C), lambda i, j: (i, 0))],
        out_shape=[jax.ShapeDtypeStruct((n, dn), BF16), jax.ShapeDtypeStruct((n, C), F32)],
        scratch_shapes=[pltpu.VMEM((tm, d), BF16)],
        compiler_params=_cparams(("parallel", "arbitrary")),
        name="inproj",
    )(x2, g, w)


def _lru_kernel(x_ref, xh_ref, g_ref, cw_ref, cb_ref, gw_ref, gb_ref, lam_ref, o_ref, carry):
    t = pl.program_id(1)
    tm, c = x_ref.shape

    @pl.when(t == 0)
    def _():
        carry[...] = jnp.zeros_like(carry)

    x = x_ref[...].astype(F32)
    halo = jnp.where(t == 0, 0.0, xh_ref[...].astype(F32))
    cw = cw_ref[...]
    xc = cb_ref[...] + cw[CONV_W - 1:CONV_W, :] * x
    for k in range(1, CONV_W):
        xc = xc + cw[CONV_W - 1 - k:CONV_W - k, :] * _shift_rows(x, halo, k)

    gates = jnp.dot(xc.astype(BF16), gw_ref[...], preferred_element_type=F32) + gb_ref[...]
    r = _sigmoid(gates[:, :c])
    i = _sigmoid(gates[:, c:])
    log_a = (-LRU_C) * r * _softplus(-lam_ref[...])
    a = jnp.exp(log_a)
    u = jnp.sqrt(1.0 - a * a) * (i * xc)

    sub = lax.broadcasted_iota(jnp.int32, (tm, c), 0) % SUBLANES
    s = 1
    while s < SUBLANES:
        a_sh = jnp.where(sub >= s, pltpu.roll(a, s, axis=0), 1.0)
        u_sh = jnp.where(sub >= s, pltpu.roll(u, s, axis=0), 0.0)
        u = u + a * u_sh
        a = a * a_sh
        s *= 2
    hp = carry[SUBLANES - 1:SUBLANES, :]
    blocks = []
    for blk in range(tm // SUBLANES):
        rows = slice(blk * SUBLANES, (blk + 1) * SUBLANES)
        hb = u[rows, :] + a[rows, :] * hp
        hp = hb[SUBLANES - 1:SUBLANES, :]
        blocks.append(hb)
    h = jnp.concatenate(blocks, axis=0)
    carry[...] = blocks[-1]
    o_ref[...] = (h * _gelu_tanh(g_ref[...].astype(F32))).astype(o_ref.dtype)


def _lru_branch(z, prm, l):
    b, t, _ = z.shape
    tm = min(256, t)
    cb = COL_LRU_X // C
    hb = tm // HALO
    return pl.pallas_call(
        _lru_kernel,
        grid=(b, t // tm),
        in_specs=[
            pl.BlockSpec((None, tm, C), lambda i, j: (i, j, cb)),
            pl.BlockSpec((None, HALO, C), lambda i, j: (i, jnp.maximum(j * hb - 1, 0), cb)),
            pl.BlockSpec((None, tm, C), lambda i, j: (i, j, cb + 1)),
        ] + [_layer_spec(a, l) for a in prm],
        out_specs=pl.BlockSpec((None, tm, C), lambda i, j: (i, j, 0)),
        out_shape=jax.ShapeDtypeStruct((b, t, C), BF16),
        scratch_shapes=[pltpu.VMEM((SUBLANES, C), F32)],
        compiler_params=_cparams(("parallel", "arbitrary")),
        name="rglru",
    )(z, z, z, *prm)


S5_L = 8
S5_OCT = LANES // S5_GROUP


def _s5_kernel(u_ref, win_ref, bd_ref, wout_ref, lstep_ref, lblk_ref, y_ref, uf, wintra, carry):
    t = pl.program_id(2)
    tm = u_ref.shape[0]
    nc = tm // S5_L
    ns = carry.shape[1]
    half = ns // 2

    @pl.when((pl.program_id(1) == 0) & (t == 0))
    def _():
        wintra[...] = jnp.zeros_like(wintra)
        for i in range(S5_L):
            for j in range(i, S5_L):
                wintra[i * LANES:(i + 1) * LANES, j * LANES:(j + 1) * LANES] = bd_ref[j - i]

    @pl.when(t == 0)
    def _():
        carry[...] = jnp.zeros_like(carry)

    for i in range(S5_L):
        uf[:, i * LANES:(i + 1) * LANES] = u_ref[pl.ds(i, nc, stride=S5_L), :].astype(BF16)
    ufv = uf[...]
    s = jnp.dot(ufv, win_ref[...], preferred_element_type=F32)

    def cmul(v, ar, ai):
        sw = jnp.concatenate([v[:, half:], v[:, :half]], axis=1)
        return v * ar + sw * ai

    row = lax.broadcasted_iota(jnp.int32, (nc, ns), 0)
    sub = row % SUBLANES
    k = 0
    st = 1
    while st < SUBLANES:
        sh = jnp.where(sub >= st, pltpu.roll(s, st, axis=0), 0.0)
        s = s + cmul(sh, lstep_ref[2 * k:2 * k + 1, :], lstep_ref[2 * k + 1:2 * k + 2, :])
        k += 1
        st *= 2
    cin = carry[0:1, :]
    prev = cin
    blocks = []
    for blk in range(nc // SUBLANES):
        sb = s[blk * SUBLANES:(blk + 1) * SUBLANES, :] + cmul(jnp.broadcast_to(prev, (SUBLANES, ns)),
                                                              lblk_ref[0], lblk_ref[1])
        prev = sb[SUBLANES - 1:SUBLANES, :]
        blocks.append(sb)
    s = jnp.concatenate(blocks, axis=0)
    carry[...] = jnp.broadcast_to(prev, carry.shape)
    sprev = jnp.where(row >= 1, pltpu.roll(s, 1, axis=0), cin)

    yf = jnp.dot(ufv, wintra[...], preferred_element_type=F32)
    yf = yf + jnp.dot(sprev.astype(BF16), wout_ref[...], preferred_element_type=F32)
    for j in range(S5_L):
        y_ref[pl.ds(j, nc, stride=S5_L), :] = yf[:, j * LANES:(j + 1) * LANES]


def _s5_tail_kernel(y_ref, u_ref, d_ref, w_ref, o_ref):
    c = u_ref.shape[1]
    y = _gelu_tanh(y_ref[...] + d_ref[...] * u_ref[...])
    z = jnp.dot(y.astype(BF16), w_ref[...], preferred_element_type=F32)
    o_ref[...] = (z[:, :c] * _sigmoid(z[:, c:])).astype(o_ref.dtype)


def _s5_tile(t):
    return min(4096, t)


def _s5_branch(u, tabs, d_skip, w_glu, l):
    win, bd, wout, lstep, lblk = tabs
    b, t, _ = u.shape
    tm = _s5_tile(t)
    nc = tm // S5_L
    noct = C // LANES
    ns = win.shape[-1]
    fl = S5_L * LANES

    def table(arr):
        zeros = (0,) * (arr.ndim - 2)
        return pl.BlockSpec((None, None) + arr.shape[2:], lambda q, i, j: (l, q) + zeros)

    y = pl.pallas_call(
        _s5_kernel,
        grid=(noct, b, t // tm),
        in_specs=[pl.BlockSpec((None, tm, LANES), lambda q, i, j: (i, j, q)),
                  table(win), table(bd), table(wout), table(lstep), table(lblk)],
        out_specs=pl.BlockSpec((None, tm, LANES), lambda q, i, j: (i, j, q)),
        out_shape=jax.ShapeDtypeStruct((b, t, C), F32),
        scratch_shapes=[pltpu.VMEM((nc, fl), BF16), pltpu.VMEM((fl, fl), BF16), pltpu.VMEM((SUBLANES, ns), F32)],
        compiler_params=_cparams(("arbitrary", "arbitrary", "arbitrary")),
        name="s5_ssm",
    )(u, win, bd, wout, lstep, lblk)

    tt = min(2048, t)
    return pl.pallas_call(
        _s5_tail_kernel,
        grid=(b, t // tt),
        in_specs=[
            pl.BlockSpec((None, tt, C), lambda i, j: (i, j, 0)),
            pl.BlockSpec((None, tt, C), lambda i, j: (i, j, 0)),
            _layer_spec(d_skip, l),
            _layer_spec(w_glu, l),
        ],
        out_specs=pl.BlockSpec((None, tt, C), lambda i, j: (i, j, 0)),
        out_shape=jax.ShapeDtypeStruct((b, t, C), BF16),
        compiler_params=_cparams(("parallel", "parallel")),
        name="s5_glu",
    )(y, u, d_skip, w_glu)


def _prep_s5(a_re, a_im, b_re, b_im, c_re, c_im, log_dt):
    nl, g, p = a_re.shape
    m = S5_GROUP
    noct = g // S5_OCT
    lam = lax.complex(a_re.astype(F32), a_im.astype(F32))
    dt = jnp.exp(log_dt.astype(F32))[..., None]
    ldt = lam * dt
    lam_bar = jnp.exp(ldt)
    b_bar = ((lam_bar - 1.0) / lam)[..., None] * lax.complex(b_re.astype(F32), b_im.astype(F32))
    cc = lax.complex(c_re.astype(F32), c_im.astype(F32))

    def powers(e):
        return jnp.exp(ldt[..., None] * e.astype(F32).reshape((1, 1, 1, -1)))

    steps = jnp.arange(S5_L)

    def octet_bd(x, row_inner, col_pre, col_post):
        ncol = col_pre * S5_OCT * col_post
        cidx = np.arange(ncol)
        src = (cidx // (S5_OCT * col_post)) * col_post + cidx % col_post
        expand = jnp.asarray(np.arange(col_pre * col_post)[:, None] == src[None, :], F32)
        ridx = np.arange(x.shape[2])
        keep = jnp.asarray(((ridx // row_inner) % S5_OCT)[:, None] == ((cidx // col_post) % S5_OCT)[None, :])
        return jnp.where(keep, jnp.einsum("lqrc,cn->lqrn", x.astype(BF16), expand.astype(BF16),
                                          preferred_element_type=F32), 0.0).astype(BF16)

    kd = jnp.einsum("lgop,lgpd,lgpi->lgdio", cc, powers(steps), b_bar).real
    kd = kd.reshape(nl, noct, S5_OCT, S5_L, m, m)
    kd = jnp.moveaxis(kd, 3, 2).reshape(nl, noct, S5_L * LANES, m)
    bd = octet_bd(kd, m, 1, m).reshape(nl, noct, S5_L, LANES, LANES)

    wi = powers(S5_L - 1 - steps)[..., None] * b_bar[:, :, :, None, :]
    wi = jnp.stack([wi.real, wi.imag], axis=2)
    wi = wi.reshape(nl, noct, S5_OCT, 2, p, S5_L, m)
    wi = jnp.transpose(wi, (0, 1, 5, 2, 6, 3, 4)).reshape(nl, noct, S5_L * LANES, 2 * p)
    win = octet_bd(wi, m, 2, p)

    wo = cc[:, :, :, :, None] * powers(steps + 1)[:, :, None, :, :]
    wo = jnp.stack([wo.real, -wo.imag], axis=2)
    wo = wo.reshape(nl, noct, S5_OCT, 2, m, p, S5_L)
    wo = jnp.transpose(wo, (0, 1, 3, 2, 5, 6, 4)).reshape(nl, noct, 2 * S5_OCT * p, S5_L * m)
    wout = octet_bd(wo, p, S5_L, m)

    def state_mult(e):
        pw = powers(e)
        pw = jnp.transpose(pw, (0, 3, 1, 2)).reshape(nl, e.shape[0], noct, S5_OCT * p)
        pw = jnp.moveaxis(pw, 2, 1)
        return (jnp.concatenate([pw.real, pw.real], axis=-1), jnp.concatenate([-pw.imag, pw.imag], axis=-1))

    nsteps = int(math.log2(SUBLANES))
    ar, ai = state_mult(S5_L * (2 ** jnp.arange(nsteps)))
    lstep = jnp.stack([ar, ai], axis=3).reshape(nl, noct, 2 * nsteps, -1)
    lblk = jnp.stack(state_mult(S5_L * (jnp.arange(SUBLANES) + 1)), axis=2)
    return win, bd, wout, lstep, lblk


RW_L = 64
HIGHEST = lax.Precision.HIGHEST


def _mm(a, b, dims=(((1,), (0,)), ((), ()))):
    return lax.dot_general(a.astype(BF16), b.astype(BF16), dims, preferred_element_type=F32)


def _mm_nt(a, b):
    return _mm(a, b, (((1,), (1,)), ((), ())))


def _mm_tn(a, b):
    return _mm(a, b, (((0,), (0,)), ((), ())))


def _mm_exact(a, b):
    return lax.dot_general(a, b, (((1,), (0,)), ((), ())), precision=HIGHEST, preferred_element_type=F32)


def _head_sum(x, m0):
    s0 = jnp.sum(jnp.where(m0, x, 0.0), axis=-1, keepdims=True)
    s1 = jnp.sum(jnp.where(m0, 0.0, x), axis=-1, keepdims=True)
    return jnp.where(m0, s0, s1)


def _rwkv_kernel(r_ref, k_ref, v_ref, lo_ref, rh_ref, kh_ref, vh_ref, loh_ref, mu_ref, mul_ref, w0_ref, a0_ref,
                 w2_ref, a2_ref, g2_ref, kk_ref, ka_ref, rk_ref, lng_ref, lnb_ref, o_ref, hstate):
    t = pl.program_id(1)
    tm, c = r_ref.shape
    npair = c // LANES
    ll = RW_L

    @pl.when(t == 0)
    def _():
        hstate[...] = jnp.zeros_like(hstate)

    def mix(ref, href, mu):
        p = ref[...].astype(F32)
        prev = _shift_rows(p, jnp.where(t == 0, 0.0, href[...].astype(F32)), 1)
        return p + (prev - p) * mu

    mu = mu_ref[...]
    r = mix(r_ref, rh_ref, mu[0:1, :])
    k = mix(k_ref, kh_ref, mu[1:2, :])
    v = mix(v_ref, vh_ref, mu[2:3, :])
    lo = mix(lo_ref, loh_ref, mul_ref[...])
    wlog = -_softplus(-(w0_ref[...] + _mm(jnp.tanh(lo), w2_ref[...]))) - 0.5
    lw = -jnp.exp(wlog)
    a = _sigmoid(a0_ref[...] + _mm(lo, a2_ref[...]))
    g = _mm(_sigmoid(lo), g2_ref[...])
    kk = k * kk_ref[...]
    k = k * (1.0 + (a - 1.0) * ka_ref[...])

    lane = lax.broadcasted_iota(jnp.int32, (ll, LANES), 1)
    m0 = lane < RWKV_HD
    ri = lax.broadcasted_iota(jnp.int32, (2 * ll, 2 * ll), 0)
    ci = lax.broadcasted_iota(jnp.int32, (2 * ll, 2 * ll), 1)
    same = (ri >= ll) == (ci >= ll)
    strict = same & (ri > ci)
    incl = same & (ri >= ci)
    eye = ri == ci
    tri = (lax.broadcasted_iota(jnp.int32, (ll, ll), 0) >= lax.broadcasted_iota(jnp.int32, (ll, ll), 1)).astype(F32)

    def st(x):
        return jnp.concatenate([jnp.where(m0, x, 0.0), jnp.where(m0, 0.0, x)], axis=0)

    nch = tm // ll
    pcs = [(ch, p) for ch in range(nch) for p in range(npair)]
    cum_all = [_mm_exact(tri, lw[ch * ll:(ch + 1) * ll, :]) for ch in range(nch)]
    pre = {}
    for (ch, p) in pcs:
        rows, cols = slice(ch * ll, (ch + 1) * ll), slice(p * LANES, (p + 1) * LANES)
        rp, kp, vp, ap, lwp, cum = r[rows, cols], k[rows, cols], v[rows, cols], a[rows, cols], lw[rows, cols], cum_all[ch][:, cols]
        kkp = kk[rows, cols]
        kkp = kkp * lax.rsqrt(_head_sum(kkp * kkp, m0) + 1e-12)
        cum_l = cum[ll - 1:ll, :]
        e_neg = jnp.exp(-cum)
        e_last = jnp.exp(cum_l - cum)
        kb = kkp * ap
        stb = lambda x: st(x.astype(BF16))
        pre[ch, p] = dict(
            rows=rows, cols=cols, rp=rp, kp=kp, vp=vp,
            a_st=stb(-kkp * jnp.exp(cum - lwp)), r_st=st(rp * jnp.exp(cum)),
            v_st=stb(vp), b_st=stb(kb * e_neg), k_st=stb(kp * e_neg),
            bh_st=stb(kb * e_last), kh_st=stb(kp * e_last), g_l=jnp.exp(cum_l))

    gram = {pc: _mm_nt(jnp.concatenate([pre[pc]["a_st"], pre[pc]["r_st"].astype(BF16)], axis=0),
                       jnp.concatenate([pre[pc]["b_st"], pre[pc]["k_st"]], axis=0)) for pc in pcs}
    a_ab = {pc: jnp.where(strict, gram[pc][:2 * ll, :2 * ll], 0.0) for pc in pcs}
    a_ak = {pc: jnp.where(strict, gram[pc][:2 * ll, 2 * ll:], 0.0).astype(BF16) for pc in pcs}
    m_rb = {pc: jnp.where(incl, gram[pc][2 * ll:, :2 * ll], 0.0).astype(BF16) for pc in pcs}
    m_rk = {pc: jnp.where(incl, gram[pc][2 * ll:, 2 * ll:], 0.0).astype(BF16) for pc in pcs}
    akv = {pc: _mm(a_ak[pc], pre[pc]["v_st"]) for pc in pcs}
    tinv = {pc: jnp.where(eye, 1.0, a_ab[pc]) for pc in pcs}
    pw = {pc: a_ab[pc].astype(BF16) for pc in pcs}
    for _ in range(int(math.log2(ll)) - 1):
        pw = {pc: _mm(pw[pc], pw[pc]).astype(BF16) for pc in pcs}
        tinv = {pc: tinv[pc] + _mm(tinv[pc], pw[pc]) for pc in pcs}
    xa = {pc: _mm(tinv[pc], jnp.concatenate([pre[pc]["a_st"], akv[pc].astype(BF16)], axis=1)).astype(BF16)
          for pc in pcs}
    ma = {pc: _mm(m_rb[pc], xa[pc][:, :LANES]) for pc in pcs}
    y_loc = {pc: _mm(jnp.concatenate([m_rb[pc], m_rk[pc]], axis=1),
                     jnp.concatenate([xa[pc][:, LANES:], pre[pc]["v_st"]], axis=0)) for pc in pcs}
    ba = {pc: _mm_tn(pre[pc]["bh_st"], xa[pc][:, :LANES]) for pc in pcs}
    f_loc = {pc: _mm_tn(jnp.concatenate([pre[pc]["bh_st"], pre[pc]["kh_st"]], axis=0),
                        jnp.concatenate([xa[pc][:, LANES:], pre[pc]["v_st"]], axis=0)) for pc in pcs}
    for (ch, p) in pcs:
        pc = (ch, p)
        q = pre[pc]
        r_eff = q["r_st"] + ma[pc]
        g_eff = jnp.where(eye, q["g_l"], 0.0) + ba[pc]
        hs = _mm(jnp.concatenate([g_eff, r_eff], axis=0), hstate[p])
        hstate[p] = hs[:2 * ll, :] + f_loc[pc]
        y_st = hs[2 * ll:, :] + y_loc[pc]
        y = y_st[:ll, :] + y_st[ll:, :]
        rows, cols = q["rows"], q["cols"]
        mean = _head_sum(y, m0) * (1.0 / RWKV_HD)
        d = y - mean
        var = _head_sum(d * d, m0) * (1.0 / RWKV_HD)
        yn = d * lax.rsqrt(var + RWKV_LN_EPS) * lng_ref[:, cols] + lnb_ref[:, cols]
        bonus = _head_sum(q["rp"] * q["kp"] * rk_ref[:, cols], m0) * q["vp"]
        o_ref[rows, cols] = ((yn + bonus) * g[rows, cols]).astype(o_ref.dtype)


def _rwkv_branch(z, prm, l):
    b, t, _ = z.shape
    tm = min(8 * RW_L, t)
    hb = tm // HALO
    cr = COL_RW_R // C
    cl = COL_LORA // 256
    npair = C // LANES

    def tile(cb, w):
        return pl.BlockSpec((None, tm, w), lambda i, j: (i, j, cb))

    def halo(cb, w):
        return pl.BlockSpec((None, HALO, w), lambda i, j: (i, jnp.maximum(j * hb - 1, 0), cb))

    return pl.pallas_call(
        _rwkv_kernel,
        grid=(b, t // tm),
        in_specs=[tile(cr, C), tile(cr + 1, C), tile(cr + 2, C), tile(cl, 256),
                  halo(cr, C), halo(cr + 1, C), halo(cr + 2, C), halo(cl, 256)]
                 + [_layer_spec(a, l) for a in prm],
        out_specs=pl.BlockSpec((None, tm, C), lambda i, j: (i, j, 0)),
        out_shape=jax.ShapeDtypeStruct((b, t, C), BF16),
        scratch_shapes=[pltpu.VMEM((npair, LANES, LANES), F32)],
        compiler_params=_cparams(("parallel", "arbitrary")),
        name="rwkv7",
    )(z, z, z, z, z, z, z, z, *prm)


def _prep_rwkv(mu_rkv, mu_w, mu_a, mu_g, w0, w2, a0, a2, g2, k_k, k_a, r_k, lnx_g, lnx_b):
    nl = mu_rkv.shape[0]
    row = lambda x: x.reshape(nl, 1, -1).astype(F32)
    mul = jnp.concatenate([mu_w, mu_a, mu_g], axis=-1)
    zw = lambda n: jnp.zeros((nl, n, C), F32)
    w2p = jnp.concatenate([w2, zw(A_LORA + G_LORA)], axis=1).astype(BF16)
    a2p = jnp.concatenate([zw(W_LORA), a2, zw(G_LORA)], axis=1).astype(BF16)
    g2p = jnp.concatenate([zw(W_LORA + A_LORA), g2], axis=1).astype(BF16)
    return (mu_rkv.astype(F32), row(mul), row(w0), row(a0), w2p, a2p, g2p, row(k_k), row(k_a), row(r_k),
            row(lnx_g), row(lnx_b))


SLOT = LANES
ROT_HALF = QK_ROPE // 2
NOPE_A = SLOT // 2 - ROT_HALF


def _slot_source():
    src = np.full((SLOT,), QK_HD, np.int32)
    src[:ROT_HALF] = QK_NOPE + np.arange(ROT_HALF)
    src[ROT_HALF:SLOT // 2] = np.arange(NOPE_A)
    src[SLOT // 2:SLOT // 2 + ROT_HALF] = QK_NOPE + ROT_HALF + np.arange(ROT_HALF)
    src[SLOT // 2 + ROT_HALF:SLOT // 2 + ROT_HALF + QK_NOPE - NOPE_A] = NOPE_A + np.arange(QK_NOPE - NOPE_A)
    return src


def _to_slot(v):
    pad = jnp.concatenate([v, jnp.zeros(v.shape[:-1] + (1,), v.dtype)], axis=-1)
    return jnp.take(pad, jnp.asarray(_slot_source()), axis=-1)
VT_ROWS = V_HD + 16
FLASH_NB = 4


FAST_BOUND = 40.0


def _mla_prep_kernel(cq_ref, ckv_ref, kr_ref, tab_ref, qn_ref, wq_ref, kvn_ref, wk_ref, wv_ref,
                     q_ref, k_ref, v_ref, v_scr):
    def rms(x, g, n):
        ms = jnp.sum(x * x, axis=-1, keepdims=True) * (1.0 / n)
        return x * lax.rsqrt(ms + EPS) * g

    def norm_rope(x, c_tab, s_tab):
        ms = jnp.sum(x * x, axis=-1, keepdims=True) * (1.0 / QK_HD)
        return (x * c_tab + pltpu.roll(x, SLOT // 2, axis=1) * s_tab) * lax.rsqrt(ms + EPS)

    cq = rms(cq_ref[...].astype(F32), qn_ref[...], Q_LORA)
    ckv = rms(ckv_ref[...].astype(F32), kvn_ref[...], KV_LORA)
    qf = _mm(cq, wq_ref[...])
    kf = _mm(ckv, wk_ref[...])
    v_scr[...] = _mm(ckv, wv_ref[...])
    vt = jnp.transpose(v_scr[...]).astype(BF16)
    ones = jnp.ones((VT_ROWS - V_HD, vt.shape[1]), BF16)
    for h in range(MLA_HEADS):
        v_ref[h * VT_ROWS:h * VT_ROWS + V_HD, :] = vt[h * V_HD:(h + 1) * V_HD, :]
        v_ref[h * VT_ROWS + V_HD:(h + 1) * VT_ROWS, :] = ones
    kr = kr_ref[...].astype(F32)
    cq_tab, sq_tab, ck_tab, sk_tab = tab_ref[0], tab_ref[1], tab_ref[2], tab_ref[3]
    for h in range(MLA_HEADS):
        cols = slice(h * SLOT, (h + 1) * SLOT)
        q_ref[:, cols] = norm_rope(qf[:, cols], cq_tab, sq_tab).astype(BF16)
        k_ref[:, cols] = norm_rope(kf[:, cols] + kr, ck_tab, sk_tab).astype(BF16)


def _flash_kernel(qi_ref, kj_ref, q_ref, k_ref, vt_ref, o_ref, acc, m_scr=None):
    online = m_scr is not None
    i = qi_ref[pl.program_id(1)]
    j = kj_ref[pl.program_id(1)]
    nb, bq, _ = q_ref.shape
    bk = k_ref.shape[1]

    @pl.when(j == 0)
    def _():
        if online:
            m_scr[...] = jnp.full_like(m_scr, -jnp.inf)
        acc[...] = jnp.zeros_like(acc)

    def step(bb, masked):
        if masked:
            causal = (lax.broadcasted_iota(jnp.int32, (bk, bq), 0) <= lax.broadcasted_iota(jnp.int32, (bk, bq), 1))

        def scores(h):
            cols = slice(h * SLOT, (h + 1) * SLOT)
            return _mm_nt(k_ref[bb, :, cols], q_ref[bb, :, cols])

        st_next = scores(0)
        for h in range(MLA_HEADS):
            rows = slice(h * VT_ROWS, (h + 1) * VT_ROWS)
            st = st_next
            if h + 1 < MLA_HEADS:
                st_next = scores(h + 1)
            if masked:
                st = jnp.where(causal, st, -jnp.inf)
            if online:
                m_prev = m_scr[bb, h:h + 1, :]
                m_new = jnp.maximum(m_prev, jnp.max(st, axis=0, keepdims=True))
                alpha = jnp.exp2(m_prev - m_new)
                pt = jnp.exp2(st - m_new)
                m_scr[bb, h:h + 1, :] = m_new
                acc[bb, rows, :] = acc[bb, rows, :] * alpha + _mm(vt_ref[bb, rows, :], pt)
            else:
                acc[bb, rows, :] += _mm(vt_ref[bb, rows, :], jnp.exp2(st))

    def finish(bb):
        outs = []
        for h in range(MLA_HEADS):
            r0 = h * VT_ROWS
            outs.append(acc[bb, r0:r0 + V_HD, :] / acc[bb, r0 + V_HD:r0 + V_HD + 1, :])
        o_ref[bb] = jnp.transpose(jnp.concatenate(outs, axis=0)).astype(o_ref.dtype)

    @pl.when(j < i)
    def _():
        @pl.loop(0, nb)
        def _(bb):
            step(bb, False)

    @pl.when(j == i)
    def _():
        @pl.loop(0, nb)
        def _(bb):
            step(bb, True)
            finish(bb)


def _score_bound(qk_norm_q, qk_norm_k):
    scale = QK_HD ** -0.5 * math.log2(math.e)
    gq = jnp.max(jnp.abs(qk_norm_q.astype(F32)), axis=-1)
    gk = jnp.max(jnp.abs(qk_norm_k.astype(F32)), axis=-1)
    return 1.01 * QK_HD * scale * gq * gk


def _mla_branch(z, prm, rope, score_bound, l):
    b, t, _ = z.shape
    tm = min(2048, t)
    hs = MLA_HEADS * SLOT
    vr = MLA_HEADS * VT_ROWS

    q, k, vt = pl.pallas_call(
        _mla_prep_kernel,
        grid=(b, t // tm),
        in_specs=[pl.BlockSpec((None, tm, Q_LORA), lambda i, j: (i, j, COL_CQ // Q_LORA)),
                  pl.BlockSpec((None, tm, KV_LORA), lambda i, j: (i, j, COL_CKV // KV_LORA)),
                  pl.BlockSpec((None, tm, LANES), lambda i, j: (i, j, COL_KR // LANES)),
                  pl.BlockSpec((None, 4, tm, SLOT), lambda i, j: (l, 0, j, 0))]
                 + [_layer_spec(a, l) for a in prm],
        out_specs=[pl.BlockSpec((None, tm, hs), lambda i, j: (i, j, 0)),
                   pl.BlockSpec((None, tm, hs), lambda i, j: (i, j, 0)),
                   pl.BlockSpec((None, vr, tm), lambda i, j: (i, 0, j))],
        out_shape=[jax.ShapeDtypeStruct((b, t, hs), BF16), jax.ShapeDtypeStruct((b, t, hs), BF16),
                   jax.ShapeDtypeStruct((b, vr, t), BF16)],
        scratch_shapes=[pltpu.VMEM((tm, C), F32)],
        compiler_params=_cparams(("parallel", "parallel")),
        name="mla_prep",
    )(z, z, z, rope, *prm)

    bq = min(512, t)
    nq = t // bq
    tri = [(i, j) for i in range(nq) for j in range(i + 1)]
    qi = jnp.asarray([p[0] for p in tri], jnp.int32)
    kj = jnp.asarray([p[1] for p in tri], jnp.int32)

    nb = math.gcd(b, FLASH_NB)

    def attend(online):
        scratch = [pltpu.VMEM((nb, vr, bq), F32)] + ([pltpu.VMEM((nb, MLA_HEADS, bq), F32)] if online else [])
        return pl.pallas_call(
            _flash_kernel,
            grid_spec=pltpu.PrefetchScalarGridSpec(
                num_scalar_prefetch=2,
                grid=(b // nb, len(tri)),
                in_specs=[pl.BlockSpec((nb, bq, hs), lambda bi, s, qi, kj: (bi, qi[s], 0)),
                          pl.BlockSpec((nb, bq, hs), lambda bi, s, qi, kj: (bi, kj[s], 0)),
                          pl.BlockSpec((nb, vr, bq), lambda bi, s, qi, kj: (bi, 0, kj[s]))],
                out_specs=pl.BlockSpec((nb, bq, C), lambda bi, s, qi, kj: (bi, qi[s], 0)),
                scratch_shapes=scratch),
            out_shape=jax.ShapeDtypeStruct((b, t, C), BF16),
            compiler_params=_cparams(("parallel", "arbitrary")),
            name="mla_attn_online" if online else "mla_attn",
        )(qi, kj, q, k, vt)

    return lax.cond(score_bound <= FAST_BOUND, lambda: attend(False), lambda: attend(True))


def _rope_tables(t, qk_norm_q, qk_norm_k):
    pos = jnp.arange(t, dtype=F32)
    inv_freq = ROPE_THETA ** (-jnp.arange(0, QK_ROPE, 2, dtype=F32) / QK_ROPE)
    ang = pos[:, None] * inv_freq[None, :]
    cos, sin = jnp.cos(ang), jnp.sin(ang)
    z = lambda n: jnp.zeros((t, n), F32)
    one = lambda n: jnp.ones((t, n), F32)
    tail = SLOT // 2 - ROT_HALF
    c_pos = jnp.concatenate([cos, one(NOPE_A), cos, one(QK_NOPE - NOPE_A), z(SLOT - QK_HD)], axis=1)
    s_pos = jnp.concatenate([-sin, z(NOPE_A), sin, z(tail)], axis=1)
    scale = QK_HD ** -0.5 * math.log2(math.e)

    def pair(g, mult):
        g = _to_slot(g.astype(F32)) * mult
        return [g[:, None, :] * c_pos[None], jnp.roll(g, SLOT // 2, axis=-1)[:, None, :] * s_pos[None]]

    return jnp.stack(pair(qk_norm_q, scale) + pair(qk_norm_k, 1.0), axis=1)


def _prep_mla(q_norm, w_uq, kv_norm, w_ukv):
    nl = q_norm.shape[0]
    nh = MLA_HEADS
    wq = _to_slot(w_uq.reshape(nl, Q_LORA, nh, QK_HD)).reshape(nl, Q_LORA, nh * SLOT).astype(BF16)
    wkv = w_ukv.reshape(nl, KV_LORA, nh, QK_NOPE + V_HD)
    wk = jnp.pad(wkv[..., :QK_NOPE], ((0, 0), (0, 0), (0, 0), (0, QK_ROPE)))
    wk = _to_slot(wk).reshape(nl, KV_LORA, nh * SLOT).astype(BF16)
    wv = wkv[..., QK_NOPE:].reshape(nl, KV_LORA, nh * V_HD).astype(BF16)
    return (q_norm.reshape(nl, 1, -1).astype(F32), wq, kv_norm.reshape(nl, 1, -1).astype(F32), wk, wv)


def _merge_kernel(y0_ref, y1_ref, y2_ref, y3_ref, gl_ref, x_ref, wb_ref, wo_ref, o_ref):
    d = x_ref.shape[1]
    merged = None
    for n, y_ref in enumerate((y0_ref, y1_ref, y2_ref, y3_ref)):
        term = _mm(y_ref[...], wb_ref[n]) * _sigmoid(gl_ref[:, n * d:(n + 1) * d].astype(F32))
        merged = term if merged is None else merged + term
    o_ref[...] = x_ref[...] + _mm(merged, wo_ref[...])


def _merge(ys, z2, x2, wb, wo, l):
    n, d = x2.shape
    tm = min(512, n)
    ytile = pl.BlockSpec((tm, C), lambda i: (i, 0))
    return pl.pallas_call(
        _merge_kernel,
        grid=(n // tm,),
        in_specs=[ytile, ytile, ytile, ytile,
                  pl.BlockSpec((tm, 4 * d), lambda i: (i, COL_GATE)),
                  pl.BlockSpec((tm, d), lambda i: (i, 0)),
                  _layer_spec(wb, l), _layer_spec(wo, l)],
        out_specs=pl.BlockSpec((tm, d), lambda i: (i, 0)),
        out_shape=jax.ShapeDtypeStruct((n, d), F32),
        compiler_params=_cparams(("parallel",)),
        name="merge",
    )(*ys, z2, x2, wb, wo)


def _mlp_kernel(x_ref, g_ref, w1_ref, w2_ref, o_ref, h_scr, acc):
    j = pl.program_id(1)

    @pl.when(j == 0)
    def _():
        x = x_ref[...]
        ms = jnp.mean(x * x, axis=-1, keepdims=True)
        h_scr[...] = (x * lax.rsqrt(ms + EPS) * g_ref[...]).astype(BF16)
        acc[...] = x

    a = jnp.maximum(jnp.dot(h_scr[...], w1_ref[...], preferred_element_type=F32), 0.0)
    acc[...] += _mm(a * a, w2_ref[...])

    @pl.when(j == pl.num_programs(1) - 1)
    def _():
        o_ref[...] = acc[...]


def _mlp(x2, g, w1, w2, l):
    n, d = x2.shape
    dff = w1.shape[2]
    tm = min(1024, n)
    tf = 2048
    return pl.pallas_call(
        _mlp_kernel,
        grid=(n // tm, dff // tf),
        in_specs=[pl.BlockSpec((tm, d), lambda i, j: (i, 0)),
                  pl.BlockSpec((None, 1, d), lambda i, j: (l, 0, 0)),
                  pl.BlockSpec((None, d, tf), lambda i, j: (l, 0, j)),
                  pl.BlockSpec((None, tf, d), lambda i, j: (l, j, 0))],
        out_specs=pl.BlockSpec((tm, d), lambda i, j: (i, 0)),
        out_shape=jax.ShapeDtypeStruct((n, d), F32),
        scratch_shapes=[pltpu.VMEM((tm, d), BF16), pltpu.VMEM((tm, d), F32)],
        compiler_params=_cparams(("parallel", "arbitrary")),
        name="mlp",
    )(x2, g, w1, w2)


def _prep_w_in(w_in):
    nl, d, _ = w_in.shape
    gate0 = 6 * C + W_LORA + A_LORA + G_LORA + Q_LORA + KV_LORA + QK_ROPE
    kr0 = gate0 - QK_ROPE
    parts = [
        w_in[:, :, gate0:],
        w_in[:, :, :kr0],
        w_in[:, :, kr0:kr0 + ROT_HALF],
        jnp.zeros((nl, d, NOPE_A), w_in.dtype),
        w_in[:, :, kr0 + ROT_HALF:gate0],
        jnp.zeros((nl, d, D_INP - COL_KR - SLOT // 2 - ROT_HALF), w_in.dtype),
    ]
    return jnp.concatenate(parts, axis=-1).astype(BF16)


def _prep_lru_gate(gate_w):
    nl, _, nh, hd, _ = gate_w.shape
    eye = jnp.eye(nh, dtype=gate_w.dtype)
    dense = jnp.einsum("lghij,hk->lghikj", gate_w, eye).reshape(nl, 2, nh * hd, nh * hd)
    return jnp.concatenate([dense[:, 0], dense[:, 1]], axis=-1).astype(BF16)


def kernel(x, norm_mix, w_in, lru_conv_w, lru_conv_b, lru_gate_w, lru_gate_b, lru_lambda, s5_a_re, s5_a_im, s5_b_re, s5_b_im, s5_c_re, s5_c_im, s5_d, s5_log_dt, s5_w_glu, rwkv_mu_rkv, rwkv_mu_w, rwkv_mu_a, rwkv_mu_g, rwkv_w0, rwkv_w2, rwkv_a0, rwkv_a2, rwkv_g2, rwkv_k_k, rwkv_k_a, rwkv_r_k, rwkv_lnx_g, rwkv_lnx_b, mla_q_norm, mla_w_uq, mla_kv_norm, mla_w_ukv, mla_qk_norm_q, mla_qk_norm_k, w_branch, w_out, norm_mlp, w_ff1, w_ff2):
    b, t, d = x.shape
    depth = w_in.shape[0]
    n = b * t
    row = lambda a: a.reshape(depth, 1, -1).astype(F32)

    w_in_p = _prep_w_in(w_in)
    lru_gw = _prep_lru_gate(lru_gate_w)
    s5_tabs = _prep_s5(s5_a_re, s5_a_im, s5_b_re, s5_b_im, s5_c_re, s5_c_im, s5_log_dt)
    rw_prm = _prep_rwkv(rwkv_mu_rkv, rwkv_mu_w, rwkv_mu_a, rwkv_mu_g, rwkv_w0, rwkv_w2, rwkv_a0, rwkv_a2, rwkv_g2,
                        rwkv_k_k, rwkv_k_a, rwkv_r_k, rwkv_lnx_g, rwkv_lnx_b)
    mla_prm = _prep_mla(mla_q_norm, mla_w_uq, mla_kv_norm, mla_w_ukv)
    rope = _rope_tables(t, mla_qk_norm_q, mla_qk_norm_k)
    score_bound = _score_bound(mla_qk_norm_q, mla_qk_norm_k)
    g_mix, g_mlp = row(norm_mix), row(norm_mlp)
    conv_b, gate_b, lam, s5_dr = row(lru_conv_b), row(lru_gate_b), row(lru_lambda), row(s5_d)
    w_glu, wb, wo = s5_w_glu.astype(BF16), w_branch.astype(BF16), w_out.astype(BF16)
    w1, w2 = w_ff1.astype(BF16), w_ff2.astype(BF16)

    lru_prm = (lru_conv_w.astype(F32), conv_b, lru_gw, gate_b, lam)
    x2 = x.reshape(n, d)
    for l in range(depth):
        z2, u2 = _inproj(x2, g_mix, w_in_p, l)
        z = z2.reshape(b, t, -1)
        y_lru = _lru_branch(z, lru_prm, l)
        y_s5 = _s5_branch(u2.reshape(b, t, C), s5_tabs, s5_dr, w_glu, l)
        y_rw = _rwkv_branch(z, rw_prm, l)
        y_mla = _mla_branch(z, mla_prm, rope, score_bound[l], l)
        ys = [y.reshape(n, C) for y in (y_lru, y_s5, y_rw, y_mla)]
        x2 = _merge(ys, z2, x2, wb, wo, l)
        x2 = _mlp(x2, g_mlp, w1, w2, l)
    return x2.reshape(b, t, d)
```

```python
import math

import jax
import jax.numpy as jnp
import numpy as np
from jax import lax
from jax.experimental import pallas as pl
from jax.experimental.pallas import tpu as pltpu

F32 = jnp.float32
BF16 = jnp.bfloat16

EPS = 1e-6
CONV_W = 4
LRU_C = 8.0
S5_GROUP = 16
RWKV_HD = 64
W_LORA, A_LORA, G_LORA = 64, 64, 128
RWKV_LN_EPS = 64e-5
MLA_HEADS = 8
QK_NOPE, QK_ROPE, V_HD = 64, 32, 64
QK_HD = QK_NOPE + QK_ROPE
Q_LORA, KV_LORA = 256, 128
ROPE_THETA = 10000.0

LANES = 128
SUBLANES = 8
VMEM_LIMIT = 56 * 1024 * 1024

D_MODEL = 1024
C = D_MODEL // 2
COL_GATE = 0
COL_LRU_X = 4 * D_MODEL
COL_LRU_G = COL_LRU_X + C
COL_S5_U = COL_LRU_G + C
COL_RW_R = COL_S5_U + C
COL_RW_K = COL_RW_R + C
COL_RW_V = COL_RW_K + C
COL_LORA = COL_RW_V + C
COL_CQ = COL_LORA + 256
COL_CKV = COL_CQ + Q_LORA
COL_KR = COL_CKV + KV_LORA
D_INP = 8 * D_MODEL


TM_INPROJ = 1024
INPROJ_TN = 2048
TM_LRU = 256
TM_S5 = 4096
TM_S5_GLU = 2048
RW_L = 64
TM_RWKV = 8 * RW_L
TM_MLA_PREP = 2048
BQ_ATTN = 512
FLASH_NB = 4
TM_MERGE = 512
TM_MLP = 1024
TF_MLP = 2048


def _cparams(sem):
    return pltpu.CompilerParams(dimension_semantics=sem, vmem_limit_bytes=VMEM_LIMIT)


def _gelu_tanh(x):
    return 0.5 * x * (1.0 + jnp.tanh(math.sqrt(2.0 / math.pi) * (x + 0.044715 * (x * x * x))))


def _sigmoid(x):
    return 1.0 / (1.0 + jnp.exp(-x))


def _softplus(x):
    return jnp.maximum(x, 0.0) + jnp.log(1.0 + jnp.exp(-jnp.abs(x)))


HALO = 16


def _shift_rows(x, halo, k):
    xe = jnp.concatenate([halo, x], axis=0)
    n = x.shape[0]
    return xe[HALO - k:HALO - k + n, :]


def _layer_spec(arr, l):
    zeros = (0,) * (arr.ndim - 1)
    return pl.BlockSpec((None,) + arr.shape[1:], lambda *idx: (l,) + zeros)


def _inproj_kernel(x_ref, g_ref, w_ref, o_ref, u_ref, h_scr):
    j = pl.program_id(1)

    @pl.when(j == 0)
    def _():
        x = x_ref[...]
        ms = jnp.mean(x * x, axis=-1, keepdims=True)
        h_scr[...] = (x * lax.rsqrt(ms + EPS) * g_ref[...]).astype(BF16)

    y = jnp.dot(h_scr[...], w_ref[...], preferred_element_type=F32)
    o_ref[...] = y.astype(BF16)
    tn = w_ref.shape[1]
    u0 = COL_S5_U % tn

    @pl.when(j == COL_S5_U // tn)
    def _():
        u_ref[...] = y[:, u0:u0 + C]


def _inproj(x2, g, w, l):
    n, d = x2.shape
    dn = w.shape[2]
    tm = min(TM_INPROJ, n)
    tn = INPROJ_TN
    assert COL_S5_U % tn + C <= tn and dn % tn == 0
    return pl.pallas_call(
        _inproj_kernel,
        grid=(n // tm, dn // tn),
        in_specs=[
            pl.BlockSpec((tm, d), lambda i, j: (i, 0)),
            pl.BlockSpec((None, 1, d), lambda i, j: (l, 0, 0)),
            pl.BlockSpec((None, d, tn), lambda i, j: (l, 0, j)),
        ],
        out_specs=[pl.BlockSpec((tm, tn), lambda i, j: (i, j)),
                   pl.BlockSpec((tm, C), lambda i, j: (i, 0))],
        out_shape=[jax.ShapeDtypeStruct((n, dn), BF16), jax.ShapeDtypeStruct((n, C), F32)],
        scratch_shapes=[pltpu.VMEM((tm, d), BF16)],
        compiler_params=_cparams(("parallel", "arbitrary")),
        name="inproj",
    )(x2, g, w)


def _lru_kernel(x_ref, xh_ref, g_ref, cw_ref, cb_ref, gw_ref, gb_ref, lam_ref, o_ref, carry):
    t = pl.program_id(1)
    tm, c = x_ref.shape

    @pl.when(t == 0)
    def _():
        carry[...] = jnp.zeros_like(carry)

    x = x_ref[...].astype(F32)
    halo = jnp.where(t == 0, 0.0, xh_ref[...].astype(F32))
    cw = cw_ref[...]
    xc = cb_ref[...] + cw[CONV_W - 1:CONV_W, :] * x
    for k in range(1, CONV_W):
        xc = xc + cw[CONV_W - 1 - k:CONV_W - k, :] * _shift_rows(x, halo, k)

    gates = jnp.dot(xc.astype(BF16), gw_ref[...], preferred_element_type=F32) + gb_ref[...]
    r = _sigmoid(gates[:, :c])
    i = _sigmoid(gates[:, c:])
    log_a = (-LRU_C) * r * _softplus(-lam_ref[...])
    a = jnp.exp(log_a)
    u = jnp.sqrt(1.0 - a * a) * (i * xc)

    sub = lax.broadcasted_iota(jnp.int32, (tm, c), 0) % SUBLANES
    s = 1
    while s < SUBLANES:
        a_sh = jnp.where(sub >= s, pltpu.roll(a, s, axis=0), 1.0)
        u_sh = jnp.where(sub >= s, pltpu.roll(u, s, axis=0), 0.0)
        u = u + a * u_sh
        a = a * a_sh
        s *= 2
    hp = carry[SUBLANES - 1:SUBLANES, :]
    blocks = []
    for blk in range(tm // SUBLANES):
        rows = slice(blk * SUBLANES, (blk + 1) * SUBLANES)
        hb = u[rows, :] + a[rows, :] * hp
        hp = hb[SUBLANES - 1:SUBLANES, :]
        blocks.append(hb)
    h = jnp.concatenate(blocks, axis=0)
    carry[...] = blocks[-1]
    o_ref[...] = (h * _gelu_tanh(g_ref[...].astype(F32))).astype(o_ref.dtype)


def _lru_branch(z, prm, l):
    b, t, _ = z.shape
    tm = min(TM_LRU, t)
    cb = COL_LRU_X // C
    hb = tm // HALO
    return pl.pallas_call(
        _lru_kernel,
        grid=(b, t // tm),
        in_specs=[
            pl.BlockSpec((None, tm, C), lambda i, j: (i, j, cb)),
            pl.BlockSpec((None, HALO, C), lambda i, j: (i, jnp.maximum(j * hb - 1, 0), cb)),
            pl.BlockSpec((None, tm, C), lambda i, j: (i, j, cb + 1)),
        ] + [_layer_spec(a, l) for a in prm],
        out_specs=pl.BlockSpec((None, tm, C), lambda i, j: (i, j, 0)),
        out_shape=jax.ShapeDtypeStruct((b, t, C), BF16),
        scratch_shapes=[pltpu.VMEM((SUBLANES, C), F32)],
        compiler_params=_cparams(("parallel", "arbitrary")),
        name="rglru",
    )(z, z, z, *prm)


S5_L = 8
S5_OCT = LANES // S5_GROUP


def _s5_kernel(u_ref, win_ref, bd_ref, wout_ref, lstep_ref, lblk_ref, y_ref, uf, wintra, carry):
    t = pl.program_id(2)
    tm = u_ref.shape[0]
    nc = tm // S5_L
    ns = carry.shape[1]
    half = ns // 2

    @pl.when((pl.program_id(1) == 0) & (t == 0))
    def _():
        wintra[...] = jnp.zeros_like(wintra)
        for i in range(S5_L):
            for j in range(i, S5_L):
                wintra[i * LANES:(i + 1) * LANES, j * LANES:(j + 1) * LANES] = bd_ref[j - i]

    @pl.when(t == 0)
    def _():
        carry[...] = jnp.zeros_like(carry)

    for i in range(S5_L):
        uf[:, i * LANES:(i + 1) * LANES] = u_ref[pl.ds(i, nc, stride=S5_L), :].astype(BF16)
    ufv = uf[...]
    s = jnp.dot(ufv, win_ref[...], preferred_element_type=F32)

    def cmul(v, ar, ai):
        sw = jnp.concatenate([v[:, half:], v[:, :half]], axis=1)
        return v * ar + sw * ai

    row = lax.broadcasted_iota(jnp.int32, (nc, ns), 0)
    sub = row % SUBLANES
    k = 0
    st = 1
    while st < SUBLANES:
        sh = jnp.where(sub >= st, pltpu.roll(s, st, axis=0), 0.0)
        s = s + cmul(sh, lstep_ref[2 * k:2 * k + 1, :], lstep_ref[2 * k + 1:2 * k + 2, :])
        k += 1
        st *= 2
    cin = carry[0:1, :]
    prev = cin
    blocks = []
    for blk in range(nc // SUBLANES):
        sb = s[blk * SUBLANES:(blk + 1) * SUBLANES, :] + cmul(jnp.broadcast_to(prev, (SUBLANES, ns)),
                                                              lblk_ref[0], lblk_ref[1])
        prev = sb[SUBLANES - 1:SUBLANES, :]
        blocks.append(sb)
    s = jnp.concatenate(blocks, axis=0)
    carry[...] = jnp.broadcast_to(prev, carry.shape)
    sprev = jnp.where(row >= 1, pltpu.roll(s, 1, axis=0), cin)

    yf = jnp.dot(ufv, wintra[...], preferred_element_type=F32)
    yf = yf + jnp.dot(sprev.astype(BF16), wout_ref[...], preferred_element_type=F32)
    for j in range(S5_L):
        y_ref[pl.ds(j, nc, stride=S5_L), :] = yf[:, j * LANES:(j + 1) * LANES]


def _s5_tail_kernel(y_ref, u_ref, d_ref, w_ref, o_ref):
    c = u_ref.shape[1]
    y = _gelu_tanh(y_ref[...] + d_ref[...] * u_ref[...])
    z = jnp.dot(y.astype(BF16), w_ref[...], preferred_element_type=F32)
    o_ref[...] = (z[:, :c] * _sigmoid(z[:, c:])).astype(o_ref.dtype)


def _s5_tile(t):
    return min(TM_S5, t)


def _s5_branch(u, tabs, d_skip, w_glu, l):
    win, bd, wout, lstep, lblk = tabs
    b, t, _ = u.shape
    tm = _s5_tile(t)
    nc = tm // S5_L
    noct = C // LANES
    ns = win.shape[-1]
    fl = S5_L * LANES

    def table(arr):
        zeros = (0,) * (arr.ndim - 2)
        return pl.BlockSpec((None, None) + arr.shape[2:], lambda q, i, j: (l, q) + zeros)

    y = pl.pallas_call(
        _s5_kernel,
        grid=(noct, b, t // tm),
        in_specs=[pl.BlockSpec((None, tm, LANES), lambda q, i, j: (i, j, q)),
                  table(win), table(bd), table(wout), table(lstep), table(lblk)],
        out_specs=pl.BlockSpec((None, tm, LANES), lambda q, i, j: (i, j, q)),
        out_shape=jax.ShapeDtypeStruct((b, t, C), F32),
        scratch_shapes=[pltpu.VMEM((nc, fl), BF16), pltpu.VMEM((fl, fl), BF16), pltpu.VMEM((SUBLANES, ns), F32)],
        compiler_params=_cparams(("arbitrary", "arbitrary", "arbitrary")),
        name="s5_ssm",
    )(u, win, bd, wout, lstep, lblk)

    tt = min(TM_S5_GLU, t)
    return pl.pallas_call(
        _s5_tail_kernel,
        grid=(b, t // tt),
        in_specs=[
            pl.BlockSpec((None, tt, C), lambda i, j: (i, j, 0)),
            pl.BlockSpec((None, tt, C), lambda i, j: (i, j, 0)),
            _layer_spec(d_skip, l),
            _layer_spec(w_glu, l),
        ],
        out_specs=pl.BlockSpec((None, tt, C), lambda i, j: (i, j, 0)),
        out_shape=jax.ShapeDtypeStruct((b, t, C), BF16),
        compiler_params=_cparams(("parallel", "parallel")),
        name="s5_glu",
    )(y, u, d_skip, w_glu)


def _prep_s5(a_re, a_im, b_re, b_im, c_re, c_im, log_dt):
    nl, g, p = a_re.shape
    m = S5_GROUP
    noct = g // S5_OCT
    lam = lax.complex(a_re.astype(F32), a_im.astype(F32))
    dt = jnp.exp(log_dt.astype(F32))[..., None]
    ldt = lam * dt
    lam_bar = jnp.exp(ldt)
    b_bar = ((lam_bar - 1.0) / lam)[..., None] * lax.complex(b_re.astype(F32), b_im.astype(F32))
    cc = lax.complex(c_re.astype(F32), c_im.astype(F32))

    def powers(e):
        return jnp.exp(ldt[..., None] * e.astype(F32).reshape((1, 1, 1, -1)))

    steps = jnp.arange(S5_L)

    def octet_bd(x, row_inner, col_pre, col_post):
        ncol = col_pre * S5_OCT * col_post
        cidx = np.arange(ncol)
        src = (cidx // (S5_OCT * col_post)) * col_post + cidx % col_post
        expand = jnp.asarray(np.arange(col_pre * col_post)[:, None] == src[None, :], F32)
        ridx = np.arange(x.shape[2])
        keep = jnp.asarray(((ridx // row_inner) % S5_OCT)[:, None] == ((cidx // col_post) % S5_OCT)[None, :])
        return jnp.where(keep, jnp.einsum("lqrc,cn->lqrn", x.astype(BF16), expand.astype(BF16),
                                          preferred_element_type=F32), 0.0).astype(BF16)

    kd = jnp.einsum("lgop,lgpd,lgpi->lgdio", cc, powers(steps), b_bar).real
    kd = kd.reshape(nl, noct, S5_OCT, S5_L, m, m)
    kd = jnp.moveaxis(kd, 3, 2).reshape(nl, noct, S5_L * LANES, m)
    bd = octet_bd(kd, m, 1, m).reshape(nl, noct, S5_L, LANES, LANES)

    wi = powers(S5_L - 1 - steps)[..., None] * b_bar[:, :, :, None, :]
    wi = jnp.stack([wi.real, wi.imag], axis=2)
    wi = wi.reshape(nl, noct, S5_OCT, 2, p, S5_L, m)
    wi = jnp.transpose(wi, (0, 1, 5, 2, 6, 3, 4)).reshape(nl, noct, S5_L * LANES, 2 * p)
    win = octet_bd(wi, m, 2, p)

    wo = cc[:, :, :, :, None] * powers(steps + 1)[:, :, None, :, :]
    wo = jnp.stack([wo.real, -wo.imag], axis=2)
    wo = wo.reshape(nl, noct, S5_OCT, 2, m, p, S5_L)
    wo = jnp.transpose(wo, (0, 1, 3, 2, 5, 6, 4)).reshape(nl, noct, 2 * S5_OCT * p, S5_L * m)
    wout = octet_bd(wo, p, S5_L, m)

    def state_mult(e):
        pw = powers(e)
        pw = jnp.transpose(pw, (0, 3, 1, 2)).reshape(nl, e.shape[0], noct, S5_OCT * p)
        pw = jnp.moveaxis(pw, 2, 1)
        return (jnp.concatenate([pw.real, pw.real], axis=-1), jnp.concatenate([-pw.imag, pw.imag], axis=-1))

    nsteps = int(math.log2(SUBLANES))
    ar, ai = state_mult(S5_L * (2 ** jnp.arange(nsteps)))
    lstep = jnp.stack([ar, ai], axis=3).reshape(nl, noct, 2 * nsteps, -1)
    lblk = jnp.stack(state_mult(S5_L * (jnp.arange(SUBLANES) + 1)), axis=2)
    return win, bd, wout, lstep, lblk


HIGHEST = lax.Precision.HIGHEST


def _mm(a, b, dims=(((1,), (0,)), ((), ()))):
    return lax.dot_general(a.astype(BF16), b.astype(BF16), dims, preferred_element_type=F32)


def _mm_nt(a, b):
    return _mm(a, b, (((1,), (1,)), ((), ())))


def _mm_tn(a, b):
    return _mm(a, b, (((0,), (0,)), ((), ())))


def _mm_exact(a, b):
    return lax.dot_general(a, b, (((1,), (0,)), ((), ())), precision=HIGHEST, preferred_element_type=F32)


def _head_sum(x, m0):
    s0 = jnp.sum(jnp.where(m0, x, 0.0), axis=-1, keepdims=True)
    s1 = jnp.sum(jnp.where(m0, 0.0, x), axis=-1, keepdims=True)
    return jnp.where(m0, s0, s1)


def _rwkv_kernel(r_ref, k_ref, v_ref, lo_ref, rh_ref, kh_ref, vh_ref, loh_ref, mu_ref, mul_ref, w0_ref, a0_ref,
                 w2_ref, a2_ref, g2_ref, kk_ref, ka_ref, rk_ref, lng_ref, lnb_ref, o_ref, hstate):
    t = pl.program_id(1)
    tm, c = r_ref.shape
    npair = c // LANES
    ll = RW_L

    @pl.when(t == 0)
    def _():
        hstate[...] = jnp.zeros_like(hstate)

    def mix(ref, href, mu):
        p = ref[...].astype(F32)
        prev = _shift_rows(p, jnp.where(t == 0, 0.0, href[...].astype(F32)), 1)
        return p + (prev - p) * mu

    mu = mu_ref[...]
    r = mix(r_ref, rh_ref, mu[0:1, :])
    k = mix(k_ref, kh_ref, mu[1:2, :])
    v = mix(v_ref, vh_ref, mu[2:3, :])
    lo = mix(lo_ref, loh_ref, mul_ref[...])
    wlog = -_softplus(-(w0_ref[...] + _mm(jnp.tanh(lo), w2_ref[...]))) - 0.5
    lw = -jnp.exp(wlog)
    a = _sigmoid(a0_ref[...] + _mm(lo, a2_ref[...]))
    g = _mm(_sigmoid(lo), g2_ref[...])
    kk = k * kk_ref[...]
    k = k * (1.0 + (a - 1.0) * ka_ref[...])

    lane = lax.broadcasted_iota(jnp.int32, (ll, LANES), 1)
    m0 = lane < RWKV_HD
    ri = lax.broadcasted_iota(jnp.int32, (2 * ll, 2 * ll), 0)
    ci = lax.broadcasted_iota(jnp.int32, (2 * ll, 2 * ll), 1)
    same = (ri >= ll) == (ci >= ll)
    strict = same & (ri > ci)
    incl = same & (ri >= ci)
    eye = ri == ci
    tri = (lax.broadcasted_iota(jnp.int32, (ll, ll), 0) >= lax.broadcasted_iota(jnp.int32, (ll, ll), 1)).astype(F32)

    def st(x):
        return jnp.concatenate([jnp.where(m0, x, 0.0), jnp.where(m0, 0.0, x)], axis=0)

    nch = tm // ll
    pcs = [(ch, p) for ch in range(nch) for p in range(npair)]
    cum_all = [_mm_exact(tri, lw[ch * ll:(ch + 1) * ll, :]) for ch in range(nch)]
    pre = {}
    for (ch, p) in pcs:
        rows, cols = slice(ch * ll, (ch + 1) * ll), slice(p * LANES, (p + 1) * LANES)
        rp, kp, vp, ap, lwp, cum = r[rows, cols], k[rows, cols], v[rows, cols], a[rows, cols], lw[rows, cols], cum_all[ch][:, cols]
        kkp = kk[rows, cols]
        kkp = kkp * lax.rsqrt(_head_sum(kkp * kkp, m0) + 1e-12)
        cum_l = cum[ll - 1:ll, :]
        e_neg = jnp.exp(-cum)
        e_last = jnp.exp(cum_l - cum)
        kb = kkp * ap
        stb = lambda x: st(x.astype(BF16))
        pre[ch, p] = dict(
            rows=rows, cols=cols, rp=rp, kp=kp, vp=vp,
            a_st=stb(-kkp * jnp.exp(cum - lwp)), r_st=st(rp * jnp.exp(cum)),
            v_st=stb(vp), b_st=stb(kb * e_neg), k_st=stb(kp * e_neg),
            bh_st=stb(kb * e_last), kh_st=stb(kp * e_last), g_l=jnp.exp(cum_l))

    gram = {pc: _mm_nt(jnp.concatenate([pre[pc]["a_st"], pre[pc]["r_st"].astype(BF16)], axis=0),
                       jnp.concatenate([pre[pc]["b_st"], pre[pc]["k_st"]], axis=0)) for pc in pcs}
    a_ab = {pc: jnp.where(strict, gram[pc][:2 * ll, :2 * ll], 0.0) for pc in pcs}
    a_ak = {pc: jnp.where(strict, gram[pc][:2 * ll, 2 * ll:], 0.0).astype(BF16) for pc in pcs}
    m_rb = {pc: jnp.where(incl, gram[pc][2 * ll:, :2 * ll], 0.0).astype(BF16) for pc in pcs}
    m_rk = {pc: jnp.where(incl, gram[pc][2 * ll:, 2 * ll:], 0.0).astype(BF16) for pc in pcs}
    akv = {pc: _mm(a_ak[pc], pre[pc]["v_st"]) for pc in pcs}
    tinv = {pc: jnp.where(eye, 1.0, a_ab[pc]) for pc in pcs}
    pw = {pc: a_ab[pc].astype(BF16) for pc in pcs}
    for _ in range(int(math.log2(ll)) - 1):
        pw = {pc: _mm(pw[pc], pw[pc]).astype(BF16) for pc in pcs}
        tinv = {pc: tinv[pc] + _mm(tinv[pc], pw[pc]) for pc in pcs}
    xa = {pc: _mm(tinv[pc], jnp.concatenate([pre[pc]["a_st"], akv[pc].astype(BF16)], axis=1)).astype(BF16)
          for pc in pcs}
    ma = {pc: _mm(m_rb[pc], xa[pc][:, :LANES]) for pc in pcs}
    y_loc = {pc: _mm(jnp.concatenate([m_rb[pc], m_rk[pc]], axis=1),
                     jnp.concatenate([xa[pc][:, LANES:], pre[pc]["v_st"]], axis=0)) for pc in pcs}
    ba = {pc: _mm_tn(pre[pc]["bh_st"], xa[pc][:, :LANES]) for pc in pcs}
    f_loc = {pc: _mm_tn(jnp.concatenate([pre[pc]["bh_st"], pre[pc]["kh_st"]], axis=0),
                        jnp.concatenate([xa[pc][:, LANES:], pre[pc]["v_st"]], axis=0)) for pc in pcs}
    for (ch, p) in pcs:
        pc = (ch, p)
        q = pre[pc]
        r_eff = q["r_st"] + ma[pc]
        g_eff = jnp.where(eye, q["g_l"], 0.0) + ba[pc]
        hs = _mm(jnp.concatenate([g_eff, r_eff], axis=0), hstate[p])
        hstate[p] = hs[:2 * ll, :] + f_loc[pc]
        y_st = hs[2 * ll:, :] + y_loc[pc]
        y = y_st[:ll, :] + y_st[ll:, :]
        rows, cols = q["rows"], q["cols"]
        mean = _head_sum(y, m0) * (1.0 / RWKV_HD)
        d = y - mean
        var = _head_sum(d * d, m0) * (1.0 / RWKV_HD)
        yn = d * lax.rsqrt(var + RWKV_LN_EPS) * lng_ref[:, cols] + lnb_ref[:, cols]
        bonus = _head_sum(q["rp"] * q["kp"] * rk_ref[:, cols], m0) * q["vp"]
        o_ref[rows, cols] = ((yn + bonus) * g[rows, cols]).astype(o_ref.dtype)


def _rwkv_branch(z, prm, l):
    b, t, _ = z.shape
    tm = min(TM_RWKV, t)
    hb = tm // HALO
    cr = COL_RW_R // C
    cl = COL_LORA // 256
    npair = C // LANES

    def tile(cb, w):
        return pl.BlockSpec((None, tm, w), lambda i, j: (i, j, cb))

    def halo(cb, w):
        return pl.BlockSpec((None, HALO, w), lambda i, j: (i, jnp.maximum(j * hb - 1, 0), cb))

    return pl.pallas_call(
        _rwkv_kernel,
        grid=(b, t // tm),
        in_specs=[tile(cr, C), tile(cr + 1, C), tile(cr + 2, C), tile(cl, 256),
                  halo(cr, C), halo(cr + 1, C), halo(cr + 2, C), halo(cl, 256)]
                 + [_layer_spec(a, l) for a in prm],
        out_specs=pl.BlockSpec((None, tm, C), lambda i, j: (i, j, 0)),
        out_shape=jax.ShapeDtypeStruct((b, t, C), BF16),
        scratch_shapes=[pltpu.VMEM((npair, LANES, LANES), F32)],
        compiler_params=_cparams(("parallel", "arbitrary")),
        name="rwkv7",
    )(z, z, z, z, z, z, z, z, *prm)


def _prep_rwkv(mu_rkv, mu_w, mu_a, mu_g, w0, w2, a0, a2, g2, k_k, k_a, r_k, lnx_g, lnx_b):
    nl = mu_rkv.shape[0]
    row = lambda x: x.reshape(nl, 1, -1).astype(F32)
    mul = jnp.concatenate([mu_w, mu_a, mu_g], axis=-1)
    zw = lambda n: jnp.zeros((nl, n, C), F32)
    w2p = jnp.concatenate([w2, zw(A_LORA + G_LORA)], axis=1).astype(BF16)
    a2p = jnp.concatenate([zw(W_LORA), a2, zw(G_LORA)], axis=1).astype(BF16)
    g2p = jnp.concatenate([zw(W_LORA + A_LORA), g2], axis=1).astype(BF16)
    return (mu_rkv.astype(F32), row(mul), row(w0), row(a0), w2p, a2p, g2p, row(k_k), row(k_a), row(r_k),
            row(lnx_g), row(lnx_b))


SLOT = LANES
ROT_HALF = QK_ROPE // 2
NOPE_A = SLOT // 2 - ROT_HALF


def _slot_source():
    src = np.full((SLOT,), QK_HD, np.int32)
    src[:ROT_HALF] = QK_NOPE + np.arange(ROT_HALF)
    src[ROT_HALF:SLOT // 2] = np.arange(NOPE_A)
    src[SLOT // 2:SLOT // 2 + ROT_HALF] = QK_NOPE + ROT_HALF + np.arange(ROT_HALF)
    src[SLOT // 2 + ROT_HALF:SLOT // 2 + ROT_HALF + QK_NOPE - NOPE_A] = NOPE_A + np.arange(QK_NOPE - NOPE_A)
    return src


def _to_slot(v):
    pad = jnp.concatenate([v, jnp.zeros(v.shape[:-1] + (1,), v.dtype)], axis=-1)
    return jnp.take(pad, jnp.asarray(_slot_source()), axis=-1)
VT_ROWS = V_HD + 16


FAST_BOUND = 40.0


def _mla_prep_kernel(cq_ref, ckv_ref, kr_ref, tab_ref, qn_ref, wq_ref, kvn_ref, wk_ref, wv_ref,
                     q_ref, k_ref, v_ref, v_scr):
    def rms(x, g, n):
        ms = jnp.sum(x * x, axis=-1, keepdims=True) * (1.0 / n)
        return x * lax.rsqrt(ms + EPS) * g

    def norm_rope(x, c_tab, s_tab):
        ms = jnp.sum(x * x, axis=-1, keepdims=True) * (1.0 / QK_HD)
        return (x * c_tab + pltpu.roll(x, SLOT // 2, axis=1) * s_tab) * lax.rsqrt(ms + EPS)

    cq = rms(cq_ref[...].astype(F32), qn_ref[...], Q_LORA)
    ckv = rms(ckv_ref[...].astype(F32), kvn_ref[...], KV_LORA)
    qf = _mm(cq, wq_ref[...])
    kf = _mm(ckv, wk_ref[...])
    v_scr[...] = _mm(ckv, wv_ref[...])
    vt = jnp.transpose(v_scr[...]).astype(BF16)
    ones = jnp.ones((VT_ROWS - V_HD, vt.shape[1]), BF16)
    for h in range(MLA_HEADS):
        v_ref[h * VT_ROWS:h * VT_ROWS + V_HD, :] = vt[h * V_HD:(h + 1) * V_HD, :]
        v_ref[h * VT_ROWS + V_HD:(h + 1) * VT_ROWS, :] = ones
    kr = kr_ref[...].astype(F32)
    cq_tab, sq_tab, ck_tab, sk_tab = tab_ref[0], tab_ref[1], tab_ref[2], tab_ref[3]
    for h in range(MLA_HEADS):
        cols = slice(h * SLOT, (h + 1) * SLOT)
        q_ref[:, cols] = norm_rope(qf[:, cols], cq_tab, sq_tab).astype(BF16)
        k_ref[:, cols] = norm_rope(kf[:, cols] + kr, ck_tab, sk_tab).astype(BF16)


def _flash_kernel(qi_ref, kj_ref, q_ref, k_ref, vt_ref, o_ref, acc, m_scr=None):
    online = m_scr is not None
    i = qi_ref[pl.program_id(1)]
    j = kj_ref[pl.program_id(1)]
    nb, bq, _ = q_ref.shape
    bk = k_ref.shape[1]

    @pl.when(j == 0)
    def _():
        if online:
            m_scr[...] = jnp.full_like(m_scr, -jnp.inf)
        acc[...] = jnp.zeros_like(acc)

    def step(bb, masked):
        if masked:
            causal = (lax.broadcasted_iota(jnp.int32, (bk, bq), 0) <= lax.broadcasted_iota(jnp.int32, (bk, bq), 1))

        def scores(h):
            cols = slice(h * SLOT, (h + 1) * SLOT)
            return _mm_nt(k_ref[bb, :, cols], q_ref[bb, :, cols])

        st_next = scores(0)
        for h in range(MLA_HEADS):
            rows = slice(h * VT_ROWS, (h + 1) * VT_ROWS)
            st = st_next
            if h + 1 < MLA_HEADS:
                st_next = scores(h + 1)
            if masked:
                st = jnp.where(causal, st, -jnp.inf)
            if online:
                m_prev = m_scr[bb, h:h + 1, :]
                m_new = jnp.maximum(m_prev, jnp.max(st, axis=0, keepdims=True))
                alpha = jnp.exp2(m_prev - m_new)
                pt = jnp.exp2(st - m_new)
                m_scr[bb, h:h + 1, :] = m_new
                acc[bb, rows, :] = acc[bb, rows, :] * alpha + _mm(vt_ref[bb, rows, :], pt)
            else:
                acc[bb, rows, :] += _mm(vt_ref[bb, rows, :], jnp.exp2(st))

    def finish(bb):
        outs = []
        for h in range(MLA_HEADS):
            r0 = h * VT_ROWS
            outs.append(acc[bb, r0:r0 + V_HD, :] / acc[bb, r0 + V_HD:r0 + V_HD + 1, :])
        o_ref[bb] = jnp.transpose(jnp.concatenate(outs, axis=0)).astype(o_ref.dtype)

    @pl.when(j < i)
    def _():
        @pl.loop(0, nb)
        def _(bb):
            step(bb, False)

    @pl.when(j == i)
    def _():
        @pl.loop(0, nb)
        def _(bb):
            step(bb, True)
            finish(bb)


def _score_bound(qk_norm_q, qk_norm_k):
    scale = QK_HD ** -0.5 * math.log2(math.e)
    gq = jnp.max(jnp.abs(qk_norm_q.astype(F32)), axis=-1)
    gk = jnp.max(jnp.abs(qk_norm_k.astype(F32)), axis=-1)
    return 1.01 * QK_HD * scale * gq * gk


def _mla_branch(z, prm, rope, score_bound, l):
    b, t, _ = z.shape
    tm = min(TM_MLA_PREP, t)
    hs = MLA_HEADS * SLOT
    vr = MLA_HEADS * VT_ROWS

    q, k, vt = pl.pallas_call(
        _mla_prep_kernel,
        grid=(b, t // tm),
        in_specs=[pl.BlockSpec((None, tm, Q_LORA), lambda i, j: (i, j, COL_CQ // Q_LORA)),
                  pl.BlockSpec((None, tm, KV_LORA), lambda i, j: (i, j, COL_CKV // KV_LORA)),
                  pl.BlockSpec((None, tm, LANES), lambda i, j: (i, j, COL_KR // LANES)),
                  pl.BlockSpec((None, 4, tm, SLOT), lambda i, j: (l, 0, j, 0))]
                 + [_layer_spec(a, l) for a in prm],
        out_specs=[pl.BlockSpec((None, tm, hs), lambda i, j: (i, j, 0)),
                   pl.BlockSpec((None, tm, hs), lambda i, j: (i, j, 0)),
                   pl.BlockSpec((None, vr, tm), lambda i, j: (i, 0, j))],
        out_shape=[jax.ShapeDtypeStruct((b, t, hs), BF16), jax.ShapeDtypeStruct((b, t, hs), BF16),
                   jax.ShapeDtypeStruct((b, vr, t), BF16)],
        scratch_shapes=[pltpu.VMEM((tm, C), F32)],
        compiler_params=_cparams(("parallel", "parallel")),
        name="mla_prep",
    )(z, z, z, rope, *prm)

    bq = min(BQ_ATTN, t)
    nq = t // bq
    tri = [(i, j) for i in range(nq) for j in range(i + 1)]
    qi = jnp.asarray([p[0] for p in tri], jnp.int32)
    kj = jnp.asarray([p[1] for p in tri], jnp.int32)

    nb = math.gcd(b, FLASH_NB)

    def attend(online):
        scratch = [pltpu.VMEM((nb, vr, bq), F32)] + ([pltpu.VMEM((nb, MLA_HEADS, bq), F32)] if online else [])
        return pl.pallas_call(
            _flash_kernel,
            grid_spec=pltpu.PrefetchScalarGridSpec(
                num_scalar_prefetch=2,
                grid=(b // nb, len(tri)),
                in_specs=[pl.BlockSpec((nb, bq, hs), lambda bi, s, qi, kj: (bi, qi[s], 0)),
                          pl.BlockSpec((nb, bq, hs), lambda bi, s, qi, kj: (bi, kj[s], 0)),
                          pl.BlockSpec((nb, vr, bq), lambda bi, s, qi, kj: (bi, 0, kj[s]))],
                out_specs=pl.BlockSpec((nb, bq, C), lambda bi, s, qi, kj: (bi, qi[s], 0)),
                scratch_shapes=scratch),
            out_shape=jax.ShapeDtypeStruct((b, t, C), BF16),
            compiler_params=_cparams(("parallel", "arbitrary")),
            name="mla_attn_online" if online else "mla_attn",
        )(qi, kj, q, k, vt)

    return lax.cond(score_bound <= FAST_BOUND, lambda: attend(False), lambda: attend(True))


def _rope_tables(t, qk_norm_q, qk_norm_k):
    pos = jnp.arange(t, dtype=F32)
    inv_freq = ROPE_THETA ** (-jnp.arange(0, QK_ROPE, 2, dtype=F32) / QK_ROPE)
    ang = pos[:, None] * inv_freq[None, :]
    cos, sin = jnp.cos(ang), jnp.sin(ang)
    z = lambda n: jnp.zeros((t, n), F32)
    one = lambda n: jnp.ones((t, n), F32)
    tail = SLOT // 2 - ROT_HALF
    c_pos = jnp.concatenate([cos, one(NOPE_A), cos, one(QK_NOPE - NOPE_A), z(SLOT - QK_HD)], axis=1)
    s_pos = jnp.concatenate([-sin, z(NOPE_A), sin, z(tail)], axis=1)
    scale = QK_HD ** -0.5 * math.log2(math.e)

    def pair(g, mult):
        g = _to_slot(g.astype(F32)) * mult
        return [g[:, None, :] * c_pos[None], jnp.roll(g, SLOT // 2, axis=-1)[:, None, :] * s_pos[None]]

    return jnp.stack(pair(qk_norm_q, scale) + pair(qk_norm_k, 1.0), axis=1)


def _prep_mla(q_norm, w_uq, kv_norm, w_ukv):
    nl = q_norm.shape[0]
    nh = MLA_HEADS
    wq = _to_slot(w_uq.reshape(nl, Q_LORA, nh, QK_HD)).reshape(nl, Q_LORA, nh * SLOT).astype(BF16)
    wkv = w_ukv.reshape(nl, KV_LORA, nh, QK_NOPE + V_HD)
    wk = jnp.pad(wkv[..., :QK_NOPE], ((0, 0), (0, 0), (0, 0), (0, QK_ROPE)))
    wk = _to_slot(wk).reshape(nl, KV_LORA, nh * SLOT).astype(BF16)
    wv = wkv[..., QK_NOPE:].reshape(nl, KV_LORA, nh * V_HD).astype(BF16)
    return (q_norm.reshape(nl, 1, -1).astype(F32), wq, kv_norm.reshape(nl, 1, -1).astype(F32), wk, wv)


def _merge_kernel(y0_ref, y1_ref, y2_ref, y3_ref, gl_ref, x_ref, wb_ref, wo_ref, o_ref):
    d = x_ref.shape[1]
    merged = None
    for n, y_ref in enumerate((y0_ref, y1_ref, y2_ref, y3_ref)):
        term = _mm(y_ref[...], wb_ref[n]) * _sigmoid(gl_ref[:, n * d:(n + 1) * d].astype(F32))
        merged = term if merged is None else merged + term
    o_ref[...] = x_ref[...] + _mm(merged, wo_ref[...])


def _merge(ys, z2, x2, wb, wo, l):
    n, d = x2.shape
    tm = min(TM_MERGE, n)
    ytile = pl.BlockSpec((tm, C), lambda i: (i, 0))
    return pl.pallas_call(
        _merge_kernel,
        grid=(n // tm,),
        in_specs=[ytile, ytile, ytile, ytile,
                  pl.BlockSpec((tm, 4 * d), lambda i: (i, COL_GATE)),
                  pl.BlockSpec((tm, d), lambda i: (i, 0)),
                  _layer_spec(wb, l), _layer_spec(wo, l)],
        out_specs=pl.BlockSpec((tm, d), lambda i: (i, 0)),
        out_shape=jax.ShapeDtypeStruct((n, d), F32),
        compiler_params=_cparams(("parallel",)),
        name="merge",
    )(*ys, z2, x2, wb, wo)


def _mlp_kernel(x_ref, g_ref, w1_ref, w2_ref, o_ref, h_scr, acc):
    j = pl.program_id(1)

    @pl.when(j == 0)
    def _():
        x = x_ref[...]
        ms = jnp.mean(x * x, axis=-1, keepdims=True)
        h_scr[...] = (x * lax.rsqrt(ms + EPS) * g_ref[...]).astype(BF16)
        acc[...] = x

    a = jnp.maximum(jnp.dot(h_scr[...], w1_ref[...], preferred_element_type=F32), 0.0)
    acc[...] += _mm(a * a, w2_ref[...])

    @pl.when(j == pl.num_programs(1) - 1)
    def _():
        o_ref[...] = acc[...]


def _mlp(x2, g, w1, w2, l):
    n, d = x2.shape
    dff = w1.shape[2]
    tm = min(TM_MLP, n)
    tf = TF_MLP
    return pl.pallas_call(
        _mlp_kernel,
        grid=(n // tm, dff // tf),
        in_specs=[pl.BlockSpec((tm, d), lambda i, j: (i, 0)),
                  pl.BlockSpec((None, 1, d), lambda i, j: (l, 0, 0)),
                  pl.BlockSpec((None, d, tf), lambda i, j: (l, 0, j)),
                  pl.BlockSpec((None, tf, d), lambda i, j: (l, j, 0))],
        out_specs=pl.BlockSpec((tm, d), lambda i, j: (i, 0)),
        out_shape=jax.ShapeDtypeStruct((n, d), F32),
        scratch_shapes=[pltpu.VMEM((tm, d), BF16), pltpu.VMEM((tm, d), F32)],
        compiler_params=_cparams(("parallel", "arbitrary")),
        name="mlp",
    )(x2, g, w1, w2)


def _prep_w_in(w_in):
    nl, d, _ = w_in.shape
    gate0 = 6 * C + W_LORA + A_LORA + G_LORA + Q_LORA + KV_LORA + QK_ROPE
    kr0 = gate0 - QK_ROPE
    parts = [
        w_in[:, :, gate0:],
        w_in[:, :, :kr0],
        w_in[:, :, kr0:kr0 + ROT_HALF],
        jnp.zeros((nl, d, NOPE_A), w_in.dtype),
        w_in[:, :, kr0 + ROT_HALF:gate0],
        jnp.zeros((nl, d, D_INP - COL_KR - SLOT // 2 - ROT_HALF), w_in.dtype),
    ]
    return jnp.concatenate(parts, axis=-1).astype(BF16)


def _prep_lru_gate(gate_w):
    nl, _, nh, hd, _ = gate_w.shape
    eye = jnp.eye(nh, dtype=gate_w.dtype)
    dense = jnp.einsum("lghij,hk->lghikj", gate_w, eye).reshape(nl, 2, nh * hd, nh * hd)
    return jnp.concatenate([dense[:, 0], dense[:, 1]], axis=-1).astype(BF16)


def kernel(x, norm_mix, w_in, lru_conv_w, lru_conv_b, lru_gate_w, lru_gate_b, lru_lambda, s5_a_re, s5_a_im, s5_b_re, s5_b_im, s5_c_re, s5_c_im, s5_d, s5_log_dt, s5_w_glu, rwkv_mu_rkv, rwkv_mu_w, rwkv_mu_a, rwkv_mu_g, rwkv_w0, rwkv_w2, rwkv_a0, rwkv_a2, rwkv_g2, rwkv_k_k, rwkv_k_a, rwkv_r_k, rwkv_lnx_g, rwkv_lnx_b, mla_q_norm, mla_w_uq, mla_kv_norm, mla_w_ukv, mla_qk_norm_q, mla_qk_norm_k, w_branch, w_out, norm_mlp, w_ff1, w_ff2):
    b, t, d = x.shape
    depth = w_in.shape[0]
    n = b * t
    row = lambda a: a.reshape(depth, 1, -1).astype(F32)

    w_in_p = _prep_w_in(w_in)
    lru_gw = _prep_lru_gate(lru_gate_w)
    s5_tabs = _prep_s5(s5_a_re, s5_a_im, s5_b_re, s5_b_im, s5_c_re, s5_c_im, s5_log_dt)
    rw_prm = _prep_rwkv(rwkv_mu_rkv, rwkv_mu_w, rwkv_mu_a, rwkv_mu_g, rwkv_w0, rwkv_w2, rwkv_a0, rwkv_a2, rwkv_g2,
                        rwkv_k_k, rwkv_k_a, rwkv_r_k, rwkv_lnx_g, rwkv_lnx_b)
    mla_prm = _prep_mla(mla_q_norm, mla_w_uq, mla_kv_norm, mla_w_ukv)
    rope = _rope_tables(t, mla_qk_norm_q, mla_qk_norm_k)
    score_bound = _score_bound(mla_qk_norm_q, mla_qk_norm_k)
    g_mix, g_mlp = row(norm_mix), row(norm_mlp)
    conv_b, gate_b, lam, s5_dr = row(lru_conv_b), row(lru_gate_b), row(lru_lambda), row(s5_d)
    w_glu, wb, wo = s5_w_glu.astype(BF16), w_branch.astype(BF16), w_out.astype(BF16)
    w1, w2 = w_ff1.astype(BF16), w_ff2.astype(BF16)

    lru_prm = (lru_conv_w.astype(F32), conv_b, lru_gw, gate_b, lam)
    x2 = x.reshape(n, d)
    for l in range(depth):
        z2, u2 = _inproj(x2, g_mix, w_in_p, l)
        z = z2.reshape(b, t, -1)
        y_lru = _lru_branch(z, lru_prm, l)
        y_s5 = _s5_branch(u2.reshape(b, t, C), s5_tabs, s5_dr, w_glu, l)
        y_rw = _rwkv_branch(z, rw_prm, l)
        y_mla = _mla_branch(z, mla_prm, rope, score_bound[l], l)
        ys = [y.reshape(n, C) for y in (y_lru, y_s5, y_rw, y_mla)]
        x2 = _merge(ys, z2, x2, wb, wo, l)
        x2 = _mlp(x2, g_mlp, w1, w2, l)
    return x2.reshape(b, t, d)
```

```python
import math

import jax
import jax.numpy as jnp
import numpy as np
from jax import lax
from jax.experimental import pallas as pl
from jax.experimental.pallas import tpu as pltpu

F32 = jnp.float32
BF16 = jnp.bfloat16

EPS = 1e-6
CONV_W = 4
LRU_C = 8.0
S5_GROUP = 16
RWKV_HD = 64
W_LORA, A_LORA, G_LORA = 64, 64, 128
RWKV_LN_EPS = 64e-5
MLA_HEADS = 8
QK_NOPE, QK_ROPE, V_HD = 64, 32, 64
QK_HD = QK_NOPE + QK_ROPE
Q_LORA, KV_LORA = 256, 128
ROPE_THETA = 10000.0

LANES = 128
SUBLANES = 8
VMEM_LIMIT = 56 * 1024 * 1024

D_MODEL = 1024
C = D_MODEL // 2
COL_GATE = 0
COL_LRU_X = 4 * D_MODEL
COL_LRU_G = COL_LRU_X + C
COL_S5_U = COL_LRU_G + C
COL_RW_R = COL_S5_U + C
COL_RW_K = COL_RW_R + C
COL_RW_V = COL_RW_K + C
COL_LORA = COL_RW_V + C
COL_CQ = COL_LORA + 256
COL_CKV = COL_CQ + Q_LORA
COL_KR = COL_CKV + KV_LORA
D_INP = 8 * D_MODEL


TM_INPROJ = 1024
INPROJ_TN = 2048
TM_LRU = 256
TM_S5 = 4096
TM_S5_GLU = 2048
RW_L = 64
TM_RWKV = 8 * RW_L
TM_MLA_PREP = 2048
BQ_ATTN = 512
FLASH_NB = 4
TM_MERGE = 512
TM_MLP = 1024
TF_MLP = 2048


def _cparams(sem):
    return pltpu.CompilerParams(dimension_semantics=sem, vmem_limit_bytes=VMEM_LIMIT)


def _gelu_tanh(x):
    return 0.5 * x * (1.0 + jnp.tanh(math.sqrt(2.0 / math.pi) * (x + 0.044715 * (x * x * x))))


def _sigmoid(x):
    return 0.5 + 0.5 * jnp.tanh(0.5 * x)


def _softplus(x):
    return jnp.maximum(x, 0.0) + jnp.log(1.0 + jnp.exp(-jnp.abs(x)))


HALO = 16


def _shift_rows(x, halo, k):
    xe = jnp.concatenate([halo, x], axis=0)
    n = x.shape[0]
    return xe[HALO - k:HALO - k + n, :]


def _layer_spec(arr, l):
    zeros = (0,) * (arr.ndim - 1)
    return pl.BlockSpec((None,) + arr.shape[1:], lambda *idx: (l,) + zeros)


def _inproj_kernel(x_ref, g_ref, w_ref, o_ref, u_ref, h_scr):
    j = pl.program_id(1)

    @pl.when(j == 0)
    def _():
        x = x_ref[...]
        ms = jnp.mean(x * x, axis=-1, keepdims=True)
        h_scr[...] = (x * lax.rsqrt(ms + EPS) * g_ref[...]).astype(BF16)

    y = jnp.dot(h_scr[...], w_ref[...], preferred_element_type=F32)
    o_ref[...] = y.astype(BF16)
    tn = w_ref.shape[1]
    u0 = COL_S5_U % tn

    @pl.when(j == COL_S5_U // tn)
    def _():
        u_ref[...] = y[:, u0:u0 + C]


def _inproj(x2, g, w, l):
    n, d = x2.shape
    dn = w.shape[2]
    tm = min(TM_INPROJ, n)
    tn = INPROJ_TN
    assert COL_S5_U % tn + C <= tn and dn % tn == 0
    return pl.pallas_call(
        _inproj_kernel,
        grid=(n // tm, dn // tn),
        in_specs=[
            pl.BlockSpec((tm, d), lambda i, j: (i, 0)),
            pl.BlockSpec((None, 1, d), lambda i, j: (l, 0, 0)),
            pl.BlockSpec((None, d, tn), lambda i, j: (l, 0, j)),
        ],
        out_specs=[pl.BlockSpec((tm, tn), lambda i, j: (i, j)),
                   pl.BlockSpec((tm, C), lambda i, j: (i, 0))],
        out_shape=[jax.ShapeDtypeStruct((n, dn), BF16), jax.ShapeDtypeStruct((n, C), F32)],
        scratch_shapes=[pltpu.VMEM((tm, d), BF16)],
        compiler_params=_cparams(("parallel", "arbitrary")),
        name="inproj",
    )(x2, g, w)


def _lru_kernel(x_ref, xh_ref, g_ref, cw_ref, cb_ref, gw_ref, gb_ref, lam_ref, o_ref, carry):
    t = pl.program_id(1)
    tm, c = x_ref.shape

    @pl.when(t == 0)
    def _():
        carry[...] = jnp.zeros_like(carry)

    x = x_ref[...].astype(F32)
    halo = jnp.where(t == 0, 0.0, xh_ref[...].astype(F32))
    cw = cw_ref[...]
    xc = cb_ref[...] + cw[CONV_W - 1:CONV_W, :] * x
    for k in range(1, CONV_W):
        xc = xc + cw[CONV_W - 1 - k:CONV_W - k, :] * _shift_rows(x, halo, k)

    gates = jnp.dot(xc.astype(BF16), gw_ref[...], preferred_element_type=F32) + gb_ref[...]
    r = _sigmoid(gates[:, :c])
    i = _sigmoid(gates[:, c:])
    log_a = (-LRU_C) * r * _softplus(-lam_ref[...])
    a = jnp.exp(log_a)
    u = jnp.sqrt(1.0 - a * a) * (i * xc)

    sub = lax.broadcasted_iota(jnp.int32, (tm, c), 0) % SUBLANES
    s = 1
    while s < SUBLANES:
        a_sh = jnp.where(sub >= s, pltpu.roll(a, s, axis=0), 1.0)
        u_sh = jnp.where(sub >= s, pltpu.roll(u, s, axis=0), 0.0)
        u = u + a * u_sh
        a = a * a_sh
        s *= 2
    hp = carry[SUBLANES - 1:SUBLANES, :]
    blocks = []
    for blk in range(tm // SUBLANES):
        rows = slice(blk * SUBLANES, (blk + 1) * SUBLANES)
        hb = u[rows, :] + a[rows, :] * hp
        hp = hb[SUBLANES - 1:SUBLANES, :]
        blocks.append(hb)
    h = jnp.concatenate(blocks, axis=0)
    carry[...] = blocks[-1]
    o_ref[...] = (h * _gelu_tanh(g_ref[...].astype(F32))).astype(o_ref.dtype)


def _lru_branch(z, prm, l):
    b, t, _ = z.shape
    tm = min(TM_LRU, t)
    cb = COL_LRU_X // C
    hb = tm // HALO
    return pl.pallas_call(
        _lru_kernel,
        grid=(b, t // tm),
        in_specs=[
            pl.BlockSpec((None, tm, C), lambda i, j: (i, j, cb)),
            pl.BlockSpec((None, HALO, C), lambda i, j: (i, jnp.maximum(j * hb - 1, 0), cb)),
            pl.BlockSpec((None, tm, C), lambda i, j: (i, j, cb + 1)),
        ] + [_layer_spec(a, l) for a in prm],
        out_specs=pl.BlockSpec((None, tm, C), lambda i, j: (i, j, 0)),
        out_shape=jax.ShapeDtypeStruct((b, t, C), BF16),
        scratch_shapes=[pltpu.VMEM((SUBLANES, C), F32)],
        compiler_params=_cparams(("parallel", "arbitrary")),
        name="rglru",
    )(z, z, z, *prm)


S5_L = 8
S5_OCT = LANES // S5_GROUP


def _s5_kernel(u_ref, win_ref, bd_ref, wout_ref, lstep_ref, lblk_ref, y_ref, uf, wintra, carry):
    t = pl.program_id(2)
    tm = u_ref.shape[0]
    nc = tm // S5_L
    ns = carry.shape[1]
    half = ns // 2

    @pl.when((pl.program_id(1) == 0) & (t == 0))
    def _():
        wintra[...] = jnp.zeros_like(wintra)
        for i in range(S5_L):
            for j in range(i, S5_L):
                wintra[i * LANES:(i + 1) * LANES, j * LANES:(j + 1) * LANES] = bd_ref[j - i]

    @pl.when(t == 0)
    def _():
        carry[...] = jnp.zeros_like(carry)

    for i in range(S5_L):
        uf[:, i * LANES:(i + 1) * LANES] = u_ref[pl.ds(i, nc, stride=S5_L), :].astype(BF16)
    ufv = uf[...]
    s = jnp.dot(ufv, win_ref[...], preferred_element_type=F32)

    def cmul(v, ar, ai):
        sw = jnp.concatenate([v[:, half:], v[:, :half]], axis=1)
        return v * ar + sw * ai

    row = lax.broadcasted_iota(jnp.int32, (nc, ns), 0)
    sub = row % SUBLANES
    k = 0
    st = 1
    while st < SUBLANES:
        sh = jnp.where(sub >= st, pltpu.roll(s, st, axis=0), 0.0)
        s = s + cmul(sh, lstep_ref[2 * k:2 * k + 1, :], lstep_ref[2 * k + 1:2 * k + 2, :])
        k += 1
        st *= 2
    cin = carry[0:1, :]
    prev = cin
    blocks = []
    for blk in range(nc // SUBLANES):
        sb = s[blk * SUBLANES:(blk + 1) * SUBLANES, :] + cmul(jnp.broadcast_to(prev, (SUBLANES, ns)),
                                                              lblk_ref[0], lblk_ref[1])
        prev = sb[SUBLANES - 1:SUBLANES, :]
        blocks.append(sb)
    s = jnp.concatenate(blocks, axis=0)
    carry[...] = jnp.broadcast_to(prev, carry.shape)
    sprev = jnp.where(row >= 1, pltpu.roll(s, 1, axis=0), cin)

    yf = jnp.dot(ufv, wintra[...], preferred_element_type=F32)
    yf = yf + jnp.dot(sprev.astype(BF16), wout_ref[...], preferred_element_type=F32)
    for j in range(S5_L):
        y_ref[pl.ds(j, nc, stride=S5_L), :] = yf[:, j * LANES:(j + 1) * LANES]


def _s5_tail_kernel(y_ref, u_ref, d_ref, w_ref, o_ref):
    c = u_ref.shape[1]
    y = _gelu_tanh(y_ref[...] + d_ref[...] * u_ref[...])
    z = jnp.dot(y.astype(BF16), w_ref[...], preferred_element_type=F32)
    o_ref[...] = (z[:, :c] * _sigmoid(z[:, c:])).astype(o_ref.dtype)


def _s5_tile(t):
    return min(TM_S5, t)


def _s5_branch(u, tabs, d_skip, w_glu, l):
    win, bd, wout, lstep, lblk = tabs
    b, t, _ = u.shape
    tm = _s5_tile(t)
    nc = tm // S5_L
    noct = C // LANES
    ns = win.shape[-1]
    fl = S5_L * LANES

    def table(arr):
        zeros = (0,) * (arr.ndim - 2)
        return pl.BlockSpec((None, None) + arr.shape[2:], lambda q, i, j: (l, q) + zeros)

    y = pl.pallas_call(
        _s5_kernel,
        grid=(noct, b, t // tm),
        in_specs=[pl.BlockSpec((None, tm, LANES), lambda q, i, j: (i, j, q)),
                  table(win), table(bd), table(wout), table(lstep), table(lblk)],
        out_specs=pl.BlockSpec((None, tm, LANES), lambda q, i, j: (i, j, q)),
        out_shape=jax.ShapeDtypeStruct((b, t, C), F32),
        scratch_shapes=[pltpu.VMEM((nc, fl), BF16), pltpu.VMEM((fl, fl), BF16), pltpu.VMEM((SUBLANES, ns), F32)],
        compiler_params=_cparams(("arbitrary", "arbitrary", "arbitrary")),
        name="s5_ssm",
    )(u, win, bd, wout, lstep, lblk)

    tt = min(TM_S5_GLU, t)
    return pl.pallas_call(
        _s5_tail_kernel,
        grid=(b, t // tt),
        in_specs=[
            pl.BlockSpec((None, tt, C), lambda i, j: (i, j, 0)),
            pl.BlockSpec((None, tt, C), lambda i, j: (i, j, 0)),
            _layer_spec(d_skip, l),
            _layer_spec(w_glu, l),
        ],
        out_specs=pl.BlockSpec((None, tt, C), lambda i, j: (i, j, 0)),
        out_shape=jax.ShapeDtypeStruct((b, t, C), BF16),
        compiler_params=_cparams(("parallel", "parallel")),
        name="s5_glu",
    )(y, u, d_skip, w_glu)


def _prep_s5(a_re, a_im, b_re, b_im, c_re, c_im, log_dt):
    nl, g, p = a_re.shape
    m = S5_GROUP
    noct = g // S5_OCT
    lam = lax.complex(a_re.astype(F32), a_im.astype(F32))
    dt = jnp.exp(log_dt.astype(F32))[..., None]
    ldt = lam * dt
    lam_bar = jnp.exp(ldt)
    b_bar = ((lam_bar - 1.0) / lam)[..., None] * lax.complex(b_re.astype(F32), b_im.astype(F32))
    cc = lax.complex(c_re.astype(F32), c_im.astype(F32))

    def powers(e):
        return jnp.exp(ldt[..., None] * e.astype(F32).reshape((1, 1, 1, -1)))

    steps = jnp.arange(S5_L)

    def octet_bd(x, row_inner, col_pre, col_post):
        ncol = col_pre * S5_OCT * col_post
        cidx = np.arange(ncol)
        src = (cidx // (S5_OCT * col_post)) * col_post + cidx % col_post
        expand = jnp.asarray(np.arange(col_pre * col_post)[:, None] == src[None, :], F32)
        ridx = np.arange(x.shape[2])
        keep = jnp.asarray(((ridx // row_inner) % S5_OCT)[:, None] == ((cidx // col_post) % S5_OCT)[None, :])
        return jnp.where(keep, jnp.einsum("lqrc,cn->lqrn", x.astype(BF16), expand.astype(BF16),
                                          preferred_element_type=F32), 0.0).astype(BF16)

    kd = jnp.einsum("lgop,lgpd,lgpi->lgdio", cc, powers(steps), b_bar).real
    kd = kd.reshape(nl, noct, S5_OCT, S5_L, m, m)
    kd = jnp.moveaxis(kd, 3, 2).reshape(nl, noct, S5_L * LANES, m)
    bd = octet_bd(kd, m, 1, m).reshape(nl, noct, S5_L, LANES, LANES)

    wi = powers(S5_L - 1 - steps)[..., None] * b_bar[:, :, :, None, :]
    wi = jnp.stack([wi.real, wi.imag], axis=2)
    wi = wi.reshape(nl, noct, S5_OCT, 2, p, S5_L, m)
    wi = jnp.transpose(wi, (0, 1, 5, 2, 6, 3, 4)).reshape(nl, noct, S5_L * LANES, 2 * p)
    win = octet_bd(wi, m, 2, p)

    wo = cc[:, :, :, :, None] * powers(steps + 1)[:, :, None, :, :]
    wo = jnp.stack([wo.real, -wo.imag], axis=2)
    wo = wo.reshape(nl, noct, S5_OCT, 2, m, p, S5_L)
    wo = jnp.transpose(wo, (0, 1, 3, 2, 5, 6, 4)).reshape(nl, noct, 2 * S5_OCT * p, S5_L * m)
    wout = octet_bd(wo, p, S5_L, m)

    def state_mult(e):
        pw = powers(e)
        pw = jnp.transpose(pw, (0, 3, 1, 2)).reshape(nl, e.shape[0], noct, S5_OCT * p)
        pw = jnp.moveaxis(pw, 2, 1)
        return (jnp.concatenate([pw.real, pw.real], axis=-1), jnp.concatenate([-pw.imag, pw.imag], axis=-1))

    nsteps = int(math.log2(SUBLANES))
    ar, ai = state_mult(S5_L * (2 ** jnp.arange(nsteps)))
    lstep = jnp.stack([ar, ai], axis=3).reshape(nl, noct, 2 * nsteps, -1)
    lblk = jnp.stack(state_mult(S5_L * (jnp.arange(SUBLANES) + 1)), axis=2)
    return win, bd, wout, lstep, lblk


HIGHEST = lax.Precision.HIGHEST


def _mm(a, b, dims=(((1,), (0,)), ((), ()))):
    return lax.dot_general(a.astype(BF16), b.astype(BF16), dims, preferred_element_type=F32)


def _mm_nt(a, b):
    return _mm(a, b, (((1,), (1,)), ((), ())))


def _mm_tn(a, b):
    return _mm(a, b, (((0,), (0,)), ((), ())))


def _mm_exact(a, b):
    return lax.dot_general(a, b, (((1,), (0,)), ((), ())), precision=HIGHEST, preferred_element_type=F32)


def _head_sum(x, m0):
    s0 = jnp.sum(jnp.where(m0, x, 0.0), axis=-1, keepdims=True)
    s1 = jnp.sum(jnp.where(m0, 0.0, x), axis=-1, keepdims=True)
    return jnp.where(m0, s0, s1)


def _rwkv_kernel(r_ref, k_ref, v_ref, lo_ref, rh_ref, kh_ref, vh_ref, loh_ref, mu_ref, mul_ref, w0_ref, a0_ref,
                 w2_ref, a2_ref, g2_ref, kk_ref, ka_ref, rk_ref, lng_ref, lnb_ref, o_ref, hstate):
    t = pl.program_id(1)
    tm, c = r_ref.shape
    npair = c // LANES
    ll = RW_L

    @pl.when(t == 0)
    def _():
        hstate[...] = jnp.zeros_like(hstate)

    def mix(ref, href, mu):
        p = ref[...].astype(F32)
        prev = _shift_rows(p, jnp.where(t == 0, 0.0, href[...].astype(F32)), 1)
        return p + (prev - p) * mu

    mu = mu_ref[...]
    r = mix(r_ref, rh_ref, mu[0:1, :])
    k = mix(k_ref, kh_ref, mu[1:2, :])
    v = mix(v_ref, vh_ref, mu[2:3, :])
    lo = mix(lo_ref, loh_ref, mul_ref[...])
    wlog = -_softplus(-(w0_ref[...] + _mm(jnp.tanh(lo), w2_ref[...]))) - 0.5
    lw = -jnp.exp(wlog)
    a = _sigmoid(a0_ref[...] + _mm(lo, a2_ref[...]))
    g = _mm(_sigmoid(lo), g2_ref[...])
    kk = k * kk_ref[...]
    k = k * (1.0 + (a - 1.0) * ka_ref[...])

    lane = lax.broadcasted_iota(jnp.int32, (ll, LANES), 1)
    m0 = lane < RWKV_HD
    ri = lax.broadcasted_iota(jnp.int32, (2 * ll, 2 * ll), 0)
    ci = lax.broadcasted_iota(jnp.int32, (2 * ll, 2 * ll), 1)
    same = (ri >= ll) == (ci >= ll)
    strict = same & (ri > ci)
    incl = same & (ri >= ci)
    eye = ri == ci
    tri = (lax.broadcasted_iota(jnp.int32, (ll, ll), 0) >= lax.broadcasted_iota(jnp.int32, (ll, ll), 1)).astype(F32)

    def st(x):
        return jnp.concatenate([jnp.where(m0, x, 0.0), jnp.where(m0, 0.0, x)], axis=0)

    nch = tm // ll
    pcs = [(ch, p) for ch in range(nch) for p in range(npair)]
    cum_all = [_mm_exact(tri, lw[ch * ll:(ch + 1) * ll, :]) for ch in range(nch)]
    pre = {}
    for (ch, p) in pcs:
        rows, cols = slice(ch * ll, (ch + 1) * ll), slice(p * LANES, (p + 1) * LANES)
        rp, kp, vp, ap, lwp, cum = r[rows, cols], k[rows, cols], v[rows, cols], a[rows, cols], lw[rows, cols], cum_all[ch][:, cols]
        kkp = kk[rows, cols]
        kkp = kkp * lax.rsqrt(_head_sum(kkp * kkp, m0) + 1e-12)
        cum_l = cum[ll - 1:ll, :]
        e_neg = jnp.exp(-cum)
        e_last = jnp.exp(cum_l - cum)
        kb = kkp * ap
        stb = lambda x: st(x.astype(BF16))
        pre[ch, p] = dict(
            rows=rows, cols=cols, rp=rp, kp=kp, vp=vp,
            a_st=stb(-kkp * jnp.exp(cum - lwp)), r_st=st(rp * jnp.exp(cum)),
            v_st=stb(vp), b_st=stb(kb * e_neg), k_st=stb(kp * e_neg),
            bh_st=stb(kb * e_last), kh_st=stb(kp * e_last), g_l=jnp.exp(cum_l))

    gram = {pc: _mm_nt(jnp.concatenate([pre[pc]["a_st"], pre[pc]["r_st"].astype(BF16)], axis=0),
                       jnp.concatenate([pre[pc]["b_st"], pre[pc]["k_st"]], axis=0)) for pc in pcs}
    a_ab = {pc: jnp.where(strict, gram[pc][:2 * ll, :2 * ll], 0.0) for pc in pcs}
    a_ak = {pc: jnp.where(strict, gram[pc][:2 * ll, 2 * ll:], 0.0).astype(BF16) for pc in pcs}
    m_rb = {pc: jnp.where(incl, gram[pc][2 * ll:, :2 * ll], 0.0).astype(BF16) for pc in pcs}
    m_rk = {pc: jnp.where(incl, gram[pc][2 * ll:, 2 * ll:], 0.0).astype(BF16) for pc in pcs}
    akv = {pc: _mm(a_ak[pc], pre[pc]["v_st"]) for pc in pcs}
    tinv = {pc: jnp.where(eye, 1.0, a_ab[pc]) for pc in pcs}
    pw = {pc: a_ab[pc].astype(BF16) for pc in pcs}
    for _ in range(int(math.log2(ll)) - 1):
        pw = {pc: _mm(pw[pc], pw[pc]).astype(BF16) for pc in pcs}
        tinv = {pc: tinv[pc] + _mm(tinv[pc], pw[pc]) for pc in pcs}
    xa = {pc: _mm(tinv[pc], jnp.concatenate([pre[pc]["a_st"], akv[pc].astype(BF16)], axis=1)).astype(BF16)
          for pc in pcs}
    ma = {pc: _mm(m_rb[pc], xa[pc][:, :LANES]) for pc in pcs}
    y_loc = {pc: _mm(jnp.concatenate([m_rb[pc], m_rk[pc]], axis=1),
                     jnp.concatenate([xa[pc][:, LANES:], pre[pc]["v_st"]], axis=0)) for pc in pcs}
    ba = {pc: _mm_tn(pre[pc]["bh_st"], xa[pc][:, :LANES]) for pc in pcs}
    f_loc = {pc: _mm_tn(jnp.concatenate([pre[pc]["bh_st"], pre[pc]["kh_st"]], axis=0),
                        jnp.concatenate([xa[pc][:, LANES:], pre[pc]["v_st"]], axis=0)) for pc in pcs}
    for (ch, p) in pcs:
        pc = (ch, p)
        q = pre[pc]
        r_eff = q["r_st"] + ma[pc]
        g_eff = jnp.where(eye, q["g_l"], 0.0) + ba[pc]
        hs = _mm(jnp.concatenate([g_eff, r_eff], axis=0), hstate[p])
        hstate[p] = hs[:2 * ll, :] + f_loc[pc]
        y_st = hs[2 * ll:, :] + y_loc[pc]
        y = y_st[:ll, :] + y_st[ll:, :]
        rows, cols = q["rows"], q["cols"]
        mean = _head_sum(y, m0) * (1.0 / RWKV_HD)
        d = y - mean
        var = _head_sum(d * d, m0) * (1.0 / RWKV_HD)
        yn = d * lax.rsqrt(var + RWKV_LN_EPS) * lng_ref[:, cols] + lnb_ref[:, cols]
        bonus = _head_sum(q["rp"] * q["kp"] * rk_ref[:, cols], m0) * q["vp"]
        o_ref[rows, cols] = ((yn + bonus) * g[rows, cols]).astype(o_ref.dtype)


def _rwkv_branch(z, prm, l):
    b, t, _ = z.shape
    tm = min(TM_RWKV, t)
    hb = tm // HALO
    cr = COL_RW_R // C
    cl = COL_LORA // 256
    npair = C // LANES

    def tile(cb, w):
        return pl.BlockSpec((None, tm, w), lambda i, j: (i, j, cb))

    def halo(cb, w):
        return pl.BlockSpec((None, HALO, w), lambda i, j: (i, jnp.maximum(j * hb - 1, 0), cb))

    return pl.pallas_call(
        _rwkv_kernel,
        grid=(b, t // tm),
        in_specs=[tile(cr, C), tile(cr + 1, C), tile(cr + 2, C), tile(cl, 256),
                  halo(cr, C), halo(cr + 1, C), halo(cr + 2, C), halo(cl, 256)]
                 + [_layer_spec(a, l) for a in prm],
        out_specs=pl.BlockSpec((None, tm, C), lambda i, j: (i, j, 0)),
        out_shape=jax.ShapeDtypeStruct((b, t, C), BF16),
        scratch_shapes=[pltpu.VMEM((npair, LANES, LANES), F32)],
        compiler_params=_cparams(("parallel", "arbitrary")),
        name="rwkv7",
    )(z, z, z, z, z, z, z, z, *prm)


def _prep_rwkv(mu_rkv, mu_w, mu_a, mu_g, w0, w2, a0, a2, g2, k_k, k_a, r_k, lnx_g, lnx_b):
    nl = mu_rkv.shape[0]
    row = lambda x: x.reshape(nl, 1, -1).astype(F32)
    mul = jnp.concatenate([mu_w, mu_a, mu_g], axis=-1)
    zw = lambda n: jnp.zeros((nl, n, C), F32)
    w2p = jnp.concatenate([w2, zw(A_LORA + G_LORA)], axis=1).astype(BF16)
    a2p = jnp.concatenate([zw(W_LORA), a2, zw(G_LORA)], axis=1).astype(BF16)
    g2p = jnp.concatenate([zw(W_LORA + A_LORA), g2], axis=1).astype(BF16)
    return (mu_rkv.astype(F32), row(mul), row(w0), row(a0), w2p, a2p, g2p, row(k_k), row(k_a), row(r_k),
            row(lnx_g), row(lnx_b))


SLOT = LANES
ROT_HALF = QK_ROPE // 2
NOPE_A = SLOT // 2 - ROT_HALF


def _slot_source():
    src = np.full((SLOT,), QK_HD, np.int32)
    src[:ROT_HALF] = QK_NOPE + np.arange(ROT_HALF)
    src[ROT_HALF:SLOT // 2] = np.arange(NOPE_A)
    src[SLOT // 2:SLOT // 2 + ROT_HALF] = QK_NOPE + ROT_HALF + np.arange(ROT_HALF)
    src[SLOT // 2 + ROT_HALF:SLOT // 2 + ROT_HALF + QK_NOPE - NOPE_A] = NOPE_A + np.arange(QK_NOPE - NOPE_A)
    return src


def _to_slot(v):
    pad = jnp.concatenate([v, jnp.zeros(v.shape[:-1] + (1,), v.dtype)], axis=-1)
    return jnp.take(pad, jnp.asarray(_slot_source()), axis=-1)
VT_ROWS = V_HD + 16


FAST_BOUND = 40.0


def _mla_prep_kernel(cq_ref, ckv_ref, kr_ref, tab_ref, qn_ref, wq_ref, kvn_ref, wk_ref, wv_ref,
                     q_ref, k_ref, v_ref, v_scr):
    def rms(x, g, n):
        ms = jnp.sum(x * x, axis=-1, keepdims=True) * (1.0 / n)
        return x * lax.rsqrt(ms + EPS) * g

    def norm_rope(x, c_tab, s_tab):
        ms = jnp.sum(x * x, axis=-1, keepdims=True) * (1.0 / QK_HD)
        return (x * c_tab + pltpu.roll(x, SLOT // 2, axis=1) * s_tab) * lax.rsqrt(ms + EPS)

    cq = rms(cq_ref[...].astype(F32), qn_ref[...], Q_LORA)
    ckv = rms(ckv_ref[...].astype(F32), kvn_ref[...], KV_LORA)
    qf = _mm(cq, wq_ref[...])
    kf = _mm(ckv, wk_ref[...])
    v_scr[...] = _mm(ckv, wv_ref[...])
    vt = jnp.transpose(v_scr[...]).astype(BF16)
    ones = jnp.ones((VT_ROWS - V_HD, vt.shape[1]), BF16)
    for h in range(MLA_HEADS):
        v_ref[h * VT_ROWS:h * VT_ROWS + V_HD, :] = vt[h * V_HD:(h + 1) * V_HD, :]
        v_ref[h * VT_ROWS + V_HD:(h + 1) * VT_ROWS, :] = ones
    kr = kr_ref[...].astype(F32)
    cq_tab, sq_tab, ck_tab, sk_tab = tab_ref[0], tab_ref[1], tab_ref[2], tab_ref[3]
    for h in range(MLA_HEADS):
        cols = slice(h * SLOT, (h + 1) * SLOT)
        q_ref[:, cols] = norm_rope(qf[:, cols], cq_tab, sq_tab).astype(BF16)
        k_ref[:, cols] = norm_rope(kf[:, cols] + kr, ck_tab, sk_tab).astype(BF16)


def _flash_kernel(qi_ref, kj_ref, q_ref, k_ref, vt_ref, o_ref, acc, m_scr=None):
    online = m_scr is not None
    i = qi_ref[pl.program_id(1)]
    j = kj_ref[pl.program_id(1)]
    nb, bq, _ = q_ref.shape
    bk = k_ref.shape[1]

    @pl.when(j == 0)
    def _():
        if online:
            m_scr[...] = jnp.full_like(m_scr, -jnp.inf)
        acc[...] = jnp.zeros_like(acc)

    def step(bb, masked):
        if masked:
            causal = (lax.broadcasted_iota(jnp.int32, (bk, bq), 0) <= lax.broadcasted_iota(jnp.int32, (bk, bq), 1))

        def scores(h):
            cols = slice(h * SLOT, (h + 1) * SLOT)
            return _mm_nt(k_ref[bb, :, cols], q_ref[bb, :, cols])

        st_next = scores(0)
        for h in range(MLA_HEADS):
            rows = slice(h * VT_ROWS, (h + 1) * VT_ROWS)
            st = st_next
            if h + 1 < MLA_HEADS:
                st_next = scores(h + 1)
            if masked:
                st = jnp.where(causal, st, -jnp.inf)
            if online:
                m_prev = m_scr[bb, h:h + 1, :]
                m_new = jnp.maximum(m_prev, jnp.max(st, axis=0, keepdims=True))
                alpha = jnp.exp2(m_prev - m_new)
                pt = jnp.exp2(st - m_new)
                m_scr[bb, h:h + 1, :] = m_new
                acc[bb, rows, :] = acc[bb, rows, :] * alpha + _mm(vt_ref[bb, rows, :], pt)
            else:
                acc[bb, rows, :] += _mm(vt_ref[bb, rows, :], jnp.exp2(st))

    def finish(bb):
        outs = []
        for h in range(MLA_HEADS):
            r0 = h * VT_ROWS
            outs.append(acc[bb, r0:r0 + V_HD, :] / acc[bb, r0 + V_HD:r0 + V_HD + 1, :])
        o_ref[bb] = jnp.transpose(jnp.concatenate(outs, axis=0)).astype(o_ref.dtype)

    @pl.when(j < i)
    def _():
        @pl.loop(0, nb)
        def _(bb):
            step(bb, False)

    @pl.when(j == i)
    def _():
        @pl.loop(0, nb)
        def _(bb):
            step(bb, True)
            finish(bb)


def _score_bound(qk_norm_q, qk_norm_k):
    scale = QK_HD ** -0.5 * math.log2(math.e)
    gq = jnp.max(jnp.abs(qk_norm_q.astype(F32)), axis=-1)
    gk = jnp.max(jnp.abs(qk_norm_k.astype(F32)), axis=-1)
    return 1.01 * QK_HD * scale * gq * gk


def _mla_branch(z, prm, rope, score_bound, l):
    b, t, _ = z.shape
    tm = min(TM_MLA_PREP, t)
    hs = MLA_HEADS * SLOT
    vr = MLA_HEADS * VT_ROWS

    q, k, vt = pl.pallas_call(
        _mla_prep_kernel,
        grid=(b, t // tm),
        in_specs=[pl.BlockSpec((None, tm, Q_LORA), lambda i, j: (i, j, COL_CQ // Q_LORA)),
                  pl.BlockSpec((None, tm, KV_LORA), lambda i, j: (i, j, COL_CKV // KV_LORA)),
                  pl.BlockSpec((None, tm, LANES), lambda i, j: (i, j, COL_KR // LANES)),
                  pl.BlockSpec((None, 4, tm, SLOT), lambda i, j: (l, 0, j, 0))]
                 + [_layer_spec(a, l) for a in prm],
        out_specs=[pl.BlockSpec((None, tm, hs), lambda i, j: (i, j, 0)),
                   pl.BlockSpec((None, tm, hs), lambda i, j: (i, j, 0)),
                   pl.BlockSpec((None, vr, tm), lambda i, j: (i, 0, j))],
        out_shape=[jax.ShapeDtypeStruct((b, t, hs), BF16), jax.ShapeDtypeStruct((b, t, hs), BF16),
                   jax.ShapeDtypeStruct((b, vr, t), BF16)],
        scratch_shapes=[pltpu.VMEM((tm, C), F32)],
        compiler_params=_cparams(("parallel", "parallel")),
        name="mla_prep",
    )(z, z, z, rope, *prm)

    bq = min(BQ_ATTN, t)
    nq = t // bq
    tri = [(i, j) for i in range(nq) for j in range(i + 1)]
    qi = jnp.asarray([p[0] for p in tri], jnp.int32)
    kj = jnp.asarray([p[1] for p in tri], jnp.int32)

    nb = math.gcd(b, FLASH_NB)

    def attend(online):
        scratch = [pltpu.VMEM((nb, vr, bq), F32)] + ([pltpu.VMEM((nb, MLA_HEADS, bq), F32)] if online else [])
        return pl.pallas_call(
            _flash_kernel,
            grid_spec=pltpu.PrefetchScalarGridSpec(
                num_scalar_prefetch=2,
                grid=(b // nb, len(tri)),
                in_specs=[pl.BlockSpec((nb, bq, hs), lambda bi, s, qi, kj: (bi, qi[s], 0)),
                          pl.BlockSpec((nb, bq, hs), lambda bi, s, qi, kj: (bi, kj[s], 0)),
                          pl.BlockSpec((nb, vr, bq), lambda bi, s, qi, kj: (bi, 0, kj[s]))],
                out_specs=pl.BlockSpec((nb, bq, C), lambda bi, s, qi, kj: (bi, qi[s], 0)),
                scratch_shapes=scratch),
            out_shape=jax.ShapeDtypeStruct((b, t, C), BF16),
            compiler_params=_cparams(("parallel", "arbitrary")),
            name="mla_attn_online" if online else "mla_attn",
        )(qi, kj, q, k, vt)

    return lax.cond(score_bound <= FAST_BOUND, lambda: attend(False), lambda: attend(True))


def _rope_tables(t, qk_norm_q, qk_norm_k):
    pos = jnp.arange(t, dtype=F32)
    inv_freq = ROPE_THETA ** (-jnp.arange(0, QK_ROPE, 2, dtype=F32) / QK_ROPE)
    ang = pos[:, None] * inv_freq[None, :]
    cos, sin = jnp.cos(ang), jnp.sin(ang)
    z = lambda n: jnp.zeros((t, n), F32)
    one = lambda n: jnp.ones((t, n), F32)
    tail = SLOT // 2 - ROT_HALF
    c_pos = jnp.concatenate([cos, one(NOPE_A), cos, one(QK_NOPE - NOPE_A), z(SLOT - QK_HD)], axis=1)
    s_pos = jnp.concatenate([-sin, z(NOPE_A), sin, z(tail)], axis=1)
    scale = QK_HD ** -0.5 * math.log2(math.e)

    def pair(g, mult):
        g = _to_slot(g.astype(F32)) * mult
        return [g[:, None, :] * c_pos[None], jnp.roll(g, SLOT // 2, axis=-1)[:, None, :] * s_pos[None]]

    return jnp.stack(pair(qk_norm_q, scale) + pair(qk_norm_k, 1.0), axis=1)


def _prep_mla(q_norm, w_uq, kv_norm, w_ukv):
    nl = q_norm.shape[0]
    nh = MLA_HEADS
    wq = _to_slot(w_uq.reshape(nl, Q_LORA, nh, QK_HD)).reshape(nl, Q_LORA, nh * SLOT).astype(BF16)
    wkv = w_ukv.reshape(nl, KV_LORA, nh, QK_NOPE + V_HD)
    wk = jnp.pad(wkv[..., :QK_NOPE], ((0, 0), (0, 0), (0, 0), (0, QK_ROPE)))
    wk = _to_slot(wk).reshape(nl, KV_LORA, nh * SLOT).astype(BF16)
    wv = wkv[..., QK_NOPE:].reshape(nl, KV_LORA, nh * V_HD).astype(BF16)
    return (q_norm.reshape(nl, 1, -1).astype(F32), wq, kv_norm.reshape(nl, 1, -1).astype(F32), wk, wv)


def _merge_kernel(y0_ref, y1_ref, y2_ref, y3_ref, gl_ref, x_ref, wb_ref, wo_ref, o_ref):
    d = x_ref.shape[1]
    merged = None
    for n, y_ref in enumerate((y0_ref, y1_ref, y2_ref, y3_ref)):
        term = _mm(y_ref[...], wb_ref[n]) * _sigmoid(gl_ref[:, n * d:(n + 1) * d].astype(F32))
        merged = term if merged is None else merged + term
    o_ref[...] = x_ref[...] + _mm(merged, wo_ref[...])


def _merge(ys, z2, x2, wb, wo, l):
    n, d = x2.shape
    tm = min(TM_MERGE, n)
    ytile = pl.BlockSpec((tm, C), lambda i: (i, 0))
    return pl.pallas_call(
        _merge_kernel,
        grid=(n // tm,),
        in_specs=[ytile, ytile, ytile, ytile,
                  pl.BlockSpec((tm, 4 * d), lambda i: (i, COL_GATE)),
                  pl.BlockSpec((tm, d), lambda i: (i, 0)),
                  _layer_spec(wb, l), _layer_spec(wo, l)],
        out_specs=pl.BlockSpec((tm, d), lambda i: (i, 0)),
        out_shape=jax.ShapeDtypeStruct((n, d), F32),
        compiler_params=_cparams(("parallel",)),
        name="merge",
    )(*ys, z2, x2, wb, wo)


def _mlp_kernel(x_ref, g_ref, w1_ref, w2_ref, o_ref, h_scr, acc):
    j = pl.program_id(1)

    @pl.when(j == 0)
    def _():
        x = x_ref[...]
        ms = jnp.mean(x * x, axis=-1, keepdims=True)
        h_scr[...] = (x * lax.rsqrt(ms + EPS) * g_ref[...]).astype(BF16)
        acc[...] = x

    a = jnp.maximum(jnp.dot(h_scr[...], w1_ref[...], preferred_element_type=F32), 0.0)
    acc[...] += _mm(a * a, w2_ref[...])

    @pl.when(j == pl.num_programs(1) - 1)
    def _():
        o_ref[...] = acc[...]


def _mlp(x2, g, w1, w2, l):
    n, d = x2.shape
    dff = w1.shape[2]
    tm = min(TM_MLP, n)
    tf = TF_MLP
    return pl.pallas_call(
        _mlp_kernel,
        grid=(n // tm, dff // tf),
        in_specs=[pl.BlockSpec((tm, d), lambda i, j: (i, 0)),
                  pl.BlockSpec((None, 1, d), lambda i, j: (l, 0, 0)),
                  pl.BlockSpec((None, d, tf), lambda i, j: (l, 0, j)),
                  pl.BlockSpec((None, tf, d), lambda i, j: (l, j, 0))],
        out_specs=pl.BlockSpec((tm, d), lambda i, j: (i, 0)),
        out_shape=jax.ShapeDtypeStruct((n, d), F32),
        scratch_shapes=[pltpu.VMEM((tm, d), BF16), pltpu.VMEM((tm, d), F32)],
        compiler_params=_cparams(("parallel", "arbitrary")),
        name="mlp",
    )(x2, g, w1, w2)


def _prep_w_in(w_in):
    nl, d, _ = w_in.shape
    gate0 = 6 * C + W_LORA + A_LORA + G_LORA + Q_LORA + KV_LORA + QK_ROPE
    kr0 = gate0 - QK_ROPE
    parts = [
        w_in[:, :, gate0:],
        w_in[:, :, :kr0],
        w_in[:, :, kr0:kr0 + ROT_HALF],
        jnp.zeros((nl, d, NOPE_A), w_in.dtype),
        w_in[:, :, kr0 + ROT_HALF:gate0],
        jnp.zeros((nl, d, D_INP - COL_KR - SLOT // 2 - ROT_HALF), w_in.dtype),
    ]
    return jnp.concatenate(parts, axis=-1).astype(BF16)


def _prep_lru_gate(gate_w):
    nl, _, nh, hd, _ = gate_w.shape
    eye = jnp.eye(nh, dtype=gate_w.dtype)
    dense = jnp.einsum("lghij,hk->lghikj", gate_w, eye).reshape(nl, 2, nh * hd, nh * hd)
    return jnp.concatenate([dense[:, 0], dense[:, 1]], axis=-1).astype(BF16)


def kernel(x, norm_mix, w_in, lru_conv_w, lru_conv_b, lru_gate_w, lru_gate_b, lru_lambda, s5_a_re, s5_a_im, s5_b_re, s5_b_im, s5_c_re, s5_c_im, s5_d, s5_log_dt, s5_w_glu, rwkv_mu_rkv, rwkv_mu_w, rwkv_mu_a, rwkv_mu_g, rwkv_w0, rwkv_w2, rwkv_a0, rwkv_a2, rwkv_g2, rwkv_k_k, rwkv_k_a, rwkv_r_k, rwkv_lnx_g, rwkv_lnx_b, mla_q_norm, mla_w_uq, mla_kv_norm, mla_w_ukv, mla_qk_norm_q, mla_qk_norm_k, w_branch, w_out, norm_mlp, w_ff1, w_ff2):
    b, t, d = x.shape
    depth = w_in.shape[0]
    n = b * t
    row = lambda a: a.reshape(depth, 1, -1).astype(F32)

    w_in_p = _prep_w_in(w_in)
    lru_gw = _prep_lru_gate(lru_gate_w)
    s5_tabs = _prep_s5(s5_a_re, s5_a_im, s5_b_re, s5_b_im, s5_c_re, s5_c_im, s5_log_dt)
    rw_prm = _prep_rwkv(rwkv_mu_rkv, rwkv_mu_w, rwkv_mu_a, rwkv_mu_g, rwkv_w0, rwkv_w2, rwkv_a0, rwkv_a2, rwkv_g2,
                        rwkv_k_k, rwkv_k_a, rwkv_r_k, rwkv_lnx_g, rwkv_lnx_b)
    mla_prm = _prep_mla(mla_q_norm, mla_w_uq, mla_kv_norm, mla_w_ukv)
    rope = _rope_tables(t, mla_qk_norm_q, mla_qk_norm_k)
    score_bound = _score_bound(mla_qk_norm_q, mla_qk_norm_k)
    g_mix, g_mlp = row(norm_mix), row(norm_mlp)
    conv_b, gate_b, lam, s5_dr = row(lru_conv_b), row(lru_gate_b), row(lru_lambda), row(s5_d)
    w_glu, wb, wo = s5_w_glu.astype(BF16), w_branch.astype(BF16), w_out.astype(BF16)
    w1, w2 = w_ff1.astype(BF16), w_ff2.astype(BF16)

    lru_prm = (lru_conv_w.astype(F32), conv_b, lru_gw, gate_b, lam)
    x2 = x.reshape(n, d)
    for l in range(depth):
        z2, u2 = _inproj(x2, g_mix, w_in_p, l)
        z = z2.reshape(b, t, -1)
        y_lru = _lru_branch(z, lru_prm, l)
        y_s5 = _s5_branch(u2.reshape(b, t, C), s5_tabs, s5_dr, w_glu, l)
        y_rw = _rwkv_branch(z, rw_prm, l)
        y_mla = _mla_branch(z, mla_prm, rope, score_bound[l], l)
        ys = [y.reshape(n, C) for y in (y_lru, y_s5, y_rw, y_mla)]
        x2 = _merge(ys, z2, x2, wb, wo, l)
        x2 = _mlp(x2, g_mlp, w1, w2, l)
    return x2.reshape(b, t, d)
```

```python
import math

import jax
import jax.numpy as jnp
import numpy as np
from jax import lax
from jax.experimental import pallas as pl
from jax.experimental.pallas import tpu as pltpu

F32 = jnp.float32
BF16 = jnp.bfloat16

EPS = 1e-6
CONV_W = 4
LRU_C = 8.0
S5_GROUP = 16
RWKV_HD = 64
W_LORA, A_LORA, G_LORA = 64, 64, 128
RWKV_LN_EPS = 64e-5
MLA_HEADS = 8
QK_NOPE, QK_ROPE, V_HD = 64, 32, 64
QK_HD = QK_NOPE + QK_ROPE
Q_LORA, KV_LORA = 256, 128
ROPE_THETA = 10000.0

LANES = 128
SUBLANES = 8
VMEM_LIMIT = 56 * 1024 * 1024

D_MODEL = 1024
C = D_MODEL // 2
COL_GATE = 0
COL_LRU_X = 4 * D_MODEL
COL_LRU_G = COL_LRU_X + C
COL_S5_U = COL_LRU_G + C
COL_RW_R = COL_S5_U + C
COL_RW_K = COL_RW_R + C
COL_RW_V = COL_RW_K + C
COL_LORA = COL_RW_V + C
COL_CQ = COL_LORA + 256
COL_CKV = COL_CQ + Q_LORA
COL_KR = COL_CKV + KV_LORA
D_INP = 8 * D_MODEL


TM_INPROJ = 1024
INPROJ_TN = 2048
TM_LRU = 256
TM_S5 = 4096
RW_L = 64
TM_RWKV = 8 * RW_L
TM_MLA_PREP = 2048
BQ_ATTN = 512
FLASH_NB = 4
TM_MERGE = 512
TM_MLP = 1024
TF_MLP = 2048


def _cparams(sem):
    return pltpu.CompilerParams(dimension_semantics=sem, vmem_limit_bytes=VMEM_LIMIT)


def _gelu_tanh(x):
    return 0.5 * x * (1.0 + jnp.tanh(math.sqrt(2.0 / math.pi) * (x + 0.044715 * (x * x * x))))


def _sigmoid(x):
    return 0.5 + 0.5 * jnp.tanh(0.5 * x)


def _softplus(x):
    return jnp.maximum(x, 0.0) + jnp.log(1.0 + jnp.exp(-jnp.abs(x)))


HALO = 16


def _shift_rows(x, halo, k):
    xe = jnp.concatenate([halo, x], axis=0)
    n = x.shape[0]
    return xe[HALO - k:HALO - k + n, :]


def _layer_spec(arr, l):
    zeros = (0,) * (arr.ndim - 1)
    return pl.BlockSpec((None,) + arr.shape[1:], lambda *idx: (l,) + zeros)


def _inproj_kernel(x_ref, g_ref, w_ref, o_ref, u_ref, h_scr):
    j = pl.program_id(1)

    @pl.when(j == 0)
    def _():
        x = x_ref[...]
        ms = jnp.mean(x * x, axis=-1, keepdims=True)
        h_scr[...] = (x * lax.rsqrt(ms + EPS) * g_ref[...]).astype(BF16)

    y = jnp.dot(h_scr[...], w_ref[...], preferred_element_type=F32)
    o_ref[...] = y.astype(BF16)
    tn = w_ref.shape[1]
    u0 = COL_S5_U % tn

    @pl.when(j == COL_S5_U // tn)
    def _():
        u_ref[...] = y[:, u0:u0 + C]


def _inproj(x2, g, w, l):
    n, d = x2.shape
    dn = w.shape[2]
    tm = min(TM_INPROJ, n)
    tn = INPROJ_TN
    assert COL_S5_U % tn + C <= tn and dn % tn == 0
    return pl.pallas_call(
        _inproj_kernel,
        grid=(n // tm, dn // tn),
        in_specs=[
            pl.BlockSpec((tm, d), lambda i, j: (i, 0)),
            pl.BlockSpec((None, 1, d), lambda i, j: (l, 0, 0)),
            pl.BlockSpec((None, d, tn), lambda i, j: (l, 0, j)),
        ],
        out_specs=[pl.BlockSpec((tm, tn), lambda i, j: (i, j)),
                   pl.BlockSpec((tm, C), lambda i, j: (i, 0))],
        out_shape=[jax.ShapeDtypeStruct((n, dn), BF16), jax.ShapeDtypeStruct((n, C), F32)],
        scratch_shapes=[pltpu.VMEM((tm, d), BF16)],
        compiler_params=_cparams(("parallel", "arbitrary")),
        name="inproj",
    )(x2, g, w)


def _lru_kernel(x_ref, xh_ref, g_ref, cw_ref, cb_ref, gw_ref, gb_ref, lam_ref, o_ref, carry):
    t = pl.program_id(1)
    tm, c = x_ref.shape

    @pl.when(t == 0)
    def _():
        carry[...] = jnp.zeros_like(carry)

    x = x_ref[...].astype(F32)
    halo = jnp.where(t == 0, 0.0, xh_ref[...].astype(F32))
    cw = cw_ref[...]
    xc = cb_ref[...] + cw[CONV_W - 1:CONV_W, :] * x
    for k in range(1, CONV_W):
        xc = xc + cw[CONV_W - 1 - k:CONV_W - k, :] * _shift_rows(x, halo, k)

    gates = jnp.dot(xc.astype(BF16), gw_ref[...], preferred_element_type=F32) + gb_ref[...]
    r = _sigmoid(gates[:, :c])
    i = _sigmoid(gates[:, c:])
    log_a = (-LRU_C) * r * _softplus(-lam_ref[...])
    a = jnp.exp(log_a)
    u = jnp.sqrt(1.0 - a * a) * (i * xc)

    sub = lax.broadcasted_iota(jnp.int32, (tm, c), 0) % SUBLANES
    s = 1
    while s < SUBLANES:
        a_sh = jnp.where(sub >= s, pltpu.roll(a, s, axis=0), 1.0)
        u_sh = jnp.where(sub >= s, pltpu.roll(u, s, axis=0), 0.0)
        u = u + a * u_sh
        a = a * a_sh
        s *= 2
    hp = carry[SUBLANES - 1:SUBLANES, :]
    blocks = []
    for blk in range(tm // SUBLANES):
        rows = slice(blk * SUBLANES, (blk + 1) * SUBLANES)
        hb = u[rows, :] + a[rows, :] * hp
        hp = hb[SUBLANES - 1:SUBLANES, :]
        blocks.append(hb)
    h = jnp.concatenate(blocks, axis=0)
    carry[...] = blocks[-1]
    o_ref[...] = (h * _gelu_tanh(g_ref[...].astype(F32))).astype(o_ref.dtype)


def _lru_branch(z, prm, l):
    b, t, _ = z.shape
    tm = min(TM_LRU, t)
    cb = COL_LRU_X // C
    hb = tm // HALO
    return pl.pallas_call(
        _lru_kernel,
        grid=(b, t // tm),
        in_specs=[
            pl.BlockSpec((None, tm, C), lambda i, j: (i, j, cb)),
            pl.BlockSpec((None, HALO, C), lambda i, j: (i, jnp.maximum(j * hb - 1, 0), cb)),
            pl.BlockSpec((None, tm, C), lambda i, j: (i, j, cb + 1)),
        ] + [_layer_spec(a, l) for a in prm],
        out_specs=pl.BlockSpec((None, tm, C), lambda i, j: (i, j, 0)),
        out_shape=jax.ShapeDtypeStruct((b, t, C), BF16),
        scratch_shapes=[pltpu.VMEM((SUBLANES, C), F32)],
        compiler_params=_cparams(("parallel", "arbitrary")),
        name="rglru",
    )(z, z, z, *prm)


S5_L = 8
S5_OCT = LANES // S5_GROUP


def _s5_kernel(u_ref, win_ref, bd_ref, wout_ref, lstep_ref, lblk_ref, y_ref, uf, wintra, carry):
    t = pl.program_id(2)
    tm = u_ref.shape[0]
    nc = tm // S5_L
    ns = carry.shape[1]
    half = ns // 2

    @pl.when((pl.program_id(1) == 0) & (t == 0))
    def _():
        wintra[...] = jnp.zeros_like(wintra)
        for i in range(S5_L):
            for j in range(i, S5_L):
                wintra[i * LANES:(i + 1) * LANES, j * LANES:(j + 1) * LANES] = bd_ref[j - i]

    @pl.when(t == 0)
    def _():
        carry[...] = jnp.zeros_like(carry)

    for i in range(S5_L):
        uf[:, i * LANES:(i + 1) * LANES] = u_ref[pl.ds(i, nc, stride=S5_L), :].astype(BF16)
    ufv = uf[...]
    s = jnp.dot(ufv, win_ref[...], preferred_element_type=F32)

    def cmul(v, ar, ai):
        sw = jnp.concatenate([v[:, half:], v[:, :half]], axis=1)
        return v * ar + sw * ai

    row = lax.broadcasted_iota(jnp.int32, (nc, ns), 0)
    sub = row % SUBLANES
    k = 0
    st = 1
    while st < SUBLANES:
        sh = jnp.where(sub >= st, pltpu.roll(s, st, axis=0), 0.0)
        s = s + cmul(sh, lstep_ref[2 * k:2 * k + 1, :], lstep_ref[2 * k + 1:2 * k + 2, :])
        k += 1
        st *= 2
    cin = carry[0:1, :]
    prev = cin
    blocks = []
    for blk in range(nc // SUBLANES):
        sb = s[blk * SUBLANES:(blk + 1) * SUBLANES, :] + cmul(jnp.broadcast_to(prev, (SUBLANES, ns)),
                                                              lblk_ref[0], lblk_ref[1])
        prev = sb[SUBLANES - 1:SUBLANES, :]
        blocks.append(sb)
    s = jnp.concatenate(blocks, axis=0)
    carry[...] = jnp.broadcast_to(prev, carry.shape)
    sprev = jnp.where(row >= 1, pltpu.roll(s, 1, axis=0), cin)

    yf = jnp.dot(ufv, wintra[...], preferred_element_type=F32)
    yf = yf + jnp.dot(sprev.astype(BF16), wout_ref[...], preferred_element_type=F32)
    for j in range(S5_L):
        y_ref[pl.ds(j, nc, stride=S5_L), :] = yf[:, j * LANES:(j + 1) * LANES]


def _s5_tail(y_ssm, u, d_skip, w_glu):
    c = u.shape[1]
    y = _gelu_tanh(y_ssm + d_skip * u)
    z = jnp.dot(y.astype(BF16), w_glu, preferred_element_type=F32)
    return (z[:, :c] * _sigmoid(z[:, c:])).astype(BF16)


def _s5_tile(t):
    return min(TM_S5, t)


def _s5_branch(u, tabs, l):
    win, bd, wout, lstep, lblk = tabs
    b, t, _ = u.shape
    tm = _s5_tile(t)
    nc = tm // S5_L
    noct = C // LANES
    ns = win.shape[-1]
    fl = S5_L * LANES

    def table(arr):
        zeros = (0,) * (arr.ndim - 2)
        return pl.BlockSpec((None, None) + arr.shape[2:], lambda q, i, j: (l, q) + zeros)

    y = pl.pallas_call(
        _s5_kernel,
        grid=(noct, b, t // tm),
        in_specs=[pl.BlockSpec((None, tm, LANES), lambda q, i, j: (i, j, q)),
                  table(win), table(bd), table(wout), table(lstep), table(lblk)],
        out_specs=pl.BlockSpec((None, tm, LANES), lambda q, i, j: (i, j, q)),
        out_shape=jax.ShapeDtypeStruct((b, t, C), F32),
        scratch_shapes=[pltpu.VMEM((nc, fl), BF16), pltpu.VMEM((fl, fl), BF16), pltpu.VMEM((SUBLANES, ns), F32)],
        compiler_params=_cparams(("arbitrary", "arbitrary", "arbitrary")),
        name="s5_ssm",
    )(u, win, bd, wout, lstep, lblk)

    return y


def _prep_s5(a_re, a_im, b_re, b_im, c_re, c_im, log_dt):
    nl, g, p = a_re.shape
    m = S5_GROUP
    noct = g // S5_OCT
    lam = lax.complex(a_re.astype(F32), a_im.astype(F32))
    dt = jnp.exp(log_dt.astype(F32))[..., None]
    ldt = lam * dt
    lam_bar = jnp.exp(ldt)
    b_bar = ((lam_bar - 1.0) / lam)[..., None] * lax.complex(b_re.astype(F32), b_im.astype(F32))
    cc = lax.complex(c_re.astype(F32), c_im.astype(F32))

    def powers(e):
        return jnp.exp(ldt[..., None] * e.astype(F32).reshape((1, 1, 1, -1)))

    steps = jnp.arange(S5_L)

    def octet_bd(x, row_inner, col_pre, col_post):
        ncol = col_pre * S5_OCT * col_post
        cidx = np.arange(ncol)
        src = (cidx // (S5_OCT * col_post)) * col_post + cidx % col_post
        expand = jnp.asarray(np.arange(col_pre * col_post)[:, None] == src[None, :], F32)
        ridx = np.arange(x.shape[2])
        keep = jnp.asarray(((ridx // row_inner) % S5_OCT)[:, None] == ((cidx // col_post) % S5_OCT)[None, :])
        return jnp.where(keep, jnp.einsum("lqrc,cn->lqrn", x.astype(BF16), expand.astype(BF16),
                                          preferred_element_type=F32), 0.0).astype(BF16)

    kd = jnp.einsum("lgop,lgpd,lgpi->lgdio", cc, powers(steps), b_bar).real
    kd = kd.reshape(nl, noct, S5_OCT, S5_L, m, m)
    kd = jnp.moveaxis(kd, 3, 2).reshape(nl, noct, S5_L * LANES, m)
    bd = octet_bd(kd, m, 1, m).reshape(nl, noct, S5_L, LANES, LANES)

    wi = powers(S5_L - 1 - steps)[..., None] * b_bar[:, :, :, None, :]
    wi = jnp.stack([wi.real, wi.imag], axis=2)
    wi = wi.reshape(nl, noct, S5_OCT, 2, p, S5_L, m)
    wi = jnp.transpose(wi, (0, 1, 5, 2, 6, 3, 4)).reshape(nl, noct, S5_L * LANES, 2 * p)
    win = octet_bd(wi, m, 2, p)

    wo = cc[:, :, :, :, None] * powers(steps + 1)[:, :, None, :, :]
    wo = jnp.stack([wo.real, -wo.imag], axis=2)
    wo = wo.reshape(nl, noct, S5_OCT, 2, m, p, S5_L)
    wo = jnp.transpose(wo, (0, 1, 3, 2, 5, 6, 4)).reshape(nl, noct, 2 * S5_OCT * p, S5_L * m)
    wout = octet_bd(wo, p, S5_L, m)

    def state_mult(e):
        pw = powers(e)
        pw = jnp.transpose(pw, (0, 3, 1, 2)).reshape(nl, e.shape[0], noct, S5_OCT * p)
        pw = jnp.moveaxis(pw, 2, 1)
        return (jnp.concatenate([pw.real, pw.real], axis=-1), jnp.concatenate([-pw.imag, pw.imag], axis=-1))

    nsteps = int(math.log2(SUBLANES))
    ar, ai = state_mult(S5_L * (2 ** jnp.arange(nsteps)))
    lstep = jnp.stack([ar, ai], axis=3).reshape(nl, noct, 2 * nsteps, -1)
    lblk = jnp.stack(state_mult(S5_L * (jnp.arange(SUBLANES) + 1)), axis=2)
    return win, bd, wout, lstep, lblk


HIGHEST = lax.Precision.HIGHEST


def _mm(a, b, dims=(((1,), (0,)), ((), ()))):
    return lax.dot_general(a.astype(BF16), b.astype(BF16), dims, preferred_element_type=F32)


def _mm_nt(a, b):
    return _mm(a, b, (((1,), (1,)), ((), ())))


def _mm_tn(a, b):
    return _mm(a, b, (((0,), (0,)), ((), ())))


def _mm_exact(a, b):
    return lax.dot_general(a, b, (((1,), (0,)), ((), ())), precision=HIGHEST, preferred_element_type=F32)


def _head_sum(x, m0):
    s0 = jnp.sum(jnp.where(m0, x, 0.0), axis=-1, keepdims=True)
    s1 = jnp.sum(jnp.where(m0, 0.0, x), axis=-1, keepdims=True)
    return jnp.where(m0, s0, s1)


def _rwkv_kernel(r_ref, k_ref, v_ref, lo_ref, rh_ref, kh_ref, vh_ref, loh_ref, mu_ref, mul_ref, w0_ref, a0_ref,
                 w2_ref, a2_ref, g2_ref, kk_ref, ka_ref, rk_ref, lng_ref, lnb_ref, o_ref, hstate):
    t = pl.program_id(1)
    tm, c = r_ref.shape
    npair = c // LANES
    ll = RW_L

    @pl.when(t == 0)
    def _():
        hstate[...] = jnp.zeros_like(hstate)

    def mix(ref, href, mu):
        p = ref[...].astype(F32)
        prev = _shift_rows(p, jnp.where(t == 0, 0.0, href[...].astype(F32)), 1)
        return p + (prev - p) * mu

    mu = mu_ref[...]
    r = mix(r_ref, rh_ref, mu[0:1, :])
    k = mix(k_ref, kh_ref, mu[1:2, :])
    v = mix(v_ref, vh_ref, mu[2:3, :])
    lo = mix(lo_ref, loh_ref, mul_ref[...])
    wlog = -_softplus(-(w0_ref[...] + _mm(jnp.tanh(lo), w2_ref[...]))) - 0.5
    lw = -jnp.exp(wlog)
    a = _sigmoid(a0_ref[...] + _mm(lo, a2_ref[...]))
    g = _mm(_sigmoid(lo), g2_ref[...])
    kk = k * kk_ref[...]
    k = k * (1.0 + (a - 1.0) * ka_ref[...])

    lane = lax.broadcasted_iota(jnp.int32, (ll, LANES), 1)
    m0 = lane < RWKV_HD
    ri = lax.broadcasted_iota(jnp.int32, (2 * ll, 2 * ll), 0)
    ci = lax.broadcasted_iota(jnp.int32, (2 * ll, 2 * ll), 1)
    same = (ri >= ll) == (ci >= ll)
    strict = same & (ri > ci)
    incl = same & (ri >= ci)
    eye = ri == ci
    tri = (lax.broadcasted_iota(jnp.int32, (ll, ll), 0) >= lax.broadcasted_iota(jnp.int32, (ll, ll), 1)).astype(F32)

    def st(x):
        return jnp.concatenate([jnp.where(m0, x, 0.0), jnp.where(m0, 0.0, x)], axis=0)

    nch = tm // ll
    pcs = [(ch, p) for ch in range(nch) for p in range(npair)]
    cum_all = [_mm_exact(tri, lw[ch * ll:(ch + 1) * ll, :]) for ch in range(nch)]
    pre = {}
    for (ch, p) in pcs:
        rows, cols = slice(ch * ll, (ch + 1) * ll), slice(p * LANES, (p + 1) * LANES)
        rp, kp, vp, ap, lwp, cum = r[rows, cols], k[rows, cols], v[rows, cols], a[rows, cols], lw[rows, cols], cum_all[ch][:, cols]
        kkp = kk[rows, cols]
        kkp = kkp * lax.rsqrt(_head_sum(kkp * kkp, m0) + 1e-12)
        cum_l = cum[ll - 1:ll, :]
        e_neg = jnp.exp(-cum)
        e_last = jnp.exp(cum_l - cum)
        kb = kkp * ap
        stb = lambda x: st(x.astype(BF16))
        pre[ch, p] = dict(
            rows=rows, cols=cols, rp=rp, kp=kp, vp=vp,
            a_st=stb(-kkp * jnp.exp(cum - lwp)), r_st=st(rp * jnp.exp(cum)),
            v_st=stb(vp), b_st=stb(kb * e_neg), k_st=stb(kp * e_neg),
            bh_st=stb(kb * e_last), kh_st=stb(kp * e_last), g_l=jnp.exp(cum_l))

    gram = {pc: _mm_nt(jnp.concatenate([pre[pc]["a_st"], pre[pc]["r_st"].astype(BF16)], axis=0),
                       jnp.concatenate([pre[pc]["b_st"], pre[pc]["k_st"]], axis=0)) for pc in pcs}
    a_ab = {pc: jnp.where(strict, gram[pc][:2 * ll, :2 * ll], 0.0) for pc in pcs}
    a_ak = {pc: jnp.where(strict, gram[pc][:2 * ll, 2 * ll:], 0.0).astype(BF16) for pc in pcs}
    m_rb = {pc: jnp.where(incl, gram[pc][2 * ll:, :2 * ll], 0.0).astype(BF16) for pc in pcs}
    m_rk = {pc: jnp.where(incl, gram[pc][2 * ll:, 2 * ll:], 0.0).astype(BF16) for pc in pcs}
    akv = {pc: _mm(a_ak[pc], pre[pc]["v_st"]) for pc in pcs}
    tinv = {pc: jnp.where(eye, 1.0, a_ab[pc]) for pc in pcs}
    pw = {pc: a_ab[pc].astype(BF16) for pc in pcs}
    for _ in range(int(math.log2(ll)) - 1):
        pw = {pc: _mm(pw[pc], pw[pc]).astype(BF16) for pc in pcs}
        tinv = {pc: tinv[pc] + _mm(tinv[pc], pw[pc]) for pc in pcs}
    xa = {pc: _mm(tinv[pc], jnp.concatenate([pre[pc]["a_st"], akv[pc].astype(BF16)], axis=1)).astype(BF16)
          for pc in pcs}
    ma = {pc: _mm(m_rb[pc], xa[pc][:, :LANES]) for pc in pcs}
    y_loc = {pc: _mm(jnp.concatenate([m_rb[pc], m_rk[pc]], axis=1),
                     jnp.concatenate([xa[pc][:, LANES:], pre[pc]["v_st"]], axis=0)) for pc in pcs}
    ba = {pc: _mm_tn(pre[pc]["bh_st"], xa[pc][:, :LANES]) for pc in pcs}
    f_loc = {pc: _mm_tn(jnp.concatenate([pre[pc]["bh_st"], pre[pc]["kh_st"]], axis=0),
                        jnp.concatenate([xa[pc][:, LANES:], pre[pc]["v_st"]], axis=0)) for pc in pcs}
    for (ch, p) in pcs:
        pc = (ch, p)
        q = pre[pc]
        r_eff = q["r_st"] + ma[pc]
        g_eff = jnp.where(eye, q["g_l"], 0.0) + ba[pc]
        hs = _mm(jnp.concatenate([g_eff, r_eff], axis=0), hstate[p])
        hstate[p] = hs[:2 * ll, :] + f_loc[pc]
        y_st = hs[2 * ll:, :] + y_loc[pc]
        y = y_st[:ll, :] + y_st[ll:, :]
        rows, cols = q["rows"], q["cols"]
        mean = _head_sum(y, m0) * (1.0 / RWKV_HD)
        d = y - mean
        var = _head_sum(d * d, m0) * (1.0 / RWKV_HD)
        yn = d * lax.rsqrt(var + RWKV_LN_EPS) * lng_ref[:, cols] + lnb_ref[:, cols]
        bonus = _head_sum(q["rp"] * q["kp"] * rk_ref[:, cols], m0) * q["vp"]
        o_ref[rows, cols] = ((yn + bonus) * g[rows, cols]).astype(o_ref.dtype)


def _rwkv_branch(z, prm, l):
    b, t, _ = z.shape
    tm = min(TM_RWKV, t)
    hb = tm // HALO
    cr = COL_RW_R // C
    cl = COL_LORA // 256
    npair = C // LANES

    def tile(cb, w):
        return pl.BlockSpec((None, tm, w), lambda i, j: (i, j, cb))

    def halo(cb, w):
        return pl.BlockSpec((None, HALO, w), lambda i, j: (i, jnp.maximum(j * hb - 1, 0), cb))

    return pl.pallas_call(
        _rwkv_kernel,
        grid=(b, t // tm),
        in_specs=[tile(cr, C), tile(cr + 1, C), tile(cr + 2, C), tile(cl, 256),
                  halo(cr, C), halo(cr + 1, C), halo(cr + 2, C), halo(cl, 256)]
                 + [_layer_spec(a, l) for a in prm],
        out_specs=pl.BlockSpec((None, tm, C), lambda i, j: (i, j, 0)),
        out_shape=jax.ShapeDtypeStruct((b, t, C), BF16),
        scratch_shapes=[pltpu.VMEM((npair, LANES, LANES), F32)],
        compiler_params=_cparams(("parallel", "arbitrary")),
        name="rwkv7",
    )(z, z, z, z, z, z, z, z, *prm)


def _prep_rwkv(mu_rkv, mu_w, mu_a, mu_g, w0, w2, a0, a2, g2, k_k, k_a, r_k, lnx_g, lnx_b):
    nl = mu_rkv.shape[0]
    row = lambda x: x.reshape(nl, 1, -1).astype(F32)
    mul = jnp.concatenate([mu_w, mu_a, mu_g], axis=-1)
    zw = lambda n: jnp.zeros((nl, n, C), F32)
    w2p = jnp.concatenate([w2, zw(A_LORA + G_LORA)], axis=1).astype(BF16)
    a2p = jnp.concatenate([zw(W_LORA), a2, zw(G_LORA)], axis=1).astype(BF16)
    g2p = jnp.concatenate([zw(W_LORA + A_LORA), g2], axis=1).astype(BF16)
    return (mu_rkv.astype(F32), row(mul), row(w0), row(a0), w2p, a2p, g2p, row(k_k), row(k_a), row(r_k),
            row(lnx_g), row(lnx_b))


SLOT = LANES
ROT_HALF = QK_ROPE // 2
NOPE_A = SLOT // 2 - ROT_HALF


def _slot_source():
    src = np.full((SLOT,), QK_HD, np.int32)
    src[:ROT_HALF] = QK_NOPE + np.arange(ROT_HALF)
    src[ROT_HALF:SLOT // 2] = np.arange(NOPE_A)
    src[SLOT // 2:SLOT // 2 + ROT_HALF] = QK_NOPE + ROT_HALF + np.arange(ROT_HALF)
    src[SLOT // 2 + ROT_HALF:SLOT // 2 + ROT_HALF + QK_NOPE - NOPE_A] = NOPE_A + np.arange(QK_NOPE - NOPE_A)
    return src


def _to_slot(v):
    pad = jnp.concatenate([v, jnp.zeros(v.shape[:-1] + (1,), v.dtype)], axis=-1)
    return jnp.take(pad, jnp.asarray(_slot_source()), axis=-1)
VT_ROWS = V_HD + 16


FAST_BOUND = 40.0


def _mla_prep_kernel(cq_ref, ckv_ref, kr_ref, tab_ref, qn_ref, wq_ref, kvn_ref, wk_ref, wv_ref,
                     q_ref, k_ref, v_ref, v_scr):
    def rms(x, g, n):
        ms = jnp.sum(x * x, axis=-1, keepdims=True) * (1.0 / n)
        return x * lax.rsqrt(ms + EPS) * g

    def norm_rope(x, c_tab, s_tab):
        ms = jnp.sum(x * x, axis=-1, keepdims=True) * (1.0 / QK_HD)
        return (x * c_tab + pltpu.roll(x, SLOT // 2, axis=1) * s_tab) * lax.rsqrt(ms + EPS)

    cq = rms(cq_ref[...].astype(F32), qn_ref[...], Q_LORA)
    ckv = rms(ckv_ref[...].astype(F32), kvn_ref[...], KV_LORA)
    qf = _mm(cq, wq_ref[...])
    kf = _mm(ckv, wk_ref[...])
    v_scr[...] = _mm(ckv, wv_ref[...])
    vt = jnp.transpose(v_scr[...]).astype(BF16)
    ones = jnp.ones((VT_ROWS - V_HD, vt.shape[1]), BF16)
    for h in range(MLA_HEADS):
        v_ref[h * VT_ROWS:h * VT_ROWS + V_HD, :] = vt[h * V_HD:(h + 1) * V_HD, :]
        v_ref[h * VT_ROWS + V_HD:(h + 1) * VT_ROWS, :] = ones
    kr = kr_ref[...].astype(F32)
    cq_tab, sq_tab, ck_tab, sk_tab = tab_ref[0], tab_ref[1], tab_ref[2], tab_ref[3]
    for h in range(MLA_HEADS):
        cols = slice(h * SLOT, (h + 1) * SLOT)
        q_ref[:, cols] = norm_rope(qf[:, cols], cq_tab, sq_tab).astype(BF16)
        k_ref[:, cols] = norm_rope(kf[:, cols] + kr, ck_tab, sk_tab).astype(BF16)


def _flash_kernel(qi_ref, kj_ref, q_ref, k_ref, vt_ref, o_ref, acc, m_scr=None):
    online = m_scr is not None
    i = qi_ref[pl.program_id(1)]
    j = kj_ref[pl.program_id(1)]
    nb, bq, _ = q_ref.shape
    bk = k_ref.shape[1]

    @pl.when(j == 0)
    def _():
        if online:
            m_scr[...] = jnp.full_like(m_scr, -jnp.inf)
        acc[...] = jnp.zeros_like(acc)

    def step(bb, masked):
        if masked:
            causal = (lax.broadcasted_iota(jnp.int32, (bk, bq), 0) <= lax.broadcasted_iota(jnp.int32, (bk, bq), 1))

        def scores(h):
            cols = slice(h * SLOT, (h + 1) * SLOT)
            return _mm_nt(k_ref[bb, :, cols], q_ref[bb, :, cols])

        st_next = scores(0)
        for h in range(MLA_HEADS):
            rows = slice(h * VT_ROWS, (h + 1) * VT_ROWS)
            st = st_next
            if h + 1 < MLA_HEADS:
                st_next = scores(h + 1)
            if masked:
                st = jnp.where(causal, st, -jnp.inf)
            if online:
                m_prev = m_scr[bb, h:h + 1, :]
                m_new = jnp.maximum(m_prev, jnp.max(st, axis=0, keepdims=True))
                alpha = jnp.exp2(m_prev - m_new)
                pt = jnp.exp2(st - m_new)
                m_scr[bb, h:h + 1, :] = m_new
                acc[bb, rows, :] = acc[bb, rows, :] * alpha + _mm(vt_ref[bb, rows, :], pt)
            else:
                acc[bb, rows, :] += _mm(vt_ref[bb, rows, :], jnp.exp2(st))

    def finish(bb):
        outs = []
        for h in range(MLA_HEADS):
            r0 = h * VT_ROWS
            outs.append(acc[bb, r0:r0 + V_HD, :] / acc[bb, r0 + V_HD:r0 + V_HD + 1, :])
        o_ref[bb] = jnp.transpose(jnp.concatenate(outs, axis=0)).astype(o_ref.dtype)

    @pl.when(j < i)
    def _():
        @pl.loop(0, nb)
        def _(bb):
            step(bb, False)

    @pl.when(j == i)
    def _():
        @pl.loop(0, nb)
        def _(bb):
            step(bb, True)
            finish(bb)


def _score_bound(qk_norm_q, qk_norm_k):
    scale = QK_HD ** -0.5 * math.log2(math.e)
    gq = jnp.max(jnp.abs(qk_norm_q.astype(F32)), axis=-1)
    gk = jnp.max(jnp.abs(qk_norm_k.astype(F32)), axis=-1)
    return 1.01 * QK_HD * scale * gq * gk


def _mla_branch(z, prm, rope, score_bound, l):
    b, t, _ = z.shape
    tm = min(TM_MLA_PREP, t)
    hs = MLA_HEADS * SLOT
    vr = MLA_HEADS * VT_ROWS

    q, k, vt = pl.pallas_call(
        _mla_prep_kernel,
        grid=(b, t // tm),
        in_specs=[pl.BlockSpec((None, tm, Q_LORA), lambda i, j: (i, j, COL_CQ // Q_LORA)),
                  pl.BlockSpec((None, tm, KV_LORA), lambda i, j: (i, j, COL_CKV // KV_LORA)),
                  pl.BlockSpec((None, tm, LANES), lambda i, j: (i, j, COL_KR // LANES)),
                  pl.BlockSpec((None, 4, tm, SLOT), lambda i, j: (l, 0, j, 0))]
                 + [_layer_spec(a, l) for a in prm],
        out_specs=[pl.BlockSpec((None, tm, hs), lambda i, j: (i, j, 0)),
                   pl.BlockSpec((None, tm, hs), lambda i, j: (i, j, 0)),
                   pl.BlockSpec((None, vr, tm), lambda i, j: (i, 0, j))],
        out_shape=[jax.ShapeDtypeStruct((b, t, hs), BF16), jax.ShapeDtypeStruct((b, t, hs), BF16),
                   jax.ShapeDtypeStruct((b, vr, t), BF16)],
        scratch_shapes=[pltpu.VMEM((tm, C), F32)],
        compiler_params=_cparams(("parallel", "parallel")),
        name="mla_prep",
    )(z, z, z, rope, *prm)

    bq = min(BQ_ATTN, t)
    nq = t // bq
    tri = [(i, j) for i in range(nq) for j in range(i + 1)]
    qi = jnp.asarray([p[0] for p in tri], jnp.int32)
    kj = jnp.asarray([p[1] for p in tri], jnp.int32)

    nb = math.gcd(b, FLASH_NB)

    def attend(online):
        scratch = [pltpu.VMEM((nb, vr, bq), F32)] + ([pltpu.VMEM((nb, MLA_HEADS, bq), F32)] if online else [])
        return pl.pallas_call(
            _flash_kernel,
            grid_spec=pltpu.PrefetchScalarGridSpec(
                num_scalar_prefetch=2,
                grid=(b // nb, len(tri)),
                in_specs=[pl.BlockSpec((nb, bq, hs), lambda bi, s, qi, kj: (bi, qi[s], 0)),
                          pl.BlockSpec((nb, bq, hs), lambda bi, s, qi, kj: (bi, kj[s], 0)),
                          pl.BlockSpec((nb, vr, bq), lambda bi, s, qi, kj: (bi, 0, kj[s]))],
                out_specs=pl.BlockSpec((nb, bq, C), lambda bi, s, qi, kj: (bi, qi[s], 0)),
                scratch_shapes=scratch),
            out_shape=jax.ShapeDtypeStruct((b, t, C), BF16),
            compiler_params=_cparams(("parallel", "arbitrary")),
            name="mla_attn_online" if online else "mla_attn",
        )(qi, kj, q, k, vt)

    return lax.cond(score_bound <= FAST_BOUND, lambda: attend(False), lambda: attend(True))


def _rope_tables(t, qk_norm_q, qk_norm_k):
    pos = jnp.arange(t, dtype=F32)
    inv_freq = ROPE_THETA ** (-jnp.arange(0, QK_ROPE, 2, dtype=F32) / QK_ROPE)
    ang = pos[:, None] * inv_freq[None, :]
    cos, sin = jnp.cos(ang), jnp.sin(ang)
    z = lambda n: jnp.zeros((t, n), F32)
    one = lambda n: jnp.ones((t, n), F32)
    tail = SLOT // 2 - ROT_HALF
    c_pos = jnp.concatenate([cos, one(NOPE_A), cos, one(QK_NOPE - NOPE_A), z(SLOT - QK_HD)], axis=1)
    s_pos = jnp.concatenate([-sin, z(NOPE_A), sin, z(tail)], axis=1)
    scale = QK_HD ** -0.5 * math.log2(math.e)

    def pair(g, mult):
        g = _to_slot(g.astype(F32)) * mult
        return [g[:, None, :] * c_pos[None], jnp.roll(g, SLOT // 2, axis=-1)[:, None, :] * s_pos[None]]

    return jnp.stack(pair(qk_norm_q, scale) + pair(qk_norm_k, 1.0), axis=1)


def _prep_mla(q_norm, w_uq, kv_norm, w_ukv):
    nl = q_norm.shape[0]
    nh = MLA_HEADS
    wq = _to_slot(w_uq.reshape(nl, Q_LORA, nh, QK_HD)).reshape(nl, Q_LORA, nh * SLOT).astype(BF16)
    wkv = w_ukv.reshape(nl, KV_LORA, nh, QK_NOPE + V_HD)
    wk = jnp.pad(wkv[..., :QK_NOPE], ((0, 0), (0, 0), (0, 0), (0, QK_ROPE)))
    wk = _to_slot(wk).reshape(nl, KV_LORA, nh * SLOT).astype(BF16)
    wv = wkv[..., QK_NOPE:].reshape(nl, KV_LORA, nh * V_HD).astype(BF16)
    return (q_norm.reshape(nl, 1, -1).astype(F32), wq, kv_norm.reshape(nl, 1, -1).astype(F32), wk, wv)


def _merge_kernel(y0_ref, ys_ref, us_ref, y2_ref, y3_ref, gl_ref, x_ref, ds_ref, wg_ref, wb_ref, wo_ref, o_ref):
    d = x_ref.shape[1]
    branches = (y0_ref[...], _s5_tail(ys_ref[...], us_ref[...], ds_ref[...], wg_ref[...]), y2_ref[...], y3_ref[...])
    merged = None
    for n, y in enumerate(branches):
        term = _mm(y, wb_ref[n]) * _sigmoid(gl_ref[:, n * d:(n + 1) * d].astype(F32))
        merged = term if merged is None else merged + term
    o_ref[...] = x_ref[...] + _mm(merged, wo_ref[...])


def _merge(y_lru, y_ssm, u_s5, y_rw, y_mla, z2, x2, d_skip, w_glu, wb, wo, l):
    n, d = x2.shape
    tm = min(TM_MERGE, n)
    ytile = pl.BlockSpec((tm, C), lambda i: (i, 0))
    return pl.pallas_call(
        _merge_kernel,
        grid=(n // tm,),
        in_specs=[ytile, ytile, ytile, ytile, ytile,
                  pl.BlockSpec((tm, 4 * d), lambda i: (i, COL_GATE)),
                  pl.BlockSpec((tm, d), lambda i: (i, 0)),
                  _layer_spec(d_skip, l), _layer_spec(w_glu, l), _layer_spec(wb, l), _layer_spec(wo, l)],
        out_specs=pl.BlockSpec((tm, d), lambda i: (i, 0)),
        out_shape=jax.ShapeDtypeStruct((n, d), F32),
        compiler_params=_cparams(("parallel",)),
        name="merge",
    )(y_lru, y_ssm, u_s5, y_rw, y_mla, z2, x2, d_skip, w_glu, wb, wo)


def _mlp_kernel(x_ref, g_ref, w1_ref, w2_ref, o_ref, h_scr, acc):
    j = pl.program_id(1)

    @pl.when(j == 0)
    def _():
        x = x_ref[...]
        ms = jnp.mean(x * x, axis=-1, keepdims=True)
        h_scr[...] = (x * lax.rsqrt(ms + EPS) * g_ref[...]).astype(BF16)
        acc[...] = x

    a = jnp.maximum(jnp.dot(h_scr[...], w1_ref[...], preferred_element_type=F32), 0.0)
    acc[...] += _mm(a * a, w2_ref[...])

    @pl.when(j == pl.num_programs(1) - 1)
    def _():
        o_ref[...] = acc[...]


def _mlp(x2, g, w1, w2, l):
    n, d = x2.shape
    dff = w1.shape[2]
    tm = min(TM_MLP, n)
    tf = TF_MLP
    return pl.pallas_call(
        _mlp_kernel,
        grid=(n // tm, dff // tf),
        in_specs=[pl.BlockSpec((tm, d), lambda i, j: (i, 0)),
                  pl.BlockSpec((None, 1, d), lambda i, j: (l, 0, 0)),
                  pl.BlockSpec((None, d, tf), lambda i, j: (l, 0, j)),
                  pl.BlockSpec((None, tf, d), lambda i, j: (l, j, 0))],
        out_specs=pl.BlockSpec((tm, d), lambda i, j: (i, 0)),
        out_shape=jax.ShapeDtypeStruct((n, d), F32),
        scratch_shapes=[pltpu.VMEM((tm, d), BF16), pltpu.VMEM((tm, d), F32)],
        compiler_params=_cparams(("parallel", "arbitrary")),
        name="mlp",
    )(x2, g, w1, w2)


def _prep_w_in(w_in):
    nl, d, _ = w_in.shape
    gate0 = 6 * C + W_LORA + A_LORA + G_LORA + Q_LORA + KV_LORA + QK_ROPE
    kr0 = gate0 - QK_ROPE
    parts = [
        w_in[:, :, gate0:],
        w_in[:, :, :kr0],
        w_in[:, :, kr0:kr0 + ROT_HALF],
        jnp.zeros((nl, d, NOPE_A), w_in.dtype),
        w_in[:, :, kr0 + ROT_HALF:gate0],
        jnp.zeros((nl, d, D_INP - COL_KR - SLOT // 2 - ROT_HALF), w_in.dtype),
    ]
    return jnp.concatenate(parts, axis=-1).astype(BF16)


def _prep_lru_gate(gate_w):
    nl, _, nh, hd, _ = gate_w.shape
    eye = jnp.eye(nh, dtype=gate_w.dtype)
    dense = jnp.einsum("lghij,hk->lghikj", gate_w, eye).reshape(nl, 2, nh * hd, nh * hd)
    return jnp.concatenate([dense[:, 0], dense[:, 1]], axis=-1).astype(BF16)


def kernel(x, norm_mix, w_in, lru_conv_w, lru_conv_b, lru_gate_w, lru_gate_b, lru_lambda, s5_a_re, s5_a_im, s5_b_re, s5_b_im, s5_c_re, s5_c_im, s5_d, s5_log_dt, s5_w_glu, rwkv_mu_rkv, rwkv_mu_w, rwkv_mu_a, rwkv_mu_g, rwkv_w0, rwkv_w2, rwkv_a0, rwkv_a2, rwkv_g2, rwkv_k_k, rwkv_k_a, rwkv_r_k, rwkv_lnx_g, rwkv_lnx_b, mla_q_norm, mla_w_uq, mla_kv_norm, mla_w_ukv, mla_qk_norm_q, mla_qk_norm_k, w_branch, w_out, norm_mlp, w_ff1, w_ff2):
    b, t, d = x.shape
    depth = w_in.shape[0]
    n = b * t
    row = lambda a: a.reshape(depth, 1, -1).astype(F32)

    w_in_p = _prep_w_in(w_in)
    lru_gw = _prep_lru_gate(lru_gate_w)
    s5_tabs = _prep_s5(s5_a_re, s5_a_im, s5_b_re, s5_b_im, s5_c_re, s5_c_im, s5_log_dt)
    rw_prm = _prep_rwkv(rwkv_mu_rkv, rwkv_mu_w, rwkv_mu_a, rwkv_mu_g, rwkv_w0, rwkv_w2, rwkv_a0, rwkv_a2, rwkv_g2,
                        rwkv_k_k, rwkv_k_a, rwkv_r_k, rwkv_lnx_g, rwkv_lnx_b)
    mla_prm = _prep_mla(mla_q_norm, mla_w_uq, mla_kv_norm, mla_w_ukv)
    rope = _rope_tables(t, mla_qk_norm_q, mla_qk_norm_k)
    score_bound = _score_bound(mla_qk_norm_q, mla_qk_norm_k)
    g_mix, g_mlp = row(norm_mix), row(norm_mlp)
    conv_b, gate_b, lam, s5_dr = row(lru_conv_b), row(lru_gate_b), row(lru_lambda), row(s5_d)
    w_glu, wb, wo = s5_w_glu.astype(BF16), w_branch.astype(BF16), w_out.astype(BF16)
    w1, w2 = w_ff1.astype(BF16), w_ff2.astype(BF16)

    lru_prm = (lru_conv_w.astype(F32), conv_b, lru_gw, gate_b, lam)
    x2 = x.reshape(n, d)
    for l in range(depth):
        z2, u2 = _inproj(x2, g_mix, w_in_p, l)
        z = z2.reshape(b, t, -1)
        y_lru = _lru_branch(z, lru_prm, l)
        y_ssm = _s5_branch(u2.reshape(b, t, C), s5_tabs, l)
        y_rw = _rwkv_branch(z, rw_prm, l)
        y_mla = _mla_branch(z, mla_prm, rope, score_bound[l], l)
        y_lru, y_ssm, y_rw, y_mla = (y.reshape(n, C) for y in (y_lru, y_ssm, y_rw, y_mla))
        x2 = _merge(y_lru, y_ssm, u2, y_rw, y_mla, z2, x2, s5_dr, w_glu, wb, wo, l)
        x2 = _mlp(x2, g_mlp, w1, w2, l)
    return x2.reshape(b, t, d)
```
